```python
import jax, jax.numpy as jnp
from jax import lax
import numpy as np

D_MODEL = 2048
BATCH = 2
SEQ = 4096
DEPTH = 1
DEC_BATCH = 8
DEC_SEQ = 1
PAST_LEN = 16384
PAGE_SIZE = 128

HEAD_DIM = 128
N_HEADS = D_MODEL // HEAD_DIM
N_KV_HEADS = 4
KV_GROUP = N_HEADS // N_KV_HEADS
ATT_WIDTH = N_HEADS * HEAD_DIM
KV_WIDTH = N_KV_HEADS * HEAD_DIM
IDX_HEADS = 16
IDX_DIM = 128
IDX_SCALE = (IDX_HEADS * IDX_DIM) ** -0.5
TOPK_MAX = 256
Q_BLOCK = 128
POOL_WINDOWS = (2, 4, 8, 16)
POOL_GROUPS = 4
POOL_WIDTH = D_MODEL // 2
POOL_GROUP_DIM = POOL_WIDTH // POOL_GROUPS
POOL_STATE = max(POOL_WINDOWS) - 1
ROPE_THETA = 10000.0
EPS = 1e-6
IN_SIZES = (ATT_WIDTH, KV_WIDTH, KV_WIDTH, IDX_HEADS * IDX_DIM, IDX_DIM, IDX_HEADS,
            ATT_WIDTH, POOL_WIDTH, POOL_WIDTH, D_MODEL, D_MODEL)
IN_WIDTH = sum(IN_SIZES)

kernel_name = 'dsa_pool_gated_hybrid_step'


def _rms(x, g):
    xf = x.astype(jnp.float32)
    y = xf * lax.rsqrt(jnp.mean(xf * xf, axis=-1, keepdims=True) + EPS)
    return (y * g.astype(jnp.float32)).astype(x.dtype)


def _rope(x, pos):
    half = x.shape[-1] // 2
    inv = ROPE_THETA ** (-jnp.arange(half, dtype=jnp.float32) / half)
    ang = pos.astype(jnp.float32)[:, None] * inv[None, :]
    cos = jnp.cos(ang)[:, None, :]
    sin = jnp.sin(ang)[:, None, :]
    xf = x.astype(jnp.float32)
    x1, x2 = xf[..., :half], xf[..., half:]
    return jnp.concatenate([x1 * cos - x2 * sin, x2 * cos + x1 * sin], axis=-1).astype(x.dtype)


def _modulate(x, c, g_norm, w_ada, b_ada):
    mod = c @ w_ada + b_ada
    shift, scale, gate = jnp.split(mod, 3, axis=-1)
    h = _rms(x, g_norm) * (1 + scale[:, None, :]) + shift[:, None, :]
    return h, gate


def _project_in(h, w_in, pos):
    z = h @ w_in
    parts = []
    off = 0
    for n in IN_SIZES:
        parts.append(z[..., off:off + n])
        off += n
    q, k, v, qi, ki, wi, ga, u, gb, ma, mb = parts
    b, t = h.shape[:2]
    q = _rope(q.reshape(b, t, N_HEADS, HEAD_DIM), pos)
    k = _rope(k.reshape(b, t, N_KV_HEADS, HEAD_DIM), pos)
    v = v.reshape(b, t, N_KV_HEADS, HEAD_DIM)
    qi = _rope(qi.reshape(b, t, IDX_HEADS, IDX_DIM), pos)
    ki = _rope(ki.reshape(b, t, 1, IDX_DIM), pos)[:, :, 0]
    wi = wi * IDX_SCALE
    return q, k, v, qi, ki, wi, ga, u, gb, ma, mb


def _indexer_topk(qi, wi, ki, q_pos, topk):
    s = jnp.einsum('bqhd,bsd->bqhs', qi, ki, preferred_element_type=jnp.float32)
    score = jnp.einsum('bqh,bqhs->bqs', wi.astype(jnp.float32), jax.nn.relu(s))
    k_pos = jnp.arange(ki.shape[1], dtype=jnp.int32)
    causal = k_pos[None, :] <= q_pos[:, None]
    score = jnp.where(causal[None], score, -jnp.inf)
    _, idx = lax.top_k(score, topk)
    valid = idx <= q_pos[None, :, None]
    return idx, valid


def _take_rows(rows, idx):
    return jax.vmap(lambda r, i: r[i])(rows, idx)


def _paged_rows(cache, page_table, new_rows, idx):
    past_len = page_table.shape[1] * PAGE_SIZE
    flat = cache.reshape((-1,) + cache.shape[2:])
    pidx = jnp.minimum(idx, past_len - 1)
    page = jax.vmap(lambda pt, i: pt[i])(page_table, pidx // PAGE_SIZE)
    past = flat[page * PAGE_SIZE + pidx % PAGE_SIZE].astype(new_rows.dtype)
    nidx = jnp.clip(idx - past_len, 0, new_rows.shape[1] - 1)
    new = _take_rows(new_rows, nidx)
    is_past = (idx < past_len).reshape(idx.shape + (1,) * (new.ndim - idx.ndim))
    return jnp.where(is_past, past, new)


def _gathered_attention(q, k_sel, v_sel, valid):
    b, t = q.shape[:2]
    qg = q.reshape(b, t, N_KV_HEADS, KV_GROUP, HEAD_DIM)
    logits = jnp.einsum('btngd,btjnd->btngj', qg, k_sel,
                        preferred_element_type=jnp.float32) * (HEAD_DIM ** -0.5)
    logits = jnp.where(valid[:, :, None, None, :], logits, -jnp.inf)
    p = jax.nn.softmax(logits, axis=-1).astype(v_sel.dtype)
    o = jnp.einsum('btngj,btjnd->btngd', p, v_sel)
    return o.reshape(b, t, ATT_WIDTH)


def _prompt_attention(q, k, v, qi, ki, wi):
    b, s = q.shape[:2]
    topk = min(TOPK_MAX, s // 4)
    nblk = s // Q_BLOCK

    def blocks(a):
        return a.reshape((b, nblk, Q_BLOCK) + a.shape[2:]).swapaxes(0, 1)

    pos = jnp.arange(s, dtype=jnp.int32).reshape(nblk, Q_BLOCK)

    def one_block(args):
        qb, qib, wib, pb = args
        idx, valid = _indexer_topk(qib, wib, ki, pb, topk)
        return _gathered_attention(qb, _take_rows(k, idx), _take_rows(v, idx), valid)

    o = lax.map(one_block, (blocks(q), blocks(qi), blocks(wi), pos))
    return o.swapaxes(0, 1).reshape(b, s, ATT_WIDTH)


def _sample_attention(q, k, v, qi, ki, wi, cache_k, cache_v, cache_kidx, page_table):
    b, t = q.shape[:2]
    past = page_table.shape[1] * PAGE_SIZE
    topk = min(TOPK_MAX, (past + t) // 4)
    ki_past = cache_kidx[page_table].reshape(b, past, IDX_DIM).astype(ki.dtype)
    ki_all = jnp.concatenate([ki_past, ki], axis=1)
    q_pos = past + jnp.arange(t, dtype=jnp.int32)
    idx, valid = _indexer_topk(qi, wi, ki_all, q_pos, topk)
    k_sel = _paged_rows(cache_k, page_table, k, idx)
    v_sel = _paged_rows(cache_v, page_table, v, idx)
    return _gathered_attention(q, k_sel, v_sel, valid)


def _pool_mixer(u, prev, pos0, w_pool, s_pool):
    b, t, c = u.shape
    ext = jnp.concatenate([prev.astype(u.dtype), u], axis=1)
    cs = jnp.cumsum(ext.astype(jnp.float32), axis=1)
    cs = jnp.concatenate([jnp.zeros((b, 1, c), jnp.float32), cs], axis=1)
    end = cs[:, POOL_STATE + 1:]
    n_avail = pos0 + jnp.arange(t, dtype=jnp.int32) + 1
    means = []
    for g, w in enumerate(POOL_WINDOWS):
        lo, hi = g * POOL_GROUP_DIM, (g + 1) * POOL_GROUP_DIM
        start = cs[:, POOL_STATE + 1 - w:POOL_STATE + 1 - w + t, lo:hi]
        cnt = jnp.minimum(n_avail, w).astype(jnp.float32)[None, :, None]
        means.append((end[..., lo:hi] - start) / cnt)
    pooled = (jnp.concatenate(means, axis=-1) - u.astype(jnp.float32)).astype(u.dtype)
    mixed = jnp.einsum('btgc,gcd->btgd', pooled.reshape(b, t, POOL_GROUPS, POOL_GROUP_DIM), w_pool)
    return mixed.reshape(b, t, c) * s_pool, ext[:, -POOL_STATE:]


def _merge(x, attn, pool, ga, gb, ma, mb, gate, w_a_proj, w_b_proj, w_out):
    a = (attn * jax.nn.silu(ga)) @ w_a_proj
    bb = (pool * jax.nn.silu(gb)) @ w_b_proj
    m = jax.nn.sigmoid(ma) * a + jax.nn.sigmoid(mb) * bb
    return x + gate[:, None, :] * (m @ w_out)


def setup_inputs(seed: int = 0) -> dict:
    key = jax.random.key(seed)
    ks = jax.random.split(key, 20)
    f = jnp.float32
    n_pages = PAST_LEN // PAGE_SIZE
    n_used = DEC_BATCH * n_pages
    n_pool = n_used + max(n_used // 4, 1)

    def nrm(k, shape, s):
        return jax.random.normal(k, shape, f) * s

    page_table = jax.random.permutation(ks[8], n_pool)[:n_used].reshape(DEC_BATCH, n_pages).astype(jnp.int32)
    return {
        'x_prompt': nrm(ks[0], (BATCH, SEQ, D_MODEL), 1.0),
        'x_sample': nrm(ks[1], (DEC_BATCH, DEC_SEQ, D_MODEL), 1.0),
        'c_prompt': nrm(ks[2], (BATCH, D_MODEL), 1.0),
        'c_sample': nrm(ks[3], (DEC_BATCH, D_MODEL), 1.0),
        'cache_k': nrm(ks[4], (DEPTH, n_pool, PAGE_SIZE, N_KV_HEADS, HEAD_DIM), 1.0),
        'cache_v': nrm(ks[5], (DEPTH, n_pool, PAGE_SIZE, N_KV_HEADS, HEAD_DIM), 1.0),
        'cache_kidx': nrm(ks[6], (DEPTH, n_pool, PAGE_SIZE, IDX_DIM), 1.0),
        'state_pool': nrm(ks[7], (DEPTH, DEC_BATCH, POOL_STATE, POOL_WIDTH), 1.0),
        'page_table': page_table,
        'g_norm': 1.0 + nrm(ks[9], (DEPTH, D_MODEL), 0.01),
        'w_ada': nrm(ks[10], (DEPTH, D_MODEL, 3 * D_MODEL), 0.5 * D_MODEL ** -0.5),
        'b_ada': nrm(ks[11], (DEPTH, 3 * D_MODEL), 0.01),
        'w_in': nrm(ks[12], (DEPTH, D_MODEL, IN_WIDTH), D_MODEL ** -0.5),
        'w_pool': nrm(ks[13], (DEPTH, POOL_GROUPS, POOL_GROUP_DIM, POOL_GROUP_DIM), POOL_GROUP_DIM ** -0.5),
        's_pool': 1.0 + nrm(ks[14], (DEPTH, POOL_WIDTH), 0.1),
        'w_a_proj': nrm(ks[15], (DEPTH, ATT_WIDTH, D_MODEL), ATT_WIDTH ** -0.5),
        'w_b_proj': nrm(ks[16], (DEPTH, POOL_WIDTH, D_MODEL), POOL_WIDTH ** -0.5),
        'w_out': nrm(ks[17], (DEPTH, D_MODEL, D_MODEL), D_MODEL ** -0.5),
        'g_final': 1.0 + nrm(ks[18], (D_MODEL,), 0.01),
    }


def reference(x_prompt, x_sample, c_prompt, c_sample, cache_k, cache_v, cache_kidx, state_pool,
              page_table, g_norm, w_ada, b_ada, w_in, w_pool, s_pool, w_a_proj, w_b_proj, w_out, g_final):
    bp, sp = x_prompt.shape[:2]
    ds = x_sample.shape[1]
    past = page_table.shape[1] * PAGE_SIZE
    pos_p = jnp.arange(sp, dtype=jnp.int32)
    pos_s = past + jnp.arange(ds, dtype=jnp.int32)
    xp, xs = x_prompt, x_sample
    kp_l, vp_l, kip_l, pp_l = [], [], [], []
    ks_l, vs_l, kis_l, ps_l = [], [], [], []
    for l in range(DEPTH):
        h, gate = _modulate(xp, c_prompt, g_norm[l], w_ada[l], b_ada[l])
        q, k, v, qi, ki, wi, ga, u, gb, ma, mb = _project_in(h, w_in[l], pos_p)
        attn = _prompt_attention(q, k, v, qi, ki, wi)
        prev0 = jnp.zeros((bp, POOL_STATE, POOL_WIDTH), u.dtype)
        pool, pst = _pool_mixer(u, prev0, 0, w_pool[l], s_pool[l])
        xp = _merge(xp, attn, pool, ga, gb, ma, mb, gate, w_a_proj[l], w_b_proj[l], w_out[l])
        kp_l.append(k); vp_l.append(v); kip_l.append(ki); pp_l.append(pst)
        h, gate = _modulate(xs, c_sample, g_norm[l], w_ada[l], b_ada[l])
        q, k, v, qi, ki, wi, ga, u, gb, ma, mb = _project_in(h, w_in[l], pos_s)
        attn = _sample_attention(q, k, v, qi, ki, wi, cache_k[l], cache_v[l], cache_kidx[l], page_table)
        pool, sst = _pool_mixer(u, state_pool[l], past, w_pool[l], s_pool[l])
        xs = _merge(xs, attn, pool, ga, gb, ma, mb, gate, w_a_proj[l], w_b_proj[l], w_out[l])
        ks_l.append(k); vs_l.append(v); kis_l.append(ki); ps_l.append(sst)
    y_prompt = _rms(xp, g_final)
    y_sample = _rms(xs, g_final)
    return (y_prompt, y_sample,
            jnp.stack(kp_l), jnp.stack(vp_l), jnp.stack(kip_l), jnp.stack(pp_l),
            jnp.stack(ks_l), jnp.stack(vs_l), jnp.stack(kis_l), jnp.stack(ps_l))
```

```python
import functools

import jax
import jax.numpy as jnp
from jax import lax
from jax.experimental import pallas as pl
from jax.experimental.pallas import tpu as pltpu

F32 = jnp.float32
BF16 = jnp.bfloat16

HEAD_DIM = 128
N_KV_HEADS = 4
IDX_HEADS = 16
IDX_DIM = 128
TOPK_MAX = 256
PAGE_SIZE = 128
POOL_WINDOWS = (2, 4, 8, 16)
POOL_STATE = max(POOL_WINDOWS) - 1
ROPE_THETA = 10000.0
EPS = 1e-6

LANES = 128
SUBLANES = 8
TN = 512
TQ = 256
CK = 256
NEG = -1e30
MAX_BISECT = 512
VMEM_LIMIT = 56 * 1024 * 1024


def _cparams(sem):
    return pltpu.CompilerParams(dimension_semantics=sem, vmem_limit_bytes=VMEM_LIMIT)


def _sigmoid(z):
    return 1.0 / (1.0 + jnp.exp(-z))


def _ada_kernel(c_ref, w_ref, b_ref, o_ref):
    o_ref[...] = jnp.dot(c_ref[...].astype(BF16), w_ref[...].astype(BF16),
                         preferred_element_type=F32) + b_ref[...]


def _ada(c, w, b):
    r, d = c.shape
    n = w.shape[1]
    return pl.pallas_call(
        _ada_kernel,
        grid=(n // TN,),
        in_specs=[pl.BlockSpec((r, d), lambda j: (0, 0)),
                  pl.BlockSpec((d, TN), lambda j: (0, j)),
                  pl.BlockSpec((1, TN), lambda j: (0, j))],
        out_specs=pl.BlockSpec((r, TN), lambda j: (0, j)),
        out_shape=jax.ShapeDtypeStruct((r, n), F32),
        compiler_params=_cparams(("arbitrary",)),
    )(c, w, b.reshape(1, n))


class _Layout:
    def __init__(self, d):
        att, kv, idx, pool = d, N_KV_HEADS * HEAD_DIM, IDX_HEADS * IDX_DIM, d // 2
        t = lambda n: n // TN
        assert att % TN == 0 and kv == TN and idx % TN == 0 and pool % TN == 0
        self.q = 0
        self.qi = self.q + t(att)
        self.ga = self.qi + t(idx)
        self.ma = self.ga + t(att)
        self.mb = self.ma + t(d)
        self.gb = self.mb + t(d)
        self.k = self.gb + t(pool)
        self.v = self.k + 1
        self.misc = self.v + 1
        self.u = self.misc + 1
        self.end = self.u + t(pool)
        self.nb = self.u
        self.pool = pool


def _rope_tile(z, cos, sin):
    parts = []
    for s in range(z.shape[1] // HEAD_DIM):
        zs = z[:, s * HEAD_DIM:(s + 1) * HEAD_DIM]
        parts.append(zs * cos + pltpu.roll(zs, HEAD_DIM // 2, 1) * sin)
    return parts[0] if len(parts) == 1 else jnp.concatenate(parts, axis=1)


def _proj_kernel(lay, idx_scale, x_ref, g_ref, shift_ref, scale_ref, w_ref, cos_ref, sin_ref,
                 zb_ref, k_ref, v_ref, misc_ref, u_ref, h_ref):
    j = pl.program_id(1)

    @pl.when(j == 0)
    def _():
        x = x_ref[...]
        r = lax.rsqrt(jnp.mean(x * x, axis=-1, keepdims=True) + EPS)
        h = (x * r * g_ref[...]) * (1.0 + scale_ref[0]) + shift_ref[0]
        h_ref[...] = h.astype(BF16)

    z = jnp.dot(h_ref[...], w_ref[...], preferred_element_type=F32)

    @pl.when((j < lay.ga) | (j == lay.k))
    def _():
        r = _rope_tile(z, cos_ref[...], sin_ref[...])
        zb_ref[...] = r.astype(BF16)

        @pl.when(j == lay.k)
        def _():
            k_ref[...] = r

    @pl.when(((j >= lay.ga) & (j < lay.ma)) | ((j >= lay.gb) & (j < lay.k)))
    def _():
        zb_ref[...] = (z * _sigmoid(z)).astype(BF16)

    @pl.when((j >= lay.ma) & (j < lay.gb))
    def _():
        zb_ref[...] = _sigmoid(z).astype(BF16)

    @pl.when(j == lay.v)
    def _():
        zb_ref[...] = z.astype(BF16)
        v_ref[...] = z

    @pl.when(j == lay.misc)
    def _():
        ki = _rope_tile(z[:, :IDX_DIM], cos_ref[...], sin_ref[...])
        m = jnp.concatenate([ki, z[:, IDX_DIM:] * idx_scale], axis=1)
        zb_ref[...] = m.astype(BF16)
        misc_ref[...] = m

    @pl.when(j >= lay.u)
    def _():
        u_ref[...] = z


def _project(x, g, shift, scale, w_p, cos, sin, lay, tm, tiles_per_group, tiles_per_pos, idx_scale):
    m, d = x.shape
    grp, r, _ = shift.shape
    nu = lay.end - lay.u
    kern = functools.partial(_proj_kernel, lay, idx_scale)
    return pl.pallas_call(
        kern,
        grid=(m // tm, lay.end),
        in_specs=[pl.BlockSpec((tm, d), lambda i, j: (i, 0)),
                  pl.BlockSpec((1, d), lambda i, j: (0, 0)),
                  pl.BlockSpec((1, r, d), lambda i, j: (i // tiles_per_group, 0, 0)),
                  pl.BlockSpec((1, r, d), lambda i, j: (i // tiles_per_group, 0, 0)),
                  pl.BlockSpec((d, TN), lambda i, j: (0, j)),
                  pl.BlockSpec((tm, HEAD_DIM), lambda i, j: (i % tiles_per_pos, 0)),
                  pl.BlockSpec((tm, HEAD_DIM), lambda i, j: (i % tiles_per_pos, 0))],
        out_specs=[pl.BlockSpec((tm, TN), lambda i, j: (i, jnp.minimum(j, lay.nb - 1))),
                   pl.BlockSpec((tm, TN), lambda i, j: (i, 0)),
                   pl.BlockSpec((tm, TN), lambda i, j: (i, 0)),
                   pl.BlockSpec((tm, TN), lambda i, j: (i, 0)),
                   pl.BlockSpec((tm, TN), lambda i, j: (i, jnp.clip(j - lay.u, 0, nu - 1)))],
        out_shape=[jax.ShapeDtypeStruct((m, lay.nb * TN), BF16),
                   jax.ShapeDtypeStruct((m, TN), F32),
                   jax.ShapeDtypeStruct((m, TN), F32),
                   jax.ShapeDtypeStruct((m, TN), F32),
                   jax.ShapeDtypeStruct((m, nu * TN), F32)],
        scratch_shapes=[pltpu.VMEM((tm, d), BF16)],
        compiler_params=_cparams(("arbitrary", "arbitrary")),
    )(x, g, shift, scale, w_p, cos, sin)


def _bisect(count_ge, lo, hi, clo, kf):
    def cond(st):
        return (st[4] > 0.0) & (st[5] < MAX_BISECT)

    def body(st):
        lo, hi, clo, done, _, it = st
        mid = 0.5 * lo + 0.5 * hi
        mid = jnp.where(mid <= lo, hi, mid)
        c = count_ge(mid)
        ge = c >= kf
        stuck = (mid >= hi) & jnp.logical_not(ge)
        lo2 = jnp.where(ge, mid, lo)
        hi2 = jnp.where(ge, hi, mid)
        clo2 = jnp.where(ge, c, clo)
        fin = (clo2 <= kf) | (lo2 >= hi2) | stuck
        done2 = jnp.maximum(done, jnp.where(fin, 1.0, 0.0))
        return lo2, hi2, clo2, done2, jnp.sum(1.0 - done2), it + 1

    done0 = jnp.where((clo <= kf) | (lo >= hi), 1.0, 0.0)
    st = (lo, hi, clo, done0, jnp.sum(1.0 - done0), jnp.int32(0))
    return lax.while_loop(cond, body, st)[0]


def _attn_kernel(topk, sm_scale, q_ref, qi_ref, sga_ref, k_ref, ki_ref, vt_ref, wt_ref, o_ref,
                 score_ref, m_ref, l_ref, acc_ref):
    i = pl.program_id(1)
    nck = i + 1
    n_heads = q_ref.shape[1] // HEAD_DIM
    group = n_heads // N_KV_HEADS
    nt = (((1,), (1,)), ((), ()))
    q_pos = i * TQ + lax.broadcasted_iota(jnp.int32, (1, TQ), 1)
    fold = lambda a: a.reshape(CK // SUBLANES, SUBLANES, TQ)

    def score_chunk(c, carry):
        mn, mx = carry
        off = pl.multiple_of(c * CK, CK)
        kc = ki_ref[pl.ds(off, CK), :]
        acc = jnp.zeros((CK, TQ), F32)
        for h in range(IDX_HEADS):
            s = lax.dot_general(kc, qi_ref[:, h * IDX_DIM:(h + 1) * IDX_DIM], nt,
                                preferred_element_type=F32)
            acc = acc + wt_ref[h:h + 1, :] * jnp.maximum(s, 0.0)
        k_pos = off + lax.broadcasted_iota(jnp.int32, (CK, 1), 0)
        causal = k_pos <= q_pos
        score_ref[pl.ds(off, CK), :] = jnp.where(causal, acc, -jnp.inf)
        mn = jnp.minimum(mn, fold(jnp.where(causal, acc, jnp.inf)).min(axis=0))
        mx = jnp.maximum(mx, fold(jnp.where(causal, acc, -jnp.inf)).max(axis=0))
        return mn, mx

    mn, mx = lax.fori_loop(0, nck, score_chunk,
                           (jnp.full((SUBLANES, TQ), jnp.inf, F32),
                            jnp.full((SUBLANES, TQ), -jnp.inf, F32)))
    lo = mn.min(axis=0, keepdims=True)
    hi = mx.max(axis=0, keepdims=True)

    def count_ge(thr):
        def body(c, cnt):
            s = score_ref[pl.ds(pl.multiple_of(c * CK, CK), CK), :]
            return cnt + fold(jnp.where(s >= thr, 1.0, 0.0)).sum(axis=0)
        cnt = lax.fori_loop(0, nck, body, jnp.zeros((SUBLANES, TQ), F32))
        return cnt.sum(axis=0, keepdims=True)

    n_valid = (q_pos + 1).astype(F32)
    thr = _bisect(count_ge, lo, hi, n_valid, float(topk))

    m_ref[...] = jnp.full(m_ref.shape, NEG, F32)
    l_ref[...] = jnp.zeros(l_ref.shape, F32)
    acc_ref[...] = jnp.zeros(acc_ref.shape, F32)

    def att_chunk(c, carry):
        off = pl.multiple_of(c * CK, CK)
        mask = score_ref[pl.ds(off, CK), :] >= thr
        kc = k_ref[pl.ds(off, CK), :]
        vtc = vt_ref[0, :, pl.ds(off, CK)]
        for h in range(n_heads):
            g = h // group
            lg = lax.dot_general(kc[:, g * HEAD_DIM:(g + 1) * HEAD_DIM],
                                 q_ref[:, h * HEAD_DIM:(h + 1) * HEAD_DIM], nt,
                                 preferred_element_type=F32) * sm_scale
            lg = jnp.where(mask, lg, NEG)
            m_old = m_ref[h:h + 1, :]
            m_new = jnp.maximum(m_old, lg.max(axis=0, keepdims=True))
            alpha = jnp.exp(m_old - m_new)
            p = jnp.exp(lg - m_new)
            l_ref[h:h + 1, :] = alpha * l_ref[h:h + 1, :] + p.sum(axis=0, keepdims=True)
            pv = jnp.dot(vtc[g * HEAD_DIM:(g + 1) * HEAD_DIM, :], p.astype(BF16),
                         preferred_element_type=F32)
            acc_ref[h] = alpha * acc_ref[h] + pv
            m_ref[h:h + 1, :] = m_new
        return carry

    lax.fori_loop(0, nck, att_chunk, 0)

    for h in range(n_heads):
        o = (acc_ref[h] / l_ref[h:h + 1, :]).T
        sl = slice(h * HEAD_DIM, (h + 1) * HEAD_DIM)
        o_ref[:, sl] = (o * sga_ref[:, sl].astype(F32)).astype(BF16)


def _prompt_attention(zb, vt, wt, lay, b, s, d, topk):
    nq = s // TQ
    n_heads = d // HEAD_DIM
    kern = functools.partial(_attn_kernel, topk, HEAD_DIM ** -0.5)
    row = lambda bb, i: bb * nq + i
    return pl.pallas_call(
        kern,
        grid=(b, nq),
        in_specs=[pl.BlockSpec((TQ, d), lambda bb, i: (row(bb, i), lay.q * TN // d)),
                  pl.BlockSpec((TQ, d), lambda bb, i: (row(bb, i), lay.qi * TN // d)),
                  pl.BlockSpec((TQ, d), lambda bb, i: (row(bb, i), lay.ga * TN // d)),
                  pl.BlockSpec((s, TN), lambda bb, i: (bb, lay.k)),
                  pl.BlockSpec((s, IDX_DIM), lambda bb, i: (bb, lay.misc * TN // IDX_DIM)),
                  pl.BlockSpec((1, TN, s), lambda bb, i: (bb, 0, 0)),
                  pl.BlockSpec((IDX_HEADS, TQ), lambda bb, i: (0, row(bb, i)))],
        out_specs=pl.BlockSpec((TQ, d), lambda bb, i: (row(bb, i), 0)),
        out_shape=jax.ShapeDtypeStruct((b * s, d), BF16),
        scratch_shapes=[pltpu.VMEM((s, TQ), F32),
                        pltpu.VMEM((n_heads, TQ), F32),
                        pltpu.VMEM((n_heads, TQ), F32),
                        pltpu.VMEM((n_heads, HEAD_DIM, TQ), F32)],
        compiler_params=_cparams(("arbitrary", "arbitrary")),
    )(zb, zb, zb, zb, zb, vt, wt)


def _pool_kernel(ts, u_ref, sgb_ref, wp_ref, sp_ref, o_ref, ext_ref):
    i = pl.program_id(1)
    halo = POOL_STATE + 1

    @pl.when(i == 0)
    def _():
        ext_ref[0:halo, :] = jnp.zeros((halo, ext_ref.shape[1]), F32)

    @pl.when(i > 0)
    def _():
        ext_ref[0:halo, :] = ext_ref[ts:ts + halo, :]

    ext_ref[halo:halo + ts, :] = u_ref[...]
    pos = i * ts + lax.broadcasted_iota(jnp.int32, (ts, 1), 0)
    gd = wp_ref.shape[1]
    for g, w in enumerate(POOL_WINDOWS):
        cols = slice(g * gd, (g + 1) * gd)
        acc = ext_ref[halo:halo + ts, cols]
        for jj in range(1, w):
            acc = acc + ext_ref[halo - jj:halo - jj + ts, cols]
        cnt = jnp.minimum(pos + 1, w).astype(F32)
        pooled = acc / cnt - u_ref[:, cols]
        mixed = jnp.dot(pooled.astype(BF16), wp_ref[g], preferred_element_type=F32) * sp_ref[:, cols]
        o_ref[:, cols] = (mixed * sgb_ref[:, cols].astype(F32)).astype(BF16)


def _pool_prompt(u, zb, wp, sp, lay, b, s, ts):
    m, pw = u.shape
    nt = s // ts
    kern = functools.partial(_pool_kernel, ts)
    return pl.pallas_call(
        kern,
        grid=(b, nt),
        in_specs=[pl.BlockSpec((ts, pw), lambda bb, i: (bb * nt + i, 0)),
                  pl.BlockSpec((ts, pw), lambda bb, i: (bb * nt + i, lay.gb * TN // pw)),
                  pl.BlockSpec(wp.shape, lambda bb, i: (0, 0, 0)),
                  pl.BlockSpec((1, pw), lambda bb, i: (0, 0))],
        out_specs=pl.BlockSpec((ts, pw), lambda bb, i: (bb * nt + i, 0)),
        out_shape=jax.ShapeDtypeStruct((m, pw), BF16),
        scratch_shapes=[pltpu.VMEM((POOL_STATE + 1 + ts, pw), F32)],
        compiler_params=_cparams(("arbitrary", "arbitrary")),
    )(u, zb, wp, sp)


def _pool_step_kernel(cnts, ext_ref, sgb_ref, wp_ref, sp_ref, o_ref):
    gd = wp_ref.shape[1]
    rows = ext_ref.shape[0]
    for g, w in enumerate(POOL_WINDOWS):
        cols = slice(g * gd, (g + 1) * gd)
        acc = ext_ref[rows - 1, :, cols]
        for jj in range(1, w):
            acc = acc + ext_ref[rows - 1 - jj, :, cols]
        pooled = acc / cnts[g] - ext_ref[rows - 1, :, cols]
        mixed = jnp.dot(pooled.astype(BF16), wp_ref[g], preferred_element_type=F32) * sp_ref[:, cols]
        o_ref[:, cols] = (mixed * sgb_ref[:, cols].astype(F32)).astype(BF16)


def _pool_step(ext_t, sgb, wp, sp, past):
    rows, n, pw = ext_t.shape
    cnts = tuple(float(min(past + 1, w)) for w in POOL_WINDOWS)
    return pl.pallas_call(
        functools.partial(_pool_step_kernel, cnts),
        out_shape=jax.ShapeDtypeStruct((n, pw), BF16),
    )(ext_t, sgb, wp, sp)


def _merge1_kernel(ab_ref, pb_ref, wa_ref, wb_ref, sma_ref, smb_ref, o_ref):
    a = jnp.dot(ab_ref[...], wa_ref[...], preferred_element_type=F32)
    bb = jnp.dot(pb_ref[...], wb_ref[...], preferred_element_type=F32)
    o_ref[...] = (sma_ref[...].astype(F32) * a + smb_ref[...].astype(F32) * bb).astype(BF16)


def _merge1(ab, pb, wa, wb, zb, lay, tm):
    m, d = ab.shape
    pw = pb.shape[1]
    return pl.pallas_call(
        _merge1_kernel,
        grid=(m // tm, d // TN),
        in_specs=[pl.BlockSpec((tm, d), lambda i, j: (i, 0)),
                  pl.BlockSpec((tm, pw), lambda i, j: (i, 0)),
                  pl.BlockSpec((d, TN), lambda i, j: (0, j)),
                  pl.BlockSpec((pw, TN), lambda i, j: (0, j)),
                  pl.BlockSpec((tm, TN), lambda i, j: (i, lay.ma + j)),
                  pl.BlockSpec((tm, TN), lambda i, j: (i, lay.mb + j))],
        out_specs=pl.BlockSpec((tm, TN), lambda i, j: (i, j)),
        out_shape=jax.ShapeDtypeStruct((m, d), BF16),
        compiler_params=_cparams(("arbitrary", "arbitrary")),
    )(ab, pb, wa, wb, zb, zb)


def _merge2_kernel(final, mm_ref, x_ref, gate_ref, wo_ref, gf_ref, o_ref):
    y = x_ref[...] + gate_ref[0] * jnp.dot(mm_ref[...], wo_ref[...], preferred_element_type=F32)
    if final:
        y = y * lax.rsqrt(jnp.mean(y * y, axis=-1, keepdims=True) + EPS) * gf_ref[...]
    o_ref[...] = y


def _merge2(mm, x, gate, wo, gf, tm, tiles_per_group, final):
    m, d = x.shape
    r = gate.shape[1]
    return pl.pallas_call(
        functools.partial(_merge2_kernel, final),
        grid=(m // tm,),
        in_specs=[pl.BlockSpec((tm, d), lambda i: (i, 0)),
                  pl.BlockSpec((tm, d), lambda i: (i, 0)),
                  pl.BlockSpec((1, r, d), lambda i: (i // tiles_per_group, 0, 0)),
                  pl.BlockSpec((d, d), lambda i: (0, 0)),
                  pl.BlockSpec((1, d), lambda i: (0, 0))],
        out_specs=pl.BlockSpec((tm, d), lambda i: (i, 0)),
        out_shape=jax.ShapeDtypeStruct((m, d), F32),
        compiler_params=_cparams(("arbitrary",)),
    )(mm, x, gate, wo, gf)


def _sidx_kernel(pt_ref, kidx_ref, qi_ref, w_ref, o_ref):
    nt = (((1,), (1,)), ((), ()))
    s = lax.dot_general(qi_ref[0], kidx_ref[0].astype(BF16), nt, preferred_element_type=F32)
    o_ref[0, 0] = jnp.sum(w_ref[0] * jnp.maximum(s, 0.0), axis=0, keepdims=True)


def _sample_scores(page_table, cache_kidx, qi3, wcol):
    n, npg = page_table.shape
    return pl.pallas_call(
        _sidx_kernel,
        grid_spec=pltpu.PrefetchScalarGridSpec(
            num_scalar_prefetch=1,
            grid=(n, npg),
            in_specs=[pl.BlockSpec((1, PAGE_SIZE, IDX_DIM), lambda b, p, pt: (pt[b * npg + p], 0, 0)),
                      pl.BlockSpec((1, IDX_HEADS, IDX_DIM), lambda b, p, pt: (b, 0, 0)),
                      pl.BlockSpec((1, IDX_HEADS, 1), lambda b, p, pt: (b, 0, 0))],
            out_specs=pl.BlockSpec((1, 1, 1, PAGE_SIZE), lambda b, p, pt: (b, p, 0, 0))),
        out_shape=jax.ShapeDtypeStruct((n, npg, 1, PAGE_SIZE), F32),
        compiler_params=_cparams(("arbitrary", "arbitrary")),
    )(page_table.reshape(-1), cache_kidx, qi3, wcol)


def _sthr_kernel(topk, sc_ref, qi_ref, kin_ref, w_ref, thr_ref, sel_ref):
    sc = sc_ref[...]
    kin = kin_ref[...].astype(BF16).astype(F32)
    s_new = jnp.sum(qi_ref[...].astype(F32) * kin, axis=2, keepdims=True)
    new = jnp.sum(w_ref[...] * jnp.maximum(s_new, 0.0), axis=1, keepdims=True)
    red = lambda f, a: f(f(a, axis=2, keepdims=True), axis=1, keepdims=True)
    lo = jnp.minimum(red(jnp.min, sc), new)
    hi = jnp.maximum(red(jnp.max, sc), new)

    def count_ge(thr):
        c = red(jnp.sum, jnp.where(sc >= thr, 1.0, 0.0))
        return c + jnp.where(new >= thr, 1.0, 0.0)

    n_all = jnp.full(new.shape, float(sc.shape[1] * sc.shape[2] + 1), F32)
    thr = _bisect(count_ge, lo, hi, n_all, float(topk))
    thr_ref[...] = jnp.broadcast_to(thr, thr_ref.shape)
    sel_ref[...] = jnp.broadcast_to(jnp.where(new >= thr, 1.0, 0.0), sel_ref.shape)


def _sample_threshold(sc, qi3, kin, wcol, topk):
    n = sc.shape[0]
    return pl.pallas_call(
        functools.partial(_sthr_kernel, topk),
        out_shape=[jax.ShapeDtypeStruct((n, 1, PAGE_SIZE), F32),
                   jax.ShapeDtypeStruct((n, 1, PAGE_SIZE), F32)],
        compiler_params=pltpu.CompilerParams(vmem_limit_bytes=VMEM_LIMIT),
    )(sc, qi3, kin, wcol)


def _sattn_kernel(sm_scale, pt_ref, k_ref, v_ref, sc_ref, thr_ref, sel_ref, q_ref, kn_ref, vn_ref, sga_ref,
                  o_ref, m_ref, l_ref, acc_ref):
    p_idx = pl.program_id(1)
    n_heads = q_ref.shape[1]
    group = n_heads // N_KV_HEADS
    nt = (((1,), (1,)), ((), ()))

    @pl.when(p_idx == 0)
    def _():
        m_ref[...] = jnp.full(m_ref.shape, NEG, F32)
        l_ref[...] = jnp.zeros(l_ref.shape, F32)
        acc_ref[...] = jnp.zeros(acc_ref.shape, F32)

    mask = sc_ref[0, 0] >= thr_ref[0]
    kp = k_ref[0].astype(BF16)
    vp = v_ref[0].astype(BF16)
    q = q_ref[0]
    lg = jnp.concatenate(
        [lax.dot_general(q[g * group:(g + 1) * group, :], kp[:, g * HEAD_DIM:(g + 1) * HEAD_DIM], nt,
                         preferred_element_type=F32) for g in range(N_KV_HEADS)], axis=0) * sm_scale
    lg = jnp.where(mask, lg, NEG)
    m_old = m_ref[...]
    m_new = jnp.maximum(m_old, lg.max(axis=1, keepdims=True))
    alpha = jnp.exp(m_old - m_new)
    p = jnp.exp(lg - m_new)
    pb = p.astype(BF16)
    pv = jnp.concatenate(
        [jnp.dot(pb[g * group:(g + 1) * group, :], vp[:, g * HEAD_DIM:(g + 1) * HEAD_DIM],
                 preferred_element_type=F32) for g in range(N_KV_HEADS)], axis=0)
    l_new = alpha * l_ref[...] + p.sum(axis=1, keepdims=True)
    acc_new = alpha * acc_ref[...] + pv
    m_ref[...] = m_new
    l_ref[...] = l_new
    acc_ref[...] = acc_new

    @pl.when(p_idx == pl.num_programs(1) - 1)
    def _():
        kn = kn_ref[0].astype(BF16).astype(F32)
        vn = vn_ref[0].astype(BF16).astype(F32)
        lgn = jnp.sum(q.astype(F32) * kn, axis=1, keepdims=True) * sm_scale
        lgn = jnp.where(sel_ref[0][:, 0:1] > 0.0, lgn, NEG)
        m_fin = jnp.maximum(m_new, lgn)
        a2 = jnp.exp(m_new - m_fin)
        pn = jnp.exp(lgn - m_fin)
        l_fin = a2 * l_new + pn
        acc_fin = a2 * acc_new + pn.astype(BF16).astype(F32) * vn
        o_ref[0] = ((acc_fin / l_fin) * sga_ref[0].astype(F32)).astype(BF16)


def _sample_attention(page_table, ck, cv, sc4, thr, sel, q3, kn, vn, sga3):
    n, npg = page_table.shape
    n_heads = q3.shape[1]
    kvw = ck.shape[2]
    per_b = lambda shape: pl.BlockSpec(shape, lambda b, p, pt: (b, 0, 0))
    return pl.pallas_call(
        functools.partial(_sattn_kernel, HEAD_DIM ** -0.5),
        grid_spec=pltpu.PrefetchScalarGridSpec(
            num_scalar_prefetch=1,
            grid=(n, npg),
            in_specs=[pl.BlockSpec((1, PAGE_SIZE, kvw), lambda b, p, pt: (pt[b * npg + p], 0, 0)),
                      pl.BlockSpec((1, PAGE_SIZE, kvw), lambda b, p, pt: (pt[b * npg + p], 0, 0)),
                      pl.BlockSpec((1, 1, 1, PAGE_SIZE), lambda b, p, pt: (b, p, 0, 0)),
                      per_b((1, 1, PAGE_SIZE)),
                      per_b((1, 1, PAGE_SIZE)),
                      per_b((1, n_heads, HEAD_DIM)),
                      per_b((1, n_heads, HEAD_DIM)),
                      per_b((1, n_heads, HEAD_DIM)),
                      per_b((1, n_heads, HEAD_DIM))],
            out_specs=per_b((1, n_heads, HEAD_DIM)),
            scratch_shapes=[pltpu.VMEM((n_heads, 1), F32),
                            pltpu.VMEM((n_heads, 1), F32),
                            pltpu.VMEM((n_heads, HEAD_DIM), F32)]),
        out_shape=jax.ShapeDtypeStruct((n, n_heads, HEAD_DIM), BF16),
        compiler_params=_cparams(("arbitrary", "arbitrary")),
    )(page_table.reshape(-1), ck, cv, sc4, thr, sel, q3, kn, vn, sga3)


def _rope_tables(pos):
    half = HEAD_DIM // 2
    inv = ROPE_THETA ** (-jnp.arange(half, dtype=F32) / half)
    ang = pos.astype(F32)[:, None] * inv[None, :]
    cos, sin = jnp.cos(ang), jnp.sin(ang)
    return jnp.concatenate([cos, cos], axis=1), jnp.concatenate([-sin, sin], axis=1)


def _pack_w_in(w_in, d, lay):
    att, kv, idx, pool = d, N_KV_HEADS * HEAD_DIM, IDX_HEADS * IDX_DIM, d // 2
    sizes = (att, kv, kv, idx, IDX_DIM, IDX_HEADS, att, pool, pool, d, d)
    offs = [0]
    for n in sizes:
        offs.append(offs[-1] + n)
    seg = lambda t: w_in[:, offs[t]:offs[t + 1]]
    q, k, v, qi, ki, wi, ga, u, gb, ma, mb = (seg(t) for t in range(len(sizes)))
    pad = jnp.zeros((d, TN - IDX_DIM - IDX_HEADS), w_in.dtype)
    w_p = jnp.concatenate([q, qi, ga, ma, mb, gb, k, v, ki, wi, pad, u], axis=1).astype(BF16)
    assert w_p.shape[1] == lay.end * TN
    return w_p


def kernel(x_prompt, x_sample, c_prompt, c_sample, cache_k, cache_v, cache_kidx, state_pool, page_table,
           g_norm, w_ada, b_ada, w_in, w_pool, s_pool, w_a_proj, w_b_proj, w_out, g_final):
    bp, sp, d = x_prompt.shape
    ns, ds, _ = x_sample.shape
    depth = g_norm.shape[0]
    npg = page_table.shape[1]
    past = npg * PAGE_SIZE
    assert ds == 1 and d % (N_KV_HEADS * HEAD_DIM) == 0 and sp % TQ == 0 and sp >= POOL_STATE
    n_heads = d // HEAD_DIM
    kvw = N_KV_HEADS * HEAD_DIM
    lay = _Layout(d)
    pw = lay.pool
    idx_scale = (IDX_HEADS * IDX_DIM) ** -0.5
    topk_p = min(TOPK_MAX, sp // 4)
    topk_s = min(TOPK_MAX, (past + ds) // 4)
    tm = 512
    ts = 512
    tm2 = 256

    cos_p, sin_p = _rope_tables(jnp.arange(sp, dtype=jnp.int32))
    cos_s, sin_s = _rope_tables(jnp.full((ns,), past, jnp.int32))
    n_c = bp + ns
    c_rows = -(-n_c // SUBLANES) * SUBLANES
    c_all = jnp.concatenate([c_prompt, c_sample, jnp.zeros((c_rows - n_c, d), F32)], axis=0)

    xp = x_prompt.reshape(bp * sp, d)
    xs = x_sample.reshape(ns, d)
    outs = {n: [] for n in ("kp", "vp", "kip", "pp", "ks", "vs", "kis", "ps")}
    for l in range(depth):
        final = l == depth - 1
        w_p = _pack_w_in(w_in[l], d, lay)
        wa, wb, wo, wpl = (w_a_proj[l].astype(BF16), w_b_proj[l].astype(BF16), w_out[l].astype(BF16),
                           w_pool[l].astype(BF16))
        g = g_norm[l].reshape(1, d)
        spl = s_pool[l].reshape(1, pw)
        gf = g_final.reshape(1, d)

        mod = _ada(c_all, w_ada[l], b_ada[l])
        shift, scale, gate = mod[:, :d], mod[:, d:2 * d], mod[:, 2 * d:]
        pg = lambda a: a[:bp].reshape(bp, 1, d)
        sg = lambda a: a[bp:n_c].reshape(1, ns, d)

        zb, k_p, v_p, misc_p, u_p = _project(xp, g, pg(shift), pg(scale), w_p, cos_p, sin_p, lay, tm,
                                             sp // tm, sp // tm, idx_scale)
        vt = zb[:, lay.v * TN:(lay.v + 1) * TN].reshape(bp, sp, kvw).transpose(0, 2, 1)
        wt = misc_p[:, IDX_DIM:IDX_DIM + IDX_HEADS].T
        ab = _prompt_attention(zb, vt, wt, lay, bp, sp, d, topk_p)
        pb = _pool_prompt(u_p, zb, wpl, spl, lay, bp, sp, ts)
        mm = _merge1(ab, pb, wa, wb, zb, lay, tm)
        xp = _merge2(mm, xp, pg(gate), wo, gf, tm2, sp // tm2, final)
        outs["kp"].append(k_p.reshape(bp, sp, N_KV_HEADS, HEAD_DIM))
        outs["vp"].append(v_p.reshape(bp, sp, N_KV_HEADS, HEAD_DIM))
        outs["kip"].append(misc_p[:, :IDX_DIM].reshape(bp, sp, IDX_DIM))
        outs["pp"].append(u_p.reshape(bp, sp, pw)[:, sp - POOL_STATE:, :])

        zs, k_s, v_s, misc_s, u_s = _project(xs, g, sg(shift), sg(scale), w_p, cos_s, sin_s, lay, ns,
                                             1, 1, idx_scale)
        col = lambda t, n: zs[:, t * TN:t * TN + n]
        q3 = col(lay.q, d).reshape(ns, n_heads, HEAD_DIM)
        qi3 = col(lay.qi, IDX_HEADS * IDX_DIM).reshape(ns, IDX_HEADS, IDX_DIM)
        sga3 = col(lay.ga, d).reshape(ns, n_heads, HEAD_DIM)
        wcol = misc_s[:, IDX_DIM:IDX_DIM + IDX_HEADS].reshape(ns, IDX_HEADS, 1)
        kin = misc_s[:, :IDX_DIM].reshape(ns, 1, IDX_DIM)
        sc4 = _sample_scores(page_table, cache_kidx[l], qi3, wcol)
        thr, sel = _sample_threshold(sc4.reshape(ns, npg, PAGE_SIZE), qi3, kin, wcol, topk_s)
        rep = lambda a: jnp.repeat(a.reshape(ns, N_KV_HEADS, HEAD_DIM), n_heads // N_KV_HEADS, axis=1)
        ab_s = _sample_attention(page_table, cache_k[l].reshape(-1, PAGE_SIZE, kvw),
                                 cache_v[l].reshape(-1, PAGE_SIZE, kvw), sc4, thr, sel, q3,
                                 rep(k_s), rep(v_s), sga3).reshape(ns, d)
        ext = jnp.concatenate([state_pool[l], u_s[:, None, :]], axis=1)
        pb_s = _pool_step(ext.transpose(1, 0, 2), col(lay.gb, pw), wpl, spl, past)
        mm_s = _merge1(ab_s, pb_s, wa, wb, zs, lay, ns)
        xs = _merge2(mm_s, xs, sg(gate), wo, gf, ns, 1, final)
        outs["ks"].append(k_s.reshape(ns, ds, N_KV_HEADS, HEAD_DIM))
        outs["vs"].append(v_s.reshape(ns, ds, N_KV_HEADS, HEAD_DIM))
        outs["kis"].append(misc_s[:, :IDX_DIM].reshape(ns, ds, IDX_DIM))
        outs["ps"].append(ext[:, 1:, :])

    st = lambda n: jnp.stack(outs[n])
    return (xp.reshape(bp, sp, d), xs.reshape(ns, ds, d),
            st("kp"), st("vp"), st("kip"), st("pp"), st("ks"), st("vs"), st("kis"), st("ps"))
```

```python
import functools

import jax
import jax.numpy as jnp
from jax import lax
from jax.experimental import pallas as pl
from jax.experimental.pallas import tpu as pltpu

F32 = jnp.float32
BF16 = jnp.bfloat16

HEAD_DIM = 128
N_KV_HEADS = 4
IDX_HEADS = 16
IDX_DIM = 128
TOPK_MAX = 256
PAGE_SIZE = 128
POOL_WINDOWS = (2, 4, 8, 16)
POOL_STATE = max(POOL_WINDOWS) - 1
ROPE_THETA = 10000.0
EPS = 1e-6

LANES = 128
SUBLANES = 8
TN = 512
TQ = 256
CK = 256
NEG = -1e30
Q_SCALE = 1.4426950408889634 * HEAD_DIM ** -0.5
PAGES_PER_STEP = 8
MAX_BISECT = 512
VMEM_LIMIT = 56 * 1024 * 1024


def _cparams(sem):
    return pltpu.CompilerParams(dimension_semantics=sem, vmem_limit_bytes=VMEM_LIMIT)


def _sigmoid(z):
    return 1.0 / (1.0 + jnp.exp(-z))


def _ada_kernel(c_ref, w_ref, b_ref, o_ref):
    o_ref[...] = jnp.dot(c_ref[...].astype(BF16), w_ref[...].astype(BF16),
                         preferred_element_type=F32) + b_ref[...]


def _ada(c, w, b):
    r, d = c.shape
    n = w.shape[1]
    return pl.pallas_call(
        _ada_kernel,
        grid=(n // TN,),
        in_specs=[pl.BlockSpec((r, d), lambda j: (0, 0)),
                  pl.BlockSpec((d, TN), lambda j: (0, j)),
                  pl.BlockSpec((1, TN), lambda j: (0, j))],
        out_specs=pl.BlockSpec((r, TN), lambda j: (0, j)),
        out_shape=jax.ShapeDtypeStruct((r, n), F32),
        compiler_params=_cparams(("arbitrary",)),
    )(c, w, b.reshape(1, n))


class _Layout:
    def __init__(self, d):
        att, kv, idx, pool = d, N_KV_HEADS * HEAD_DIM, IDX_HEADS * IDX_DIM, d // 2
        t = lambda n: n // TN
        assert att % TN == 0 and kv == TN and idx % TN == 0 and pool % TN == 0
        self.q = 0
        self.qi = self.q + t(att)
        self.ga = self.qi + t(idx)
        self.ma = self.ga + t(att)
        self.mb = self.ma + t(d)
        self.gb = self.mb + t(d)
        self.k = self.gb + t(pool)
        self.v = self.k + 1
        self.misc = self.v + 1
        self.u = self.misc + 1
        self.end = self.u + t(pool)
        self.nb = self.u
        self.pool = pool


def _rope_tile(z, cos, sin):
    parts = []
    for s in range(z.shape[1] // HEAD_DIM):
        zs = z[:, s * HEAD_DIM:(s + 1) * HEAD_DIM]
        parts.append(zs * cos + pltpu.roll(zs, HEAD_DIM // 2, 1) * sin)
    return parts[0] if len(parts) == 1 else jnp.concatenate(parts, axis=1)


def _proj_kernel(lay, idx_scale, q_scale, x_ref, g_ref, shift_ref, scale_ref, w_ref, cos_ref, sin_ref,
                 zb_ref, k_ref, v_ref, misc_ref, u_ref, h_ref):
    j = pl.program_id(1)

    @pl.when(j == 0)
    def _():
        x = x_ref[...]
        r = lax.rsqrt(jnp.mean(x * x, axis=-1, keepdims=True) + EPS)
        h = (x * r * g_ref[...]) * (1.0 + scale_ref[0]) + shift_ref[0]
        h_ref[...] = h.astype(BF16)

    z = jnp.dot(h_ref[...], w_ref[...], preferred_element_type=F32)

    @pl.when((j < lay.ga) | (j == lay.k))
    def _():
        r = _rope_tile(z, cos_ref[...], sin_ref[...])
        zb_ref[...] = (r * jnp.where(j < lay.qi, q_scale, 1.0)).astype(BF16)

        @pl.when(j == lay.k)
        def _():
            k_ref[...] = r

    @pl.when(((j >= lay.ga) & (j < lay.ma)) | ((j >= lay.gb) & (j < lay.k)))
    def _():
        zb_ref[...] = (z * _sigmoid(z)).astype(BF16)

    @pl.when((j >= lay.ma) & (j < lay.gb))
    def _():
        zb_ref[...] = _sigmoid(z).astype(BF16)

    @pl.when(j == lay.v)
    def _():
        zb_ref[...] = z.astype(BF16)
        v_ref[...] = z

    @pl.when(j == lay.misc)
    def _():
        ki = _rope_tile(z[:, :IDX_DIM], cos_ref[...], sin_ref[...])
        m = jnp.concatenate([ki, z[:, IDX_DIM:] * idx_scale], axis=1)
        zb_ref[...] = m.astype(BF16)
        misc_ref[...] = m

    @pl.when(j >= lay.u)
    def _():
        u_ref[...] = z


def _project(x, g, shift, scale, w_p, cos, sin, lay, tm, tiles_per_group, tiles_per_pos, idx_scale):
    m, d = x.shape
    grp, r, _ = shift.shape
    nu = lay.end - lay.u
    kern = functools.partial(_proj_kernel, lay, idx_scale, Q_SCALE)
    return pl.pallas_call(
        kern,
        grid=(m // tm, lay.end),
        in_specs=[pl.BlockSpec((tm, d), lambda i, j: (i, 0)),
                  pl.BlockSpec((1, d), lambda i, j: (0, 0)),
                  pl.BlockSpec((1, r, d), lambda i, j: (i // tiles_per_group, 0, 0)),
                  pl.BlockSpec((1, r, d), lambda i, j: (i // tiles_per_group, 0, 0)),
                  pl.BlockSpec((d, TN), lambda i, j: (0, j)),
                  pl.BlockSpec((tm, HEAD_DIM), lambda i, j: (i % tiles_per_pos, 0)),
                  pl.BlockSpec((tm, HEAD_DIM), lambda i, j: (i % tiles_per_pos, 0))],
        out_specs=[pl.BlockSpec((tm, TN), lambda i, j: (i, jnp.minimum(j, lay.nb - 1))),
                   pl.BlockSpec((tm, TN), lambda i, j: (i, 0)),
                   pl.BlockSpec((tm, TN), lambda i, j: (i, 0)),
                   pl.BlockSpec((tm, TN), lambda i, j: (i, 0)),
                   pl.BlockSpec((tm, TN), lambda i, j: (i, jnp.clip(j - lay.u, 0, nu - 1)))],
        out_shape=[jax.ShapeDtypeStruct((m, lay.nb * TN), BF16),
                   jax.ShapeDtypeStruct((m, TN), F32),
                   jax.ShapeDtypeStruct((m, TN), F32),
                   jax.ShapeDtypeStruct((m, TN), F32),
                   jax.ShapeDtypeStruct((m, nu * TN), F32)],
        scratch_shapes=[pltpu.VMEM((tm, d), BF16)],
        compiler_params=_cparams(("arbitrary", "arbitrary")),
    )(x, g, shift, scale, w_p, cos, sin)


def _bisect(count_ge, lo, hi, clo, kf):
    def cond(st):
        return (st[4] > 0.0) & (st[5] < MAX_BISECT)

    def body(st):
        lo, hi, clo, done, _, it = st
        mid = 0.5 * lo + 0.5 * hi
        mid = jnp.where(mid <= lo, hi, mid)
        c = count_ge(mid)
        ge = c >= kf
        stuck = (mid >= hi) & jnp.logical_not(ge)
        lo2 = jnp.where(ge, mid, lo)
        hi2 = jnp.where(ge, hi, mid)
        clo2 = jnp.where(ge, c, clo)
        fin = (clo2 <= kf) | (lo2 >= hi2) | stuck
        done2 = jnp.maximum(done, jnp.where(fin, 1.0, 0.0))
        return lo2, hi2, clo2, done2, jnp.sum(1.0 - done2), it + 1

    done0 = jnp.where((clo <= kf) | (lo >= hi), 1.0, 0.0)
    st = (lo, hi, clo, done0, jnp.sum(1.0 - done0), jnp.int32(0))
    return lax.while_loop(cond, body, st)[0]


def _attn_kernel(topk, q_ref, qi_ref, sga_ref, k_ref, ki_ref, vt_ref, wt_ref, o_ref,
                 score_ref, bias_ref, m_ref, l_ref, alpha_ref, acc_ref, lg_ref, p_ref):
    i = pl.program_id(1)
    nck = i + 1
    n_heads = q_ref.shape[1] // HEAD_DIM
    group = n_heads // N_KV_HEADS
    nt = (((1,), (1,)), ((), ()))
    q_pos = i * TQ + lax.broadcasted_iota(jnp.int32, (1, TQ), 1)
    fold = lambda a: a.reshape(CK // SUBLANES, SUBLANES, TQ)

    def score_chunk(c, carry):
        mn, mx = carry
        off = pl.multiple_of(c * CK, CK)
        kc = ki_ref[pl.ds(off, CK), :]
        acc = jnp.zeros((CK, TQ), F32)
        for h in range(IDX_HEADS):
            s = lax.dot_general(kc, qi_ref[:, h * IDX_DIM:(h + 1) * IDX_DIM], nt,
                                preferred_element_type=F32)
            acc = acc + wt_ref[h:h + 1, :] * jnp.maximum(s, 0.0)
        k_pos = off + lax.broadcasted_iota(jnp.int32, (CK, 1), 0)
        causal = k_pos <= q_pos
        score_ref[pl.ds(off, CK), :] = jnp.where(causal, acc, -jnp.inf)
        mn = jnp.minimum(mn, fold(jnp.where(causal, acc, jnp.inf)).min(axis=0))
        mx = jnp.maximum(mx, fold(jnp.where(causal, acc, -jnp.inf)).max(axis=0))
        return mn, mx

    mn, mx = lax.fori_loop(0, nck, score_chunk,
                           (jnp.full((SUBLANES, TQ), jnp.inf, F32),
                            jnp.full((SUBLANES, TQ), -jnp.inf, F32)))
    lo = mn.min(axis=0, keepdims=True)
    hi = mx.max(axis=0, keepdims=True)

    def count_ge(thr):
        def body(c, cnt):
            s = score_ref[pl.ds(pl.multiple_of(c * CK, CK), CK), :]
            return cnt + fold(jnp.where(s >= thr, 1.0, 0.0)).sum(axis=0)
        cnt = lax.fori_loop(0, nck, body, jnp.zeros((SUBLANES, TQ), F32))
        return cnt.sum(axis=0, keepdims=True)

    n_valid = (q_pos + 1).astype(F32)
    thr = _bisect(count_ge, lo, hi, n_valid, float(topk))

    m_ref[...] = jnp.full(m_ref.shape, NEG, F32)
    l_ref[...] = jnp.zeros(l_ref.shape, F32)
    acc_ref[...] = jnp.zeros(acc_ref.shape, F32)

    def att_chunk(c, carry):
        off = pl.multiple_of(c * CK, CK)
        bias_ref[...] = jnp.where(score_ref[pl.ds(off, CK), :] >= thr, 0.0, NEG)
        kc = k_ref[pl.ds(off, CK), :]
        vtc = vt_ref[0, :, pl.ds(off, CK)]

        def qk(h):
            g = h // group
            return lax.dot_general(kc[:, g * HEAD_DIM:(g + 1) * HEAD_DIM],
                                   q_ref[:, h * HEAD_DIM:(h + 1) * HEAD_DIM], nt,
                                   preferred_element_type=F32)

        for h in range(n_heads):
            lg = qk(h) + bias_ref[...]
            m_old = m_ref[h:h + 1, :]
            m_new = jnp.maximum(m_old, lg.max(axis=0, keepdims=True))
            lg_ref[h] = lg
            alpha_ref[h:h + 1, :] = jnp.exp2(m_old - m_new)
            m_ref[h:h + 1, :] = m_new
        for h in range(n_heads):
            p = jnp.exp2(lg_ref[h] - m_ref[h:h + 1, :])
            l_ref[h:h + 1, :] = alpha_ref[h:h + 1, :] * l_ref[h:h + 1, :] + p.sum(axis=0, keepdims=True)
            p_ref[h] = p.astype(BF16)
        for h in range(n_heads):
            g = h // group
            pv = jnp.dot(vtc[g * HEAD_DIM:(g + 1) * HEAD_DIM, :], p_ref[h], preferred_element_type=F32)
            acc_ref[h] = alpha_ref[h:h + 1, :] * acc_ref[h] + pv
        return carry

    lax.fori_loop(0, nck, att_chunk, 0)

    for h in range(n_heads):
        o = (acc_ref[h] / l_ref[h:h + 1, :]).T
        sl = slice(h * HEAD_DIM, (h + 1) * HEAD_DIM)
        o_ref[:, sl] = (o * sga_ref[:, sl].astype(F32)).astype(BF16)


def _prompt_attention(zb, vt, wt, lay, b, s, d, topk):
    nq = s // TQ
    n_heads = d // HEAD_DIM
    kern = functools.partial(_attn_kernel, topk)
    row = lambda bb, i: bb * nq + i
    return pl.pallas_call(
        kern,
        grid=(b, nq),
        in_specs=[pl.BlockSpec((TQ, d), lambda bb, i: (row(bb, i), lay.q * TN // d)),
                  pl.BlockSpec((TQ, d), lambda bb, i: (row(bb, i), lay.qi * TN // d)),
                  pl.BlockSpec((TQ, d), lambda bb, i: (row(bb, i), lay.ga * TN // d)),
                  pl.BlockSpec((s, TN), lambda bb, i: (bb, lay.k)),
                  pl.BlockSpec((s, IDX_DIM), lambda bb, i: (bb, lay.misc * TN // IDX_DIM)),
                  pl.BlockSpec((1, TN, s), lambda bb, i: (bb, 0, 0)),
                  pl.BlockSpec((IDX_HEADS, TQ), lambda bb, i: (0, row(bb, i)))],
        out_specs=pl.BlockSpec((TQ, d), lambda bb, i: (row(bb, i), 0)),
        out_shape=jax.ShapeDtypeStruct((b * s, d), BF16),
        scratch_shapes=[pltpu.VMEM((s, TQ), F32),
                        pltpu.VMEM((CK, TQ), F32),
                        pltpu.VMEM((n_heads, TQ), F32),
                        pltpu.VMEM((n_heads, TQ), F32),
                        pltpu.VMEM((n_heads, TQ), F32),
                        pltpu.VMEM((n_heads, HEAD_DIM, TQ), F32),
                        pltpu.VMEM((n_heads, CK, TQ), F32),
                        pltpu.VMEM((n_heads, CK, TQ), BF16)],
        compiler_params=_cparams(("arbitrary", "arbitrary")),
    )(zb, zb, zb, zb, zb, vt, wt)


def _pool_kernel(ts, u_ref, sgb_ref, wp_ref, sp_ref, o_ref, ext_ref):
    i = pl.program_id(1)
    halo = POOL_STATE + 1

    @pl.when(i == 0)
    def _():
        ext_ref[0:halo, :] = jnp.zeros((halo, ext_ref.shape[1]), F32)

    @pl.when(i > 0)
    def _():
        ext_ref[0:halo, :] = ext_ref[ts:ts + halo, :]

    ext_ref[halo:halo + ts, :] = u_ref[...]
    pos = i * ts + lax.broadcasted_iota(jnp.int32, (ts, 1), 0)
    gd = wp_ref.shape[1]
    for g, w in enumerate(POOL_WINDOWS):
        cols = slice(g * gd, (g + 1) * gd)
        acc = ext_ref[halo:halo + ts, cols]
        for jj in range(1, w):
            acc = acc + ext_ref[halo - jj:halo - jj + ts, cols]
        cnt = jnp.minimum(pos + 1, w).astype(F32)
        pooled = acc / cnt - u_ref[:, cols]
        mixed = jnp.dot(pooled.astype(BF16), wp_ref[g], preferred_element_type=F32) * sp_ref[:, cols]
        o_ref[:, cols] = (mixed * sgb_ref[:, cols].astype(F32)).astype(BF16)


def _pool_prompt(u, zb, wp, sp, lay, b, s, ts):
    m, pw = u.shape
    nt = s // ts
    kern = functools.partial(_pool_kernel, ts)
    return pl.pallas_call(
        kern,
        grid=(b, nt),
        in_specs=[pl.BlockSpec((ts, pw), lambda bb, i: (bb * nt + i, 0)),
                  pl.BlockSpec((ts, pw), lambda bb, i: (bb * nt + i, lay.gb * TN // pw)),
                  pl.BlockSpec(wp.shape, lambda bb, i: (0, 0, 0)),
                  pl.BlockSpec((1, pw), lambda bb, i: (0, 0))],
        out_specs=pl.BlockSpec((ts, pw), lambda bb, i: (bb * nt + i, 0)),
        out_shape=jax.ShapeDtypeStruct((m, pw), BF16),
        scratch_shapes=[pltpu.VMEM((POOL_STATE + 1 + ts, pw), F32)],
        compiler_params=_cparams(("arbitrary", "arbitrary")),
    )(u, zb, wp, sp)


def _pool_step_kernel(cnts, ext_ref, sgb_ref, wp_ref, sp_ref, o_ref):
    gd = wp_ref.shape[1]
    rows = ext_ref.shape[0]
    for g, w in enumerate(POOL_WINDOWS):
        cols = slice(g * gd, (g + 1) * gd)
        acc = ext_ref[rows - 1, :, cols]
        for jj in range(1, w):
            acc = acc + ext_ref[rows - 1 - jj, :, cols]
        pooled = acc / cnts[g] - ext_ref[rows - 1, :, cols]
        mixed = jnp.dot(pooled.astype(BF16), wp_ref[g], preferred_element_type=F32) * sp_ref[:, cols]
        o_ref[:, cols] = (mixed * sgb_ref[:, cols].astype(F32)).astype(BF16)


def _pool_step(ext_t, sgb, wp, sp, past):
    rows, n, pw = ext_t.shape
    cnts = tuple(float(min(past + 1, w)) for w in POOL_WINDOWS)
    return pl.pallas_call(
        functools.partial(_pool_step_kernel, cnts),
        out_shape=jax.ShapeDtypeStruct((n, pw), BF16),
    )(ext_t, sgb, wp, sp)


def _merge1_kernel(ab_ref, pb_ref, wa_ref, wb_ref, sma_ref, smb_ref, o_ref):
    a = jnp.dot(ab_ref[...], wa_ref[...], preferred_element_type=F32)
    bb = jnp.dot(pb_ref[...], wb_ref[...], preferred_element_type=F32)
    o_ref[...] = (sma_ref[...].astype(F32) * a + smb_ref[...].astype(F32) * bb).astype(BF16)


def _merge1(ab, pb, wa, wb, zb, lay, tm):
    m, d = ab.shape
    pw = pb.shape[1]
    return pl.pallas_call(
        _merge1_kernel,
        grid=(m // tm, d // TN),
        in_specs=[pl.BlockSpec((tm, d), lambda i, j: (i, 0)),
                  pl.BlockSpec((tm, pw), lambda i, j: (i, 0)),
                  pl.BlockSpec((d, TN), lambda i, j: (0, j)),
                  pl.BlockSpec((pw, TN), lambda i, j: (0, j)),
                  pl.BlockSpec((tm, TN), lambda i, j: (i, lay.ma + j)),
                  pl.BlockSpec((tm, TN), lambda i, j: (i, lay.mb + j))],
        out_specs=pl.BlockSpec((tm, TN), lambda i, j: (i, j)),
        out_shape=jax.ShapeDtypeStruct((m, d), BF16),
        compiler_params=_cparams(("arbitrary", "arbitrary")),
    )(ab, pb, wa, wb, zb, zb)


def _merge2_kernel(final, mm_ref, x_ref, gate_ref, wo_ref, gf_ref, o_ref):
    y = x_ref[...] + gate_ref[0] * jnp.dot(mm_ref[...], wo_ref[...], preferred_element_type=F32)
    if final:
        y = y * lax.rsqrt(jnp.mean(y * y, axis=-1, keepdims=True) + EPS) * gf_ref[...]
    o_ref[...] = y


def _merge2(mm, x, gate, wo, gf, tm, tiles_per_group, final):
    m, d = x.shape
    r = gate.shape[1]
    return pl.pallas_call(
        functools.partial(_merge2_kernel, final),
        grid=(m // tm,),
        in_specs=[pl.BlockSpec((tm, d), lambda i: (i, 0)),
                  pl.BlockSpec((tm, d), lambda i: (i, 0)),
                  pl.BlockSpec((1, r, d), lambda i: (i // tiles_per_group, 0, 0)),
                  pl.BlockSpec((d, d), lambda i: (0, 0)),
                  pl.BlockSpec((1, d), lambda i: (0, 0))],
        out_specs=pl.BlockSpec((tm, d), lambda i: (i, 0)),
        out_shape=jax.ShapeDtypeStruct((m, d), F32),
        compiler_params=_cparams(("arbitrary",)),
    )(mm, x, gate, wo, gf)


def _page_specs(width, npg, npp):
    return [pl.BlockSpec((1, PAGE_SIZE, width),
                         lambda b, p, pt, r=r: (pt[b * npg + p * npp + r], 0, 0)) for r in range(npp)]


def _sidx_kernel(npp, pt_ref, *refs):
    kidx_refs = refs[:npp]
    qi_ref, w_ref, o_ref = refs[npp:]
    nt = (((1,), (1,)), ((), ()))
    qi, w = qi_ref[0], w_ref[0]
    rows = []
    for r in range(npp):
        s = lax.dot_general(qi, kidx_refs[r][0].astype(BF16), nt, preferred_element_type=F32)
        rows.append(jnp.sum(w * jnp.maximum(s, 0.0), axis=0, keepdims=True))
    o_ref[0] = jnp.concatenate(rows, axis=1)


def _sample_scores(page_table, cache_kidx, qi3, wcol, npp):
    n, npg = page_table.shape
    return pl.pallas_call(
        functools.partial(_sidx_kernel, npp),
        grid_spec=pltpu.PrefetchScalarGridSpec(
            num_scalar_prefetch=1,
            grid=(n, npg // npp),
            in_specs=_page_specs(IDX_DIM, npg, npp) + [
                pl.BlockSpec((1, IDX_HEADS, IDX_DIM), lambda b, p, pt: (b, 0, 0)),
                pl.BlockSpec((1, IDX_HEADS, 1), lambda b, p, pt: (b, 0, 0))],
            out_specs=pl.BlockSpec((1, 1, npp * PAGE_SIZE), lambda b, p, pt: (b, 0, p))),
        out_shape=jax.ShapeDtypeStruct((n, 1, npg * PAGE_SIZE), F32),
        compiler_params=_cparams(("arbitrary", "arbitrary")),
    )(page_table.reshape(-1), *([cache_kidx] * npp), qi3, wcol)


def _sthr_kernel(topk, sc_ref, qi_ref, kin_ref, w_ref, thr_ref, sel_ref):
    sc = sc_ref[...]
    kin = kin_ref[...].astype(BF16).astype(F32)
    s_new = jnp.sum(qi_ref[...].astype(F32) * kin, axis=2, keepdims=True)
    new = jnp.sum(w_ref[...] * jnp.maximum(s_new, 0.0), axis=1, keepdims=True)
    red = lambda f, a: f(f(a, axis=2, keepdims=True), axis=1, keepdims=True)
    lo = jnp.minimum(red(jnp.min, sc), new)
    hi = jnp.maximum(red(jnp.max, sc), new)

    def count_ge(thr):
        c = red(jnp.sum, jnp.where(sc >= thr, 1.0, 0.0))
        return c + jnp.where(new >= thr, 1.0, 0.0)

    n_all = jnp.full(new.shape, float(sc.shape[1] * sc.shape[2] + 1), F32)
    thr = _bisect(count_ge, lo, hi, n_all, float(topk))
    thr_ref[...] = jnp.broadcast_to(thr, thr_ref.shape)
    sel_ref[...] = jnp.broadcast_to(jnp.where(new >= thr, 1.0, 0.0), sel_ref.shape)


def _sample_threshold(sc, qi3, kin, wcol, topk):
    n = sc.shape[0]
    return pl.pallas_call(
        functools.partial(_sthr_kernel, topk),
        out_shape=[jax.ShapeDtypeStruct((n, 1, PAGE_SIZE), F32),
                   jax.ShapeDtypeStruct((n, 1, PAGE_SIZE), F32)],
        compiler_params=pltpu.CompilerParams(vmem_limit_bytes=VMEM_LIMIT),
    )(sc, qi3, kin, wcol)


def _sattn_kernel(npp, pt_ref, *refs):
    k_refs, v_refs = refs[:npp], refs[npp:2 * npp]
    (sc_ref, thr_ref, sel_ref, qbd_ref, q_ref, kn_ref, vn_ref, sga_ref,
     o_ref, m_ref, l_ref, acc_ref) = refs[2 * npp:]
    p_idx = pl.program_id(1)
    n_heads = q_ref.shape[1]
    group = n_heads // N_KV_HEADS
    nt = (((1,), (1,)), ((), ()))

    @pl.when(p_idx == 0)
    def _():
        m_ref[...] = jnp.full(m_ref.shape, NEG, F32)
        l_ref[...] = jnp.zeros(l_ref.shape, F32)
        acc_ref[...] = jnp.zeros(acc_ref.shape, F32)

    qbd = qbd_ref[0]
    lg = jnp.concatenate(
        [lax.dot_general(qbd, k_refs[r][0].astype(BF16), nt, preferred_element_type=F32)
         for r in range(npp)], axis=1)
    lg = lg + jnp.where(sc_ref[0] >= thr_ref[0][:, 0:1], 0.0, NEG)
    m_old = m_ref[...]
    m_new = jnp.maximum(m_old, lg.max(axis=1, keepdims=True))
    alpha = jnp.exp2(m_old - m_new)
    p = jnp.exp2(lg - m_new)
    pb = p.astype(BF16)
    pv = jnp.zeros(acc_ref.shape, F32)
    for r in range(npp):
        pv = pv + jnp.dot(pb[:, r * PAGE_SIZE:(r + 1) * PAGE_SIZE], v_refs[r][0].astype(BF16),
                          preferred_element_type=F32)
    l_new = alpha * l_ref[...] + p.sum(axis=1, keepdims=True)
    acc_new = alpha * acc_ref[...] + pv
    m_ref[...] = m_new
    l_ref[...] = l_new
    acc_ref[...] = acc_new

    @pl.when(p_idx == pl.num_programs(1) - 1)
    def _():
        head = lax.broadcasted_iota(jnp.int32, (n_heads, 1), 0)
        own = jnp.zeros((n_heads, HEAD_DIM), F32)
        for g in range(N_KV_HEADS):
            in_g = (head >= g * group) & (head < (g + 1) * group)
            own = own + jnp.where(in_g, acc_new[:, g * HEAD_DIM:(g + 1) * HEAD_DIM], 0.0)
        kn = kn_ref[0].astype(BF16).astype(F32)
        vn = vn_ref[0].astype(BF16).astype(F32)
        lgn = jnp.sum(q_ref[0].astype(F32) * kn, axis=1, keepdims=True)
        lgn = jnp.where(sel_ref[0][:, 0:1] > 0.0, lgn, NEG)
        m_fin = jnp.maximum(m_new, lgn)
        a2 = jnp.exp2(m_new - m_fin)
        pn = jnp.exp2(lgn - m_fin)
        l_fin = a2 * l_new + pn
        acc_fin = a2 * own + pn.astype(BF16).astype(F32) * vn
        o_ref[0] = ((acc_fin / l_fin) * sga_ref[0].astype(F32)).astype(BF16)


def _sample_attention(page_table, ck, cv, sc, thr, sel, qbd, q3, kn, vn, sga3, npp):
    n, npg = page_table.shape
    n_heads = q3.shape[1]
    kvw = ck.shape[2]
    per_b = lambda shape: pl.BlockSpec(shape, lambda b, p, pt: (b, 0, 0))
    return pl.pallas_call(
        functools.partial(_sattn_kernel, npp),
        grid_spec=pltpu.PrefetchScalarGridSpec(
            num_scalar_prefetch=1,
            grid=(n, npg // npp),
            in_specs=_page_specs(kvw, npg, npp) + _page_specs(kvw, npg, npp) + [
                pl.BlockSpec((1, 1, npp * PAGE_SIZE), lambda b, p, pt: (b, 0, p)),
                per_b((1, 1, PAGE_SIZE)),
                per_b((1, 1, PAGE_SIZE)),
                per_b((1, n_heads, kvw)),
                per_b((1, n_heads, HEAD_DIM)),
                per_b((1, n_heads, HEAD_DIM)),
                per_b((1, n_heads, HEAD_DIM)),
                per_b((1, n_heads, HEAD_DIM))],
            out_specs=per_b((1, n_heads, HEAD_DIM)),
            scratch_shapes=[pltpu.VMEM((n_heads, 1), F32),
                            pltpu.VMEM((n_heads, 1), F32),
                            pltpu.VMEM((n_heads, kvw), F32)]),
        out_shape=jax.ShapeDtypeStruct((n, n_heads, HEAD_DIM), BF16),
        compiler_params=_cparams(("arbitrary", "arbitrary")),
    )(page_table.reshape(-1), *([ck] * npp), *([cv] * npp), sc, thr, sel, qbd, q3, kn, vn, sga3)


def _rope_tables(pos):
    half = HEAD_DIM // 2
    inv = ROPE_THETA ** (-jnp.arange(half, dtype=F32) / half)
    ang = pos.astype(F32)[:, None] * inv[None, :]
    cos, sin = jnp.cos(ang), jnp.sin(ang)
    return jnp.concatenate([cos, cos], axis=1), jnp.concatenate([-sin, sin], axis=1)


def _pack_w_in(w_in, d, lay):
    att, kv, idx, pool = d, N_KV_HEADS * HEAD_DIM, IDX_HEADS * IDX_DIM, d // 2
    sizes = (att, kv, kv, idx, IDX_DIM, IDX_HEADS, att, pool, pool, d, d)
    offs = [0]
    for n in sizes:
        offs.append(offs[-1] + n)
    seg = lambda t: w_in[:, offs[t]:offs[t + 1]]
    q, k, v, qi, ki, wi, ga, u, gb, ma, mb = (seg(t) for t in range(len(sizes)))
    pad = jnp.zeros((d, TN - IDX_DIM - IDX_HEADS), w_in.dtype)
    w_p = jnp.concatenate([q, qi, ga, ma, mb, gb, k, v, ki, wi, pad, u], axis=1).astype(BF16)
    assert w_p.shape[1] == lay.end * TN
    return w_p


def kernel(x_prompt, x_sample, c_prompt, c_sample, cache_k, cache_v, cache_kidx, state_pool, page_table,
           g_norm, w_ada, b_ada, w_in, w_pool, s_pool, w_a_proj, w_b_proj, w_out, g_final):
    bp, sp, d = x_prompt.shape
    ns, ds, _ = x_sample.shape
    depth = g_norm.shape[0]
    npg = page_table.shape[1]
    past = npg * PAGE_SIZE
    assert ds == 1 and d % (N_KV_HEADS * HEAD_DIM) == 0 and sp % TQ == 0 and sp >= POOL_STATE
    n_heads = d // HEAD_DIM
    kvw = N_KV_HEADS * HEAD_DIM
    lay = _Layout(d)
    pw = lay.pool
    idx_scale = (IDX_HEADS * IDX_DIM) ** -0.5
    topk_p = min(TOPK_MAX, sp // 4)
    topk_s = min(TOPK_MAX, (past + ds) // 4)
    tm = 512
    ts = 512
    tm2 = 256

    cos_p, sin_p = _rope_tables(jnp.arange(sp, dtype=jnp.int32))
    cos_s, sin_s = _rope_tables(jnp.full((ns,), past, jnp.int32))
    n_c = bp + ns
    c_rows = -(-n_c // SUBLANES) * SUBLANES
    c_all = jnp.concatenate([c_prompt, c_sample, jnp.zeros((c_rows - n_c, d), F32)], axis=0)

    n_pool = cache_k.shape[1]
    k_pages = cache_k.reshape(depth * n_pool, PAGE_SIZE, kvw)
    v_pages = cache_v.reshape(depth * n_pool, PAGE_SIZE, kvw)
    kidx_pages = cache_kidx.reshape(depth * n_pool, PAGE_SIZE, IDX_DIM)
    npp = PAGES_PER_STEP
    assert npg % npp == 0
    head_group = jnp.arange(n_heads, dtype=jnp.int32) // (n_heads // N_KV_HEADS)
    in_group = (head_group[:, None] == jnp.arange(N_KV_HEADS, dtype=jnp.int32)[None, :]).astype(BF16)

    xp = x_prompt.reshape(bp * sp, d)
    xs = x_sample.reshape(ns, d)
    outs = {n: [] for n in ("kp", "vp", "kip", "pp", "ks", "vs", "kis", "ps")}
    for l in range(depth):
        final = l == depth - 1
        w_p = _pack_w_in(w_in[l], d, lay)
        wa, wb, wo, wpl = (w_a_proj[l].astype(BF16), w_b_proj[l].astype(BF16), w_out[l].astype(BF16),
                           w_pool[l].astype(BF16))
        g = g_norm[l].reshape(1, d)
        spl = s_pool[l].reshape(1, pw)
        gf = g_final.reshape(1, d)

        mod = _ada(c_all, w_ada[l], b_ada[l])
        shift, scale, gate = mod[:, :d], mod[:, d:2 * d], mod[:, 2 * d:]
        pg = lambda a: a[:bp].reshape(bp, 1, d)
        sg = lambda a: a[bp:n_c].reshape(1, ns, d)

        zb, k_p, v_p, misc_p, u_p = _project(xp, g, pg(shift), pg(scale), w_p, cos_p, sin_p, lay, tm,
                                             sp // tm, sp // tm, idx_scale)
        vt = zb[:, lay.v * TN:(lay.v + 1) * TN].reshape(bp, sp, kvw).transpose(0, 2, 1)
        wt = misc_p[:, IDX_DIM:IDX_DIM + IDX_HEADS].T
        ab = _prompt_attention(zb, vt, wt, lay, bp, sp, d, topk_p)
        pb = _pool_prompt(u_p, zb, wpl, spl, lay, bp, sp, ts)
        mm = _merge1(ab, pb, wa, wb, zb, lay, tm)
        xp = _merge2(mm, xp, pg(gate), wo, gf, tm2, sp // tm2, final)
        outs["kp"].append(k_p.reshape(bp, sp, N_KV_HEADS, HEAD_DIM))
        outs["vp"].append(v_p.reshape(bp, sp, N_KV_HEADS, HEAD_DIM))
        outs["kip"].append(misc_p[:, :IDX_DIM].reshape(bp, sp, IDX_DIM))
        outs["pp"].append(u_p.reshape(bp, sp, pw)[:, sp - POOL_STATE:, :])

        zs, k_s, v_s, misc_s, u_s = _project(xs, g, sg(shift), sg(scale), w_p, cos_s, sin_s, lay, ns,
                                             1, 1, idx_scale)
        col = lambda t, n: zs[:, t * TN:t * TN + n]
        q3 = col(lay.q, d).reshape(ns, n_heads, HEAD_DIM)
        qi3 = col(lay.qi, IDX_HEADS * IDX_DIM).reshape(ns, IDX_HEADS, IDX_DIM)
        sga3 = col(lay.ga, d).reshape(ns, n_heads, HEAD_DIM)
        wcol = misc_s[:, IDX_DIM:IDX_DIM + IDX_HEADS].reshape(ns, IDX_HEADS, 1)
        kin = misc_s[:, :IDX_DIM].reshape(ns, 1, IDX_DIM)
        pt_l = page_table + l * n_pool
        sc = _sample_scores(pt_l, kidx_pages, qi3, wcol, npp)
        thr, sel = _sample_threshold(sc.reshape(ns, npg, PAGE_SIZE), qi3, kin, wcol, topk_s)
        rep = lambda a: jnp.repeat(a.reshape(ns, N_KV_HEADS, HEAD_DIM), n_heads // N_KV_HEADS, axis=1)
        qbd = (q3[:, :, None, :] * in_group[None, :, :, None]).reshape(ns, n_heads, kvw)
        ab_s = _sample_attention(pt_l, k_pages, v_pages, sc, thr, sel, qbd, q3,
                                 rep(k_s), rep(v_s), sga3, npp).reshape(ns, d)
        ext = jnp.concatenate([state_pool[l], u_s[:, None, :]], axis=1)
        pb_s = _pool_step(ext.transpose(1, 0, 2), col(lay.gb, pw), wpl, spl, past)
        mm_s = _merge1(ab_s, pb_s, wa, wb, zs, lay, ns)
        xs = _merge2(mm_s, xs, sg(gate), wo, gf, ns, 1, final)
        outs["ks"].append(k_s.reshape(ns, ds, N_KV_HEADS, HEAD_DIM))
        outs["vs"].append(v_s.reshape(ns, ds, N_KV_HEADS, HEAD_DIM))
        outs["kis"].append(misc_s[:, :IDX_DIM].reshape(ns, ds, IDX_DIM))
        outs["ps"].append(ext[:, 1:, :])

    st = lambda n: outs[n][0][None] if depth == 1 else jnp.stack(outs[n])
    return (xp.reshape(bp, sp, d), xs.reshape(ns, ds, d),
            st("kp"), st("vp"), st("kip"), st("pp"), st("ks"), st("vs"), st("kis"), st("ps"))
```

```python
import functools

import jax
import jax.numpy as jnp
from jax import lax
from jax.experimental import pallas as pl
from jax.experimental.pallas import tpu as pltpu

F32 = jnp.float32
BF16 = jnp.bfloat16

HEAD_DIM = 128
N_KV_HEADS = 4
IDX_HEADS = 16
IDX_DIM = 128
TOPK_MAX = 256
PAGE_SIZE = 128
POOL_WINDOWS = (2, 4, 8, 16)
POOL_STATE = max(POOL_WINDOWS) - 1
ROPE_THETA = 10000.0
EPS = 1e-6

LANES = 128
SUBLANES = 8
TN = 512
TQ = 256
CK = 256
NEG = -1e30
Q_SCALE = 1.4426950408889634 * HEAD_DIM ** -0.5
PAGES_PER_STEP = 8
MAX_BISECT = 512
VMEM_LIMIT = 56 * 1024 * 1024


def _cparams(sem):
    return pltpu.CompilerParams(dimension_semantics=sem, vmem_limit_bytes=VMEM_LIMIT)


def _sigmoid(z):
    return 1.0 / (1.0 + jnp.exp(-z))


def _ada_kernel(c_ref, w_ref, b_ref, o_ref):
    o_ref[...] = jnp.dot(c_ref[...].astype(BF16), w_ref[...].astype(BF16),
                         preferred_element_type=F32) + b_ref[...]


def _ada(c, w, b):
    r, d = c.shape
    n = w.shape[1]
    return pl.pallas_call(
        _ada_kernel,
        grid=(n // TN,),
        in_specs=[pl.BlockSpec((r, d), lambda j: (0, 0)),
                  pl.BlockSpec((d, TN), lambda j: (0, j)),
                  pl.BlockSpec((1, TN), lambda j: (0, j))],
        out_specs=pl.BlockSpec((r, TN), lambda j: (0, j)),
        out_shape=jax.ShapeDtypeStruct((r, n), F32),
        compiler_params=_cparams(("arbitrary",)),
    )(c, w, b.reshape(1, n))


class _Layout:
    def __init__(self, d):
        att, kv, idx, pool = d, N_KV_HEADS * HEAD_DIM, IDX_HEADS * IDX_DIM, d // 2
        t = lambda n: n // TN
        assert att % TN == 0 and kv == TN and idx % TN == 0 and pool % TN == 0
        self.q = 0
        self.qi = self.q + t(att)
        self.ga = self.qi + t(idx)
        self.ma = self.ga + t(att)
        self.mb = self.ma + t(d)
        self.gb = self.mb + t(d)
        self.k = self.gb + t(pool)
        self.v = self.k + 1
        self.misc = self.v + 1
        self.u = self.misc + 1
        self.end = self.u + t(pool)
        self.nb = self.u
        self.pool = pool


def _rope_tile(z, cos, sin):
    parts = []
    for s in range(z.shape[1] // HEAD_DIM):
        zs = z[:, s * HEAD_DIM:(s + 1) * HEAD_DIM]
        parts.append(zs * cos + pltpu.roll(zs, HEAD_DIM // 2, 1) * sin)
    return parts[0] if len(parts) == 1 else jnp.concatenate(parts, axis=1)


def _proj_kernel(lay, idx_scale, q_scale, x_ref, g_ref, shift_ref, scale_ref, w_ref, cos_ref, sin_ref,
                 zb_ref, k_ref, v_ref, misc_ref, u_ref, h_ref):
    j = pl.program_id(1)

    @pl.when(j == 0)
    def _():
        x = x_ref[...]
        r = lax.rsqrt(jnp.mean(x * x, axis=-1, keepdims=True) + EPS)
        h = (x * r * g_ref[...]) * (1.0 + scale_ref[0]) + shift_ref[0]
        h_ref[...] = h.astype(BF16)

    z = jnp.dot(h_ref[...], w_ref[...], preferred_element_type=F32)

    @pl.when((j < lay.ga) | (j == lay.k))
    def _():
        r = _rope_tile(z, cos_ref[...], sin_ref[...])
        zb_ref[...] = (r * jnp.where(j < lay.qi, q_scale, 1.0)).astype(BF16)

        @pl.when(j == lay.k)
        def _():
            k_ref[...] = r

    @pl.when(((j >= lay.ga) & (j < lay.ma)) | ((j >= lay.gb) & (j < lay.k)))
    def _():
        zb_ref[...] = (z * _sigmoid(z)).astype(BF16)

    @pl.when((j >= lay.ma) & (j < lay.gb))
    def _():
        zb_ref[...] = _sigmoid(z).astype(BF16)

    @pl.when(j == lay.v)
    def _():
        zb_ref[...] = z.astype(BF16)
        v_ref[...] = z

    @pl.when(j == lay.misc)
    def _():
        ki = _rope_tile(z[:, :IDX_DIM], cos_ref[...], sin_ref[...])
        m = jnp.concatenate([ki, z[:, IDX_DIM:] * idx_scale], axis=1)
        zb_ref[...] = m.astype(BF16)
        misc_ref[...] = m

    @pl.when(j >= lay.u)
    def _():
        u_ref[...] = z


def _project(x, g, shift, scale, w_p, cos, sin, lay, tm, tiles_per_group, tiles_per_pos, idx_scale):
    m, d = x.shape
    grp, r, _ = shift.shape
    nu = lay.end - lay.u
    kern = functools.partial(_proj_kernel, lay, idx_scale, Q_SCALE)
    return pl.pallas_call(
        kern,
        grid=(m // tm, lay.end),
        in_specs=[pl.BlockSpec((tm, d), lambda i, j: (i, 0)),
                  pl.BlockSpec((1, d), lambda i, j: (0, 0)),
                  pl.BlockSpec((1, r, d), lambda i, j: (i // tiles_per_group, 0, 0)),
                  pl.BlockSpec((1, r, d), lambda i, j: (i // tiles_per_group, 0, 0)),
                  pl.BlockSpec((d, TN), lambda i, j: (0, j)),
                  pl.BlockSpec((tm, HEAD_DIM), lambda i, j: (i % tiles_per_pos, 0)),
                  pl.BlockSpec((tm, HEAD_DIM), lambda i, j: (i % tiles_per_pos, 0))],
        out_specs=[pl.BlockSpec((tm, TN), lambda i, j: (i, jnp.minimum(j, lay.nb - 1))),
                   pl.BlockSpec((tm, TN), lambda i, j: (i, 0)),
                   pl.BlockSpec((tm, TN), lambda i, j: (i, 0)),
                   pl.BlockSpec((tm, TN), lambda i, j: (i, 0)),
                   pl.BlockSpec((tm, TN), lambda i, j: (i, jnp.clip(j - lay.u, 0, nu - 1)))],
        out_shape=[jax.ShapeDtypeStruct((m, lay.nb * TN), BF16),
                   jax.ShapeDtypeStruct((m, TN), F32),
                   jax.ShapeDtypeStruct((m, TN), F32),
                   jax.ShapeDtypeStruct((m, TN), F32),
                   jax.ShapeDtypeStruct((m, nu * TN), F32)],
        scratch_shapes=[pltpu.VMEM((tm, d), BF16)],
        compiler_params=_cparams(("arbitrary", "arbitrary")),
    )(x, g, shift, scale, w_p, cos, sin)


def _bisect(count_ge, lo, hi, clo, kf):
    def cond(st):
        return (st[4] > 0.0) & (st[5] < MAX_BISECT)

    def body(st):
        lo, hi, clo, done, _, it = st
        mid = 0.5 * lo + 0.5 * hi
        mid = jnp.where(mid <= lo, hi, mid)
        c = count_ge(mid)
        ge = c >= kf
        stuck = (mid >= hi) & jnp.logical_not(ge)
        lo2 = jnp.where(ge, mid, lo)
        hi2 = jnp.where(ge, hi, mid)
        clo2 = jnp.where(ge, c, clo)
        fin = (clo2 <= kf) | (lo2 >= hi2) | stuck
        done2 = jnp.maximum(done, jnp.where(fin, 1.0, 0.0))
        return lo2, hi2, clo2, done2, jnp.sum(1.0 - done2), it + 1

    done0 = jnp.where((clo <= kf) | (lo >= hi), 1.0, 0.0)
    st = (lo, hi, clo, done0, jnp.sum(1.0 - done0), jnp.int32(0))
    return lax.while_loop(cond, body, st)[0]


def _attn_kernel(topk, q_ref, qi_ref, sga_ref, k_ref, ki_ref, vt_ref, wt_ref, o_ref,
                 score_ref, bias_ref, m_ref, l_ref, alpha_ref, acc_ref, lg_ref, p_ref):
    i = pl.program_id(1)
    nck = i + 1
    n_heads = q_ref.shape[1] // HEAD_DIM
    group = n_heads // N_KV_HEADS
    nt = (((1,), (1,)), ((), ()))
    q_pos = i * TQ + lax.broadcasted_iota(jnp.int32, (1, TQ), 1)
    fold = lambda a: a.reshape(CK // SUBLANES, SUBLANES, TQ)

    def score_chunk(c, carry):
        mn, mx = carry
        off = pl.multiple_of(c * CK, CK)
        kc = ki_ref[pl.ds(off, CK), :]
        acc = jnp.zeros((CK, TQ), F32)
        for h in range(IDX_HEADS):
            s = lax.dot_general(kc, qi_ref[:, h * IDX_DIM:(h + 1) * IDX_DIM], nt,
                                preferred_element_type=F32)
            acc = acc + wt_ref[h:h + 1, :] * jnp.maximum(s, 0.0)
        k_pos = off + lax.broadcasted_iota(jnp.int32, (CK, 1), 0)
        causal = k_pos <= q_pos
        score_ref[pl.ds(off, CK), :] = jnp.where(causal, acc, -jnp.inf)
        mn = jnp.minimum(mn, fold(jnp.where(causal, acc, jnp.inf)).min(axis=0))
        mx = jnp.maximum(mx, fold(jnp.where(causal, acc, -jnp.inf)).max(axis=0))
        return mn, mx

    mn, mx = lax.fori_loop(0, nck, score_chunk,
                           (jnp.full((SUBLANES, TQ), jnp.inf, F32),
                            jnp.full((SUBLANES, TQ), -jnp.inf, F32)))
    lo = mn.min(axis=0, keepdims=True)
    hi = mx.max(axis=0, keepdims=True)

    def count_ge(thr):
        def body(c, cnt):
            s = score_ref[pl.ds(pl.multiple_of(c * CK, CK), CK), :]
            return cnt + fold(jnp.where(s >= thr, 1.0, 0.0)).sum(axis=0)
        cnt = lax.fori_loop(0, nck, body, jnp.zeros((SUBLANES, TQ), F32))
        return cnt.sum(axis=0, keepdims=True)

    n_valid = (q_pos + 1).astype(F32)
    thr = _bisect(count_ge, lo, hi, n_valid, float(topk))

    m_ref[...] = jnp.full(m_ref.shape, NEG, F32)
    l_ref[...] = jnp.zeros(l_ref.shape, F32)
    acc_ref[...] = jnp.zeros(acc_ref.shape, F32)

    def att_chunk(c, carry):
        off = pl.multiple_of(c * CK, CK)
        bias_ref[...] = jnp.where(score_ref[pl.ds(off, CK), :] >= thr, 0.0, NEG)
        kc = k_ref[pl.ds(off, CK), :]
        vtc = vt_ref[0, :, pl.ds(off, CK)]

        def qk(h):
            g = h // group
            return lax.dot_general(kc[:, g * HEAD_DIM:(g + 1) * HEAD_DIM],
                                   q_ref[:, h * HEAD_DIM:(h + 1) * HEAD_DIM], nt,
                                   preferred_element_type=F32)

        for h in range(n_heads):
            lg = qk(h) + bias_ref[...]
            m_old = m_ref[h:h + 1, :]
            m_new = jnp.maximum(m_old, lg.max(axis=0, keepdims=True))
            lg_ref[h] = lg
            alpha_ref[h:h + 1, :] = jnp.exp2(m_old - m_new)
            m_ref[h:h + 1, :] = m_new
        for h in range(n_heads):
            p = jnp.exp2(lg_ref[h] - m_ref[h:h + 1, :])
            l_ref[h:h + 1, :] = alpha_ref[h:h + 1, :] * l_ref[h:h + 1, :] + p.sum(axis=0, keepdims=True)
            p_ref[h] = p.astype(BF16)
        for h in range(n_heads):
            g = h // group
            pv = jnp.dot(vtc[g * HEAD_DIM:(g + 1) * HEAD_DIM, :], p_ref[h], preferred_element_type=F32)
            acc_ref[h] = alpha_ref[h:h + 1, :] * acc_ref[h] + pv
        return carry

    lax.fori_loop(0, nck, att_chunk, 0)

    for h in range(n_heads):
        o = (acc_ref[h] / l_ref[h:h + 1, :]).T
        sl = slice(h * HEAD_DIM, (h + 1) * HEAD_DIM)
        o_ref[:, sl] = (o * sga_ref[:, sl].astype(F32)).astype(BF16)


def _prompt_attention(zb, vt, wt, lay, b, s, d, topk):
    nq = s // TQ
    n_heads = d // HEAD_DIM
    kern = functools.partial(_attn_kernel, topk)
    row = lambda bb, i: bb * nq + i
    return pl.pallas_call(
        kern,
        grid=(b, nq),
        in_specs=[pl.BlockSpec((TQ, d), lambda bb, i: (row(bb, i), lay.q * TN // d)),
                  pl.BlockSpec((TQ, d), lambda bb, i: (row(bb, i), lay.qi * TN // d)),
                  pl.BlockSpec((TQ, d), lambda bb, i: (row(bb, i), lay.ga * TN // d)),
                  pl.BlockSpec((s, TN), lambda bb, i: (bb, lay.k)),
                  pl.BlockSpec((s, IDX_DIM), lambda bb, i: (bb, lay.misc * TN // IDX_DIM)),
                  pl.BlockSpec((1, TN, s), lambda bb, i: (bb, 0, 0)),
                  pl.BlockSpec((IDX_HEADS, TQ), lambda bb, i: (0, row(bb, i)))],
        out_specs=pl.BlockSpec((TQ, d), lambda bb, i: (row(bb, i), 0)),
        out_shape=jax.ShapeDtypeStruct((b * s, d), BF16),
        scratch_shapes=[pltpu.VMEM((s, TQ), F32),
                        pltpu.VMEM((CK, TQ), F32),
                        pltpu.VMEM((n_heads, TQ), F32),
                        pltpu.VMEM((n_heads, TQ), F32),
                        pltpu.VMEM((n_heads, TQ), F32),
                        pltpu.VMEM((n_heads, HEAD_DIM, TQ), F32),
                        pltpu.VMEM((n_heads, CK, TQ), F32),
                        pltpu.VMEM((n_heads, CK, TQ), BF16)],
        compiler_params=_cparams(("arbitrary", "arbitrary")),
    )(zb, zb, zb, zb, zb, vt, wt)


def _pool_kernel(ts, u_ref, sgb_ref, wp_ref, sp_ref, o_ref, ext_ref):
    i = pl.program_id(1)
    halo = POOL_STATE + 1

    @pl.when(i == 0)
    def _():
        ext_ref[0:halo, :] = jnp.zeros((halo, ext_ref.shape[1]), F32)

    @pl.when(i > 0)
    def _():
        ext_ref[0:halo, :] = ext_ref[ts:ts + halo, :]

    ext_ref[halo:halo + ts, :] = u_ref[...]
    pos = i * ts + lax.broadcasted_iota(jnp.int32, (ts, 1), 0)
    gd = wp_ref.shape[1]
    for g, w in enumerate(POOL_WINDOWS):
        cols = slice(g * gd, (g + 1) * gd)
        acc = ext_ref[halo:halo + ts, cols]
        for jj in range(1, w):
            acc = acc + ext_ref[halo - jj:halo - jj + ts, cols]
        cnt = jnp.minimum(pos + 1, w).astype(F32)
        pooled = acc / cnt - u_ref[:, cols]
        mixed = jnp.dot(pooled.astype(BF16), wp_ref[g], preferred_element_type=F32) * sp_ref[:, cols]
        o_ref[:, cols] = (mixed * sgb_ref[:, cols].astype(F32)).astype(BF16)


def _pool_prompt(u, zb, wp, sp, lay, b, s, ts):
    m, pw = u.shape
    nt = s // ts
    kern = functools.partial(_pool_kernel, ts)
    return pl.pallas_call(
        kern,
        grid=(b, nt),
        in_specs=[pl.BlockSpec((ts, pw), lambda bb, i: (bb * nt + i, 0)),
                  pl.BlockSpec((ts, pw), lambda bb, i: (bb * nt + i, lay.gb * TN // pw)),
                  pl.BlockSpec(wp.shape, lambda bb, i: (0, 0, 0)),
                  pl.BlockSpec((1, pw), lambda bb, i: (0, 0))],
        out_specs=pl.BlockSpec((ts, pw), lambda bb, i: (bb * nt + i, 0)),
        out_shape=jax.ShapeDtypeStruct((m, pw), BF16),
        scratch_shapes=[pltpu.VMEM((POOL_STATE + 1 + ts, pw), F32)],
        compiler_params=_cparams(("arbitrary", "arbitrary")),
    )(u, zb, wp, sp)


def _pool_step_kernel(cnts, ext_ref, sgb_ref, wp_ref, sp_ref, o_ref):
    gd = wp_ref.shape[1]
    rows = ext_ref.shape[0]
    for g, w in enumerate(POOL_WINDOWS):
        cols = slice(g * gd, (g + 1) * gd)
        acc = ext_ref[rows - 1, :, cols]
        for jj in range(1, w):
            acc = acc + ext_ref[rows - 1 - jj, :, cols]
        pooled = acc / cnts[g] - ext_ref[rows - 1, :, cols]
        mixed = jnp.dot(pooled.astype(BF16), wp_ref[g], preferred_element_type=F32) * sp_ref[:, cols]
        o_ref[:, cols] = (mixed * sgb_ref[:, cols].astype(F32)).astype(BF16)


def _pool_step(ext_t, sgb, wp, sp, past):
    rows, n, pw = ext_t.shape
    cnts = tuple(float(min(past + 1, w)) for w in POOL_WINDOWS)
    return pl.pallas_call(
        functools.partial(_pool_step_kernel, cnts),
        out_shape=jax.ShapeDtypeStruct((n, pw), BF16),
    )(ext_t, sgb, wp, sp)


def _merge1_kernel(ab_ref, pb_ref, wa_ref, wb_ref, sma_ref, smb_ref, o_ref):
    a = jnp.dot(ab_ref[...], wa_ref[...], preferred_element_type=F32)
    bb = jnp.dot(pb_ref[...], wb_ref[...], preferred_element_type=F32)
    o_ref[...] = (sma_ref[...].astype(F32) * a + smb_ref[...].astype(F32) * bb).astype(BF16)


def _merge1(ab, pb, wa, wb, zb, lay, tm):
    m, d = ab.shape
    pw = pb.shape[1]
    return pl.pallas_call(
        _merge1_kernel,
        grid=(m // tm, d // TN),
        in_specs=[pl.BlockSpec((tm, d), lambda i, j: (i, 0)),
                  pl.BlockSpec((tm, pw), lambda i, j: (i, 0)),
                  pl.BlockSpec((d, TN), lambda i, j: (0, j)),
                  pl.BlockSpec((pw, TN), lambda i, j: (0, j)),
                  pl.BlockSpec((tm, TN), lambda i, j: (i, lay.ma + j)),
                  pl.BlockSpec((tm, TN), lambda i, j: (i, lay.mb + j))],
        out_specs=pl.BlockSpec((tm, TN), lambda i, j: (i, j)),
        out_shape=jax.ShapeDtypeStruct((m, d), BF16),
        compiler_params=_cparams(("arbitrary", "arbitrary")),
    )(ab, pb, wa, wb, zb, zb)


def _merge2_kernel(final, mm_ref, x_ref, gate_ref, wo_ref, gf_ref, o_ref):
    y = x_ref[...] + gate_ref[0] * jnp.dot(mm_ref[...], wo_ref[...], preferred_element_type=F32)
    if final:
        y = y * lax.rsqrt(jnp.mean(y * y, axis=-1, keepdims=True) + EPS) * gf_ref[...]
    o_ref[...] = y


def _merge2(mm, x, gate, wo, gf, tm, tiles_per_group, final):
    m, d = x.shape
    r = gate.shape[1]
    return pl.pallas_call(
        functools.partial(_merge2_kernel, final),
        grid=(m // tm,),
        in_specs=[pl.BlockSpec((tm, d), lambda i: (i, 0)),
                  pl.BlockSpec((tm, d), lambda i: (i, 0)),
                  pl.BlockSpec((1, r, d), lambda i: (i // tiles_per_group, 0, 0)),
                  pl.BlockSpec((d, d), lambda i: (0, 0)),
                  pl.BlockSpec((1, d), lambda i: (0, 0))],
        out_specs=pl.BlockSpec((tm, d), lambda i: (i, 0)),
        out_shape=jax.ShapeDtypeStruct((m, d), F32),
        compiler_params=_cparams(("arbitrary",)),
    )(mm, x, gate, wo, gf)


def _page_specs(width, npg, npp):
    return [pl.BlockSpec((1, PAGE_SIZE, width),
                         lambda b, p, pt, r=r: (pt[b * npg + p * npp + r], 0, 0)) for r in range(npp)]


def _sidx_kernel(npp, pt_ref, *refs):
    kidx_refs = refs[:npp]
    qi_ref, w_ref, o_ref = refs[npp:]
    nt = (((1,), (1,)), ((), ()))
    qi, w = qi_ref[0], w_ref[0]
    rows = []
    for r in range(npp):
        s = lax.dot_general(qi, kidx_refs[r][0].astype(BF16), nt, preferred_element_type=F32)
        rows.append(jnp.sum(w * jnp.maximum(s, 0.0), axis=0, keepdims=True))
    o_ref[0] = jnp.concatenate(rows, axis=1)


def _sample_scores(page_table, cache_kidx, qi3, wcol, npp):
    n, npg = page_table.shape
    return pl.pallas_call(
        functools.partial(_sidx_kernel, npp),
        grid_spec=pltpu.PrefetchScalarGridSpec(
            num_scalar_prefetch=1,
            grid=(n, npg // npp),
            in_specs=_page_specs(IDX_DIM, npg, npp) + [
                pl.BlockSpec((1, IDX_HEADS, IDX_DIM), lambda b, p, pt: (b, 0, 0)),
                pl.BlockSpec((1, IDX_HEADS, 1), lambda b, p, pt: (b, 0, 0))],
            out_specs=pl.BlockSpec((1, 1, npp * PAGE_SIZE), lambda b, p, pt: (b, 0, p))),
        out_shape=jax.ShapeDtypeStruct((n, 1, npg * PAGE_SIZE), F32),
        compiler_params=_cparams(("arbitrary", "arbitrary")),
    )(page_table.reshape(-1), *([cache_kidx] * npp), qi3, wcol)


def _sthr_kernel(topk, sc_ref, qi_ref, kin_ref, w_ref, pos_ref, cnt_ref, sel_ref, thr_scr, new_scr):
    sc = sc_ref[...]
    kin = kin_ref[...].astype(BF16).astype(F32)
    s_new = jnp.sum(qi_ref[...].astype(F32) * kin, axis=2, keepdims=True)
    new = jnp.sum(w_ref[...] * jnp.maximum(s_new, 0.0), axis=1, keepdims=True)
    red = lambda f, a: f(f(a, axis=2, keepdims=True), axis=1, keepdims=True)
    lo = jnp.minimum(red(jnp.min, sc), new)
    hi = jnp.maximum(red(jnp.max, sc), new)

    def count_ge(thr):
        c = red(jnp.sum, jnp.where(sc >= thr, 1.0, 0.0))
        return c + jnp.where(new >= thr, 1.0, 0.0)

    n_all = jnp.full(new.shape, float(sc.shape[1] * sc.shape[2] + 1), F32)
    thr = _bisect(count_ge, lo, hi, n_all, float(topk))
    thr_scr[...] = jnp.broadcast_to(thr, thr_scr.shape)
    new_scr[...] = jnp.broadcast_to(new, new_scr.shape)

    rows = sc.shape[1]
    ii = lambda shape, ax: lax.broadcasted_iota(jnp.int32, shape, ax)
    one = lambda m: jnp.where(m, 1.0, 0.0)
    tri_lane = one(ii((PAGE_SIZE, PAGE_SIZE), 0) <= ii((PAGE_SIZE, PAGE_SIZE), 1)).astype(BF16)
    lower = one(ii((rows, rows), 1) < ii((rows, rows), 0)).astype(BF16)
    upper = one(ii((rows, rows), 0) < ii((rows, rows), 1)).astype(BF16)
    ones8 = jnp.ones((SUBLANES, PAGE_SIZE), BF16)
    nt = (((1,), (1,)), ((), ()))
    total = lambda a: jnp.sum(jnp.sum(a, axis=1, keepdims=True), axis=0, keepdims=True)
    dotf = lambda a, b: jnp.dot(a.astype(BF16), b.astype(BF16), preferred_element_type=F32)
    kf = float(topk)
    j_col = ii((topk, 1), 0).astype(F32)
    r_row = ii((1, rows), 1).astype(F32)
    c_row = ii((1, PAGE_SIZE), 1).astype(F32)

    def compact(b, carry):
        s = sc_ref[b]
        t = thr_scr[b][:, 0:1]
        nw = new_scr[b][:, 0:1]
        gt, eq = s > t, s == t
        need = kf - total(one(gt)) - one(nw > t)
        e_lane = dotf(one(eq), tri_lane)
        e_rank = e_lane + dotf(lower, e_lane)[:, PAGE_SIZE - 1:PAGE_SIZE]
        x = one(gt | (eq & (e_rank <= need)))
        sel_new = (nw > t) | ((nw == t) & (total(one(eq)) < need))
        x_lane = dotf(x, tri_lane)
        n_rows = lax.dot_general(ones8, x.astype(BF16), nt, preferred_element_type=F32)
        n_row = n_rows[0:1, :]
        start_row = dotf(n_rows, upper)[0:1, :]
        hit = one((start_row <= j_col) & (j_col < start_row + n_row))
        lane_sum = lambda a: jnp.sum(a, axis=1, keepdims=True)
        page = lane_sum(hit * r_row)
        target = j_col - lane_sum(hit * start_row) + 1.0
        in_row = dotf(hit, x_lane * x)
        off = lane_sum(jnp.where(in_row == target, c_row, 0.0))
        pos_ref[b] = (page * PAGE_SIZE + off).astype(jnp.int32)
        cnt_ref[b] = jnp.broadcast_to(total(x), cnt_ref.shape[1:])
        sel_ref[b] = jnp.broadcast_to(one(sel_new), sel_ref.shape[1:])
        return carry

    lax.fori_loop(0, sc.shape[0], compact, 0)


def _sample_select(sc, qi3, kin, wcol, topk):
    n = sc.shape[0]
    return pl.pallas_call(
        functools.partial(_sthr_kernel, topk),
        out_shape=[jax.ShapeDtypeStruct((n, topk, 1), jnp.int32),
                   jax.ShapeDtypeStruct((n, 1, PAGE_SIZE), F32),
                   jax.ShapeDtypeStruct((n, 1, PAGE_SIZE), F32)],
        scratch_shapes=[pltpu.VMEM((n, 1, PAGE_SIZE), F32),
                        pltpu.VMEM((n, 1, PAGE_SIZE), F32)],
        compiler_params=pltpu.CompilerParams(vmem_limit_bytes=VMEM_LIMIT),
    )(sc, qi3, kin, wcol)


def _row_copy(cache_hbm, buf, sem, layer, page, off, j):
    return pltpu.make_async_copy(cache_hbm.at[layer, page, off], buf.at[j], sem)


def _sattn_kernel(layer, npg, pos_ref, pt_ref, ck_hbm, cv_hbm, cnt_ref, sel_ref, q_ref, kn_ref, vn_ref,
                  sga_ref, o_ref, kbuf, vbuf, ksem, vsem):
    b = pl.program_id(0)
    topk = kbuf.shape[0]
    n_heads = q_ref.shape[1]
    group = n_heads // N_KV_HEADS
    nt = (((1,), (1,)), ((), ()))

    def issue(j, carry):
        pos = pos_ref[b * topk + j]
        page = pt_ref[b * npg + pos // PAGE_SIZE]
        off = pos % PAGE_SIZE
        _row_copy(ck_hbm, kbuf, ksem, layer, page, off, j).start()
        _row_copy(cv_hbm, vbuf, vsem, layer, page, off, j).start()
        return carry

    lax.fori_loop(0, topk, issue, 0)

    def wait(j, carry):
        _row_copy(ck_hbm, kbuf, ksem, layer, 0, 0, j).wait()
        _row_copy(cv_hbm, vbuf, vsem, layer, 0, 0, j).wait()
        return carry

    lax.fori_loop(0, topk, wait, 0)

    q = q_ref[0]
    head = lax.broadcasted_iota(jnp.int32, (n_heads, 1), 0)
    in_group = [(head >= g * group) & (head < (g + 1) * group) for g in range(N_KV_HEADS)]
    lg = jnp.zeros((n_heads, topk), F32)
    for g in range(N_KV_HEADS):
        lg_g = lax.dot_general(q, kbuf[:, g, :].astype(BF16), nt, preferred_element_type=F32)
        lg = lg + jnp.where(in_group[g], lg_g, 0.0)
    valid = lax.broadcasted_iota(jnp.int32, (1, topk), 1).astype(F32) < cnt_ref[0][:, 0:1]
    lg = jnp.where(valid, lg, NEG)
    kn = kn_ref[0].astype(BF16).astype(F32)
    vn = vn_ref[0].astype(BF16).astype(F32)
    lgn = jnp.sum(q.astype(F32) * kn, axis=1, keepdims=True)
    lgn = jnp.where(sel_ref[0][:, 0:1] > 0.0, lgn, NEG)
    m = jnp.maximum(lg.max(axis=1, keepdims=True), lgn)
    p = jnp.exp2(lg - m)
    pn = jnp.exp2(lgn - m)
    l = p.sum(axis=1, keepdims=True) + pn
    pb = p.astype(BF16)
    acc = pn.astype(BF16).astype(F32) * vn
    for g in range(N_KV_HEADS):
        pv = jnp.dot(pb, vbuf[:, g, :].astype(BF16), preferred_element_type=F32)
        acc = acc + jnp.where(in_group[g], pv, 0.0)
    o_ref[0] = ((acc / l) * sga_ref[0].astype(F32)).astype(BF16)


def _sample_attention(page_table, cache_k, cache_v, layer, pos, cnt, sel, q3, kn, vn, sga3):
    n, npg = page_table.shape
    n_heads = q3.shape[1]
    topk = pos.shape[1]
    per_b = lambda shape: pl.BlockSpec(shape, lambda b, ps, pt: (b, 0, 0))
    hbm = pl.BlockSpec(memory_space=pl.ANY)
    return pl.pallas_call(
        functools.partial(_sattn_kernel, layer, npg),
        grid_spec=pltpu.PrefetchScalarGridSpec(
            num_scalar_prefetch=2,
            grid=(n,),
            in_specs=[hbm, hbm,
                      per_b((1, 1, PAGE_SIZE)),
                      per_b((1, 1, PAGE_SIZE)),
                      per_b((1, n_heads, HEAD_DIM)),
                      per_b((1, n_heads, HEAD_DIM)),
                      per_b((1, n_heads, HEAD_DIM)),
                      per_b((1, n_heads, HEAD_DIM))],
            out_specs=per_b((1, n_heads, HEAD_DIM)),
            scratch_shapes=[pltpu.VMEM((topk, N_KV_HEADS, HEAD_DIM), cache_k.dtype),
                            pltpu.VMEM((topk, N_KV_HEADS, HEAD_DIM), cache_v.dtype),
                            pltpu.SemaphoreType.DMA(()),
                            pltpu.SemaphoreType.DMA(())]),
        out_shape=jax.ShapeDtypeStruct((n, n_heads, HEAD_DIM), BF16),
        compiler_params=_cparams(("arbitrary",)),
    )(pos.reshape(-1), page_table.reshape(-1), cache_k, cache_v, cnt, sel, q3, kn, vn, sga3)


def _rope_tables(pos):
    half = HEAD_DIM // 2
    inv = ROPE_THETA ** (-jnp.arange(half, dtype=F32) / half)
    ang = pos.astype(F32)[:, None] * inv[None, :]
    cos, sin = jnp.cos(ang), jnp.sin(ang)
    return jnp.concatenate([cos, cos], axis=1), jnp.concatenate([-sin, sin], axis=1)


def _pack_w_in(w_in, d, lay):
    att, kv, idx, pool = d, N_KV_HEADS * HEAD_DIM, IDX_HEADS * IDX_DIM, d // 2
    sizes = (att, kv, kv, idx, IDX_DIM, IDX_HEADS, att, pool, pool, d, d)
    offs = [0]
    for n in sizes:
        offs.append(offs[-1] + n)
    seg = lambda t: w_in[:, offs[t]:offs[t + 1]]
    q, k, v, qi, ki, wi, ga, u, gb, ma, mb = (seg(t) for t in range(len(sizes)))
    pad = jnp.zeros((d, TN - IDX_DIM - IDX_HEADS), w_in.dtype)
    w_p = jnp.concatenate([q, qi, ga, ma, mb, gb, k, v, ki, wi, pad, u], axis=1).astype(BF16)
    assert w_p.shape[1] == lay.end * TN
    return w_p


def kernel(x_prompt, x_sample, c_prompt, c_sample, cache_k, cache_v, cache_kidx, state_pool, page_table,
           g_norm, w_ada, b_ada, w_in, w_pool, s_pool, w_a_proj, w_b_proj, w_out, g_final):
    bp, sp, d = x_prompt.shape
    ns, ds, _ = x_sample.shape
    depth = g_norm.shape[0]
    npg = page_table.shape[1]
    past = npg * PAGE_SIZE
    assert ds == 1 and d % (N_KV_HEADS * HEAD_DIM) == 0 and sp % TQ == 0 and sp >= POOL_STATE
    n_heads = d // HEAD_DIM
    kvw = N_KV_HEADS * HEAD_DIM
    lay = _Layout(d)
    pw = lay.pool
    idx_scale = (IDX_HEADS * IDX_DIM) ** -0.5
    topk_p = min(TOPK_MAX, sp // 4)
    topk_s = min(TOPK_MAX, (past + ds) // 4)
    tm = 512
    ts = 512
    tm2 = 256

    cos_p, sin_p = _rope_tables(jnp.arange(sp, dtype=jnp.int32))
    cos_s, sin_s = _rope_tables(jnp.full((ns,), past, jnp.int32))
    n_c = bp + ns
    c_rows = -(-n_c // SUBLANES) * SUBLANES
    c_all = jnp.concatenate([c_prompt, c_sample, jnp.zeros((c_rows - n_c, d), F32)], axis=0)

    n_pool = cache_kidx.shape[1]
    kidx_pages = cache_kidx.reshape(depth * n_pool, PAGE_SIZE, IDX_DIM)
    npp = PAGES_PER_STEP
    assert npg % npp == 0

    xp = x_prompt.reshape(bp * sp, d)
    xs = x_sample.reshape(ns, d)
    outs = {n: [] for n in ("kp", "vp", "kip", "pp", "ks", "vs", "kis", "ps")}
    for l in range(depth):
        final = l == depth - 1
        w_p = _pack_w_in(w_in[l], d, lay)
        wa, wb, wo, wpl = (w_a_proj[l].astype(BF16), w_b_proj[l].astype(BF16), w_out[l].astype(BF16),
                           w_pool[l].astype(BF16))
        g = g_norm[l].reshape(1, d)
        spl = s_pool[l].reshape(1, pw)
        gf = g_final.reshape(1, d)

        mod = _ada(c_all, w_ada[l], b_ada[l])
        shift, scale, gate = mod[:, :d], mod[:, d:2 * d], mod[:, 2 * d:]
        pg = lambda a: a[:bp].reshape(bp, 1, d)
        sg = lambda a: a[bp:n_c].reshape(1, ns, d)

        zb, k_p, v_p, misc_p, u_p = _project(xp, g, pg(shift), pg(scale), w_p, cos_p, sin_p, lay, tm,
                                             sp // tm, sp // tm, idx_scale)
        vt = zb[:, lay.v * TN:(lay.v + 1) * TN].reshape(bp, sp, kvw).transpose(0, 2, 1)
        wt = misc_p[:, IDX_DIM:IDX_DIM + IDX_HEADS].T
        ab = _prompt_attention(zb, vt, wt, lay, bp, sp, d, topk_p)
        pb = _pool_prompt(u_p, zb, wpl, spl, lay, bp, sp, ts)
        mm = _merge1(ab, pb, wa, wb, zb, lay, tm)
        xp = _merge2(mm, xp, pg(gate), wo, gf, tm2, sp // tm2, final)
        outs["kp"].append(k_p.reshape(bp, sp, N_KV_HEADS, HEAD_DIM))
        outs["vp"].append(v_p.reshape(bp, sp, N_KV_HEADS, HEAD_DIM))
        outs["kip"].append(misc_p[:, :IDX_DIM].reshape(bp, sp, IDX_DIM))
        outs["pp"].append(u_p.reshape(bp, sp, pw)[:, sp - POOL_STATE:, :])

        zs, k_s, v_s, misc_s, u_s = _project(xs, g, sg(shift), sg(scale), w_p, cos_s, sin_s, lay, ns,
                                             1, 1, idx_scale)
        col = lambda t, n: zs[:, t * TN:t * TN + n]
        q3 = col(lay.q, d).reshape(ns, n_heads, HEAD_DIM)
        qi3 = col(lay.qi, IDX_HEADS * IDX_DIM).reshape(ns, IDX_HEADS, IDX_DIM)
        sga3 = col(lay.ga, d).reshape(ns, n_heads, HEAD_DIM)
        wcol = misc_s[:, IDX_DIM:IDX_DIM + IDX_HEADS].reshape(ns, IDX_HEADS, 1)
        kin = misc_s[:, :IDX_DIM].reshape(ns, 1, IDX_DIM)
        pt_l = page_table + l * n_pool
        sc = _sample_scores(pt_l, kidx_pages, qi3, wcol, npp)
        pos, cnt, sel = _sample_select(sc.reshape(ns, npg, PAGE_SIZE), qi3, kin, wcol, topk_s)
        rep = lambda a: jnp.repeat(a.reshape(ns, N_KV_HEADS, HEAD_DIM), n_heads // N_KV_HEADS, axis=1)
        ab_s = _sample_attention(page_table, cache_k, cache_v, l, pos, cnt, sel, q3,
                                 rep(k_s), rep(v_s), sga3).reshape(ns, d)
        ext = jnp.concatenate([state_pool[l], u_s[:, None, :]], axis=1)
        pb_s = _pool_step(ext.transpose(1, 0, 2), col(lay.gb, pw), wpl, spl, past)
        mm_s = _merge1(ab_s, pb_s, wa, wb, zs, lay, ns)
        xs = _merge2(mm_s, xs, sg(gate), wo, gf, ns, 1, final)
        outs["ks"].append(k_s.reshape(ns, ds, N_KV_HEADS, HEAD_DIM))
        outs["vs"].append(v_s.reshape(ns, ds, N_KV_HEADS, HEAD_DIM))
        outs["kis"].append(misc_s[:, :IDX_DIM].reshape(ns, ds, IDX_DIM))
        outs["ps"].append(ext[:, 1:, :])

    st = lambda n: outs[n][0][None] if depth == 1 else jnp.stack(outs[n])
    return (xp.reshape(bp, sp, d), xs.reshape(ns, ds, d),
            st("kp"), st("vp"), st("kip"), st("pp"), st("ks"), st("vs"), st("kis"), st("ps"))
```

```python
import functools

import jax
import jax.numpy as jnp
from jax import lax
from jax.experimental import pallas as pl
from jax.experimental.pallas import tpu as pltpu

F32 = jnp.float32
BF16 = jnp.bfloat16

HEAD_DIM = 128
N_KV_HEADS = 4
IDX_HEADS = 16
IDX_DIM = 128
TOPK_MAX = 256
PAGE_SIZE = 128
POOL_WINDOWS = (2, 4, 8, 16)
POOL_STATE = max(POOL_WINDOWS) - 1
ROPE_THETA = 10000.0
EPS = 1e-6

LANES = 128
SUBLANES = 8
TN = 512
TQ = 256
CK = 256
NEG = -1e30
Q_SCALE = 1.4426950408889634 * HEAD_DIM ** -0.5
PAGES_PER_STEP = 8
MAX_BISECT = 512
VMEM_LIMIT = 56 * 1024 * 1024


def _cparams(sem):
    return pltpu.CompilerParams(dimension_semantics=sem, vmem_limit_bytes=VMEM_LIMIT)


def _sigmoid(z):
    return 1.0 / (1.0 + jnp.exp(-z))


def _ada_kernel(c_ref, w_ref, b_ref, o_ref):
    o_ref[...] = jnp.dot(c_ref[...].astype(BF16), w_ref[...].astype(BF16),
                         preferred_element_type=F32) + b_ref[...]


def _ada(c, w, b):
    r, d = c.shape
    n = w.shape[1]
    return pl.pallas_call(
        _ada_kernel,
        grid=(n // TN,),
        in_specs=[pl.BlockSpec((r, d), lambda j: (0, 0)),
                  pl.BlockSpec((d, TN), lambda j: (0, j)),
                  pl.BlockSpec((1, TN), lambda j: (0, j))],
        out_specs=pl.BlockSpec((r, TN), lambda j: (0, j)),
        out_shape=jax.ShapeDtypeStruct((r, n), F32),
        compiler_params=_cparams(("arbitrary",)),
    )(c, w, b.reshape(1, n))


class _Layout:
    def __init__(self, d):
        att, kv, idx, pool = d, N_KV_HEADS * HEAD_DIM, IDX_HEADS * IDX_DIM, d // 2
        t = lambda n: n // TN
        assert att % TN == 0 and kv == TN and idx % TN == 0 and pool % TN == 0
        self.q = 0
        self.qi = self.q + t(att)
        self.ga = self.qi + t(idx)
        self.ma = self.ga + t(att)
        self.mb = self.ma + t(d)
        self.gb = self.mb + t(d)
        self.k = self.gb + t(pool)
        self.v = self.k + 1
        self.misc = self.v + 1
        self.u = self.misc + 1
        self.end = self.u + t(pool)
        self.nb = self.u
        self.pool = pool


def _rope_tile(z, cos, sin):
    parts = []
    for s in range(z.shape[1] // HEAD_DIM):
        zs = z[:, s * HEAD_DIM:(s + 1) * HEAD_DIM]
        parts.append(zs * cos + pltpu.roll(zs, HEAD_DIM // 2, 1) * sin)
    return parts[0] if len(parts) == 1 else jnp.concatenate(parts, axis=1)


def _proj_kernel(lay, idx_scale, q_scale, with_vt, x_ref, g_ref, shift_ref, scale_ref, w_ref, cos_ref, sin_ref,
                 zb_ref, k_ref, v_ref, misc_ref, u_ref, *rest):
    vt_ref = rest[0] if with_vt else None
    h_ref = rest[-1]
    j = pl.program_id(1)

    @pl.when(j == 0)
    def _():
        x = x_ref[...]
        r = lax.rsqrt(jnp.mean(x * x, axis=-1, keepdims=True) + EPS)
        h = (x * r * g_ref[...]) * (1.0 + scale_ref[0]) + shift_ref[0]
        h_ref[...] = h.astype(BF16)

    def tile():
        return jnp.dot(h_ref[...], w_ref[...], preferred_element_type=F32)

    @pl.when(j < lay.qi)
    def _():
        zb_ref[...] = (_rope_tile(tile(), cos_ref[...], sin_ref[...]) * q_scale).astype(BF16)

    @pl.when((j >= lay.qi) & (j < lay.ga))
    def _():
        zb_ref[...] = _rope_tile(tile(), cos_ref[...], sin_ref[...]).astype(BF16)

    @pl.when(j == lay.k)
    def _():
        r = _rope_tile(tile(), cos_ref[...], sin_ref[...])
        zb_ref[...] = r.astype(BF16)
        k_ref[...] = r

    @pl.when(((j >= lay.ga) & (j < lay.ma)) | ((j >= lay.gb) & (j < lay.k)))
    def _():
        z = tile()
        zb_ref[...] = (z * _sigmoid(z)).astype(BF16)

    @pl.when((j >= lay.ma) & (j < lay.gb))
    def _():
        zb_ref[...] = _sigmoid(tile()).astype(BF16)

    @pl.when(j == lay.v)
    def _():
        z = tile()
        zb_ref[...] = z.astype(BF16)
        v_ref[...] = z
        if with_vt:
            vt_ref[0] = z.T.astype(BF16)

    @pl.when(j == lay.misc)
    def _():
        z = tile()
        ki = _rope_tile(z[:, :IDX_DIM], cos_ref[...], sin_ref[...])
        m = jnp.concatenate([ki, z[:, IDX_DIM:] * idx_scale], axis=1)
        zb_ref[...] = m.astype(BF16)
        misc_ref[...] = m

    @pl.when(j >= lay.u)
    def _():
        u_ref[...] = tile()


def _project(x, g, shift, scale, w_p, cos, sin, lay, tm, tiles_per_group, tiles_per_pos, idx_scale, with_vt):
    m, d = x.shape
    grp, r, _ = shift.shape
    nu = lay.end - lay.u
    kern = functools.partial(_proj_kernel, lay, idx_scale, Q_SCALE, with_vt)
    out_specs = [pl.BlockSpec((tm, TN), lambda i, j: (i, jnp.minimum(j, lay.nb - 1))),
                 pl.BlockSpec((tm, TN), lambda i, j: (i, 0)),
                 pl.BlockSpec((tm, TN), lambda i, j: (i, 0)),
                 pl.BlockSpec((tm, TN), lambda i, j: (i, 0)),
                 pl.BlockSpec((tm, TN), lambda i, j: (i, jnp.clip(j - lay.u, 0, nu - 1)))]
    out_shape = [jax.ShapeDtypeStruct((m, lay.nb * TN), BF16),
                 jax.ShapeDtypeStruct((m, TN), F32),
                 jax.ShapeDtypeStruct((m, TN), F32),
                 jax.ShapeDtypeStruct((m, TN), F32),
                 jax.ShapeDtypeStruct((m, nu * TN), F32)]
    if with_vt:
        out_specs.append(pl.BlockSpec((1, TN, tm), lambda i, j: (i // tiles_per_group, 0, i % tiles_per_group)))
        out_shape.append(jax.ShapeDtypeStruct((grp, TN, tiles_per_group * tm), BF16))
    return pl.pallas_call(
        kern,
        grid=(m // tm, lay.end),
        in_specs=[pl.BlockSpec((tm, d), lambda i, j: (i, 0)),
                  pl.BlockSpec((1, d), lambda i, j: (0, 0)),
                  pl.BlockSpec((1, r, d), lambda i, j: (i // tiles_per_group, 0, 0)),
                  pl.BlockSpec((1, r, d), lambda i, j: (i // tiles_per_group, 0, 0)),
                  pl.BlockSpec((d, TN), lambda i, j: (0, j)),
                  pl.BlockSpec((tm, HEAD_DIM), lambda i, j: (i % tiles_per_pos, 0)),
                  pl.BlockSpec((tm, HEAD_DIM), lambda i, j: (i % tiles_per_pos, 0))],
        out_specs=out_specs,
        out_shape=out_shape,
        scratch_shapes=[pltpu.VMEM((tm, d), BF16)],
        compiler_params=_cparams(("arbitrary", "arbitrary")),
    )(x, g, shift, scale, w_p, cos, sin)


def _bisect(count_ge, lo, hi, clo, kf):
    def cond(st):
        return (st[4] > 0.0) & (st[5] < MAX_BISECT)

    def body(st):
        lo, hi, clo, done, _, it = st
        mid = 0.5 * lo + 0.5 * hi
        mid = jnp.where(mid <= lo, hi, mid)
        c = count_ge(mid)
        ge = c >= kf
        stuck = (mid >= hi) & jnp.logical_not(ge)
        lo2 = jnp.where(ge, mid, lo)
        hi2 = jnp.where(ge, hi, mid)
        clo2 = jnp.where(ge, c, clo)
        fin = (clo2 <= kf) | (lo2 >= hi2) | stuck
        done2 = jnp.maximum(done, jnp.where(fin, 1.0, 0.0))
        return lo2, hi2, clo2, done2, jnp.sum(1.0 - done2), it + 1

    done0 = jnp.where((clo <= kf) | (lo >= hi), 1.0, 0.0)
    st = (lo, hi, clo, done0, jnp.sum(1.0 - done0), jnp.int32(0))
    out = lax.while_loop(cond, body, st)
    return out[0], out[2]


def _attn_kernel(topk, q_ref, qi_ref, sga_ref, k_ref, ki_ref, vt_ref, wt_ref, o_ref,
                 score_ref, bias_ref, cut_ref, m_ref, l_ref, alpha_ref, acc_ref, lg_ref, p_ref):
    i = pl.program_id(1)
    nck = i + 1
    n_heads = q_ref.shape[1] // HEAD_DIM
    group = n_heads // N_KV_HEADS
    nt = (((1,), (1,)), ((), ()))
    q_pos = i * TQ + lax.broadcasted_iota(jnp.int32, (1, TQ), 1)
    fold = lambda a: a.reshape(CK // SUBLANES, SUBLANES, TQ)

    def score_chunk(c, carry):
        mn, mx = carry
        off = pl.multiple_of(c * CK, CK)
        kc = ki_ref[pl.ds(off, CK), :]
        acc = jnp.zeros((CK, TQ), F32)
        for h in range(IDX_HEADS):
            s = lax.dot_general(kc, qi_ref[:, h * IDX_DIM:(h + 1) * IDX_DIM], nt,
                                preferred_element_type=F32)
            acc = acc + wt_ref[h:h + 1, :] * jnp.maximum(s, 0.0)
        k_pos = off + lax.broadcasted_iota(jnp.int32, (CK, 1), 0)
        causal = k_pos <= q_pos
        score_ref[pl.ds(off, CK), :] = jnp.where(causal, acc, -jnp.inf)
        mn = jnp.minimum(mn, fold(jnp.where(causal, acc, jnp.inf)).min(axis=0))
        mx = jnp.maximum(mx, fold(jnp.where(causal, acc, -jnp.inf)).max(axis=0))
        return mn, mx

    mn, mx = lax.fori_loop(0, nck, score_chunk,
                           (jnp.full((SUBLANES, TQ), jnp.inf, F32),
                            jnp.full((SUBLANES, TQ), -jnp.inf, F32)))
    lo = mn.min(axis=0, keepdims=True)
    hi = mx.max(axis=0, keepdims=True)

    def count_ge(thr):
        def body(c, cnt):
            s = score_ref[pl.ds(pl.multiple_of(c * CK, CK), CK), :]
            return cnt + fold(jnp.where(s >= thr, 1.0, 0.0)).sum(axis=0)
        cnt = lax.fori_loop(0, nck, body, jnp.zeros((SUBLANES, TQ), F32))
        return cnt.sum(axis=0, keepdims=True)

    n_valid = (q_pos + 1).astype(F32)
    kf = float(topk)
    thr, n_ge = _bisect(count_ge, lo, hi, n_valid, kf)

    s_len = score_ref.shape[0]
    has_ties = jnp.sum(jnp.where(n_ge > kf, 1.0, 0.0)) > 0.0
    cut_ref[...] = jnp.full(cut_ref.shape, float(s_len), F32)

    @pl.when(has_ties)
    def _():
        def count(pred):
            def body(c, cnt):
                off = pl.multiple_of(c * CK, CK)
                k_pos = (off + lax.broadcasted_iota(jnp.int32, (CK, 1), 0)).astype(F32)
                hit = pred(score_ref[pl.ds(off, CK), :], k_pos)
                return cnt + fold(jnp.where(hit, 1.0, 0.0)).sum(axis=0)
            cnt = lax.fori_loop(0, nck, body, jnp.zeros((SUBLANES, TQ), F32))
            return cnt.sum(axis=0, keepdims=True)

        need = kf - count(lambda s, kp: s > thr)

        def step(_, st):
            lo_i, hi_i = st
            mid = jnp.floor(0.5 * (lo_i + hi_i))
            ge = count(lambda s, kp: (s == thr) & (kp <= mid)) >= need
            return jnp.where(ge, lo_i, mid), jnp.where(ge, mid, hi_i)

        st0 = (jnp.full((1, TQ), -1.0, F32), jnp.full((1, TQ), s_len - 1.0, F32))
        cut = lax.fori_loop(0, (s_len - 1).bit_length() + 1, step, st0)[1]
        cut_ref[...] = jnp.broadcast_to(cut, cut_ref.shape)

    m_ref[...] = jnp.full(m_ref.shape, NEG, F32)
    l_ref[...] = jnp.zeros(l_ref.shape, F32)
    acc_ref[...] = jnp.zeros(acc_ref.shape, F32)

    def att_chunk(c, carry):
        off = pl.multiple_of(c * CK, CK)
        sc = score_ref[pl.ds(off, CK), :]

        @pl.when(jnp.logical_not(has_ties))
        def _():
            bias_ref[...] = jnp.where(sc >= thr, 0.0, NEG)

        @pl.when(has_ties)
        def _():
            k_pos = (off + lax.broadcasted_iota(jnp.int32, (CK, 1), 0)).astype(F32)
            keep = (sc > thr) | ((sc == thr) & (k_pos <= cut_ref[0:1, :]))
            bias_ref[...] = jnp.where(keep, 0.0, NEG)

        kc = k_ref[pl.ds(off, CK), :]
        vtc = vt_ref[0, :, pl.ds(off, CK)]

        def qk(h):
            g = h // group
            return lax.dot_general(kc[:, g * HEAD_DIM:(g + 1) * HEAD_DIM],
                                   q_ref[:, h * HEAD_DIM:(h + 1) * HEAD_DIM], nt,
                                   preferred_element_type=F32)

        for h in range(n_heads):
            lg = qk(h) + bias_ref[...]
            m_old = m_ref[h:h + 1, :]
            m_new = jnp.maximum(m_old, lg.max(axis=0, keepdims=True))
            lg_ref[h] = lg
            alpha_ref[h:h + 1, :] = jnp.exp2(m_old - m_new)
            m_ref[h:h + 1, :] = m_new
        for h in range(n_heads):
            p = jnp.exp2(lg_ref[h] - m_ref[h:h + 1, :])
            l_ref[h:h + 1, :] = alpha_ref[h:h + 1, :] * l_ref[h:h + 1, :] + p.sum(axis=0, keepdims=True)
            p_ref[h] = p.astype(BF16)
        for h in range(n_heads):
            g = h // group
            pv = jnp.dot(vtc[g * HEAD_DIM:(g + 1) * HEAD_DIM, :], p_ref[h], preferred_element_type=F32)
            acc_ref[h] = alpha_ref[h:h + 1, :] * acc_ref[h] + pv
        return carry

    lax.fori_loop(0, nck, att_chunk, 0)

    for h in range(n_heads):
        o = (acc_ref[h] / l_ref[h:h + 1, :]).T
        sl = slice(h * HEAD_DIM, (h + 1) * HEAD_DIM)
        o_ref[:, sl] = (o * sga_ref[:, sl].astype(F32)).astype(BF16)


def _prompt_attention(zb, vt, wt, lay, b, s, d, topk):
    nq = s // TQ
    n_heads = d // HEAD_DIM
    kern = functools.partial(_attn_kernel, topk)
    row = lambda bb, i: bb * nq + i
    return pl.pallas_call(
        kern,
        grid=(b, nq),
        in_specs=[pl.BlockSpec((TQ, d), lambda bb, i: (row(bb, i), lay.q * TN // d)),
                  pl.BlockSpec((TQ, d), lambda bb, i: (row(bb, i), lay.qi * TN // d)),
                  pl.BlockSpec((TQ, d), lambda bb, i: (row(bb, i), lay.ga * TN // d)),
                  pl.BlockSpec((s, TN), lambda bb, i: (bb, lay.k)),
                  pl.BlockSpec((s, IDX_DIM), lambda bb, i: (bb, lay.misc * TN // IDX_DIM)),
                  pl.BlockSpec((1, TN, s), lambda bb, i: (bb, 0, 0)),
                  pl.BlockSpec((IDX_HEADS, TQ), lambda bb, i: (0, row(bb, i)))],
        out_specs=pl.BlockSpec((TQ, d), lambda bb, i: (row(bb, i), 0)),
        out_shape=jax.ShapeDtypeStruct((b * s, d), BF16),
        scratch_shapes=[pltpu.VMEM((s, TQ), F32),
                        pltpu.VMEM((CK, TQ), F32),
                        pltpu.VMEM((SUBLANES, TQ), F32),
                        pltpu.VMEM((n_heads, TQ), F32),
                        pltpu.VMEM((n_heads, TQ), F32),
                        pltpu.VMEM((n_heads, TQ), F32),
                        pltpu.VMEM((n_heads, HEAD_DIM, TQ), F32),
                        pltpu.VMEM((n_heads, CK, TQ), F32),
                        pltpu.VMEM((n_heads, CK, TQ), BF16)],
        compiler_params=_cparams(("arbitrary", "arbitrary")),
    )(zb, zb, zb, zb, zb, vt, wt)


def _pool_kernel(ts, u_ref, sgb_ref, wp_ref, sp_ref, o_ref, ext_ref):
    i = pl.program_id(1)
    halo = POOL_STATE + 1

    @pl.when(i == 0)
    def _():
        ext_ref[0:halo, :] = jnp.zeros((halo, ext_ref.shape[1]), F32)

    @pl.when(i > 0)
    def _():
        ext_ref[0:halo, :] = ext_ref[ts:ts + halo, :]

    ext_ref[halo:halo + ts, :] = u_ref[...]
    pos = i * ts + lax.broadcasted_iota(jnp.int32, (ts, 1), 0)
    gd = wp_ref.shape[1]
    for g, w in enumerate(POOL_WINDOWS):
        cols = slice(g * gd, (g + 1) * gd)
        acc = ext_ref[halo:halo + ts, cols]
        for jj in range(1, w):
            acc = acc + ext_ref[halo - jj:halo - jj + ts, cols]
        cnt = jnp.minimum(pos + 1, w).astype(F32)
        pooled = acc / cnt - u_ref[:, cols]
        mixed = jnp.dot(pooled.astype(BF16), wp_ref[g], preferred_element_type=F32) * sp_ref[:, cols]
        o_ref[:, cols] = (mixed * sgb_ref[:, cols].astype(F32)).astype(BF16)


def _pool_prompt(u, zb, wp, sp, lay, b, s, ts):
    m, pw = u.shape
    nt = s // ts
    kern = functools.partial(_pool_kernel, ts)
    return pl.pallas_call(
        kern,
        grid=(b, nt),
        in_specs=[pl.BlockSpec((ts, pw), lambda bb, i: (bb * nt + i, 0)),
                  pl.BlockSpec((ts, pw), lambda bb, i: (bb * nt + i, lay.gb * TN // pw)),
                  pl.BlockSpec(wp.shape, lambda bb, i: (0, 0, 0)),
                  pl.BlockSpec((1, pw), lambda bb, i: (0, 0))],
        out_specs=pl.BlockSpec((ts, pw), lambda bb, i: (bb * nt + i, 0)),
        out_shape=jax.ShapeDtypeStruct((m, pw), BF16),
        scratch_shapes=[pltpu.VMEM((POOL_STATE + 1 + ts, pw), F32)],
        compiler_params=_cparams(("arbitrary", "arbitrary")),
    )(u, zb, wp, sp)


def _pool_step_kernel(cnts, ext_ref, sgb_ref, wp_ref, sp_ref, o_ref):
    gd = wp_ref.shape[1]
    rows = ext_ref.shape[0]
    for g, w in enumerate(POOL_WINDOWS):
        cols = slice(g * gd, (g + 1) * gd)
        acc = ext_ref[rows - 1, :, cols]
        for jj in range(1, w):
            acc = acc + ext_ref[rows - 1 - jj, :, cols]
        pooled = acc / cnts[g] - ext_ref[rows - 1, :, cols]
        mixed = jnp.dot(pooled.astype(BF16), wp_ref[g], preferred_element_type=F32) * sp_ref[:, cols]
        o_ref[:, cols] = (mixed * sgb_ref[:, cols].astype(F32)).astype(BF16)


def _pool_step(ext_t, sgb, wp, sp, past):
    rows, n, pw = ext_t.shape
    cnts = tuple(float(min(past + 1, w)) for w in POOL_WINDOWS)
    return pl.pallas_call(
        functools.partial(_pool_step_kernel, cnts),
        out_shape=jax.ShapeDtypeStruct((n, pw), BF16),
    )(ext_t, sgb, wp, sp)


def _merge1_kernel(ab_ref, pb_ref, wa_ref, wb_ref, sma_ref, smb_ref, o_ref):
    a = jnp.dot(ab_ref[...], wa_ref[...], preferred_element_type=F32)
    bb = jnp.dot(pb_ref[...], wb_ref[...], preferred_element_type=F32)
    o_ref[...] = (sma_ref[...].astype(F32) * a + smb_ref[...].astype(F32) * bb).astype(BF16)


def _merge1(ab, pb, wa, wb, zb, lay, tm):
    m, d = ab.shape
    pw = pb.shape[1]
    return pl.pallas_call(
        _merge1_kernel,
        grid=(m // tm, d // TN),
        in_specs=[pl.BlockSpec((tm, d), lambda i, j: (i, 0)),
                  pl.BlockSpec((tm, pw), lambda i, j: (i, 0)),
                  pl.BlockSpec((d, TN), lambda i, j: (0, j)),
                  pl.BlockSpec((pw, TN), lambda i, j: (0, j)),
                  pl.BlockSpec((tm, TN), lambda i, j: (i, lay.ma + j)),
                  pl.BlockSpec((tm, TN), lambda i, j: (i, lay.mb + j))],
        out_specs=pl.BlockSpec((tm, TN), lambda i, j: (i, j)),
        out_shape=jax.ShapeDtypeStruct((m, d), BF16),
        compiler_params=_cparams(("arbitrary", "arbitrary")),
    )(ab, pb, wa, wb, zb, zb)


def _merge2_kernel(final, mm_ref, x_ref, gate_ref, wo_ref, gf_ref, o_ref):
    y = x_ref[...] + gate_ref[0] * jnp.dot(mm_ref[...], wo_ref[...], preferred_element_type=F32)
    if final:
        y = y * lax.rsqrt(jnp.mean(y * y, axis=-1, keepdims=True) + EPS) * gf_ref[...]
    o_ref[...] = y


def _merge2(mm, x, gate, wo, gf, tm, tiles_per_group, final):
    m, d = x.shape
    r = gate.shape[1]
    return pl.pallas_call(
        functools.partial(_merge2_kernel, final),
        grid=(m // tm,),
        in_specs=[pl.BlockSpec((tm, d), lambda i: (i, 0)),
                  pl.BlockSpec((tm, d), lambda i: (i, 0)),
                  pl.BlockSpec((1, r, d), lambda i: (i // tiles_per_group, 0, 0)),
                  pl.BlockSpec((d, d), lambda i: (0, 0)),
                  pl.BlockSpec((1, d), lambda i: (0, 0))],
        out_specs=pl.BlockSpec((tm, d), lambda i: (i, 0)),
        out_shape=jax.ShapeDtypeStruct((m, d), F32),
        compiler_params=_cparams(("arbitrary",)),
    )(mm, x, gate, wo, gf)


def _page_specs(width, npg, npp):
    return [pl.BlockSpec((1, PAGE_SIZE, width),
                         lambda b, p, pt, r=r: (pt[b * npg + p * npp + r], 0, 0)) for r in range(npp)]


def _sidx_kernel(npp, pt_ref, *refs):
    kidx_refs = refs[:npp]
    qi_ref, w_ref, o_ref = refs[npp:]
    nt = (((1,), (1,)), ((), ()))
    qi, w = qi_ref[0], w_ref[0]
    rows = []
    for r in range(npp):
        s = lax.dot_general(qi, kidx_refs[r][0].astype(BF16), nt, preferred_element_type=F32)
        rows.append(jnp.sum(w * jnp.maximum(s, 0.0), axis=0, keepdims=True))
    o_ref[0] = jnp.concatenate(rows, axis=1)


def _sample_scores(page_table, cache_kidx, qi3, wcol, npp):
    n, npg = page_table.shape
    return pl.pallas_call(
        functools.partial(_sidx_kernel, npp),
        grid_spec=pltpu.PrefetchScalarGridSpec(
            num_scalar_prefetch=1,
            grid=(n, npg // npp),
            in_specs=_page_specs(IDX_DIM, npg, npp) + [
                pl.BlockSpec((1, IDX_HEADS, IDX_DIM), lambda b, p, pt: (b, 0, 0)),
                pl.BlockSpec((1, IDX_HEADS, 1), lambda b, p, pt: (b, 0, 0))],
            out_specs=pl.BlockSpec((1, 1, npp * PAGE_SIZE), lambda b, p, pt: (b, 0, p))),
        out_shape=jax.ShapeDtypeStruct((n, 1, npg * PAGE_SIZE), F32),
        compiler_params=_cparams(("arbitrary", "arbitrary")),
    )(page_table.reshape(-1), *([cache_kidx] * npp), qi3, wcol)


def _sthr_kernel(topk, sc_ref, qi_ref, kin_ref, w_ref, pos_ref, cnt_ref, sel_ref, thr_scr, new_scr):
    sc = sc_ref[...]
    kin = kin_ref[...].astype(BF16).astype(F32)
    s_new = jnp.sum(qi_ref[...].astype(F32) * kin, axis=2, keepdims=True)
    new = jnp.sum(w_ref[...] * jnp.maximum(s_new, 0.0), axis=1, keepdims=True)
    red = lambda f, a: f(f(a, axis=2, keepdims=True), axis=1, keepdims=True)
    lo = jnp.minimum(red(jnp.min, sc), new)
    hi = jnp.maximum(red(jnp.max, sc), new)

    def count_ge(thr):
        c = red(jnp.sum, jnp.where(sc >= thr, 1.0, 0.0))
        return c + jnp.where(new >= thr, 1.0, 0.0)

    n_all = jnp.full(new.shape, float(sc.shape[1] * sc.shape[2] + 1), F32)
    thr, _ = _bisect(count_ge, lo, hi, n_all, float(topk))
    thr_scr[...] = jnp.broadcast_to(thr, thr_scr.shape)
    new_scr[...] = jnp.broadcast_to(new, new_scr.shape)

    rows = sc.shape[1]
    ii = lambda shape, ax: lax.broadcasted_iota(jnp.int32, shape, ax)
    one = lambda m: jnp.where(m, 1.0, 0.0)
    tri_lane = one(ii((PAGE_SIZE, PAGE_SIZE), 0) <= ii((PAGE_SIZE, PAGE_SIZE), 1)).astype(BF16)
    lower = one(ii((rows, rows), 1) < ii((rows, rows), 0)).astype(BF16)
    upper = one(ii((rows, rows), 0) < ii((rows, rows), 1)).astype(BF16)
    ones8 = jnp.ones((SUBLANES, PAGE_SIZE), BF16)
    nt = (((1,), (1,)), ((), ()))
    total = lambda a: jnp.sum(jnp.sum(a, axis=1, keepdims=True), axis=0, keepdims=True)
    dotf = lambda a, b: jnp.dot(a.astype(BF16), b.astype(BF16), preferred_element_type=F32)
    kf = float(topk)
    j_col = ii((topk, 1), 0).astype(F32)
    r_row = ii((1, rows), 1).astype(F32)
    c_row = ii((1, PAGE_SIZE), 1).astype(F32)

    def compact(b, carry):
        s = sc_ref[b]
        t = thr_scr[b][:, 0:1]
        nw = new_scr[b][:, 0:1]
        gt, eq = s > t, s == t
        need = kf - total(one(gt)) - one(nw > t)
        e_lane = dotf(one(eq), tri_lane)
        e_rank = e_lane + dotf(lower, e_lane)[:, PAGE_SIZE - 1:PAGE_SIZE]
        x = one(gt | (eq & (e_rank <= need)))
        sel_new = (nw > t) | ((nw == t) & (total(one(eq)) < need))
        x_lane = dotf(x, tri_lane)
        n_rows = lax.dot_general(ones8, x.astype(BF16), nt, preferred_element_type=F32)
        n_row = n_rows[0:1, :]
        start_row = dotf(n_rows, upper)[0:1, :]
        hit = one((start_row <= j_col) & (j_col < start_row + n_row))
        lane_sum = lambda a: jnp.sum(a, axis=1, keepdims=True)
        page = lane_sum(hit * r_row)
        target = j_col - lane_sum(hit * start_row) + 1.0
        in_row = dotf(hit, x_lane * x)
        off = lane_sum(jnp.where(in_row == target, c_row, 0.0))
        pos_ref[b] = (page * PAGE_SIZE + off).astype(jnp.int32)
        cnt_ref[b] = jnp.broadcast_to(total(x), cnt_ref.shape[1:])
        sel_ref[b] = jnp.broadcast_to(one(sel_new), sel_ref.shape[1:])
        return carry

    lax.fori_loop(0, sc.shape[0], compact, 0)


def _sample_select(sc, qi3, kin, wcol, topk):
    n = sc.shape[0]
    return pl.pallas_call(
        functools.partial(_sthr_kernel, topk),
        out_shape=[jax.ShapeDtypeStruct((n, topk, 1), jnp.int32),
                   jax.ShapeDtypeStruct((n, 1, PAGE_SIZE), F32),
                   jax.ShapeDtypeStruct((n, 1, PAGE_SIZE), F32)],
        scratch_shapes=[pltpu.VMEM((n, 1, PAGE_SIZE), F32),
                        pltpu.VMEM((n, 1, PAGE_SIZE), F32)],
        compiler_params=pltpu.CompilerParams(vmem_limit_bytes=VMEM_LIMIT),
    )(sc, qi3, kin, wcol)


def _row_copy(cache_hbm, buf, sem, layer, page, off, j):
    return pltpu.make_async_copy(cache_hbm.at[layer, page, off], buf.at[j], sem)


def _sattn_kernel(layer, npg, pos_ref, pt_ref, ck_hbm, cv_hbm, cnt_ref, sel_ref, q_ref, kn_ref, vn_ref,
                  sga_ref, o_ref, kbuf, vbuf, ksem, vsem):
    b = pl.program_id(0)
    topk = kbuf.shape[0]
    n_heads = q_ref.shape[1]
    group = n_heads // N_KV_HEADS
    nt = (((1,), (1,)), ((), ()))

    def issue(j, carry):
        pos = pos_ref[b * topk + j]
        page = pt_ref[b * npg + pos // PAGE_SIZE]
        off = pos % PAGE_SIZE
        _row_copy(ck_hbm, kbuf, ksem, layer, page, off, j).start()
        _row_copy(cv_hbm, vbuf, vsem, layer, page, off, j).start()
        return carry

    lax.fori_loop(0, topk, issue, 0)

    def wait(j, carry):
        _row_copy(ck_hbm, kbuf, ksem, layer, 0, 0, j).wait()
        _row_copy(cv_hbm, vbuf, vsem, layer, 0, 0, j).wait()
        return carry

    lax.fori_loop(0, topk, wait, 0)

    q = q_ref[0]
    head = lax.broadcasted_iota(jnp.int32, (n_heads, 1), 0)
    in_group = [(head >= g * group) & (head < (g + 1) * group) for g in range(N_KV_HEADS)]
    lg = jnp.zeros((n_heads, topk), F32)
    for g in range(N_KV_HEADS):
        lg_g = lax.dot_general(q, kbuf[:, g, :].astype(BF16), nt, preferred_element_type=F32)
        lg = lg + jnp.where(in_group[g], lg_g, 0.0)
    valid = lax.broadcasted_iota(jnp.int32, (1, topk), 1).astype(F32) < cnt_ref[0][:, 0:1]
    lg = jnp.where(valid, lg, NEG)
    kn = kn_ref[0].astype(BF16).astype(F32)
    vn = vn_ref[0].astype(BF16).astype(F32)
    lgn = jnp.sum(q.astype(F32) * kn, axis=1, keepdims=True)
    lgn = jnp.where(sel_ref[0][:, 0:1] > 0.0, lgn, NEG)
    m = jnp.maximum(lg.max(axis=1, keepdims=True), lgn)
    p = jnp.exp2(lg - m)
    pn = jnp.exp2(lgn - m)
    l = p.sum(axis=1, keepdims=True) + pn
    pb = p.astype(BF16)
    acc = pn.astype(BF16).astype(F32) * vn
    for g in range(N_KV_HEADS):
        pv = jnp.dot(pb, vbuf[:, g, :].astype(BF16), preferred_element_type=F32)
        acc = acc + jnp.where(in_group[g], pv, 0.0)
    o_ref[0] = ((acc / l) * sga_ref[0].astype(F32)).astype(BF16)


def _sample_attention(page_table, cache_k, cache_v, layer, pos, cnt, sel, q3, kn, vn, sga3):
    n, npg = page_table.shape
    n_heads = q3.shape[1]
    topk = pos.shape[1]
    per_b = lambda shape: pl.BlockSpec(shape, lambda b, ps, pt: (b, 0, 0))
    hbm = pl.BlockSpec(memory_space=pl.ANY)
    return pl.pallas_call(
        functools.partial(_sattn_kernel, layer, npg),
        grid_spec=pltpu.PrefetchScalarGridSpec(
            num_scalar_prefetch=2,
            grid=(n,),
            in_specs=[hbm, hbm,
                      per_b((1, 1, PAGE_SIZE)),
                      per_b((1, 1, PAGE_SIZE)),
                      per_b((1, n_heads, HEAD_DIM)),
                      per_b((1, n_heads, HEAD_DIM)),
                      per_b((1, n_heads, HEAD_DIM)),
                      per_b((1, n_heads, HEAD_DIM))],
            out_specs=per_b((1, n_heads, HEAD_DIM)),
            scratch_shapes=[pltpu.VMEM((topk, N_KV_HEADS, HEAD_DIM), cache_k.dtype),
                            pltpu.VMEM((topk, N_KV_HEADS, HEAD_DIM), cache_v.dtype),
                            pltpu.SemaphoreType.DMA(()),
                            pltpu.SemaphoreType.DMA(())]),
        out_shape=jax.ShapeDtypeStruct((n, n_heads, HEAD_DIM), BF16),
        compiler_params=_cparams(("arbitrary",)),
    )(pos.reshape(-1), page_table.reshape(-1), cache_k, cache_v, cnt, sel, q3, kn, vn, sga3)


def _rope_tables(pos):
    half = HEAD_DIM // 2
    inv = ROPE_THETA ** (-jnp.arange(half, dtype=F32) / half)
    ang = pos.astype(F32)[:, None] * inv[None, :]
    cos, sin = jnp.cos(ang), jnp.sin(ang)
    return jnp.concatenate([cos, cos], axis=1), jnp.concatenate([-sin, sin], axis=1)


def _pack_w_in(w_in, d, lay):
    att, kv, idx, pool = d, N_KV_HEADS * HEAD_DIM, IDX_HEADS * IDX_DIM, d // 2
    sizes = (att, kv, kv, idx, IDX_DIM, IDX_HEADS, att, pool, pool, d, d)
    offs = [0]
    for n in sizes:
        offs.append(offs[-1] + n)
    seg = lambda t: w_in[:, offs[t]:offs[t + 1]]
    q, k, v, qi, ki, wi, ga, u, gb, ma, mb = (seg(t) for t in range(len(sizes)))
    pad = jnp.zeros((d, TN - IDX_DIM - IDX_HEADS), w_in.dtype)
    w_p = jnp.concatenate([q, qi, ga, ma, mb, gb, k, v, ki, wi, pad, u], axis=1).astype(BF16)
    assert w_p.shape[1] == lay.end * TN
    return w_p


def kernel(x_prompt, x_sample, c_prompt, c_sample, cache_k, cache_v, cache_kidx, state_pool, page_table,
           g_norm, w_ada, b_ada, w_in, w_pool, s_pool, w_a_proj, w_b_proj, w_out, g_final):
    bp, sp, d = x_prompt.shape
    ns, ds, _ = x_sample.shape
    depth = g_norm.shape[0]
    npg = page_table.shape[1]
    past = npg * PAGE_SIZE
    assert ds == 1 and d % (N_KV_HEADS * HEAD_DIM) == 0 and sp % TQ == 0 and sp >= POOL_STATE
    n_heads = d // HEAD_DIM
    kvw = N_KV_HEADS * HEAD_DIM
    lay = _Layout(d)
    pw = lay.pool
    idx_scale = (IDX_HEADS * IDX_DIM) ** -0.5
    topk_p = min(TOPK_MAX, sp // 4)
    topk_s = min(TOPK_MAX, (past + ds) // 4)
    tm = 512
    ts = 512
    tm2 = 256

    cos_p, sin_p = _rope_tables(jnp.arange(sp, dtype=jnp.int32))
    cos_s, sin_s = _rope_tables(jnp.full((ns,), past, jnp.int32))
    n_c = bp + ns
    c_rows = -(-n_c // SUBLANES) * SUBLANES
    c_all = jnp.concatenate([c_prompt, c_sample, jnp.zeros((c_rows - n_c, d), F32)], axis=0)

    n_pool = cache_kidx.shape[1]
    kidx_pages = cache_kidx.reshape(depth * n_pool, PAGE_SIZE, IDX_DIM)
    npp = PAGES_PER_STEP
    assert npg % npp == 0

    xp = x_prompt.reshape(bp * sp, d)
    xs = x_sample.reshape(ns, d)
    outs = {n: [] for n in ("kp", "vp", "kip", "pp", "ks", "vs", "kis", "ps")}
    for l in range(depth):
        final = l == depth - 1
        w_p = _pack_w_in(w_in[l], d, lay)
        wa, wb, wo, wpl = (w_a_proj[l].astype(BF16), w_b_proj[l].astype(BF16), w_out[l].astype(BF16),
                           w_pool[l].astype(BF16))
        g = g_norm[l].reshape(1, d)
        spl = s_pool[l].reshape(1, pw)
        gf = g_final.reshape(1, d)

        mod = _ada(c_all, w_ada[l], b_ada[l])
        shift, scale, gate = mod[:, :d], mod[:, d:2 * d], mod[:, 2 * d:]
        pg = lambda a: a[:bp].reshape(bp, 1, d)
        sg = lambda a: a[bp:n_c].reshape(1, ns, d)

        zb, k_p, v_p, misc_p, u_p, vt = _project(xp, g, pg(shift), pg(scale), w_p, cos_p, sin_p, lay, tm,
                                                 sp // tm, sp // tm, idx_scale, True)
        wt = misc_p[:, IDX_DIM:IDX_DIM + IDX_HEADS].T
        ab = _prompt_attention(zb, vt, wt, lay, bp, sp, d, topk_p)
        pb = _pool_prompt(u_p, zb, wpl, spl, lay, bp, sp, ts)
        mm = _merge1(ab, pb, wa, wb, zb, lay, tm)
        xp = _merge2(mm, xp, pg(gate), wo, gf, tm2, sp // tm2, final)
        outs["kp"].append(k_p.reshape(bp, sp, N_KV_HEADS, HEAD_DIM))
        outs["vp"].append(v_p.reshape(bp, sp, N_KV_HEADS, HEAD_DIM))
        outs["kip"].append(misc_p[:, :IDX_DIM].reshape(bp, sp, IDX_DIM))
        outs["pp"].append(u_p.reshape(bp, sp, pw)[:, sp - POOL_STATE:, :])

        zs, k_s, v_s, misc_s, u_s = _project(xs, g, sg(shift), sg(scale), w_p, cos_s, sin_s, lay, ns,
                                             1, 1, idx_scale, False)
        col = lambda t, n: zs[:, t * TN:t * TN + n]
        q3 = col(lay.q, d).reshape(ns, n_heads, HEAD_DIM)
        qi3 = col(lay.qi, IDX_HEADS * IDX_DIM).reshape(ns, IDX_HEADS, IDX_DIM)
        sga3 = col(lay.ga, d).reshape(ns, n_heads, HEAD_DIM)
        wcol = misc_s[:, IDX_DIM:IDX_DIM + IDX_HEADS].reshape(ns, IDX_HEADS, 1)
        kin = misc_s[:, :IDX_DIM].reshape(ns, 1, IDX_DIM)
        pt_l = page_table + l * n_pool
        sc = _sample_scores(pt_l, kidx_pages, qi3, wcol, npp)
        pos, cnt, sel = _sample_select(sc.reshape(ns, npg, PAGE_SIZE), qi3, kin, wcol, topk_s)
        rep = lambda a: jnp.repeat(a.reshape(ns, N_KV_HEADS, HEAD_DIM), n_heads // N_KV_HEADS, axis=1)
        ab_s = _sample_attention(page_table, cache_k, cache_v, l, pos, cnt, sel, q3,
                                 rep(k_s), rep(v_s), sga3).reshape(ns, d)
        ext = jnp.concatenate([state_pool[l], u_s[:, None, :]], axis=1)
        pb_s = _pool_step(ext.transpose(1, 0, 2), col(lay.gb, pw), wpl, spl, past)
        mm_s = _merge1(ab_s, pb_s, wa, wb, zs, lay, ns)
        xs = _merge2(mm_s, xs, sg(gate), wo, gf, ns, 1, final)
        outs["ks"].append(k_s.reshape(ns, ds, N_KV_HEADS, HEAD_DIM))
        outs["vs"].append(v_s.reshape(ns, ds, N_KV_HEADS, HEAD_DIM))
        outs["kis"].append(misc_s[:, :IDX_DIM].reshape(ns, ds, IDX_DIM))
        outs["ps"].append(ext[:, 1:, :])

    st = lambda n: outs[n][0][None] if depth == 1 else jnp.stack(outs[n])
    return (xp.reshape(bp, sp, d), xs.reshape(ns, ds, d),
            st("kp"), st("vp"), st("kip"), st("pp"), st("ks"), st("vs"), st("kis"), st("ps"))
```

```python
import functools

import jax
import jax.numpy as jnp
from jax import lax
from jax.experimental import pallas as pl
from jax.experimental.pallas import tpu as pltpu

F32 = jnp.float32
BF16 = jnp.bfloat16

HEAD_DIM = 128
N_KV_HEADS = 4
IDX_HEADS = 16
IDX_DIM = 128
TOPK_MAX = 256
PAGE_SIZE = 128
POOL_WINDOWS = (2, 4, 8, 16)
POOL_STATE = max(POOL_WINDOWS) - 1
ROPE_THETA = 10000.0
EPS = 1e-6

LANES = 128
SUBLANES = 8
TN = 512
TQ = 256
CK = 256
NEG = -1e30
Q_SCALE = 1.4426950408889634 * HEAD_DIM ** -0.5
PAGES_PER_STEP = 8
MAX_BISECT = 512
VMEM_LIMIT = 56 * 1024 * 1024


def _cparams(sem):
    return pltpu.CompilerParams(dimension_semantics=sem, vmem_limit_bytes=VMEM_LIMIT)


def _sigmoid(z):
    return 1.0 / (1.0 + jnp.exp(-z))


def _ada_kernel(c_ref, w_ref, b_ref, o_ref):
    o_ref[...] = jnp.dot(c_ref[...].astype(BF16), w_ref[...].astype(BF16),
                         preferred_element_type=F32) + b_ref[...]


def _ada(c, w, b):
    r, d = c.shape
    n = w.shape[1]
    return pl.pallas_call(
        _ada_kernel,
        grid=(n // TN,),
        in_specs=[pl.BlockSpec((r, d), lambda j: (0, 0)),
                  pl.BlockSpec((d, TN), lambda j: (0, j)),
                  pl.BlockSpec((1, TN), lambda j: (0, j))],
        out_specs=pl.BlockSpec((r, TN), lambda j: (0, j)),
        out_shape=jax.ShapeDtypeStruct((r, n), F32),
        compiler_params=_cparams(("arbitrary",)),
    )(c, w, b.reshape(1, n))


class _Layout:
    def __init__(self, d):
        att, kv, idx, pool = d, N_KV_HEADS * HEAD_DIM, IDX_HEADS * IDX_DIM, d // 2
        t = lambda n: n // TN
        assert att % TN == 0 and kv == TN and idx % TN == 0 and pool % TN == 0
        self.q = 0
        self.qi = self.q + t(att)
        self.ga = self.qi + t(idx)
        self.ma = self.ga + t(att)
        self.mb = self.ma + t(d)
        self.gb = self.mb + t(d)
        self.k = self.gb + t(pool)
        self.v = self.k + 1
        self.misc = self.v + 1
        self.u = self.misc + 1
        self.end = self.u + t(pool)
        self.nb = self.u
        self.pool = pool


def _rope_tile(z, cos, sin):
    parts = []
    for s in range(z.shape[1] // HEAD_DIM):
        zs = z[:, s * HEAD_DIM:(s + 1) * HEAD_DIM]
        parts.append(zs * cos + pltpu.roll(zs, HEAD_DIM // 2, 1) * sin)
    return parts[0] if len(parts) == 1 else jnp.concatenate(parts, axis=1)


def _proj_kernel(lay, idx_scale, q_scale, with_vt, x_ref, g_ref, shift_ref, scale_ref, w_ref, cos_ref, sin_ref,
                 zb_ref, k_ref, v_ref, misc_ref, u_ref, *rest):
    vt_ref, wt_ref = rest[:2] if with_vt else (None, None)
    h_ref = rest[-1]
    j = pl.program_id(1)

    @pl.when(j == 0)
    def _():
        x = x_ref[...]
        r = lax.rsqrt(jnp.mean(x * x, axis=-1, keepdims=True) + EPS)
        h = (x * r * g_ref[...]) * (1.0 + scale_ref[0]) + shift_ref[0]
        h_ref[...] = h.astype(BF16)

    def tile():
        return jnp.dot(h_ref[...], w_ref[...], preferred_element_type=F32)

    @pl.when(j < lay.qi)
    def _():
        zb_ref[...] = (_rope_tile(tile(), cos_ref[...], sin_ref[...]) * q_scale).astype(BF16)

    @pl.when((j >= lay.qi) & (j < lay.ga))
    def _():
        zb_ref[...] = _rope_tile(tile(), cos_ref[...], sin_ref[...]).astype(BF16)

    @pl.when(j == lay.k)
    def _():
        r = _rope_tile(tile(), cos_ref[...], sin_ref[...])
        zb_ref[...] = r.astype(BF16)
        k_ref[...] = r

    @pl.when(((j >= lay.ga) & (j < lay.ma)) | ((j >= lay.gb) & (j < lay.k)))
    def _():
        z = tile()
        zb_ref[...] = (z * _sigmoid(z)).astype(BF16)

    @pl.when((j >= lay.ma) & (j < lay.gb))
    def _():
        zb_ref[...] = _sigmoid(tile()).astype(BF16)

    @pl.when(j == lay.v)
    def _():
        z = tile()
        zb_ref[...] = z.astype(BF16)
        v_ref[...] = z
        if with_vt:
            vt_ref[0] = z.T.astype(BF16)

    @pl.when(j == lay.misc)
    def _():
        z = tile()
        ki = _rope_tile(z[:, :IDX_DIM], cos_ref[...], sin_ref[...])
        m = jnp.concatenate([ki, z[:, IDX_DIM:] * idx_scale], axis=1)
        zb_ref[...] = m.astype(BF16)
        misc_ref[...] = m
        if with_vt:
            wt_ref[...] = m[:, IDX_DIM:2 * IDX_DIM].T

    @pl.when(j >= lay.u)
    def _():
        u_ref[...] = tile()


def _project(x, g, shift, scale, w_p, cos, sin, lay, tm, tiles_per_group, tiles_per_pos, idx_scale, with_vt):
    m, d = x.shape
    grp, r, _ = shift.shape
    nu = lay.end - lay.u
    kern = functools.partial(_proj_kernel, lay, idx_scale, Q_SCALE, with_vt)
    out_specs = [pl.BlockSpec((tm, TN), lambda i, j: (i, jnp.minimum(j, lay.nb - 1))),
                 pl.BlockSpec((tm, TN), lambda i, j: (i, 0)),
                 pl.BlockSpec((tm, TN), lambda i, j: (i, 0)),
                 pl.BlockSpec((tm, TN), lambda i, j: (i, 0)),
                 pl.BlockSpec((tm, TN), lambda i, j: (i, jnp.clip(j - lay.u, 0, nu - 1)))]
    out_shape = [jax.ShapeDtypeStruct((m, lay.nb * TN), BF16),
                 jax.ShapeDtypeStruct((m, TN), F32),
                 jax.ShapeDtypeStruct((m, TN), F32),
                 jax.ShapeDtypeStruct((m, TN), F32),
                 jax.ShapeDtypeStruct((m, nu * TN), F32)]
    if with_vt:
        out_specs.append(pl.BlockSpec((1, TN, tm), lambda i, j: (i // tiles_per_group, 0, i % tiles_per_group)))
        out_shape.append(jax.ShapeDtypeStruct((grp, TN, tiles_per_group * tm), BF16))
        out_specs.append(pl.BlockSpec((IDX_DIM, tm), lambda i, j: (0, i)))
        out_shape.append(jax.ShapeDtypeStruct((IDX_DIM, m), F32))
    return pl.pallas_call(
        kern,
        grid=(m // tm, lay.end),
        in_specs=[pl.BlockSpec((tm, d), lambda i, j: (i, 0)),
                  pl.BlockSpec((1, d), lambda i, j: (0, 0)),
                  pl.BlockSpec((1, r, d), lambda i, j: (i // tiles_per_group, 0, 0)),
                  pl.BlockSpec((1, r, d), lambda i, j: (i // tiles_per_group, 0, 0)),
                  pl.BlockSpec((d, TN), lambda i, j: (0, j)),
                  pl.BlockSpec((tm, HEAD_DIM), lambda i, j: (i % tiles_per_pos, 0)),
                  pl.BlockSpec((tm, HEAD_DIM), lambda i, j: (i % tiles_per_pos, 0))],
        out_specs=out_specs,
        out_shape=out_shape,
        scratch_shapes=[pltpu.VMEM((tm, d), BF16)],
        compiler_params=_cparams(("arbitrary", "arbitrary")),
    )(x, g, shift, scale, w_p, cos, sin)


def _bisect(count_ge, lo, hi, clo, kf):
    def cond(st):
        return (st[4] > 0.0) & (st[5] < MAX_BISECT)

    def halve(lo, hi, clo, done):
        mid = 0.5 * lo + 0.5 * hi
        mid = jnp.where(mid <= lo, hi, mid)
        c = count_ge(mid)
        ge = c >= kf
        stuck = (mid >= hi) & jnp.logical_not(ge)
        lo2 = jnp.where(ge, mid, lo)
        hi2 = jnp.where(ge, hi, mid)
        clo2 = jnp.where(ge, c, clo)
        fin = (clo2 <= kf) | (lo2 >= hi2) | stuck
        return lo2, hi2, clo2, jnp.maximum(done, jnp.where(fin, 1.0, 0.0))

    def body(st):
        lo, hi, clo, done = halve(*halve(*st[:4]))
        return lo, hi, clo, done, jnp.sum(1.0 - done), st[5] + 1

    done0 = jnp.where((clo <= kf) | (lo >= hi), 1.0, 0.0)
    st = (lo, hi, clo, done0, jnp.sum(1.0 - done0), jnp.int32(0))
    out = lax.while_loop(cond, body, st)
    return out[0], out[2]


def _attn_kernel(topk, q_ref, qi_ref, sga_ref, k_ref, ki_ref, vt_ref, wt_ref, o_ref,
                 score_ref, bias_ref, cut_ref, m_ref, l_ref, alpha_ref, acc_ref, lg_ref, p_ref):
    i = pl.program_id(1)
    nck = i + 1
    n_heads = q_ref.shape[1] // HEAD_DIM
    group = n_heads // N_KV_HEADS
    nt = (((1,), (1,)), ((), ()))
    q_pos = i * TQ + lax.broadcasted_iota(jnp.int32, (1, TQ), 1)
    fold = lambda a: a.reshape(CK // SUBLANES, SUBLANES, TQ)

    def score_chunk(c, carry):
        mn, mx = carry
        off = pl.multiple_of(c * CK, CK)
        kc = ki_ref[pl.ds(off, CK), :]
        acc = jnp.zeros((CK, TQ), F32)
        for h in range(IDX_HEADS):
            s = lax.dot_general(kc, qi_ref[:, h * IDX_DIM:(h + 1) * IDX_DIM], nt,
                                preferred_element_type=F32)
            acc = acc + wt_ref[h:h + 1, :] * jnp.maximum(s, 0.0)
        k_pos = off + lax.broadcasted_iota(jnp.int32, (CK, 1), 0)
        causal = k_pos <= q_pos
        score_ref[pl.ds(off, CK), :] = jnp.where(causal, acc, -jnp.inf)
        mn = jnp.minimum(mn, fold(jnp.where(causal, acc, jnp.inf)).min(axis=0))
        mx = jnp.maximum(mx, fold(jnp.where(causal, acc, -jnp.inf)).max(axis=0))
        return mn, mx

    mn, mx = lax.fori_loop(0, nck, score_chunk,
                           (jnp.full((SUBLANES, TQ), jnp.inf, F32),
                            jnp.full((SUBLANES, TQ), -jnp.inf, F32)))
    lo = mn.min(axis=0, keepdims=True)
    hi = mx.max(axis=0, keepdims=True)

    def count_ge(thr):
        def body(c, cnt):
            s = score_ref[pl.ds(pl.multiple_of(c * CK, CK), CK), :]
            return cnt + fold(jnp.where(s >= thr, 1.0, 0.0)).sum(axis=0)
        cnt = lax.fori_loop(0, nck, body, jnp.zeros((SUBLANES, TQ), F32))
        return cnt.sum(axis=0, keepdims=True)

    n_valid = (q_pos + 1).astype(F32)
    kf = float(topk)
    thr, n_ge = _bisect(count_ge, lo, hi, n_valid, kf)

    s_len = score_ref.shape[0]
    has_ties = jnp.sum(jnp.where(n_ge > kf, 1.0, 0.0)) > 0.0
    cut_ref[...] = jnp.full(cut_ref.shape, float(s_len), F32)

    @pl.when(has_ties)
    def _():
        def count(pred):
            def body(c, cnt):
                off = pl.multiple_of(c * CK, CK)
                k_pos = (off + lax.broadcasted_iota(jnp.int32, (CK, 1), 0)).astype(F32)
                hit = pred(score_ref[pl.ds(off, CK), :], k_pos)
                return cnt + fold(jnp.where(hit, 1.0, 0.0)).sum(axis=0)
            cnt = lax.fori_loop(0, nck, body, jnp.zeros((SUBLANES, TQ), F32))
            return cnt.sum(axis=0, keepdims=True)

        need = kf - count(lambda s, kp: s > thr)

        def step(_, st):
            lo_i, hi_i = st
            mid = jnp.floor(0.5 * (lo_i + hi_i))
            ge = count(lambda s, kp: (s == thr) & (kp <= mid)) >= need
            return jnp.where(ge, lo_i, mid), jnp.where(ge, mid, hi_i)

        st0 = (jnp.full((1, TQ), -1.0, F32), jnp.full((1, TQ), s_len - 1.0, F32))
        cut = lax.fori_loop(0, (s_len - 1).bit_length() + 1, step, st0)[1]
        cut_ref[...] = jnp.broadcast_to(cut, cut_ref.shape)

    m_ref[...] = jnp.full(m_ref.shape, NEG, F32)
    l_ref[...] = jnp.zeros(l_ref.shape, F32)
    acc_ref[...] = jnp.zeros(acc_ref.shape, F32)

    def att_chunk(c, carry):
        off = pl.multiple_of(c * CK, CK)
        sc = score_ref[pl.ds(off, CK), :]

        @pl.when(jnp.logical_not(has_ties))
        def _():
            bias_ref[...] = jnp.where(sc >= thr, 0.0, NEG)

        @pl.when(has_ties)
        def _():
            k_pos = (off + lax.broadcasted_iota(jnp.int32, (CK, 1), 0)).astype(F32)
            keep = (sc > thr) | ((sc == thr) & (k_pos <= cut_ref[0:1, :]))
            bias_ref[...] = jnp.where(keep, 0.0, NEG)

        kc = k_ref[pl.ds(off, CK), :]
        vtc = vt_ref[0, :, pl.ds(off, CK)]

        def qk(h):
            g = h // group
            return lax.dot_general(kc[:, g * HEAD_DIM:(g + 1) * HEAD_DIM],
                                   q_ref[:, h * HEAD_DIM:(h + 1) * HEAD_DIM], nt,
                                   preferred_element_type=F32)

        for h in range(n_heads):
            lg = qk(h) + bias_ref[...]
            m_old = m_ref[h:h + 1, :]
            m_new = jnp.maximum(m_old, lg.max(axis=0, keepdims=True))
            lg_ref[h] = lg
            alpha_ref[h:h + 1, :] = jnp.exp2(m_old - m_new)
            m_ref[h:h + 1, :] = m_new
        for h in range(n_heads):
            p = jnp.exp2(lg_ref[h] - m_ref[h:h + 1, :])
            l_ref[h:h + 1, :] = alpha_ref[h:h + 1, :] * l_ref[h:h + 1, :] + p.sum(axis=0, keepdims=True)
            p_ref[h] = p.astype(BF16)
        for h in range(n_heads):
            g = h // group
            pv = jnp.dot(vtc[g * HEAD_DIM:(g + 1) * HEAD_DIM, :], p_ref[h], preferred_element_type=F32)
            acc_ref[h] = alpha_ref[h:h + 1, :] * acc_ref[h] + pv
        return carry

    lax.fori_loop(0, nck, att_chunk, 0)

    for h in range(n_heads):
        o = (acc_ref[h] / l_ref[h:h + 1, :]).T
        sl = slice(h * HEAD_DIM, (h + 1) * HEAD_DIM)
        o_ref[:, sl] = (o * sga_ref[:, sl].astype(F32)).astype(BF16)


def _prompt_attention(zb, vt, wt, lay, b, s, d, topk):
    nq = s // TQ
    n_heads = d // HEAD_DIM
    kern = functools.partial(_attn_kernel, topk)
    row = lambda bb, i: bb * nq + i
    return pl.pallas_call(
        kern,
        grid=(b, nq),
        in_specs=[pl.BlockSpec((TQ, d), lambda bb, i: (row(bb, i), lay.q * TN // d)),
                  pl.BlockSpec((TQ, d), lambda bb, i: (row(bb, i), lay.qi * TN // d)),
                  pl.BlockSpec((TQ, d), lambda bb, i: (row(bb, i), lay.ga * TN // d)),
                  pl.BlockSpec((s, TN), lambda bb, i: (bb, lay.k)),
                  pl.BlockSpec((s, IDX_DIM), lambda bb, i: (bb, lay.misc * TN // IDX_DIM)),
                  pl.BlockSpec((1, TN, s), lambda bb, i: (bb, 0, 0)),
                  pl.BlockSpec((IDX_HEADS, TQ), lambda bb, i: (0, row(bb, i)))],
        out_specs=pl.BlockSpec((TQ, d), lambda bb, i: (row(bb, i), 0)),
        out_shape=jax.ShapeDtypeStruct((b * s, d), BF16),
        scratch_shapes=[pltpu.VMEM((s, TQ), F32),
                        pltpu.VMEM((CK, TQ), F32),
                        pltpu.VMEM((SUBLANES, TQ), F32),
                        pltpu.VMEM((n_heads, TQ), F32),
                        pltpu.VMEM((n_heads, TQ), F32),
                        pltpu.VMEM((n_heads, TQ), F32),
                        pltpu.VMEM((n_heads, HEAD_DIM, TQ), F32),
                        pltpu.VMEM((n_heads, CK, TQ), F32),
                        pltpu.VMEM((n_heads, CK, TQ), BF16)],
        compiler_params=_cparams(("arbitrary", "arbitrary")),
    )(zb, zb, zb, zb, zb, vt, wt)


def _pool_kernel(ts, u_ref, sgb_ref, wp_ref, sp_ref, o_ref, ext_ref):
    i = pl.program_id(1)
    halo = POOL_STATE + 1

    @pl.when(i == 0)
    def _():
        ext_ref[0:halo, :] = jnp.zeros((halo, ext_ref.shape[1]), F32)

    @pl.when(i > 0)
    def _():
        ext_ref[0:halo, :] = ext_ref[ts:ts + halo, :]

    ext_ref[halo:halo + ts, :] = u_ref[...]
    pos = i * ts + lax.broadcasted_iota(jnp.int32, (ts, 1), 0)
    gd = wp_ref.shape[1]
    for g, w in enumerate(POOL_WINDOWS):
        cols = slice(g * gd, (g + 1) * gd)
        acc = ext_ref[halo:halo + ts, cols]
        for jj in range(1, w):
            acc = acc + ext_ref[halo - jj:halo - jj + ts, cols]
        cnt = jnp.minimum(pos + 1, w).astype(F32)
        pooled = acc / cnt - u_ref[:, cols]
        mixed = jnp.dot(pooled.astype(BF16), wp_ref[g], preferred_element_type=F32) * sp_ref[:, cols]
        o_ref[:, cols] = (mixed * sgb_ref[:, cols].astype(F32)).astype(BF16)


def _pool_prompt(u, zb, wp, sp, lay, b, s, ts):
    m, pw = u.shape
    nt = s // ts
    kern = functools.partial(_pool_kernel, ts)
    return pl.pallas_call(
        kern,
        grid=(b, nt),
        in_specs=[pl.BlockSpec((ts, pw), lambda bb, i: (bb * nt + i, 0)),
                  pl.BlockSpec((ts, pw), lambda bb, i: (bb * nt + i, lay.gb * TN // pw)),
                  pl.BlockSpec(wp.shape, lambda bb, i: (0, 0, 0)),
                  pl.BlockSpec((1, pw), lambda bb, i: (0, 0))],
        out_specs=pl.BlockSpec((ts, pw), lambda bb, i: (bb * nt + i, 0)),
        out_shape=jax.ShapeDtypeStruct((m, pw), BF16),
        scratch_shapes=[pltpu.VMEM((POOL_STATE + 1 + ts, pw), F32)],
        compiler_params=_cparams(("arbitrary", "arbitrary")),
    )(u, zb, wp, sp)


def _pool_step_kernel(cnts, ext_ref, sgb_ref, wp_ref, sp_ref, o_ref):
    gd = wp_ref.shape[1]
    rows = ext_ref.shape[0]
    for g, w in enumerate(POOL_WINDOWS):
        cols = slice(g * gd, (g + 1) * gd)
        acc = ext_ref[rows - 1, :, cols]
        for jj in range(1, w):
            acc = acc + ext_ref[rows - 1 - jj, :, cols]
        pooled = acc / cnts[g] - ext_ref[rows - 1, :, cols]
        mixed = jnp.dot(pooled.astype(BF16), wp_ref[g], preferred_element_type=F32) * sp_ref[:, cols]
        o_ref[:, cols] = (mixed * sgb_ref[:, cols].astype(F32)).astype(BF16)


def _pool_step(ext_t, sgb, wp, sp, past):
    rows, n, pw = ext_t.shape
    cnts = tuple(float(min(past + 1, w)) for w in POOL_WINDOWS)
    return pl.pallas_call(
        functools.partial(_pool_step_kernel, cnts),
        out_shape=jax.ShapeDtypeStruct((n, pw), BF16),
    )(ext_t, sgb, wp, sp)


def _merge1_kernel(ab_ref, pb_ref, wa_ref, wb_ref, sma_ref, smb_ref, o_ref):
    a = jnp.dot(ab_ref[...], wa_ref[...], preferred_element_type=F32)
    bb = jnp.dot(pb_ref[...], wb_ref[...], preferred_element_type=F32)
    o_ref[...] = (sma_ref[...].astype(F32) * a + smb_ref[...].astype(F32) * bb).astype(BF16)


def _merge1(ab, pb, wa, wb, zb, lay, tm):
    m, d = ab.shape
    pw = pb.shape[1]
    return pl.pallas_call(
        _merge1_kernel,
        grid=(m // tm, d // TN),
        in_specs=[pl.BlockSpec((tm, d), lambda i, j: (i, 0)),
                  pl.BlockSpec((tm, pw), lambda i, j: (i, 0)),
                  pl.BlockSpec((d, TN), lambda i, j: (0, j)),
                  pl.BlockSpec((pw, TN), lambda i, j: (0, j)),
                  pl.BlockSpec((tm, TN), lambda i, j: (i, lay.ma + j)),
                  pl.BlockSpec((tm, TN), lambda i, j: (i, lay.mb + j))],
        out_specs=pl.BlockSpec((tm, TN), lambda i, j: (i, j)),
        out_shape=jax.ShapeDtypeStruct((m, d), BF16),
        compiler_params=_cparams(("arbitrary", "arbitrary")),
    )(ab, pb, wa, wb, zb, zb)


def _merge2_kernel(final, mm_ref, x_ref, gate_ref, wo_ref, gf_ref, o_ref):
    y = x_ref[...] + gate_ref[0] * jnp.dot(mm_ref[...], wo_ref[...], preferred_element_type=F32)
    if final:
        y = y * lax.rsqrt(jnp.mean(y * y, axis=-1, keepdims=True) + EPS) * gf_ref[...]
    o_ref[...] = y


def _merge2(mm, x, gate, wo, gf, tm, tiles_per_group, final):
    m, d = x.shape
    r = gate.shape[1]
    return pl.pallas_call(
        functools.partial(_merge2_kernel, final),
        grid=(m // tm,),
        in_specs=[pl.BlockSpec((tm, d), lambda i: (i, 0)),
                  pl.BlockSpec((tm, d), lambda i: (i, 0)),
                  pl.BlockSpec((1, r, d), lambda i: (i // tiles_per_group, 0, 0)),
                  pl.BlockSpec((d, d), lambda i: (0, 0)),
                  pl.BlockSpec((1, d), lambda i: (0, 0))],
        out_specs=pl.BlockSpec((tm, d), lambda i: (i, 0)),
        out_shape=jax.ShapeDtypeStruct((m, d), F32),
        compiler_params=_cparams(("arbitrary",)),
    )(mm, x, gate, wo, gf)


def _page_specs(width, npg, npp):
    return [pl.BlockSpec((1, PAGE_SIZE, width),
                         lambda b, p, pt, r=r: (pt[b * npg + p * npp + r], 0, 0)) for r in range(npp)]


def _sidx_kernel(npp, pt_ref, *refs):
    kidx_refs = refs[:npp]
    qi_ref, w_ref, o_ref = refs[npp:]
    nt = (((1,), (1,)), ((), ()))
    qi, w = qi_ref[0], w_ref[0]
    rows = []
    for r in range(npp):
        s = lax.dot_general(qi, kidx_refs[r][0].astype(BF16), nt, preferred_element_type=F32)
        rows.append(jnp.sum(w * jnp.maximum(s, 0.0), axis=0, keepdims=True))
    o_ref[0] = jnp.concatenate(rows, axis=1)


def _sample_scores(page_table, cache_kidx, qi3, wcol, npp):
    n, npg = page_table.shape
    return pl.pallas_call(
        functools.partial(_sidx_kernel, npp),
        grid_spec=pltpu.PrefetchScalarGridSpec(
            num_scalar_prefetch=1,
            grid=(n, npg // npp),
            in_specs=_page_specs(IDX_DIM, npg, npp) + [
                pl.BlockSpec((1, IDX_HEADS, IDX_DIM), lambda b, p, pt: (b, 0, 0)),
                pl.BlockSpec((1, IDX_HEADS, 1), lambda b, p, pt: (b, 0, 0))],
            out_specs=pl.BlockSpec((1, 1, npp * PAGE_SIZE), lambda b, p, pt: (b, 0, p))),
        out_shape=jax.ShapeDtypeStruct((n, 1, npg * PAGE_SIZE), F32),
        compiler_params=_cparams(("arbitrary", "arbitrary")),
    )(page_table.reshape(-1), *([cache_kidx] * npp), qi3, wcol)


def _sthr_kernel(topk, sc_ref, qi_ref, kin_ref, w_ref, pos_ref, cnt_ref, sel_ref, thr_scr, new_scr):
    sc = sc_ref[...]
    kin = kin_ref[...].astype(BF16).astype(F32)
    s_new = jnp.sum(qi_ref[...].astype(F32) * kin, axis=2, keepdims=True)
    new = jnp.sum(w_ref[...] * jnp.maximum(s_new, 0.0), axis=1, keepdims=True)
    red = lambda f, a: f(f(a, axis=2, keepdims=True), axis=1, keepdims=True)
    lo = jnp.minimum(red(jnp.min, sc), new)
    hi = jnp.maximum(red(jnp.max, sc), new)

    def count_ge(thr):
        c = red(jnp.sum, jnp.where(sc >= thr, 1.0, 0.0))
        return c + jnp.where(new >= thr, 1.0, 0.0)

    n_all = jnp.full(new.shape, float(sc.shape[1] * sc.shape[2] + 1), F32)
    thr, _ = _bisect(count_ge, lo, hi, n_all, float(topk))
    thr_scr[...] = jnp.broadcast_to(thr, thr_scr.shape)
    new_scr[...] = jnp.broadcast_to(new, new_scr.shape)

    rows = sc.shape[1]
    ii = lambda shape, ax: lax.broadcasted_iota(jnp.int32, shape, ax)
    one = lambda m: jnp.where(m, 1.0, 0.0)
    tri_lane = one(ii((PAGE_SIZE, PAGE_SIZE), 0) <= ii((PAGE_SIZE, PAGE_SIZE), 1)).astype(BF16)
    lower = one(ii((rows, rows), 1) < ii((rows, rows), 0)).astype(BF16)
    upper = one(ii((rows, rows), 0) < ii((rows, rows), 1)).astype(BF16)
    ones8 = jnp.ones((SUBLANES, PAGE_SIZE), BF16)
    nt = (((1,), (1,)), ((), ()))
    total = lambda a: jnp.sum(jnp.sum(a, axis=1, keepdims=True), axis=0, keepdims=True)
    dotf = lambda a, b: jnp.dot(a.astype(BF16), b.astype(BF16), preferred_element_type=F32)
    kf = float(topk)
    j_col = ii((topk, 1), 0).astype(F32)
    r_row = ii((1, rows), 1).astype(F32)
    c_row = ii((1, PAGE_SIZE), 1).astype(F32)

    def compact(b, carry):
        s = sc_ref[b]
        t = thr_scr[b][:, 0:1]
        nw = new_scr[b][:, 0:1]
        gt, eq = s > t, s == t
        need = kf - total(one(gt)) - one(nw > t)
        e_lane = dotf(one(eq), tri_lane)
        e_rank = e_lane + dotf(lower, e_lane)[:, PAGE_SIZE - 1:PAGE_SIZE]
        x = one(gt | (eq & (e_rank <= need)))
        sel_new = (nw > t) | ((nw == t) & (total(one(eq)) < need))
        x_lane = dotf(x, tri_lane)
        n_rows = lax.dot_general(ones8, x.astype(BF16), nt, preferred_element_type=F32)
        n_row = n_rows[0:1, :]
        start_row = dotf(n_rows, upper)[0:1, :]
        hit = one((start_row <= j_col) & (j_col < start_row + n_row))
        lane_sum = lambda a: jnp.sum(a, axis=1, keepdims=True)
        page = lane_sum(hit * r_row)
        target = j_col - lane_sum(hit * start_row) + 1.0
        in_row = dotf(hit, x_lane * x)
        off = lane_sum(jnp.where(in_row == target, c_row, 0.0))
        pos_ref[b] = (page * PAGE_SIZE + off).astype(jnp.int32)
        cnt_ref[b] = jnp.broadcast_to(total(x), cnt_ref.shape[1:])
        sel_ref[b] = jnp.broadcast_to(one(sel_new), sel_ref.shape[1:])
        return carry

    lax.fori_loop(0, sc.shape[0], compact, 0)


def _sample_select(sc, qi3, kin, wcol, topk):
    n = sc.shape[0]
    return pl.pallas_call(
        functools.partial(_sthr_kernel, topk),
        out_shape=[jax.ShapeDtypeStruct((n, topk, 1), jnp.int32),
                   jax.ShapeDtypeStruct((n, 1, PAGE_SIZE), F32),
                   jax.ShapeDtypeStruct((n, 1, PAGE_SIZE), F32)],
        scratch_shapes=[pltpu.VMEM((n, 1, PAGE_SIZE), F32),
                        pltpu.VMEM((n, 1, PAGE_SIZE), F32)],
        compiler_params=pltpu.CompilerParams(vmem_limit_bytes=VMEM_LIMIT),
    )(sc, qi3, kin, wcol)


def _row_copy(cache_hbm, buf, sem, layer, page, off, j):
    return pltpu.make_async_copy(cache_hbm.at[layer, page, off], buf.at[j], sem)


def _sattn_kernel(layer, npg, pos_ref, pt_ref, ck_hbm, cv_hbm, cnt_ref, sel_ref, q_ref, kn_ref, vn_ref,
                  sga_ref, o_ref, kbuf, vbuf, ksem, vsem):
    b = pl.program_id(0)
    topk = kbuf.shape[0]
    n_heads = q_ref.shape[1]
    group = n_heads // N_KV_HEADS
    nt = (((1,), (1,)), ((), ()))

    def issue(j, carry):
        pos = pos_ref[b * topk + j]
        page = pt_ref[b * npg + pos // PAGE_SIZE]
        off = pos % PAGE_SIZE
        _row_copy(ck_hbm, kbuf, ksem, layer, page, off, j).start()
        _row_copy(cv_hbm, vbuf, vsem, layer, page, off, j).start()
        return carry

    lax.fori_loop(0, topk, issue, 0)

    def wait(j, carry):
        _row_copy(ck_hbm, kbuf, ksem, layer, 0, 0, j).wait()
        _row_copy(cv_hbm, vbuf, vsem, layer, 0, 0, j).wait()
        return carry

    lax.fori_loop(0, topk, wait, 0)

    q = q_ref[0]
    head = lax.broadcasted_iota(jnp.int32, (n_heads, 1), 0)
    in_group = [(head >= g * group) & (head < (g + 1) * group) for g in range(N_KV_HEADS)]
    lg = jnp.zeros((n_heads, topk), F32)
    for g in range(N_KV_HEADS):
        lg_g = lax.dot_general(q, kbuf[:, g, :].astype(BF16), nt, preferred_element_type=F32)
        lg = lg + jnp.where(in_group[g], lg_g, 0.0)
    valid = lax.broadcasted_iota(jnp.int32, (1, topk), 1).astype(F32) < cnt_ref[0][:, 0:1]
    lg = jnp.where(valid, lg, NEG)
    kn = kn_ref[0].astype(BF16).astype(F32)
    vn = vn_ref[0].astype(BF16).astype(F32)
    lgn = jnp.sum(q.astype(F32) * kn, axis=1, keepdims=True)
    lgn = jnp.where(sel_ref[0][:, 0:1] > 0.0, lgn, NEG)
    m = jnp.maximum(lg.max(axis=1, keepdims=True), lgn)
    p = jnp.exp2(lg - m)
    pn = jnp.exp2(lgn - m)
    l = p.sum(axis=1, keepdims=True) + pn
    pb = p.astype(BF16)
    acc = pn.astype(BF16).astype(F32) * vn
    for g in range(N_KV_HEADS):
        pv = jnp.dot(pb, vbuf[:, g, :].astype(BF16), preferred_element_type=F32)
        acc = acc + jnp.where(in_group[g], pv, 0.0)
    o_ref[0] = ((acc / l) * sga_ref[0].astype(F32)).astype(BF16)


def _sample_attention(page_table, cache_k, cache_v, layer, pos, cnt, sel, q3, kn, vn, sga3):
    n, npg = page_table.shape
    n_heads = q3.shape[1]
    topk = pos.shape[1]
    per_b = lambda shape: pl.BlockSpec(shape, lambda b, ps, pt: (b, 0, 0))
    hbm = pl.BlockSpec(memory_space=pl.ANY)
    return pl.pallas_call(
        functools.partial(_sattn_kernel, layer, npg),
        grid_spec=pltpu.PrefetchScalarGridSpec(
            num_scalar_prefetch=2,
            grid=(n,),
            in_specs=[hbm, hbm,
                      per_b((1, 1, PAGE_SIZE)),
                      per_b((1, 1, PAGE_SIZE)),
                      per_b((1, n_heads, HEAD_DIM)),
                      per_b((1, n_heads, HEAD_DIM)),
                      per_b((1, n_heads, HEAD_DIM)),
                      per_b((1, n_heads, HEAD_DIM))],
            out_specs=per_b((1, n_heads, HEAD_DIM)),
            scratch_shapes=[pltpu.VMEM((topk, N_KV_HEADS, HEAD_DIM), cache_k.dtype),
                            pltpu.VMEM((topk, N_KV_HEADS, HEAD_DIM), cache_v.dtype),
                            pltpu.SemaphoreType.DMA(()),
                            pltpu.SemaphoreType.DMA(())]),
        out_shape=jax.ShapeDtypeStruct((n, n_heads, HEAD_DIM), BF16),
        compiler_params=_cparams(("arbitrary",)),
    )(pos.reshape(-1), page_table.reshape(-1), cache_k, cache_v, cnt, sel, q3, kn, vn, sga3)


def _rope_tables(pos):
    half = HEAD_DIM // 2
    inv = ROPE_THETA ** (-jnp.arange(half, dtype=F32) / half)
    ang = pos.astype(F32)[:, None] * inv[None, :]
    cos, sin = jnp.cos(ang), jnp.sin(ang)
    return jnp.concatenate([cos, cos], axis=1), jnp.concatenate([-sin, sin], axis=1)


def _tile_starts(d, lay):
    att, kv, idx, pool = d, N_KV_HEADS * HEAD_DIM, IDX_HEADS * IDX_DIM, d // 2
    sizes = (att, kv, kv, idx, IDX_DIM, IDX_HEADS, att, pool, pool, d, d)
    offs = [0]
    for n in sizes:
        offs.append(offs[-1] + n)
    q, k, v, qi, ki, wi, ga, u, gb, ma, mb = offs[:-1]
    starts = [0] * lay.end
    for first, tiles, col in ((lay.q, lay.qi - lay.q, q), (lay.qi, lay.ga - lay.qi, qi),
                              (lay.ga, lay.ma - lay.ga, ga), (lay.ma, lay.mb - lay.ma, ma),
                              (lay.mb, lay.gb - lay.mb, mb), (lay.gb, lay.k - lay.gb, gb),
                              (lay.k, 1, k), (lay.v, 1, v), (lay.misc, 1, ki), (lay.u, lay.end - lay.u, u)):
        for t in range(tiles):
            starts[first + t] = col + t * TN
    assert wi == ki + IDX_DIM and all(s + TN <= offs[-1] and s % SUBLANES == 0 for s in starts)
    return starts


def _pack_kernel(misc_tile, starts_ref, w_hbm, o_ref, buf, sem):
    t = pl.program_id(0)
    slot = t % 2

    def copy(tt, s):
        first = pl.multiple_of(starts_ref[tt], SUBLANES)
        return pltpu.make_async_copy(w_hbm.at[pl.ds(first, TN)], buf.at[s], sem.at[s])

    @pl.when(t == 0)
    def _():
        copy(t, slot).start()

    @pl.when(t + 1 < pl.num_programs(0))
    def _():
        copy(t + 1, 1 - slot).start()

    copy(t, slot).wait()
    x = buf[slot]
    row = lax.broadcasted_iota(jnp.int32, (TN, 1), 0)
    x = jnp.where((t == misc_tile) & (row >= IDX_DIM + IDX_HEADS), 0.0, x)
    o_ref[...] = x.T.astype(BF16)


def _pack_w_in(w_t, layer, in_width, d, lay):
    starts = jnp.asarray([layer * in_width + s for s in _tile_starts(d, lay)], jnp.int32)
    return pl.pallas_call(
        functools.partial(_pack_kernel, lay.misc),
        grid_spec=pltpu.PrefetchScalarGridSpec(
            num_scalar_prefetch=1,
            grid=(lay.end,),
            in_specs=[pl.BlockSpec(memory_space=pl.ANY)],
            out_specs=pl.BlockSpec((d, TN), lambda t, st: (0, t)),
            scratch_shapes=[pltpu.VMEM((2, TN, d), w_t.dtype),
                            pltpu.SemaphoreType.DMA((2,))]),
        out_shape=jax.ShapeDtypeStruct((d, lay.end * TN), BF16),
        compiler_params=_cparams(("arbitrary",)),
    )(starts, w_t)


def kernel(x_prompt, x_sample, c_prompt, c_sample, cache_k, cache_v, cache_kidx, state_pool, page_table,
           g_norm, w_ada, b_ada, w_in, w_pool, s_pool, w_a_proj, w_b_proj, w_out, g_final):
    bp, sp, d = x_prompt.shape
    ns, ds, _ = x_sample.shape
    depth = g_norm.shape[0]
    npg = page_table.shape[1]
    past = npg * PAGE_SIZE
    assert ds == 1 and d % (N_KV_HEADS * HEAD_DIM) == 0 and sp % TQ == 0 and sp >= POOL_STATE
    n_heads = d // HEAD_DIM
    kvw = N_KV_HEADS * HEAD_DIM
    lay = _Layout(d)
    pw = lay.pool
    idx_scale = (IDX_HEADS * IDX_DIM) ** -0.5
    topk_p = min(TOPK_MAX, sp // 4)
    topk_s = min(TOPK_MAX, (past + ds) // 4)
    tm = 512
    ts = 512
    tm2 = 256

    cos_p, sin_p = _rope_tables(jnp.arange(sp, dtype=jnp.int32))
    cos_s, sin_s = _rope_tables(jnp.full((ns,), past, jnp.int32))
    n_c = bp + ns
    c_rows = -(-n_c // SUBLANES) * SUBLANES
    c_all = jnp.concatenate([c_prompt, c_sample, jnp.zeros((c_rows - n_c, d), F32)], axis=0)

    n_pool = cache_kidx.shape[1]
    kidx_pages = cache_kidx.reshape(depth * n_pool, PAGE_SIZE, IDX_DIM)
    npp = PAGES_PER_STEP
    assert npg % npp == 0

    in_width = w_in.shape[2]
    w_t = jnp.swapaxes(w_in, 1, 2).reshape(depth * in_width, d)

    xp = x_prompt.reshape(bp * sp, d)
    xs = x_sample.reshape(ns, d)
    outs = {n: [] for n in ("kp", "vp", "kip", "pp", "ks", "vs", "kis", "ps")}
    for l in range(depth):
        final = l == depth - 1
        w_p = _pack_w_in(w_t, l, in_width, d, lay)
        wa, wb, wo, wpl = (w_a_proj[l].astype(BF16), w_b_proj[l].astype(BF16), w_out[l].astype(BF16),
                           w_pool[l].astype(BF16))
        g = g_norm[l].reshape(1, d)
        spl = s_pool[l].reshape(1, pw)
        gf = g_final.reshape(1, d)

        mod = _ada(c_all, w_ada[l], b_ada[l])
        shift, scale, gate = mod[:, :d], mod[:, d:2 * d], mod[:, 2 * d:]
        pg = lambda a: a[:bp].reshape(bp, 1, d)
        sg = lambda a: a[bp:n_c].reshape(1, ns, d)

        zb, k_p, v_p, misc_p, u_p, vt, wt = _project(xp, g, pg(shift), pg(scale), w_p, cos_p, sin_p, lay, tm,
                                                     sp // tm, sp // tm, idx_scale, True)
        ab = _prompt_attention(zb, vt, wt, lay, bp, sp, d, topk_p)
        pb = _pool_prompt(u_p, zb, wpl, spl, lay, bp, sp, ts)
        mm = _merge1(ab, pb, wa, wb, zb, lay, tm)
        xp = _merge2(mm, xp, pg(gate), wo, gf, tm2, sp // tm2, final)
        outs["kp"].append(k_p.reshape(bp, sp, N_KV_HEADS, HEAD_DIM))
        outs["vp"].append(v_p.reshape(bp, sp, N_KV_HEADS, HEAD_DIM))
        outs["kip"].append(misc_p[:, :IDX_DIM].reshape(bp, sp, IDX_DIM))
        outs["pp"].append(u_p.reshape(bp, sp, pw)[:, sp - POOL_STATE:, :])

        zs, k_s, v_s, misc_s, u_s = _project(xs, g, sg(shift), sg(scale), w_p, cos_s, sin_s, lay, ns,
                                             1, 1, idx_scale, False)
        col = lambda t, n: zs[:, t * TN:t * TN + n]
        q3 = col(lay.q, d).reshape(ns, n_heads, HEAD_DIM)
        qi3 = col(lay.qi, IDX_HEADS * IDX_DIM).reshape(ns, IDX_HEADS, IDX_DIM)
        sga3 = col(lay.ga, d).reshape(ns, n_heads, HEAD_DIM)
        wcol = misc_s[:, IDX_DIM:IDX_DIM + IDX_HEADS].reshape(ns, IDX_HEADS, 1)
        kin = misc_s[:, :IDX_DIM].reshape(ns, 1, IDX_DIM)
        pt_l = page_table + l * n_pool
        sc = _sample_scores(pt_l, kidx_pages, qi3, wcol, npp)
        pos, cnt, sel = _sample_select(sc.reshape(ns, npg, PAGE_SIZE), qi3, kin, wcol, topk_s)
        rep = lambda a: jnp.repeat(a.reshape(ns, N_KV_HEADS, HEAD_DIM), n_heads // N_KV_HEADS, axis=1)
        ab_s = _sample_attention(page_table, cache_k, cache_v, l, pos, cnt, sel, q3,
                                 rep(k_s), rep(v_s), sga3).reshape(ns, d)
        ext = jnp.concatenate([state_pool[l], u_s[:, None, :]], axis=1)
        pb_s = _pool_step(ext.transpose(1, 0, 2), col(lay.gb, pw), wpl, spl, past)
        mm_s = _merge1(ab_s, pb_s, wa, wb, zs, lay, ns)
        xs = _merge2(mm_s, xs, sg(gate), wo, gf, ns, 1, final)
        outs["ks"].append(k_s.reshape(ns, ds, N_KV_HEADS, HEAD_DIM))
        outs["vs"].append(v_s.reshape(ns, ds, N_KV_HEADS, HEAD_DIM))
        outs["kis"].append(misc_s[:, :IDX_DIM].reshape(ns, ds, IDX_DIM))
        outs["ps"].append(ext[:, 1:, :])

    st = lambda n: outs[n][0][None] if depth == 1 else jnp.stack(outs[n])
    return (xp.reshape(bp, sp, d), xs.reshape(ns, ds, d),
            st("kp"), st("vp"), st("kip"), st("pp"), st("ks"), st("vs"), st("kis"), st("ps"))
```

```python
import functools
import math

import jax
import jax.numpy as jnp
from jax import lax
from jax.experimental import pallas as pl
from jax.experimental.pallas import tpu as pltpu

F32 = jnp.float32
BF16 = jnp.bfloat16

HEAD_DIM = 128
N_KV_HEADS = 4
IDX_HEADS = 16
IDX_DIM = 128
TOPK_MAX = 256
PAGE_SIZE = 128
POOL_WINDOWS = (2, 4, 8, 16)
POOL_STATE = max(POOL_WINDOWS) - 1
ROPE_THETA = 10000.0
EPS = 1e-6

LANES = 128
SUBLANES = 8
TN = 512
TQ = 256
CK = 256
NEG = -1e30
Q_SCALE = 1.4426950408889634 * HEAD_DIM ** -0.5
PAGES_PER_STEP = 32
MAX_BISECT = 512
VMEM_LIMIT = 56 * 1024 * 1024


def _cparams(sem):
    return pltpu.CompilerParams(dimension_semantics=sem, vmem_limit_bytes=VMEM_LIMIT)


def _sigmoid(z):
    return 1.0 / (1.0 + jnp.exp(-z))


def _ada_kernel(c_ref, w_ref, b_ref, o_ref):
    o_ref[...] = jnp.dot(c_ref[...].astype(BF16), w_ref[...].astype(BF16),
                         preferred_element_type=F32) + b_ref[...]


def _ada(c, w, b):
    r, d = c.shape
    n = w.shape[1]
    return pl.pallas_call(
        _ada_kernel,
        grid=(n // TN,),
        in_specs=[pl.BlockSpec((r, d), lambda j: (0, 0)),
                  pl.BlockSpec((d, TN), lambda j: (0, j)),
                  pl.BlockSpec((1, TN), lambda j: (0, j))],
        out_specs=pl.BlockSpec((r, TN), lambda j: (0, j)),
        out_shape=jax.ShapeDtypeStruct((r, n), F32),
        compiler_params=_cparams(("arbitrary",)),
    )(c, w, b.reshape(1, n))


class _Layout:
    def __init__(self, d):
        att, kv, idx, pool = d, N_KV_HEADS * HEAD_DIM, IDX_HEADS * IDX_DIM, d // 2
        t = lambda n: n // TN
        assert att % TN == 0 and kv == TN and idx % TN == 0 and pool % TN == 0
        self.q = 0
        self.qi = self.q + t(att)
        self.ga = self.qi + t(idx)
        self.ma = self.ga + t(att)
        self.mb = self.ma + t(d)
        self.gb = self.mb + t(d)
        self.k = self.gb + t(pool)
        self.v = self.k + 1
        self.misc = self.v + 1
        self.u = self.misc + 1
        self.end = self.u + t(pool)
        self.nb = self.u
        self.pool = pool


def _rope_tile(z, cos, sin):
    parts = []
    for s in range(z.shape[1] // HEAD_DIM):
        zs = z[:, s * HEAD_DIM:(s + 1) * HEAD_DIM]
        parts.append(zs * cos + pltpu.roll(zs, HEAD_DIM // 2, 1) * sin)
    return parts[0] if len(parts) == 1 else jnp.concatenate(parts, axis=1)


def _proj_kernel(lay, idx_scale, q_scale, with_vt, x_ref, g_ref, shift_ref, scale_ref, w_ref, cos_ref, sin_ref,
                 zb_ref, k_ref, v_ref, misc_ref, u_ref, *rest):
    vt_ref, wt_ref = rest[:2] if with_vt else (None, None)
    h_ref = rest[-1]
    j = pl.program_id(1)

    @pl.when(j == 0)
    def _():
        x = x_ref[...]
        r = lax.rsqrt(jnp.mean(x * x, axis=-1, keepdims=True) + EPS)
        h = (x * r * g_ref[...]) * (1.0 + scale_ref[0]) + shift_ref[0]
        h_ref[...] = h.astype(BF16)

    def tile():
        return jnp.dot(h_ref[...], w_ref[...], preferred_element_type=F32)

    @pl.when(j < lay.qi)
    def _():
        zb_ref[...] = (_rope_tile(tile(), cos_ref[...], sin_ref[...]) * q_scale).astype(BF16)

    @pl.when((j >= lay.qi) & (j < lay.ga))
    def _():
        zb_ref[...] = _rope_tile(tile(), cos_ref[...], sin_ref[...]).astype(BF16)

    @pl.when(j == lay.k)
    def _():
        r = _rope_tile(tile(), cos_ref[...], sin_ref[...])
        zb_ref[...] = r.astype(BF16)
        k_ref[...] = r

    @pl.when(((j >= lay.ga) & (j < lay.ma)) | ((j >= lay.gb) & (j < lay.k)))
    def _():
        z = tile()
        zb_ref[...] = (z * _sigmoid(z)).astype(BF16)

    @pl.when((j >= lay.ma) & (j < lay.gb))
    def _():
        zb_ref[...] = _sigmoid(tile()).astype(BF16)

    @pl.when(j == lay.v)
    def _():
        z = tile()
        zb_ref[...] = z.astype(BF16)
        v_ref[...] = z
        if with_vt:
            vt_ref[0] = z.T.astype(BF16)

    @pl.when(j == lay.misc)
    def _():
        z = tile()
        ki = _rope_tile(z[:, :IDX_DIM], cos_ref[...], sin_ref[...])
        m = jnp.concatenate([ki, z[:, IDX_DIM:] * idx_scale], axis=1)
        zb_ref[...] = m.astype(BF16)
        misc_ref[...] = m
        if with_vt:
            wt_ref[...] = m[:, IDX_DIM:2 * IDX_DIM].T

    @pl.when(j >= lay.u)
    def _():
        u_ref[...] = tile()


def _project(x, g, shift, scale, w_p, cos, sin, lay, tm, tiles_per_group, tiles_per_pos, idx_scale, with_vt):
    m, d = x.shape
    grp, r, _ = shift.shape
    nu = lay.end - lay.u
    kern = functools.partial(_proj_kernel, lay, idx_scale, Q_SCALE, with_vt)
    out_specs = [pl.BlockSpec((tm, TN), lambda i, j: (i, jnp.minimum(j, lay.nb - 1))),
                 pl.BlockSpec((tm, TN), lambda i, j: (i, 0)),
                 pl.BlockSpec((tm, TN), lambda i, j: (i, 0)),
                 pl.BlockSpec((tm, TN), lambda i, j: (i, 0)),
                 pl.BlockSpec((tm, TN), lambda i, j: (i, jnp.clip(j - lay.u, 0, nu - 1)))]
    out_shape = [jax.ShapeDtypeStruct((m, lay.nb * TN), BF16),
                 jax.ShapeDtypeStruct((m, TN), F32),
                 jax.ShapeDtypeStruct((m, TN), F32),
                 jax.ShapeDtypeStruct((m, TN), F32),
                 jax.ShapeDtypeStruct((m, nu * TN), F32)]
    if with_vt:
        out_specs.append(pl.BlockSpec((1, TN, tm), lambda i, j: (i // tiles_per_group, 0, i % tiles_per_group)))
        out_shape.append(jax.ShapeDtypeStruct((grp, TN, tiles_per_group * tm), BF16))
        out_specs.append(pl.BlockSpec((IDX_DIM, tm), lambda i, j: (0, i)))
        out_shape.append(jax.ShapeDtypeStruct((IDX_DIM, m), F32))
    return pl.pallas_call(
        kern,
        grid=(m // tm, lay.end),
        in_specs=[pl.BlockSpec((tm, d), lambda i, j: (i, 0), pipeline_mode=pl.Buffered(1)),
                  pl.BlockSpec((1, d), lambda i, j: (0, 0)),
                  pl.BlockSpec((1, r, d), lambda i, j: (i // tiles_per_group, 0, 0)),
                  pl.BlockSpec((1, r, d), lambda i, j: (i // tiles_per_group, 0, 0)),
                  pl.BlockSpec((d, TN), lambda i, j: (0, j)),
                  pl.BlockSpec((tm, HEAD_DIM), lambda i, j: (i % tiles_per_pos, 0)),
                  pl.BlockSpec((tm, HEAD_DIM), lambda i, j: (i % tiles_per_pos, 0))],
        out_specs=out_specs,
        out_shape=out_shape,
        scratch_shapes=[pltpu.VMEM((tm, d), BF16)],
        compiler_params=_cparams(("arbitrary", "arbitrary")),
    )(x, g, shift, scale, w_p, cos, sin)


def _bisect(count_ge, lo, hi, clo, kf):
    def cond(st):
        return (st[4] > 0.0) & (st[5] < MAX_BISECT)

    def halve(lo, hi, clo, done):
        mid = 0.5 * lo + 0.5 * hi
        mid = jnp.where(mid <= lo, hi, mid)
        c = count_ge(mid)
        ge = c >= kf
        stuck = (mid >= hi) & jnp.logical_not(ge)
        lo2 = jnp.where(ge, mid, lo)
        hi2 = jnp.where(ge, hi, mid)
        clo2 = jnp.where(ge, c, clo)
        fin = (clo2 <= kf) | (lo2 >= hi2) | stuck
        return lo2, hi2, clo2, jnp.maximum(done, jnp.where(fin, 1.0, 0.0))

    def body(st):
        lo, hi, clo, done = halve(*halve(*st[:4]))
        return lo, hi, clo, done, jnp.sum(1.0 - done), st[5] + 1

    done0 = jnp.where((clo <= kf) | (lo >= hi), 1.0, 0.0)
    st = (lo, hi, clo, done0, jnp.sum(1.0 - done0), jnp.int32(0))
    out = lax.while_loop(cond, body, st)
    return out[0], out[2]


def _attn_kernel(topk, q_ref, qi_ref, sga_ref, k_ref, ki_ref, vt_ref, wt_ref, o_ref,
                 score_ref, bias_ref, cut_ref, m_ref, l_ref, alpha_ref, acc_ref, lg_ref, p_ref):
    i = pl.program_id(1)
    nck = i + 1
    n_heads = q_ref.shape[1] // HEAD_DIM
    group = n_heads // N_KV_HEADS
    nt = (((1,), (1,)), ((), ()))
    q_pos = i * TQ + lax.broadcasted_iota(jnp.int32, (1, TQ), 1)
    fold = lambda a: a.reshape(CK // SUBLANES, SUBLANES, TQ)

    def score_chunk(c, carry):
        mn, mx = carry
        off = pl.multiple_of(c * CK, CK)
        kc = ki_ref[pl.ds(off, CK), :]
        acc = jnp.zeros((CK, TQ), F32)
        for h in range(IDX_HEADS):
            s = lax.dot_general(kc, qi_ref[:, h * IDX_DIM:(h + 1) * IDX_DIM], nt,
                                preferred_element_type=F32)
            acc = acc + wt_ref[h:h + 1, :] * jnp.maximum(s, 0.0)
        k_pos = off + lax.broadcasted_iota(jnp.int32, (CK, 1), 0)
        causal = k_pos <= q_pos
        score_ref[pl.ds(off, CK), :] = jnp.where(causal, acc, -jnp.inf)
        mn = jnp.minimum(mn, fold(jnp.where(causal, acc, jnp.inf)).min(axis=0))
        mx = jnp.maximum(mx, fold(jnp.where(causal, acc, -jnp.inf)).max(axis=0))
        return mn, mx

    mn, mx = lax.fori_loop(0, nck, score_chunk,
                           (jnp.full((SUBLANES, TQ), jnp.inf, F32),
                            jnp.full((SUBLANES, TQ), -jnp.inf, F32)))
    lo = mn.min(axis=0, keepdims=True)
    hi = mx.max(axis=0, keepdims=True)

    def count_ge(thr):
        def body(c, cnt):
            s = score_ref[pl.ds(pl.multiple_of(c * CK, CK), CK), :]
            return cnt + fold(jnp.where(s >= thr, 1.0, 0.0)).sum(axis=0)
        cnt = lax.fori_loop(0, nck, body, jnp.zeros((SUBLANES, TQ), F32))
        return cnt.sum(axis=0, keepdims=True)

    n_valid = (q_pos + 1).astype(F32)
    kf = float(topk)
    thr, n_ge = _bisect(count_ge, lo, hi, n_valid, kf)

    s_len = score_ref.shape[0]
    has_ties = jnp.sum(jnp.where(n_ge > kf, 1.0, 0.0)) > 0.0
    cut_ref[...] = jnp.full(cut_ref.shape, float(s_len), F32)

    @pl.when(has_ties)
    def _():
        def count(pred):
            def body(c, cnt):
                off = pl.multiple_of(c * CK, CK)
                k_pos = (off + lax.broadcasted_iota(jnp.int32, (CK, 1), 0)).astype(F32)
                hit = pred(score_ref[pl.ds(off, CK), :], k_pos)
                return cnt + fold(jnp.where(hit, 1.0, 0.0)).sum(axis=0)
            cnt = lax.fori_loop(0, nck, body, jnp.zeros((SUBLANES, TQ), F32))
            return cnt.sum(axis=0, keepdims=True)

        need = kf - count(lambda s, kp: s > thr)

        def step(_, st):
            lo_i, hi_i = st
            mid = jnp.floor(0.5 * (lo_i + hi_i))
            ge = count(lambda s, kp: (s == thr) & (kp <= mid)) >= need
            return jnp.where(ge, lo_i, mid), jnp.where(ge, mid, hi_i)

        st0 = (jnp.full((1, TQ), -1.0, F32), jnp.full((1, TQ), s_len - 1.0, F32))
        cut = lax.fori_loop(0, (s_len - 1).bit_length() + 1, step, st0)[1]
        cut_ref[...] = jnp.broadcast_to(cut, cut_ref.shape)

    m_ref[...] = jnp.full(m_ref.shape, NEG, F32)
    l_ref[...] = jnp.zeros(l_ref.shape, F32)
    acc_ref[...] = jnp.zeros(acc_ref.shape, F32)

    def att_chunk(c, carry):
        off = pl.multiple_of(c * CK, CK)
        sc = score_ref[pl.ds(off, CK), :]

        @pl.when(jnp.logical_not(has_ties))
        def _():
            bias_ref[...] = jnp.where(sc >= thr, 0.0, NEG)

        @pl.when(has_ties)
        def _():
            k_pos = (off + lax.broadcasted_iota(jnp.int32, (CK, 1), 0)).astype(F32)
            keep = (sc > thr) | ((sc == thr) & (k_pos <= cut_ref[0:1, :]))
            bias_ref[...] = jnp.where(keep, 0.0, NEG)

        kc = k_ref[pl.ds(off, CK), :]
        vtc = vt_ref[0, :, pl.ds(off, CK)]

        def qk(h):
            g = h // group
            return lax.dot_general(kc[:, g * HEAD_DIM:(g + 1) * HEAD_DIM],
                                   q_ref[:, h * HEAD_DIM:(h + 1) * HEAD_DIM], nt,
                                   preferred_element_type=F32)

        for h in range(n_heads):
            lg = qk(h) + bias_ref[...]
            m_old = m_ref[h:h + 1, :]
            m_new = jnp.maximum(m_old, lg.max(axis=0, keepdims=True))
            lg_ref[h] = lg
            alpha_ref[h:h + 1, :] = jnp.exp2(m_old - m_new)
            m_ref[h:h + 1, :] = m_new
        for h in range(n_heads):
            p = jnp.exp2(lg_ref[h] - m_ref[h:h + 1, :])
            l_ref[h:h + 1, :] = alpha_ref[h:h + 1, :] * l_ref[h:h + 1, :] + p.sum(axis=0, keepdims=True)
            p_ref[h] = p.astype(BF16)
        for h in range(n_heads):
            g = h // group
            pv = jnp.dot(vtc[g * HEAD_DIM:(g + 1) * HEAD_DIM, :], p_ref[h], preferred_element_type=F32)
            acc_ref[h] = alpha_ref[h:h + 1, :] * acc_ref[h] + pv
        return carry

    lax.fori_loop(0, nck, att_chunk, 0)

    for h in range(n_heads):
        o = (acc_ref[h] / l_ref[h:h + 1, :]).T
        sl = slice(h * HEAD_DIM, (h + 1) * HEAD_DIM)
        o_ref[:, sl] = (o * sga_ref[:, sl].astype(F32)).astype(BF16)


def _prompt_attention(zb, vt, wt, lay, b, s, d, topk):
    nq = s // TQ
    n_heads = d // HEAD_DIM
    kern = functools.partial(_attn_kernel, topk)
    row = lambda bb, i: bb * nq + i
    return pl.pallas_call(
        kern,
        grid=(b, nq),
        in_specs=[pl.BlockSpec((TQ, d), lambda bb, i: (row(bb, i), lay.q * TN // d)),
                  pl.BlockSpec((TQ, d), lambda bb, i: (row(bb, i), lay.qi * TN // d)),
                  pl.BlockSpec((TQ, d), lambda bb, i: (row(bb, i), lay.ga * TN // d)),
                  pl.BlockSpec((s, TN), lambda bb, i: (bb, lay.k)),
                  pl.BlockSpec((s, IDX_DIM), lambda bb, i: (bb, lay.misc * TN // IDX_DIM)),
                  pl.BlockSpec((1, TN, s), lambda bb, i: (bb, 0, 0)),
                  pl.BlockSpec((IDX_HEADS, TQ), lambda bb, i: (0, row(bb, i)))],
        out_specs=pl.BlockSpec((TQ, d), lambda bb, i: (row(bb, i), 0)),
        out_shape=jax.ShapeDtypeStruct((b * s, d), BF16),
        scratch_shapes=[pltpu.VMEM((s, TQ), F32),
                        pltpu.VMEM((CK, TQ), F32),
                        pltpu.VMEM((SUBLANES, TQ), F32),
                        pltpu.VMEM((n_heads, TQ), F32),
                        pltpu.VMEM((n_heads, TQ), F32),
                        pltpu.VMEM((n_heads, TQ), F32),
                        pltpu.VMEM((n_heads, HEAD_DIM, TQ), F32),
                        pltpu.VMEM((n_heads, CK, TQ), F32),
                        pltpu.VMEM((n_heads, CK, TQ), BF16)],
        compiler_params=_cparams(("arbitrary", "arbitrary")),
    )(zb, zb, zb, zb, zb, vt, wt)


def _pool_kernel(ts, u_ref, sgb_ref, wp_ref, sp_ref, o_ref, ext_ref):
    i = pl.program_id(1)
    halo = POOL_STATE + 1

    @pl.when(i == 0)
    def _():
        ext_ref[0:halo, :] = jnp.zeros((halo, ext_ref.shape[1]), F32)

    @pl.when(i > 0)
    def _():
        ext_ref[0:halo, :] = ext_ref[ts:ts + halo, :]

    ext_ref[halo:halo + ts, :] = u_ref[...]
    pos = i * ts + lax.broadcasted_iota(jnp.int32, (ts, 1), 0)
    gd = wp_ref.shape[1]
    for g, w in enumerate(POOL_WINDOWS):
        cols = slice(g * gd, (g + 1) * gd)
        acc = ext_ref[halo:halo + ts, cols]
        for jj in range(1, w):
            acc = acc + ext_ref[halo - jj:halo - jj + ts, cols]
        cnt = jnp.minimum(pos + 1, w).astype(F32)
        pooled = acc / cnt - u_ref[:, cols]
        mixed = jnp.dot(pooled.astype(BF16), wp_ref[g], preferred_element_type=F32) * sp_ref[:, cols]
        o_ref[:, cols] = (mixed * sgb_ref[:, cols].astype(F32)).astype(BF16)


def _pool_prompt(u, zb, wp, sp, lay, b, s, ts):
    m, pw = u.shape
    nt = s // ts
    kern = functools.partial(_pool_kernel, ts)
    return pl.pallas_call(
        kern,
        grid=(b, nt),
        in_specs=[pl.BlockSpec((ts, pw), lambda bb, i: (bb * nt + i, 0)),
                  pl.BlockSpec((ts, pw), lambda bb, i: (bb * nt + i, lay.gb * TN // pw)),
                  pl.BlockSpec(wp.shape, lambda bb, i: (0, 0, 0)),
                  pl.BlockSpec((1, pw), lambda bb, i: (0, 0))],
        out_specs=pl.BlockSpec((ts, pw), lambda bb, i: (bb * nt + i, 0)),
        out_shape=jax.ShapeDtypeStruct((m, pw), BF16),
        scratch_shapes=[pltpu.VMEM((POOL_STATE + 1 + ts, pw), F32)],
        compiler_params=_cparams(("arbitrary", "arbitrary")),
    )(u, zb, wp, sp)


def _pool_step_kernel(cnts, ext_ref, sgb_ref, wp_ref, sp_ref, o_ref):
    gd = wp_ref.shape[1]
    rows = ext_ref.shape[0]
    for g, w in enumerate(POOL_WINDOWS):
        cols = slice(g * gd, (g + 1) * gd)
        acc = ext_ref[rows - 1, :, cols]
        for jj in range(1, w):
            acc = acc + ext_ref[rows - 1 - jj, :, cols]
        pooled = acc / cnts[g] - ext_ref[rows - 1, :, cols]
        mixed = jnp.dot(pooled.astype(BF16), wp_ref[g], preferred_element_type=F32) * sp_ref[:, cols]
        o_ref[:, cols] = (mixed * sgb_ref[:, cols].astype(F32)).astype(BF16)


def _pool_step(ext_t, sgb, wp, sp, past):
    rows, n, pw = ext_t.shape
    cnts = tuple(float(min(past + 1, w)) for w in POOL_WINDOWS)
    return pl.pallas_call(
        functools.partial(_pool_step_kernel, cnts),
        out_shape=jax.ShapeDtypeStruct((n, pw), BF16),
    )(ext_t, sgb, wp, sp)


def _merge1_kernel(ab_ref, pb_ref, wa_ref, wb_ref, sma_ref, smb_ref, o_ref):
    a = jnp.dot(ab_ref[...], wa_ref[...], preferred_element_type=F32)
    bb = jnp.dot(pb_ref[...], wb_ref[...], preferred_element_type=F32)
    o_ref[...] = (sma_ref[...].astype(F32) * a + smb_ref[...].astype(F32) * bb).astype(BF16)


def _merge1(ab, pb, wa, wb, zb, lay, tm):
    m, d = ab.shape
    pw = pb.shape[1]
    return pl.pallas_call(
        _merge1_kernel,
        grid=(m // tm, d // TN),
        in_specs=[pl.BlockSpec((tm, d), lambda i, j: (i, 0)),
                  pl.BlockSpec((tm, pw), lambda i, j: (i, 0)),
                  pl.BlockSpec((d, TN), lambda i, j: (0, j)),
                  pl.BlockSpec((pw, TN), lambda i, j: (0, j)),
                  pl.BlockSpec((tm, TN), lambda i, j: (i, lay.ma + j)),
                  pl.BlockSpec((tm, TN), lambda i, j: (i, lay.mb + j))],
        out_specs=pl.BlockSpec((tm, TN), lambda i, j: (i, j)),
        out_shape=jax.ShapeDtypeStruct((m, d), BF16),
        compiler_params=_cparams(("arbitrary", "arbitrary")),
    )(ab, pb, wa, wb, zb, zb)


def _merge2_kernel(final, mm_ref, x_ref, gate_ref, wo_ref, gf_ref, o_ref):
    y = x_ref[...] + gate_ref[0] * jnp.dot(mm_ref[...], wo_ref[...], preferred_element_type=F32)
    if final:
        y = y * lax.rsqrt(jnp.mean(y * y, axis=-1, keepdims=True) + EPS) * gf_ref[...]
    o_ref[...] = y


def _merge2(mm, x, gate, wo, gf, tm, tiles_per_group, final):
    m, d = x.shape
    r = gate.shape[1]
    return pl.pallas_call(
        functools.partial(_merge2_kernel, final),
        grid=(m // tm,),
        in_specs=[pl.BlockSpec((tm, d), lambda i: (i, 0)),
                  pl.BlockSpec((tm, d), lambda i: (i, 0)),
                  pl.BlockSpec((1, r, d), lambda i: (i // tiles_per_group, 0, 0)),
                  pl.BlockSpec((d, d), lambda i: (0, 0)),
                  pl.BlockSpec((1, d), lambda i: (0, 0))],
        out_specs=pl.BlockSpec((tm, d), lambda i: (i, 0)),
        out_shape=jax.ShapeDtypeStruct((m, d), F32),
        compiler_params=_cparams(("arbitrary",)),
    )(mm, x, gate, wo, gf)


def _page_specs(width, npg, npp):
    return [pl.BlockSpec((1, PAGE_SIZE, width),
                         lambda b, p, pt, r=r: (pt[b * npg + p * npp + r], 0, 0)) for r in range(npp)]


def _sidx_kernel(npp, pt_ref, *refs):
    kidx_refs = refs[:npp]
    qi_ref, w_ref, o_ref = refs[npp:]
    nt = (((1,), (1,)), ((), ()))
    qi, w = qi_ref[0], w_ref[0]
    rows = []
    for r in range(npp):
        s = lax.dot_general(qi, kidx_refs[r][0].astype(BF16), nt, preferred_element_type=F32)
        rows.append(jnp.sum(w * jnp.maximum(s, 0.0), axis=0, keepdims=True))
    o_ref[0] = jnp.concatenate(rows, axis=1)


def _sample_scores(page_table, cache_kidx, qi3, wcol, npp):
    n, npg = page_table.shape
    return pl.pallas_call(
        functools.partial(_sidx_kernel, npp),
        grid_spec=pltpu.PrefetchScalarGridSpec(
            num_scalar_prefetch=1,
            grid=(n, npg // npp),
            in_specs=_page_specs(IDX_DIM, npg, npp) + [
                pl.BlockSpec((1, IDX_HEADS, IDX_DIM), lambda b, p, pt: (b, 0, 0)),
                pl.BlockSpec((1, IDX_HEADS, 1), lambda b, p, pt: (b, 0, 0))],
            out_specs=pl.BlockSpec((1, 1, npp * PAGE_SIZE), lambda b, p, pt: (b, 0, p))),
        out_shape=jax.ShapeDtypeStruct((n, 1, npg * PAGE_SIZE), F32),
        compiler_params=_cparams(("arbitrary", "arbitrary")),
    )(page_table.reshape(-1), *([cache_kidx] * npp), qi3, wcol)


def _sthr_kernel(topk, sc_ref, qi_ref, kin_ref, w_ref, pos_ref, cnt_ref, sel_ref, thr_scr, new_scr):
    sc = sc_ref[...]
    kin = kin_ref[...].astype(BF16).astype(F32)
    s_new = jnp.sum(qi_ref[...].astype(F32) * kin, axis=2, keepdims=True)
    new = jnp.sum(w_ref[...] * jnp.maximum(s_new, 0.0), axis=1, keepdims=True)
    red = lambda f, a: f(f(a, axis=2, keepdims=True), axis=1, keepdims=True)
    lo = jnp.minimum(red(jnp.min, sc), new)
    hi = jnp.maximum(red(jnp.max, sc), new)

    def count_ge(thr):
        c = red(jnp.sum, jnp.where(sc >= thr, 1.0, 0.0))
        return c + jnp.where(new >= thr, 1.0, 0.0)

    n_all = jnp.full(new.shape, float(sc.shape[1] * sc.shape[2] + 1), F32)
    thr, _ = _bisect(count_ge, lo, hi, n_all, float(topk))
    thr_scr[...] = jnp.broadcast_to(thr, thr_scr.shape)
    new_scr[...] = jnp.broadcast_to(new, new_scr.shape)

    rows = sc.shape[1]
    ii = lambda shape, ax: lax.broadcasted_iota(jnp.int32, shape, ax)
    one = lambda m: jnp.where(m, 1.0, 0.0)
    tri_lane = one(ii((PAGE_SIZE, PAGE_SIZE), 0) <= ii((PAGE_SIZE, PAGE_SIZE), 1)).astype(BF16)
    lower = one(ii((rows, rows), 1) < ii((rows, rows), 0)).astype(BF16)
    upper = one(ii((rows, rows), 0) < ii((rows, rows), 1)).astype(BF16)
    ones8 = jnp.ones((SUBLANES, PAGE_SIZE), BF16)
    nt = (((1,), (1,)), ((), ()))
    total = lambda a: jnp.sum(jnp.sum(a, axis=1, keepdims=True), axis=0, keepdims=True)
    dotf = lambda a, b: jnp.dot(a.astype(BF16), b.astype(BF16), preferred_element_type=F32)
    kf = float(topk)
    j_col = ii((topk, 1), 0).astype(F32)
    r_row = ii((1, rows), 1).astype(F32)
    c_row = ii((1, PAGE_SIZE), 1).astype(F32)

    def compact(b, carry):
        s = sc_ref[b]
        t = thr_scr[b][:, 0:1]
        nw = new_scr[b][:, 0:1]
        gt, eq = s > t, s == t
        need = kf - total(one(gt)) - one(nw > t)
        e_lane = dotf(one(eq), tri_lane)
        e_rank = e_lane + dotf(lower, e_lane)[:, PAGE_SIZE - 1:PAGE_SIZE]
        x = one(gt | (eq & (e_rank <= need)))
        sel_new = (nw > t) | ((nw == t) & (total(one(eq)) < need))
        x_lane = dotf(x, tri_lane)
        n_rows = lax.dot_general(ones8, x.astype(BF16), nt, preferred_element_type=F32)
        n_row = n_rows[0:1, :]
        start_row = dotf(n_rows, upper)[0:1, :]
        hit = one((start_row <= j_col) & (j_col < start_row + n_row))
        lane_sum = lambda a: jnp.sum(a, axis=1, keepdims=True)
        page = lane_sum(hit * r_row)
        target = j_col - lane_sum(hit * start_row) + 1.0
        in_row = dotf(hit, x_lane * x)
        off = lane_sum(jnp.where(in_row == target, c_row, 0.0))
        pos_ref[b] = (page * PAGE_SIZE + off).astype(jnp.int32)
        cnt_ref[b] = jnp.broadcast_to(total(x), cnt_ref.shape[1:])
        sel_ref[b] = jnp.broadcast_to(one(sel_new), sel_ref.shape[1:])
        return carry

    lax.fori_loop(0, sc.shape[0], compact, 0)


def _sample_select(sc, qi3, kin, wcol, topk):
    n = sc.shape[0]
    return pl.pallas_call(
        functools.partial(_sthr_kernel, topk),
        out_shape=[jax.ShapeDtypeStruct((n, topk, 1), jnp.int32),
                   jax.ShapeDtypeStruct((n, 1, PAGE_SIZE), F32),
                   jax.ShapeDtypeStruct((n, 1, PAGE_SIZE), F32)],
        scratch_shapes=[pltpu.VMEM((n, 1, PAGE_SIZE), F32),
                        pltpu.VMEM((n, 1, PAGE_SIZE), F32)],
        compiler_params=pltpu.CompilerParams(vmem_limit_bytes=VMEM_LIMIT),
    )(sc, qi3, kin, wcol)


def _row_copy(cache_hbm, buf, sem, layer, page, off, j):
    return pltpu.make_async_copy(cache_hbm.at[layer, page, off], buf.at[j], sem)


def _sattn_kernel(layer, npg, pos_ref, pt_ref, ck_hbm, cv_hbm, cnt_ref, sel_ref, q_ref, kn_ref, vn_ref,
                  sga_ref, o_ref, kbuf, vbuf, ksem, vsem):
    b = pl.program_id(0)
    slot = b % 2
    topk = kbuf.shape[1]
    n_heads = q_ref.shape[1]
    group = n_heads // N_KV_HEADS
    nt = (((1,), (1,)), ((), ()))

    def fetch(bb, s):
        def issue(j, carry):
            pos = pos_ref[bb * topk + j]
            page = pt_ref[bb * npg + pos // PAGE_SIZE]
            off = pos % PAGE_SIZE
            _row_copy(ck_hbm, kbuf.at[s], ksem.at[s], layer, page, off, j).start()
            _row_copy(cv_hbm, vbuf.at[s], vsem.at[s], layer, page, off, j).start()
            return carry
        lax.fori_loop(0, topk, issue, 0)

    @pl.when(b == 0)
    def _():
        fetch(b, slot)

    @pl.when(b + 1 < pl.num_programs(0))
    def _():
        fetch(b + 1, 1 - slot)

    def wait(j, carry):
        _row_copy(ck_hbm, kbuf.at[slot], ksem.at[slot], layer, 0, 0, j).wait()
        _row_copy(cv_hbm, vbuf.at[slot], vsem.at[slot], layer, 0, 0, j).wait()
        return carry

    lax.fori_loop(0, topk, wait, 0)
    kbuf, vbuf = kbuf.at[slot], vbuf.at[slot]

    q = q_ref[0]
    head = lax.broadcasted_iota(jnp.int32, (n_heads, 1), 0)
    in_group = [(head >= g * group) & (head < (g + 1) * group) for g in range(N_KV_HEADS)]
    lg = jnp.zeros((n_heads, topk), F32)
    for g in range(N_KV_HEADS):
        lg_g = lax.dot_general(q, kbuf[:, g, :].astype(BF16), nt, preferred_element_type=F32)
        lg = lg + jnp.where(in_group[g], lg_g, 0.0)
    valid = lax.broadcasted_iota(jnp.int32, (1, topk), 1).astype(F32) < cnt_ref[0][:, 0:1]
    lg = jnp.where(valid, lg, NEG)
    kn = kn_ref[0].astype(BF16).astype(F32)
    vn = vn_ref[0].astype(BF16).astype(F32)
    lgn = jnp.sum(q.astype(F32) * kn, axis=1, keepdims=True)
    lgn = jnp.where(sel_ref[0][:, 0:1] > 0.0, lgn, NEG)
    m = jnp.maximum(lg.max(axis=1, keepdims=True), lgn)
    p = jnp.exp2(lg - m)
    pn = jnp.exp2(lgn - m)
    l = p.sum(axis=1, keepdims=True) + pn
    pb = p.astype(BF16)
    acc = pn.astype(BF16).astype(F32) * vn
    for g in range(N_KV_HEADS):
        pv = jnp.dot(pb, vbuf[:, g, :].astype(BF16), preferred_element_type=F32)
        acc = acc + jnp.where(in_group[g], pv, 0.0)
    o_ref[0] = ((acc / l) * sga_ref[0].astype(F32)).astype(BF16)


def _sample_attention(page_table, cache_k, cache_v, layer, pos, cnt, sel, q3, kn, vn, sga3):
    n, npg = page_table.shape
    n_heads = q3.shape[1]
    topk = pos.shape[1]
    per_b = lambda shape: pl.BlockSpec(shape, lambda b, ps, pt: (b, 0, 0))
    hbm = pl.BlockSpec(memory_space=pl.ANY)
    return pl.pallas_call(
        functools.partial(_sattn_kernel, layer, npg),
        grid_spec=pltpu.PrefetchScalarGridSpec(
            num_scalar_prefetch=2,
            grid=(n,),
            in_specs=[hbm, hbm,
                      per_b((1, 1, PAGE_SIZE)),
                      per_b((1, 1, PAGE_SIZE)),
                      per_b((1, n_heads, HEAD_DIM)),
                      per_b((1, n_heads, HEAD_DIM)),
                      per_b((1, n_heads, HEAD_DIM)),
                      per_b((1, n_heads, HEAD_DIM))],
            out_specs=per_b((1, n_heads, HEAD_DIM)),
            scratch_shapes=[pltpu.VMEM((2, topk, N_KV_HEADS, HEAD_DIM), cache_k.dtype),
                            pltpu.VMEM((2, topk, N_KV_HEADS, HEAD_DIM), cache_v.dtype),
                            pltpu.SemaphoreType.DMA((2,)),
                            pltpu.SemaphoreType.DMA((2,))]),
        out_shape=jax.ShapeDtypeStruct((n, n_heads, HEAD_DIM), BF16),
        compiler_params=_cparams(("arbitrary",)),
    )(pos.reshape(-1), page_table.reshape(-1), cache_k, cache_v, cnt, sel, q3, kn, vn, sga3)


def _rope_tables(pos):
    half = HEAD_DIM // 2
    inv = ROPE_THETA ** (-jnp.arange(half, dtype=F32) / half)
    ang = pos.astype(F32)[:, None] * inv[None, :]
    cos, sin = jnp.cos(ang), jnp.sin(ang)
    return jnp.concatenate([cos, cos], axis=1), jnp.concatenate([-sin, sin], axis=1)


def _tile_starts(d, lay):
    att, kv, idx, pool = d, N_KV_HEADS * HEAD_DIM, IDX_HEADS * IDX_DIM, d // 2
    sizes = (att, kv, kv, idx, IDX_DIM, IDX_HEADS, att, pool, pool, d, d)
    offs = [0]
    for n in sizes:
        offs.append(offs[-1] + n)
    q, k, v, qi, ki, wi, ga, u, gb, ma, mb = offs[:-1]
    starts = [0] * lay.end
    for first, tiles, col in ((lay.q, lay.qi - lay.q, q), (lay.qi, lay.ga - lay.qi, qi),
                              (lay.ga, lay.ma - lay.ga, ga), (lay.ma, lay.mb - lay.ma, ma),
                              (lay.mb, lay.gb - lay.mb, mb), (lay.gb, lay.k - lay.gb, gb),
                              (lay.k, 1, k), (lay.v, 1, v), (lay.misc, 1, ki), (lay.u, lay.end - lay.u, u)):
        for t in range(tiles):
            starts[first + t] = col + t * TN
    assert wi == ki + IDX_DIM and all(s + TN <= offs[-1] and s % SUBLANES == 0 for s in starts)
    return starts


def _pack_kernel(misc_tile, starts_ref, w_hbm, o_ref, buf, sem):
    t = pl.program_id(0)
    slot = t % 2

    def copy(tt, s):
        first = pl.multiple_of(starts_ref[tt], SUBLANES)
        return pltpu.make_async_copy(w_hbm.at[pl.ds(first, TN)], buf.at[s], sem.at[s])

    @pl.when(t == 0)
    def _():
        copy(t, slot).start()

    @pl.when(t + 1 < pl.num_programs(0))
    def _():
        copy(t + 1, 1 - slot).start()

    copy(t, slot).wait()
    x = buf[slot]
    row = lax.broadcasted_iota(jnp.int32, (TN, 1), 0)
    x = jnp.where((t == misc_tile) & (row >= IDX_DIM + IDX_HEADS), 0.0, x)
    o_ref[...] = x.T.astype(BF16)


def _pack_w_in(w_t, layer, in_width, d, lay):
    starts = jnp.asarray([layer * in_width + s for s in _tile_starts(d, lay)], jnp.int32)
    return pl.pallas_call(
        functools.partial(_pack_kernel, lay.misc),
        grid_spec=pltpu.PrefetchScalarGridSpec(
            num_scalar_prefetch=1,
            grid=(lay.end,),
            in_specs=[pl.BlockSpec(memory_space=pl.ANY)],
            out_specs=pl.BlockSpec((d, TN), lambda t, st: (0, t)),
            scratch_shapes=[pltpu.VMEM((2, TN, d), w_t.dtype),
                            pltpu.SemaphoreType.DMA((2,))]),
        out_shape=jax.ShapeDtypeStruct((d, lay.end * TN), BF16),
        compiler_params=_cparams(("arbitrary",)),
    )(starts, w_t)


def kernel(x_prompt, x_sample, c_prompt, c_sample, cache_k, cache_v, cache_kidx, state_pool, page_table,
           g_norm, w_ada, b_ada, w_in, w_pool, s_pool, w_a_proj, w_b_proj, w_out, g_final):
    bp, sp, d = x_prompt.shape
    ns, ds, _ = x_sample.shape
    depth = g_norm.shape[0]
    npg = page_table.shape[1]
    past = npg * PAGE_SIZE
    assert ds == 1 and d % (N_KV_HEADS * HEAD_DIM) == 0 and sp % TQ == 0 and sp >= POOL_STATE
    n_heads = d // HEAD_DIM
    kvw = N_KV_HEADS * HEAD_DIM
    lay = _Layout(d)
    pw = lay.pool
    idx_scale = (IDX_HEADS * IDX_DIM) ** -0.5
    topk_p = min(TOPK_MAX, sp // 4)
    topk_s = min(TOPK_MAX, (past + ds) // 4)
    tm = 1024
    ts = 512
    tm2 = 256

    cos_p, sin_p = _rope_tables(jnp.arange(sp, dtype=jnp.int32))
    cos_s, sin_s = _rope_tables(jnp.full((ns,), past, jnp.int32))
    n_c = bp + ns
    c_rows = -(-n_c // SUBLANES) * SUBLANES
    c_all = jnp.concatenate([c_prompt, c_sample, jnp.zeros((c_rows - n_c, d), F32)], axis=0)

    n_pool = cache_kidx.shape[1]
    kidx_pages = cache_kidx.reshape(depth * n_pool, PAGE_SIZE, IDX_DIM)
    npp = math.gcd(npg, PAGES_PER_STEP)
    assert npg % npp == 0

    in_width = w_in.shape[2]
    w_t = jnp.swapaxes(w_in, 1, 2).reshape(depth * in_width, d)

    xp = x_prompt.reshape(bp * sp, d)
    xs = x_sample.reshape(ns, d)
    outs = {n: [] for n in ("kp", "vp", "kip", "pp", "ks", "vs", "kis", "ps")}
    for l in range(depth):
        final = l == depth - 1
        w_p = _pack_w_in(w_t, l, in_width, d, lay)
        wa, wb, wo, wpl = (w_a_proj[l].astype(BF16), w_b_proj[l].astype(BF16), w_out[l].astype(BF16),
                           w_pool[l].astype(BF16))
        g = g_norm[l].reshape(1, d)
        spl = s_pool[l].reshape(1, pw)
        gf = g_final.reshape(1, d)

        mod = _ada(c_all, w_ada[l], b_ada[l])
        shift, scale, gate = mod[:, :d], mod[:, d:2 * d], mod[:, 2 * d:]
        pg = lambda a: a[:bp].reshape(bp, 1, d)
        sg = lambda a: a[bp:n_c].reshape(1, ns, d)

        zb, k_p, v_p, misc_p, u_p, vt, wt = _project(xp, g, pg(shift), pg(scale), w_p, cos_p, sin_p, lay, tm,
                                                     sp // tm, sp // tm, idx_scale, True)
        ab = _prompt_attention(zb, vt, wt, lay, bp, sp, d, topk_p)
        pb = _pool_prompt(u_p, zb, wpl, spl, lay, bp, sp, ts)
        mm = _merge1(ab, pb, wa, wb, zb, lay, tm)
        xp = _merge2(mm, xp, pg(gate), wo, gf, tm2, sp // tm2, final)
        outs["kp"].append(k_p.reshape(bp, sp, N_KV_HEADS, HEAD_DIM))
        outs["vp"].append(v_p.reshape(bp, sp, N_KV_HEADS, HEAD_DIM))
        outs["kip"].append(misc_p[:, :IDX_DIM].reshape(bp, sp, IDX_DIM))
        outs["pp"].append(u_p.reshape(bp, sp, pw)[:, sp - POOL_STATE:, :])

        zs, k_s, v_s, misc_s, u_s = _project(xs, g, sg(shift), sg(scale), w_p, cos_s, sin_s, lay, ns,
                                             1, 1, idx_scale, False)
        col = lambda t, n: zs[:, t * TN:t * TN + n]
        q3 = col(lay.q, d).reshape(ns, n_heads, HEAD_DIM)
        qi3 = col(lay.qi, IDX_HEADS * IDX_DIM).reshape(ns, IDX_HEADS, IDX_DIM)
        sga3 = col(lay.ga, d).reshape(ns, n_heads, HEAD_DIM)
        wcol = misc_s[:, IDX_DIM:IDX_DIM + IDX_HEADS].reshape(ns, IDX_HEADS, 1)
        kin = misc_s[:, :IDX_DIM].reshape(ns, 1, IDX_DIM)
        pt_l = page_table + l * n_pool
        sc = _sample_scores(pt_l, kidx_pages, qi3, wcol, npp)
        pos, cnt, sel = _sample_select(sc.reshape(ns, npg, PAGE_SIZE), qi3, kin, wcol, topk_s)
        rep = lambda a: jnp.repeat(a.reshape(ns, N_KV_HEADS, HEAD_DIM), n_heads // N_KV_HEADS, axis=1)
        ab_s = _sample_attention(page_table, cache_k, cache_v, l, pos, cnt, sel, q3,
                                 rep(k_s), rep(v_s), sga3).reshape(ns, d)
        ext = jnp.concatenate([state_pool[l], u_s[:, None, :]], axis=1)
        pb_s = _pool_step(ext.transpose(1, 0, 2), col(lay.gb, pw), wpl, spl, past)
        mm_s = _merge1(ab_s, pb_s, wa, wb, zs, lay, ns)
        xs = _merge2(mm_s, xs, sg(gate), wo, gf, ns, 1, final)
        outs["ks"].append(k_s.reshape(ns, ds, N_KV_HEADS, HEAD_DIM))
        outs["vs"].append(v_s.reshape(ns, ds, N_KV_HEADS, HEAD_DIM))
        outs["kis"].append(misc_s[:, :IDX_DIM].reshape(ns, ds, IDX_DIM))
        outs["ps"].append(ext[:, 1:, :])

    st = lambda n: outs[n][0][None] if depth == 1 else jnp.stack(outs[n])
    return (xp.reshape(bp, sp, d), xs.reshape(ns, ds, d),
            st("kp"), st("vp"), st("kip"), st("pp"), st("ks"), st("vs"), st("kis"), st("ps"))
```

```python
import functools
import math

import jax
import jax.numpy as jnp
from jax import lax
from jax.experimental import pallas as pl
from jax.experimental.pallas import tpu as pltpu

F32 = jnp.float32
BF16 = jnp.bfloat16

HEAD_DIM = 128
N_KV_HEADS = 4
IDX_HEADS = 16
IDX_DIM = 128
TOPK_MAX = 256
PAGE_SIZE = 128
POOL_WINDOWS = (2, 4, 8, 16)
POOL_STATE = max(POOL_WINDOWS) - 1
ROPE_THETA = 10000.0
EPS = 1e-6

LANES = 128
SUBLANES = 8
TN = 512
TQ = 256
CK = 256
NEG = -1e30
Q_SCALE = 1.4426950408889634 * HEAD_DIM ** -0.5
PAGES_PER_STEP = 32
MAX_BISECT = 512
VMEM_LIMIT = 56 * 1024 * 1024


def _cparams(sem):
    return pltpu.CompilerParams(dimension_semantics=sem, vmem_limit_bytes=VMEM_LIMIT)


def _sigmoid(z):
    return 0.5 * jnp.tanh(0.5 * z) + 0.5


def _ada_kernel(c_ref, w_ref, b_ref, o_ref):
    o_ref[...] = jnp.dot(c_ref[...].astype(BF16), w_ref[...].astype(BF16),
                         preferred_element_type=F32) + b_ref[...]


def _ada(c, w, b):
    r, d = c.shape
    n = w.shape[1]
    return pl.pallas_call(
        _ada_kernel,
        grid=(n // TN,),
        in_specs=[pl.BlockSpec((r, d), lambda j: (0, 0)),
                  pl.BlockSpec((d, TN), lambda j: (0, j)),
                  pl.BlockSpec((1, TN), lambda j: (0, j))],
        out_specs=pl.BlockSpec((r, TN), lambda j: (0, j)),
        out_shape=jax.ShapeDtypeStruct((r, n), F32),
        compiler_params=_cparams(("arbitrary",)),
    )(c, w, b.reshape(1, n))


class _Layout:
    def __init__(self, d):
        att, kv, idx, pool = d, N_KV_HEADS * HEAD_DIM, IDX_HEADS * IDX_DIM, d // 2
        t = lambda n: n // TN
        assert att % TN == 0 and kv == TN and idx % TN == 0 and pool % TN == 0
        self.q = 0
        self.qi = self.q + t(att)
        self.ga = self.qi + t(idx)
        self.ma = self.ga + t(att)
        self.mb = self.ma + t(d)
        self.gb = self.mb + t(d)
        self.k = self.gb + t(pool)
        self.v = self.k + 1
        self.misc = self.v + 1
        self.u = self.misc + 1
        self.end = self.u + t(pool)
        self.nb = self.u
        self.pool = pool


def _rope_tile(z, cos, sin):
    parts = []
    for s in range(z.shape[1] // HEAD_DIM):
        zs = z[:, s * HEAD_DIM:(s + 1) * HEAD_DIM]
        parts.append(zs * cos + pltpu.roll(zs, HEAD_DIM // 2, 1) * sin)
    return parts[0] if len(parts) == 1 else jnp.concatenate(parts, axis=1)


def _proj_kernel(lay, idx_scale, q_scale, with_vt, x_ref, g_ref, shift_ref, scale_ref, w_ref, cos_ref, sin_ref,
                 zb_ref, k_ref, v_ref, misc_ref, u_ref, *rest):
    vt_ref, wt_ref = rest[:2] if with_vt else (None, None)
    h_ref = rest[-1]
    j = pl.program_id(1)

    @pl.when(j == 0)
    def _():
        x = x_ref[...]
        r = lax.rsqrt(jnp.mean(x * x, axis=-1, keepdims=True) + EPS)
        h = (x * r * g_ref[...]) * (1.0 + scale_ref[0]) + shift_ref[0]
        h_ref[...] = h.astype(BF16)

    def tile():
        return jnp.dot(h_ref[...], w_ref[...], preferred_element_type=F32)

    def store_heads(ref, val):
        if len(ref.shape) == 2:
            ref[...] = val
        else:
            for g in range(N_KV_HEADS):
                ref[:, g, :] = val[:, g * HEAD_DIM:(g + 1) * HEAD_DIM]

    @pl.when(j < lay.qi)
    def _():
        zb_ref[...] = (_rope_tile(tile(), cos_ref[...], sin_ref[...]) * q_scale).astype(BF16)

    @pl.when((j >= lay.qi) & (j < lay.ga))
    def _():
        zb_ref[...] = _rope_tile(tile(), cos_ref[...], sin_ref[...]).astype(BF16)

    @pl.when(j == lay.k)
    def _():
        r = _rope_tile(tile(), cos_ref[...], sin_ref[...])
        zb_ref[...] = r.astype(BF16)
        store_heads(k_ref, r)

    @pl.when(((j >= lay.ga) & (j < lay.ma)) | ((j >= lay.gb) & (j < lay.k)))
    def _():
        z = tile()
        zb_ref[...] = (z * _sigmoid(z)).astype(BF16)

    @pl.when((j >= lay.ma) & (j < lay.gb))
    def _():
        zb_ref[...] = _sigmoid(tile()).astype(BF16)

    @pl.when(j == lay.v)
    def _():
        z = tile()
        zb_ref[...] = z.astype(BF16)
        store_heads(v_ref, z)
        if with_vt:
            vt_ref[0] = z.T.astype(BF16)

    @pl.when(j == lay.misc)
    def _():
        z = tile()
        ki = _rope_tile(z[:, :IDX_DIM], cos_ref[...], sin_ref[...])
        m = jnp.concatenate([ki, z[:, IDX_DIM:] * idx_scale], axis=1)
        zb_ref[...] = m.astype(BF16)
        misc_ref[...] = m
        if with_vt:
            wt_ref[...] = m[:, IDX_DIM:2 * IDX_DIM].T

    @pl.when(j >= lay.u)
    def _():
        u_ref[...] = tile()


def _project(x, g, shift, scale, w_p, cos, sin, lay, tm, tiles_per_group, tiles_per_pos, idx_scale, with_vt):
    m, d = x.shape
    grp, r, _ = shift.shape
    nu = lay.end - lay.u
    kern = functools.partial(_proj_kernel, lay, idx_scale, Q_SCALE, with_vt)
    out_specs = [pl.BlockSpec((tm, TN), lambda i, j: (i, jnp.minimum(j, lay.nb - 1))),
                 pl.BlockSpec((tm, TN), lambda i, j: (i, 0)),
                 pl.BlockSpec((tm, TN), lambda i, j: (i, 0)),
                 pl.BlockSpec((tm, TN), lambda i, j: (i, 0)),
                 pl.BlockSpec((tm, TN), lambda i, j: (i, jnp.clip(j - lay.u, 0, nu - 1)))]
    out_shape = [jax.ShapeDtypeStruct((m, lay.nb * TN), BF16),
                 jax.ShapeDtypeStruct((m, TN), F32),
                 jax.ShapeDtypeStruct((m, TN), F32),
                 jax.ShapeDtypeStruct((m, TN), F32),
                 jax.ShapeDtypeStruct((m, nu * TN), F32)]
    if with_vt:
        for o in (1, 2):
            out_specs[o] = pl.BlockSpec((tm, N_KV_HEADS, HEAD_DIM), lambda i, j: (i, 0, 0))
            out_shape[o] = jax.ShapeDtypeStruct((m, N_KV_HEADS, HEAD_DIM), F32)
        out_specs.append(pl.BlockSpec((1, TN, tm), lambda i, j: (i // tiles_per_group, 0, i % tiles_per_group)))
        out_shape.append(jax.ShapeDtypeStruct((grp, TN, tiles_per_group * tm), BF16))
        out_specs.append(pl.BlockSpec((IDX_DIM, tm), lambda i, j: (0, i)))
        out_shape.append(jax.ShapeDtypeStruct((IDX_DIM, m), F32))
    return pl.pallas_call(
        kern,
        grid=(m // tm, lay.end),
        in_specs=[pl.BlockSpec((tm, d), lambda i, j: (i, 0), pipeline_mode=pl.Buffered(1)),
                  pl.BlockSpec((1, d), lambda i, j: (0, 0)),
                  pl.BlockSpec((1, r, d), lambda i, j: (i // tiles_per_group, 0, 0)),
                  pl.BlockSpec((1, r, d), lambda i, j: (i // tiles_per_group, 0, 0)),
                  pl.BlockSpec((d, TN), lambda i, j: (0, j)),
                  pl.BlockSpec((tm, HEAD_DIM), lambda i, j: (i % tiles_per_pos, 0)),
                  pl.BlockSpec((tm, HEAD_DIM), lambda i, j: (i % tiles_per_pos, 0))],
        out_specs=out_specs,
        out_shape=out_shape,
        scratch_shapes=[pltpu.VMEM((tm, d), BF16)],
        compiler_params=_cparams(("arbitrary", "arbitrary")),
    )(x, g, shift, scale, w_p, cos, sin)


def _bisect(count_ge, lo, hi, clo, kf):
    def cond(st):
        return (st[4] > 0.0) & (st[5] < MAX_BISECT)

    def halve(lo, hi, clo, done):
        mid = 0.5 * lo + 0.5 * hi
        mid = jnp.where(mid <= lo, hi, mid)
        c = count_ge(mid)
        ge = c >= kf
        stuck = (mid >= hi) & jnp.logical_not(ge)
        lo2 = jnp.where(ge, mid, lo)
        hi2 = jnp.where(ge, hi, mid)
        clo2 = jnp.where(ge, c, clo)
        fin = (clo2 <= kf) | (lo2 >= hi2) | stuck
        return lo2, hi2, clo2, jnp.maximum(done, jnp.where(fin, 1.0, 0.0))

    def body(st):
        lo, hi, clo, done = halve(*halve(*st[:4]))
        return lo, hi, clo, done, jnp.sum(1.0 - done), st[5] + 1

    done0 = jnp.where((clo <= kf) | (lo >= hi), 1.0, 0.0)
    st = (lo, hi, clo, done0, jnp.sum(1.0 - done0), jnp.int32(0))
    out = lax.while_loop(cond, body, st)
    return out[0], out[2]


def _attn_kernel(topk, q_ref, qi_ref, sga_ref, k_ref, ki_ref, vt_ref, wt_ref, o_ref,
                 score_ref, bias_ref, cut_ref, m_ref, l_ref, alpha_ref, acc_ref, lg_ref, p_ref):
    i = pl.program_id(1)
    nck = i + 1
    n_heads = q_ref.shape[1] // HEAD_DIM
    group = n_heads // N_KV_HEADS
    nt = (((1,), (1,)), ((), ()))
    q_pos = i * TQ + lax.broadcasted_iota(jnp.int32, (1, TQ), 1)
    fold = lambda a: a.reshape(CK // SUBLANES, SUBLANES, TQ)

    def score_chunk(c, carry):
        mn, mx = carry
        off = pl.multiple_of(c * CK, CK)
        kc = ki_ref[pl.ds(off, CK), :]
        acc = jnp.zeros((CK, TQ), F32)
        for h in range(IDX_HEADS):
            s = lax.dot_general(kc, qi_ref[:, h * IDX_DIM:(h + 1) * IDX_DIM], nt,
                                preferred_element_type=F32)
            acc = acc + wt_ref[h:h + 1, :] * jnp.maximum(s, 0.0)
        k_pos = off + lax.broadcasted_iota(jnp.int32, (CK, 1), 0)
        causal = k_pos <= q_pos
        score_ref[pl.ds(off, CK), :] = jnp.where(causal, acc, -jnp.inf)
        mn = jnp.minimum(mn, fold(jnp.where(causal, acc, jnp.inf)).min(axis=0))
        mx = jnp.maximum(mx, fold(jnp.where(causal, acc, -jnp.inf)).max(axis=0))
        return mn, mx

    mn, mx = lax.fori_loop(0, nck, score_chunk,
                           (jnp.full((SUBLANES, TQ), jnp.inf, F32),
                            jnp.full((SUBLANES, TQ), -jnp.inf, F32)))
    lo = mn.min(axis=0, keepdims=True)
    hi = mx.max(axis=0, keepdims=True)

    ones_rows = jnp.ones((SUBLANES, CK), BF16)

    def count_ge(thr):
        def body(c, cnt):
            s = score_ref[pl.ds(pl.multiple_of(c * CK, CK), CK), :]
            return cnt + fold(jnp.where(s >= thr, 1.0, 0.0)).sum(axis=0)
        cnt = lax.fori_loop(0, nck, body, jnp.zeros((SUBLANES, TQ), F32))
        return cnt.sum(axis=0, keepdims=True)

    n_valid = (q_pos + 1).astype(F32)
    kf = float(topk)
    thr, n_ge = _bisect(count_ge, lo, hi, n_valid, kf)

    s_len = score_ref.shape[0]
    has_ties = jnp.sum(jnp.where(n_ge > kf, 1.0, 0.0)) > 0.0
    cut_ref[...] = jnp.full(cut_ref.shape, float(s_len), F32)

    @pl.when(has_ties)
    def _():
        def count(pred):
            def body(c, cnt):
                off = pl.multiple_of(c * CK, CK)
                k_pos = (off + lax.broadcasted_iota(jnp.int32, (CK, 1), 0)).astype(F32)
                hit = pred(score_ref[pl.ds(off, CK), :], k_pos)
                return cnt + fold(jnp.where(hit, 1.0, 0.0)).sum(axis=0)
            cnt = lax.fori_loop(0, nck, body, jnp.zeros((SUBLANES, TQ), F32))
            return cnt.sum(axis=0, keepdims=True)

        need = kf - count(lambda s, kp: s > thr)

        def step(_, st):
            lo_i, hi_i = st
            mid = jnp.floor(0.5 * (lo_i + hi_i))
            ge = count(lambda s, kp: (s == thr) & (kp <= mid)) >= need
            return jnp.where(ge, lo_i, mid), jnp.where(ge, mid, hi_i)

        st0 = (jnp.full((1, TQ), -1.0, F32), jnp.full((1, TQ), s_len - 1.0, F32))
        cut = lax.fori_loop(0, (s_len - 1).bit_length() + 1, step, st0)[1]
        cut_ref[...] = jnp.broadcast_to(cut, cut_ref.shape)

    m_ref[...] = jnp.full(m_ref.shape, NEG, F32)
    l_ref[...] = jnp.zeros(l_ref.shape, F32)
    acc_ref[...] = jnp.zeros(acc_ref.shape, F32)

    def att_chunk(c, carry):
        off = pl.multiple_of(c * CK, CK)
        sc = score_ref[pl.ds(off, CK), :]

        @pl.when(jnp.logical_not(has_ties))
        def _():
            bias_ref[...] = jnp.where(sc >= thr, 0.0, NEG)

        @pl.when(has_ties)
        def _():
            k_pos = (off + lax.broadcasted_iota(jnp.int32, (CK, 1), 0)).astype(F32)
            keep = (sc > thr) | ((sc == thr) & (k_pos <= cut_ref[0:1, :]))
            bias_ref[...] = jnp.where(keep, 0.0, NEG)

        kc = k_ref[pl.ds(off, CK), :]
        vtc = vt_ref[0, :, pl.ds(off, CK)]

        def qk(h):
            g = h // group
            return lax.dot_general(kc[:, g * HEAD_DIM:(g + 1) * HEAD_DIM],
                                   q_ref[:, h * HEAD_DIM:(h + 1) * HEAD_DIM], nt,
                                   preferred_element_type=F32)

        for h in range(n_heads):
            lg = qk(h) + bias_ref[...]
            m_old = m_ref[h:h + 1, :]
            m_new = jnp.maximum(m_old, lg.max(axis=0, keepdims=True))
            lg_ref[h] = lg
            alpha_ref[h:h + 1, :] = jnp.exp2(m_old - m_new)
            m_ref[h:h + 1, :] = m_new
        for h in range(n_heads):
            p_ref[h] = jnp.exp2(lg_ref[h] - m_ref[h:h + 1, :]).astype(BF16)
        for h in range(n_heads):
            g = h // group
            pv = jnp.dot(vtc[g * HEAD_DIM:(g + 1) * HEAD_DIM, :], p_ref[h], preferred_element_type=F32)
            acc_ref[h] = alpha_ref[h:h + 1, :] * acc_ref[h] + pv
            psum = jnp.dot(ones_rows, p_ref[h], preferred_element_type=F32)[0:1, :]
            l_ref[h:h + 1, :] = alpha_ref[h:h + 1, :] * l_ref[h:h + 1, :] + psum
        return carry

    lax.fori_loop(0, nck, att_chunk, 0)

    for h in range(n_heads):
        o = (acc_ref[h] / l_ref[h:h + 1, :]).T
        sl = slice(h * HEAD_DIM, (h + 1) * HEAD_DIM)
        o_ref[:, sl] = (o * sga_ref[:, sl].astype(F32)).astype(BF16)


def _prompt_attention(zb, vt, wt, lay, b, s, d, topk):
    nq = s // TQ
    n_heads = d // HEAD_DIM
    kern = functools.partial(_attn_kernel, topk)
    row = lambda bb, i: bb * nq + i
    return pl.pallas_call(
        kern,
        grid=(b, nq),
        in_specs=[pl.BlockSpec((TQ, d), lambda bb, i: (row(bb, i), lay.q * TN // d)),
                  pl.BlockSpec((TQ, d), lambda bb, i: (row(bb, i), lay.qi * TN // d)),
                  pl.BlockSpec((TQ, d), lambda bb, i: (row(bb, i), lay.ga * TN // d)),
                  pl.BlockSpec((s, TN), lambda bb, i: (bb, lay.k)),
                  pl.BlockSpec((s, IDX_DIM), lambda bb, i: (bb, lay.misc * TN // IDX_DIM)),
                  pl.BlockSpec((1, TN, s), lambda bb, i: (bb, 0, 0)),
                  pl.BlockSpec((IDX_HEADS, TQ), lambda bb, i: (0, row(bb, i)))],
        out_specs=pl.BlockSpec((TQ, d), lambda bb, i: (row(bb, i), 0)),
        out_shape=jax.ShapeDtypeStruct((b * s, d), BF16),
        scratch_shapes=[pltpu.VMEM((s, TQ), F32),
                        pltpu.VMEM((CK, TQ), F32),
                        pltpu.VMEM((SUBLANES, TQ), F32),
                        pltpu.VMEM((n_heads, TQ), F32),
                        pltpu.VMEM((n_heads, TQ), F32),
                        pltpu.VMEM((n_heads, TQ), F32),
                        pltpu.VMEM((n_heads, HEAD_DIM, TQ), F32),
                        pltpu.VMEM((n_heads, CK, TQ), F32),
                        pltpu.VMEM((n_heads, CK, TQ), BF16)],
        compiler_params=_cparams(("arbitrary", "arbitrary")),
    )(zb, zb, zb, zb, zb, vt, wt)


def _pool_kernel(ts, u_ref, sgb_ref, wp_ref, sp_ref, o_ref, ext_ref):
    i = pl.program_id(1)
    halo = POOL_STATE + 1

    @pl.when(i == 0)
    def _():
        ext_ref[0:halo, :] = jnp.zeros((halo, ext_ref.shape[1]), F32)

    @pl.when(i > 0)
    def _():
        ext_ref[0:halo, :] = ext_ref[ts:ts + halo, :]

    ext_ref[halo:halo + ts, :] = u_ref[...]
    pos = i * ts + lax.broadcasted_iota(jnp.int32, (ts, 1), 0)
    gd = wp_ref.shape[1]
    for g, w in enumerate(POOL_WINDOWS):
        cols = slice(g * gd, (g + 1) * gd)
        acc = ext_ref[halo:halo + ts, cols]
        for jj in range(1, w):
            acc = acc + ext_ref[halo - jj:halo - jj + ts, cols]
        cnt = jnp.minimum(pos + 1, w).astype(F32)
        pooled = acc / cnt - u_ref[:, cols]
        mixed = jnp.dot(pooled.astype(BF16), wp_ref[g], preferred_element_type=F32) * sp_ref[:, cols]
        o_ref[:, cols] = (mixed * sgb_ref[:, cols].astype(F32)).astype(BF16)


def _pool_prompt(u, zb, wp, sp, lay, b, s, ts):
    m, pw = u.shape
    nt = s // ts
    kern = functools.partial(_pool_kernel, ts)
    return pl.pallas_call(
        kern,
        grid=(b, nt),
        in_specs=[pl.BlockSpec((ts, pw), lambda bb, i: (bb * nt + i, 0)),
                  pl.BlockSpec((ts, pw), lambda bb, i: (bb * nt + i, lay.gb * TN // pw)),
                  pl.BlockSpec(wp.shape, lambda bb, i: (0, 0, 0)),
                  pl.BlockSpec((1, pw), lambda bb, i: (0, 0))],
        out_specs=pl.BlockSpec((ts, pw), lambda bb, i: (bb * nt + i, 0)),
        out_shape=jax.ShapeDtypeStruct((m, pw), BF16),
        scratch_shapes=[pltpu.VMEM((POOL_STATE + 1 + ts, pw), F32)],
        compiler_params=_cparams(("arbitrary", "arbitrary")),
    )(u, zb, wp, sp)


def _pool_step_kernel(cnts, ext_ref, sgb_ref, wp_ref, sp_ref, o_ref):
    gd = wp_ref.shape[1]
    rows = ext_ref.shape[0]
    for g, w in enumerate(POOL_WINDOWS):
        cols = slice(g * gd, (g + 1) * gd)
        acc = ext_ref[rows - 1, :, cols]
        for jj in range(1, w):
            acc = acc + ext_ref[rows - 1 - jj, :, cols]
        pooled = acc / cnts[g] - ext_ref[rows - 1, :, cols]
        mixed = jnp.dot(pooled.astype(BF16), wp_ref[g], preferred_element_type=F32) * sp_ref[:, cols]
        o_ref[:, cols] = (mixed * sgb_ref[:, cols].astype(F32)).astype(BF16)


def _pool_step(ext_t, sgb, wp, sp, past):
    rows, n, pw = ext_t.shape
    cnts = tuple(float(min(past + 1, w)) for w in POOL_WINDOWS)
    return pl.pallas_call(
        functools.partial(_pool_step_kernel, cnts),
        out_shape=jax.ShapeDtypeStruct((n, pw), BF16),
    )(ext_t, sgb, wp, sp)


def _merge1_kernel(ab_ref, pb_ref, wa_ref, wb_ref, sma_ref, smb_ref, o_ref):
    a = jnp.dot(ab_ref[...], wa_ref[...], preferred_element_type=F32)
    bb = jnp.dot(pb_ref[...], wb_ref[...], preferred_element_type=F32)
    o_ref[...] = (sma_ref[...].astype(F32) * a + smb_ref[...].astype(F32) * bb).astype(BF16)


def _merge1(ab, pb, wa, wb, zb, lay, tm):
    m, d = ab.shape
    pw = pb.shape[1]
    return pl.pallas_call(
        _merge1_kernel,
        grid=(m // tm, d // TN),
        in_specs=[pl.BlockSpec((tm, d), lambda i, j: (i, 0)),
                  pl.BlockSpec((tm, pw), lambda i, j: (i, 0)),
                  pl.BlockSpec((d, TN), lambda i, j: (0, j)),
                  pl.BlockSpec((pw, TN), lambda i, j: (0, j)),
                  pl.BlockSpec((tm, TN), lambda i, j: (i, lay.ma + j)),
                  pl.BlockSpec((tm, TN), lambda i, j: (i, lay.mb + j))],
        out_specs=pl.BlockSpec((tm, TN), lambda i, j: (i, j)),
        out_shape=jax.ShapeDtypeStruct((m, d), BF16),
        compiler_params=_cparams(("arbitrary", "arbitrary")),
    )(ab, pb, wa, wb, zb, zb)


def _merge2_kernel(final, mm_ref, x_ref, gate_ref, wo_ref, gf_ref, o_ref):
    y = x_ref[...] + gate_ref[0] * jnp.dot(mm_ref[...], wo_ref[...], preferred_element_type=F32)
    if final:
        y = y * lax.rsqrt(jnp.mean(y * y, axis=-1, keepdims=True) + EPS) * gf_ref[...]
    o_ref[...] = y


def _merge2(mm, x, gate, wo, gf, tm, tiles_per_group, final):
    m, d = x.shape
    r = gate.shape[1]
    return pl.pallas_call(
        functools.partial(_merge2_kernel, final),
        grid=(m // tm,),
        in_specs=[pl.BlockSpec((tm, d), lambda i: (i, 0)),
                  pl.BlockSpec((tm, d), lambda i: (i, 0)),
                  pl.BlockSpec((1, r, d), lambda i: (i // tiles_per_group, 0, 0)),
                  pl.BlockSpec((d, d), lambda i: (0, 0)),
                  pl.BlockSpec((1, d), lambda i: (0, 0))],
        out_specs=pl.BlockSpec((tm, d), lambda i: (i, 0)),
        out_shape=jax.ShapeDtypeStruct((m, d), F32),
        compiler_params=_cparams(("arbitrary",)),
    )(mm, x, gate, wo, gf)


def _page_specs(width, npg, npp):
    return [pl.BlockSpec((1, PAGE_SIZE, width),
                         lambda b, p, pt, r=r: (pt[b * npg + p * npp + r], 0, 0)) for r in range(npp)]


def _sidx_kernel(npp, pt_ref, *refs):
    kidx_refs = refs[:npp]
    qi_ref, w_ref, o_ref = refs[npp:]
    nt = (((1,), (1,)), ((), ()))
    qi, w = qi_ref[0], w_ref[0]
    rows = []
    for r in range(npp):
        s = lax.dot_general(qi, kidx_refs[r][0].astype(BF16), nt, preferred_element_type=F32)
        rows.append(jnp.sum(w * jnp.maximum(s, 0.0), axis=0, keepdims=True))
    o_ref[0] = jnp.concatenate(rows, axis=1)


def _sample_scores(page_table, cache_kidx, qi3, wcol, npp):
    n, npg = page_table.shape
    return pl.pallas_call(
        functools.partial(_sidx_kernel, npp),
        grid_spec=pltpu.PrefetchScalarGridSpec(
            num_scalar_prefetch=1,
            grid=(n, npg // npp),
            in_specs=_page_specs(IDX_DIM, npg, npp) + [
                pl.BlockSpec((1, IDX_HEADS, IDX_DIM), lambda b, p, pt: (b, 0, 0)),
                pl.BlockSpec((1, IDX_HEADS, 1), lambda b, p, pt: (b, 0, 0))],
            out_specs=pl.BlockSpec((1, 1, npp * PAGE_SIZE), lambda b, p, pt: (b, 0, p))),
        out_shape=jax.ShapeDtypeStruct((n, 1, npg * PAGE_SIZE), F32),
        compiler_params=_cparams(("arbitrary", "arbitrary")),
    )(page_table.reshape(-1), *([cache_kidx] * npp), qi3, wcol)


def _sthr_kernel(topk, sc_ref, qi_ref, kin_ref, w_ref, pos_ref, cnt_ref, sel_ref, thr_scr, new_scr):
    sc = sc_ref[...]
    kin = kin_ref[...].astype(BF16).astype(F32)
    s_new = jnp.sum(qi_ref[...].astype(F32) * kin, axis=2, keepdims=True)
    new = jnp.sum(w_ref[...] * jnp.maximum(s_new, 0.0), axis=1, keepdims=True)
    red = lambda f, a: f(f(a, axis=2, keepdims=True), axis=1, keepdims=True)
    lo = jnp.minimum(red(jnp.min, sc), new)
    hi = jnp.maximum(red(jnp.max, sc), new)

    def count_ge(thr):
        c = red(jnp.sum, jnp.where(sc >= thr, 1.0, 0.0))
        return c + jnp.where(new >= thr, 1.0, 0.0)

    n_all = jnp.full(new.shape, float(sc.shape[1] * sc.shape[2] + 1), F32)
    thr, _ = _bisect(count_ge, lo, hi, n_all, float(topk))
    thr_scr[...] = jnp.broadcast_to(thr, thr_scr.shape)
    new_scr[...] = jnp.broadcast_to(new, new_scr.shape)

    rows = sc.shape[1]
    ii = lambda shape, ax: lax.broadcasted_iota(jnp.int32, shape, ax)
    one = lambda m: jnp.where(m, 1.0, 0.0)
    tri_lane = one(ii((PAGE_SIZE, PAGE_SIZE), 0) <= ii((PAGE_SIZE, PAGE_SIZE), 1)).astype(BF16)
    lower = one(ii((rows, rows), 1) < ii((rows, rows), 0)).astype(BF16)
    upper = one(ii((rows, rows), 0) < ii((rows, rows), 1)).astype(BF16)
    ones8 = jnp.ones((SUBLANES, PAGE_SIZE), BF16)
    nt = (((1,), (1,)), ((), ()))
    total = lambda a: jnp.sum(jnp.sum(a, axis=1, keepdims=True), axis=0, keepdims=True)
    dotf = lambda a, b: jnp.dot(a.astype(BF16), b.astype(BF16), preferred_element_type=F32)
    kf = float(topk)
    j_col = ii((topk, 1), 0).astype(F32)
    r_row = ii((1, rows), 1).astype(F32)
    c_row = ii((1, PAGE_SIZE), 1).astype(F32)

    def compact(b, carry):
        s = sc_ref[b]
        t = thr_scr[b][:, 0:1]
        nw = new_scr[b][:, 0:1]
        gt, eq = s > t, s == t
        need = kf - total(one(gt)) - one(nw > t)
        e_lane = dotf(one(eq), tri_lane)
        e_rank = e_lane + dotf(lower, e_lane)[:, PAGE_SIZE - 1:PAGE_SIZE]
        x = one(gt | (eq & (e_rank <= need)))
        sel_new = (nw > t) | ((nw == t) & (total(one(eq)) < need))
        x_lane = dotf(x, tri_lane)
        n_rows = lax.dot_general(ones8, x.astype(BF16), nt, preferred_element_type=F32)
        n_row = n_rows[0:1, :]
        start_row = dotf(n_rows, upper)[0:1, :]
        hit = one((start_row <= j_col) & (j_col < start_row + n_row))
        lane_sum = lambda a: jnp.sum(a, axis=1, keepdims=True)
        page = lane_sum(hit * r_row)
        target = j_col - lane_sum(hit * start_row) + 1.0
        in_row = dotf(hit, x_lane * x)
        off = lane_sum(jnp.where(in_row == target, c_row, 0.0))
        pos_ref[b] = (page * PAGE_SIZE + off).astype(jnp.int32)
        cnt_ref[b] = jnp.broadcast_to(total(x), cnt_ref.shape[1:])
        sel_ref[b] = jnp.broadcast_to(one(sel_new), sel_ref.shape[1:])
        return carry

    lax.fori_loop(0, sc.shape[0], compact, 0)


def _sample_select(sc, qi3, kin, wcol, topk):
    n = sc.shape[0]
    return pl.pallas_call(
        functools.partial(_sthr_kernel, topk),
        out_shape=[jax.ShapeDtypeStruct((n, topk, 1), jnp.int32),
                   jax.ShapeDtypeStruct((n, 1, PAGE_SIZE), F32),
                   jax.ShapeDtypeStruct((n, 1, PAGE_SIZE), F32)],
        scratch_shapes=[pltpu.VMEM((n, 1, PAGE_SIZE), F32),
                        pltpu.VMEM((n, 1, PAGE_SIZE), F32)],
        compiler_params=pltpu.CompilerParams(vmem_limit_bytes=VMEM_LIMIT),
    )(sc, qi3, kin, wcol)


def _row_copy(cache_hbm, buf, sem, layer, page, off, j):
    return pltpu.make_async_copy(cache_hbm.at[layer, page, off], buf.at[j], sem)


def _sattn_kernel(layer, npg, pos_ref, pt_ref, ck_hbm, cv_hbm, cnt_ref, sel_ref, q_ref, kn_ref, vn_ref,
                  sga_ref, o_ref, kbuf, vbuf, ksem, vsem):
    b = pl.program_id(0)
    slot = b % 2
    topk = kbuf.shape[1]
    n_heads = q_ref.shape[1]
    group = n_heads // N_KV_HEADS
    nt = (((1,), (1,)), ((), ()))

    def fetch(bb, s):
        def issue(j, carry):
            pos = pos_ref[bb * topk + j]
            page = pt_ref[bb * npg + pos // PAGE_SIZE]
            off = pos % PAGE_SIZE
            _row_copy(ck_hbm, kbuf.at[s], ksem.at[s], layer, page, off, j).start()
            _row_copy(cv_hbm, vbuf.at[s], vsem.at[s], layer, page, off, j).start()
            return carry
        lax.fori_loop(0, topk, issue, 0)

    @pl.when(b == 0)
    def _():
        fetch(b, slot)

    @pl.when(b + 1 < pl.num_programs(0))
    def _():
        fetch(b + 1, 1 - slot)

    kbuf, vbuf = kbuf.at[slot], vbuf.at[slot]
    if topk % PAGE_SIZE == 0:
        for w in range(topk // PAGE_SIZE):
            rows = pl.ds(w * PAGE_SIZE, PAGE_SIZE)
            pltpu.make_async_copy(ck_hbm.at[layer, 0], kbuf.at[rows], ksem.at[slot]).wait()
            pltpu.make_async_copy(cv_hbm.at[layer, 0], vbuf.at[rows], vsem.at[slot]).wait()
    else:
        def wait(j, carry):
            _row_copy(ck_hbm, kbuf, ksem.at[slot], layer, 0, 0, j).wait()
            _row_copy(cv_hbm, vbuf, vsem.at[slot], layer, 0, 0, j).wait()
            return carry

        lax.fori_loop(0, topk, wait, 0)

    q = q_ref[0]
    head = lax.broadcasted_iota(jnp.int32, (n_heads, 1), 0)
    in_group = [(head >= g * group) & (head < (g + 1) * group) for g in range(N_KV_HEADS)]
    lg = jnp.zeros((n_heads, topk), F32)
    for g in range(N_KV_HEADS):
        lg_g = lax.dot_general(q, kbuf[:, g, :].astype(BF16), nt, preferred_element_type=F32)
        lg = lg + jnp.where(in_group[g], lg_g, 0.0)
    valid = lax.broadcasted_iota(jnp.int32, (1, topk), 1).astype(F32) < cnt_ref[0][:, 0:1]
    lg = jnp.where(valid, lg, NEG)
    kn = kn_ref[0].astype(BF16).astype(F32)
    vn = vn_ref[0].astype(BF16).astype(F32)
    lgn = jnp.sum(q.astype(F32) * kn, axis=1, keepdims=True)
    lgn = jnp.where(sel_ref[0][:, 0:1] > 0.0, lgn, NEG)
    m = jnp.maximum(lg.max(axis=1, keepdims=True), lgn)
    p = jnp.exp2(lg - m)
    pn = jnp.exp2(lgn - m)
    l = p.sum(axis=1, keepdims=True) + pn
    pb = p.astype(BF16)
    acc = pn.astype(BF16).astype(F32) * vn
    for g in range(N_KV_HEADS):
        pv = jnp.dot(pb, vbuf[:, g, :].astype(BF16), preferred_element_type=F32)
        acc = acc + jnp.where(in_group[g], pv, 0.0)
    o_ref[0] = ((acc / l) * sga_ref[0].astype(F32)).astype(BF16)


def _sample_attention(page_table, cache_k, cache_v, layer, pos, cnt, sel, q3, kn, vn, sga3):
    n, npg = page_table.shape
    n_heads = q3.shape[1]
    topk = pos.shape[1]
    per_b = lambda shape: pl.BlockSpec(shape, lambda b, ps, pt: (b, 0, 0))
    hbm = pl.BlockSpec(memory_space=pl.ANY)
    return pl.pallas_call(
        functools.partial(_sattn_kernel, layer, npg),
        grid_spec=pltpu.PrefetchScalarGridSpec(
            num_scalar_prefetch=2,
            grid=(n,),
            in_specs=[hbm, hbm,
                      per_b((1, 1, PAGE_SIZE)),
                      per_b((1, 1, PAGE_SIZE)),
                      per_b((1, n_heads, HEAD_DIM)),
                      per_b((1, n_heads, HEAD_DIM)),
                      per_b((1, n_heads, HEAD_DIM)),
                      per_b((1, n_heads, HEAD_DIM))],
            out_specs=per_b((1, n_heads, HEAD_DIM)),
            scratch_shapes=[pltpu.VMEM((2, topk, N_KV_HEADS, HEAD_DIM), cache_k.dtype),
                            pltpu.VMEM((2, topk, N_KV_HEADS, HEAD_DIM), cache_v.dtype),
                            pltpu.SemaphoreType.DMA((2,)),
                            pltpu.SemaphoreType.DMA((2,))]),
        out_shape=jax.ShapeDtypeStruct((n, n_heads, HEAD_DIM), BF16),
        compiler_params=_cparams(("arbitrary",)),
    )(pos.reshape(-1), page_table.reshape(-1), cache_k, cache_v, cnt, sel, q3, kn, vn, sga3)


def _rope_tables(pos):
    half = HEAD_DIM // 2
    inv = ROPE_THETA ** (-jnp.arange(half, dtype=F32) / half)
    ang = pos.astype(F32)[:, None] * inv[None, :]
    cos, sin = jnp.cos(ang), jnp.sin(ang)
    return jnp.concatenate([cos, cos], axis=1), jnp.concatenate([-sin, sin], axis=1)


def _tile_starts(d, lay):
    att, kv, idx, pool = d, N_KV_HEADS * HEAD_DIM, IDX_HEADS * IDX_DIM, d // 2
    sizes = (att, kv, kv, idx, IDX_DIM, IDX_HEADS, att, pool, pool, d, d)
    offs = [0]
    for n in sizes:
        offs.append(offs[-1] + n)
    q, k, v, qi, ki, wi, ga, u, gb, ma, mb = offs[:-1]
    starts = [0] * lay.end
    for first, tiles, col in ((lay.q, lay.qi - lay.q, q), (lay.qi, lay.ga - lay.qi, qi),
                              (lay.ga, lay.ma - lay.ga, ga), (lay.ma, lay.mb - lay.ma, ma),
                              (lay.mb, lay.gb - lay.mb, mb), (lay.gb, lay.k - lay.gb, gb),
                              (lay.k, 1, k), (lay.v, 1, v), (lay.misc, 1, ki), (lay.u, lay.end - lay.u, u)):
        for t in range(tiles):
            starts[first + t] = col + t * TN
    assert wi == ki + IDX_DIM and all(s + TN <= offs[-1] and s % SUBLANES == 0 for s in starts)
    return starts


def _pack_kernel(misc_tile, starts_ref, w_hbm, o_ref, buf, sem):
    t = pl.program_id(0)
    slot = t % 2

    def copy(tt, s):
        first = pl.multiple_of(starts_ref[tt], SUBLANES)
        return pltpu.make_async_copy(w_hbm.at[pl.ds(first, TN)], buf.at[s], sem.at[s])

    @pl.when(t == 0)
    def _():
        copy(t, slot).start()

    @pl.when(t + 1 < pl.num_programs(0))
    def _():
        copy(t + 1, 1 - slot).start()

    copy(t, slot).wait()
    x = buf[slot]
    row = lax.broadcasted_iota(jnp.int32, (TN, 1), 0)
    x = jnp.where((t == misc_tile) & (row >= IDX_DIM + IDX_HEADS), 0.0, x)
    o_ref[...] = x.T.astype(BF16)


def _pack_w_in(w_t, layer, in_width, d, lay):
    starts = jnp.asarray([layer * in_width + s for s in _tile_starts(d, lay)], jnp.int32)
    return pl.pallas_call(
        functools.partial(_pack_kernel, lay.misc),
        grid_spec=pltpu.PrefetchScalarGridSpec(
            num_scalar_prefetch=1,
            grid=(lay.end,),
            in_specs=[pl.BlockSpec(memory_space=pl.ANY)],
            out_specs=pl.BlockSpec((d, TN), lambda t, st: (0, t)),
            scratch_shapes=[pltpu.VMEM((2, TN, d), w_t.dtype),
                            pltpu.SemaphoreType.DMA((2,))]),
        out_shape=jax.ShapeDtypeStruct((d, lay.end * TN), BF16),
        compiler_params=_cparams(("arbitrary",)),
    )(starts, w_t)


def kernel(x_prompt, x_sample, c_prompt, c_sample, cache_k, cache_v, cache_kidx, state_pool, page_table,
           g_norm, w_ada, b_ada, w_in, w_pool, s_pool, w_a_proj, w_b_proj, w_out, g_final):
    bp, sp, d = x_prompt.shape
    ns, ds, _ = x_sample.shape
    depth = g_norm.shape[0]
    npg = page_table.shape[1]
    past = npg * PAGE_SIZE
    assert ds == 1 and d % (N_KV_HEADS * HEAD_DIM) == 0 and sp % TQ == 0 and sp >= POOL_STATE
    n_heads = d // HEAD_DIM
    kvw = N_KV_HEADS * HEAD_DIM
    lay = _Layout(d)
    pw = lay.pool
    idx_scale = (IDX_HEADS * IDX_DIM) ** -0.5
    topk_p = min(TOPK_MAX, sp // 4)
    topk_s = min(TOPK_MAX, (past + ds) // 4)
    tm = 1024
    ts = 512
    tm2 = 256

    cos_p, sin_p = _rope_tables(jnp.arange(sp, dtype=jnp.int32))
    cos_s, sin_s = _rope_tables(jnp.full((ns,), past, jnp.int32))
    n_c = bp + ns
    c_rows = -(-n_c // SUBLANES) * SUBLANES
    c_all = jnp.concatenate([c_prompt, c_sample, jnp.zeros((c_rows - n_c, d), F32)], axis=0)

    n_pool = cache_kidx.shape[1]
    kidx_pages = cache_kidx.reshape(depth * n_pool, PAGE_SIZE, IDX_DIM)
    npp = math.gcd(npg, PAGES_PER_STEP)
    assert npg % npp == 0

    in_width = w_in.shape[2]
    w_t = jnp.swapaxes(w_in, 1, 2).reshape(depth * in_width, d)

    xp = x_prompt.reshape(bp * sp, d)
    xs = x_sample.reshape(ns, d)
    outs = {n: [] for n in ("kp", "vp", "kip", "pp", "ks", "vs", "kis", "ps")}
    for l in range(depth):
        final = l == depth - 1
        w_p = _pack_w_in(w_t, l, in_width, d, lay)
        wa, wb, wo, wpl = (w_a_proj[l].astype(BF16), w_b_proj[l].astype(BF16), w_out[l].astype(BF16),
                           w_pool[l].astype(BF16))
        g = g_norm[l].reshape(1, d)
        spl = s_pool[l].reshape(1, pw)
        gf = g_final.reshape(1, d)

        mod = _ada(c_all, w_ada[l], b_ada[l])
        shift, scale, gate = mod[:, :d], mod[:, d:2 * d], mod[:, 2 * d:]
        pg = lambda a: a[:bp].reshape(bp, 1, d)
        sg = lambda a: a[bp:n_c].reshape(1, ns, d)

        zb, k_p, v_p, misc_p, u_p, vt, wt = _project(xp, g, pg(shift), pg(scale), w_p, cos_p, sin_p, lay, tm,
                                                     sp // tm, sp // tm, idx_scale, True)
        ab = _prompt_attention(zb, vt, wt, lay, bp, sp, d, topk_p)
        pb = _pool_prompt(u_p, zb, wpl, spl, lay, bp, sp, ts)
        mm = _merge1(ab, pb, wa, wb, zb, lay, tm)
        xp = _merge2(mm, xp, pg(gate), wo, gf, tm2, sp // tm2, final)
        outs["kp"].append(k_p.reshape(bp, sp, N_KV_HEADS, HEAD_DIM))
        outs["vp"].append(v_p.reshape(bp, sp, N_KV_HEADS, HEAD_DIM))
        outs["kip"].append(misc_p[:, :IDX_DIM].reshape(bp, sp, IDX_DIM))
        outs["pp"].append(u_p.reshape(bp, sp, pw)[:, sp - POOL_STATE:, :])

        zs, k_s, v_s, misc_s, u_s = _project(xs, g, sg(shift), sg(scale), w_p, cos_s, sin_s, lay, ns,
                                             1, 1, idx_scale, False)
        col = lambda t, n: zs[:, t * TN:t * TN + n]
        q3 = col(lay.q, d).reshape(ns, n_heads, HEAD_DIM)
        qi3 = col(lay.qi, IDX_HEADS * IDX_DIM).reshape(ns, IDX_HEADS, IDX_DIM)
        sga3 = col(lay.ga, d).reshape(ns, n_heads, HEAD_DIM)
        wcol = misc_s[:, IDX_DIM:IDX_DIM + IDX_HEADS].reshape(ns, IDX_HEADS, 1)
        kin = misc_s[:, :IDX_DIM].reshape(ns, 1, IDX_DIM)
        pt_l = page_table + l * n_pool
        sc = _sample_scores(pt_l, kidx_pages, qi3, wcol, npp)
        pos, cnt, sel = _sample_select(sc.reshape(ns, npg, PAGE_SIZE), qi3, kin, wcol, topk_s)
        rep = lambda a: jnp.repeat(a.reshape(ns, N_KV_HEADS, HEAD_DIM), n_heads // N_KV_HEADS, axis=1)
        ab_s = _sample_attention(page_table, cache_k, cache_v, l, pos, cnt, sel, q3,
                                 rep(k_s), rep(v_s), sga3).reshape(ns, d)
        ext = jnp.concatenate([state_pool[l], u_s[:, None, :]], axis=1)
        pb_s = _pool_step(ext.transpose(1, 0, 2), col(lay.gb, pw), wpl, spl, past)
        mm_s = _merge1(ab_s, pb_s, wa, wb, zs, lay, ns)
        xs = _merge2(mm_s, xs, sg(gate), wo, gf, ns, 1, final)
        outs["ks"].append(k_s.reshape(ns, ds, N_KV_HEADS, HEAD_DIM))
        outs["vs"].append(v_s.reshape(ns, ds, N_KV_HEADS, HEAD_DIM))
        outs["kis"].append(misc_s[:, :IDX_DIM].reshape(ns, ds, IDX_DIM))
        outs["ps"].append(ext[:, 1:, :])

    st = lambda n: outs[n][0][None] if depth == 1 else jnp.stack(outs[n])
    return (xp.reshape(bp, sp, d), xs.reshape(ns, ds, d),
            st("kp"), st("vp"), st("kip"), st("pp"), st("ks"), st("vs"), st("kis"), st("ps"))
```

```python
import functools
import math

import jax
import jax.numpy as jnp
from jax import lax
from jax.experimental import pallas as pl
from jax.experimental.pallas import tpu as pltpu

F32 = jnp.float32
BF16 = jnp.bfloat16

HEAD_DIM = 128
N_KV_HEADS = 4
IDX_HEADS = 16
IDX_DIM = 128
TOPK_MAX = 256
PAGE_SIZE = 128
POOL_WINDOWS = (2, 4, 8, 16)
POOL_STATE = max(POOL_WINDOWS) - 1
ROPE_THETA = 10000.0
EPS = 1e-6

LANES = 128
SUBLANES = 8
TN = 512
TQ = 256
CK = 256
NEG = -1e30
Q_SCALE = 1.4426950408889634 * HEAD_DIM ** -0.5
PAGES_PER_STEP = 32
MAX_BISECT = 512
VMEM_LIMIT = 56 * 1024 * 1024


def _cparams(sem):
    return pltpu.CompilerParams(dimension_semantics=sem, vmem_limit_bytes=VMEM_LIMIT)


def _sigmoid(z):
    return 0.5 * jnp.tanh(0.5 * z) + 0.5


def _ada_kernel(c_ref, w_ref, b_ref, o_ref):
    o_ref[...] = jnp.dot(c_ref[...].astype(BF16), w_ref[...].astype(BF16),
                         preferred_element_type=F32) + b_ref[...]


def _ada(c, w, b):
    r, d = c.shape
    n = w.shape[1]
    return pl.pallas_call(
        _ada_kernel,
        grid=(n // TN,),
        in_specs=[pl.BlockSpec((r, d), lambda j: (0, 0)),
                  pl.BlockSpec((d, TN), lambda j: (0, j)),
                  pl.BlockSpec((1, TN), lambda j: (0, j))],
        out_specs=pl.BlockSpec((r, TN), lambda j: (0, j)),
        out_shape=jax.ShapeDtypeStruct((r, n), F32),
        compiler_params=_cparams(("arbitrary",)),
    )(c, w, b.reshape(1, n))


class _Layout:
    def __init__(self, d):
        att, kv, idx, pool = d, N_KV_HEADS * HEAD_DIM, IDX_HEADS * IDX_DIM, d // 2
        t = lambda n: n // TN
        assert att % TN == 0 and kv == TN and idx % TN == 0 and pool % TN == 0
        self.q = 0
        self.qi = self.q + t(att)
        self.ga = self.qi + t(idx)
        self.ma = self.ga + t(att)
        self.mb = self.ma + t(d)
        self.gb = self.mb + t(d)
        self.k = self.gb + t(pool)
        self.v = self.k + 1
        self.misc = self.v + 1
        self.u = self.misc + 1
        self.end = self.u + t(pool)
        self.nb = self.u
        self.pool = pool


def _rope_tile(z, cos, sin):
    parts = []
    for s in range(z.shape[1] // HEAD_DIM):
        zs = z[:, s * HEAD_DIM:(s + 1) * HEAD_DIM]
        parts.append(zs * cos + pltpu.roll(zs, HEAD_DIM // 2, 1) * sin)
    return parts[0] if len(parts) == 1 else jnp.concatenate(parts, axis=1)


def _proj_kernel(lay, idx_scale, q_scale, with_vt, x_ref, g_ref, shift_ref, scale_ref, w_ref, cos_ref, sin_ref,
                 zb_ref, k_ref, v_ref, misc_ref, u_ref, *rest):
    vt_ref, wt_ref = rest[:2] if with_vt else (None, None)
    h_ref = rest[-1]
    j = pl.program_id(1)

    @pl.when(j == 0)
    def _():
        x = x_ref[...]
        r = lax.rsqrt(jnp.mean(x * x, axis=-1, keepdims=True) + EPS)
        h = (x * r * g_ref[...]) * (1.0 + scale_ref[0]) + shift_ref[0]
        h_ref[...] = h.astype(BF16)

    def tile():
        return jnp.dot(h_ref[...], w_ref[...], preferred_element_type=F32)

    def store_heads(ref, val):
        if len(ref.shape) == 2:
            ref[...] = val
        else:
            for g in range(N_KV_HEADS):
                ref[:, g, :] = val[:, g * HEAD_DIM:(g + 1) * HEAD_DIM]

    @pl.when(j < lay.qi)
    def _():
        zb_ref[...] = (_rope_tile(tile(), cos_ref[...], sin_ref[...]) * q_scale).astype(BF16)

    @pl.when((j >= lay.qi) & (j < lay.ga))
    def _():
        zb_ref[...] = _rope_tile(tile(), cos_ref[...], sin_ref[...]).astype(BF16)

    @pl.when(j == lay.k)
    def _():
        r = _rope_tile(tile(), cos_ref[...], sin_ref[...])
        zb_ref[...] = r.astype(BF16)
        store_heads(k_ref, r)

    @pl.when(((j >= lay.ga) & (j < lay.ma)) | ((j >= lay.gb) & (j < lay.k)))
    def _():
        z = tile()
        zb_ref[...] = (z * _sigmoid(z)).astype(BF16)

    @pl.when((j >= lay.ma) & (j < lay.gb))
    def _():
        zb_ref[...] = _sigmoid(tile()).astype(BF16)

    @pl.when(j == lay.v)
    def _():
        z = tile()
        zb_ref[...] = z.astype(BF16)
        store_heads(v_ref, z)
        if with_vt:
            vt_ref[0] = z.T.astype(BF16)

    @pl.when(j == lay.misc)
    def _():
        z = tile()
        ki = _rope_tile(z[:, :IDX_DIM], cos_ref[...], sin_ref[...])
        m = jnp.concatenate([ki, z[:, IDX_DIM:] * idx_scale], axis=1)
        zb_ref[...] = m.astype(BF16)
        misc_ref[...] = m
        if with_vt:
            wt_ref[...] = m[:, IDX_DIM:2 * IDX_DIM].T

    @pl.when(j >= lay.u)
    def _():
        u_ref[...] = tile()


def _project(x, g, shift, scale, w_p, cos, sin, lay, tm, tiles_per_group, tiles_per_pos, idx_scale, with_vt):
    m, d = x.shape
    grp, r, _ = shift.shape
    nu = lay.end - lay.u
    kern = functools.partial(_proj_kernel, lay, idx_scale, Q_SCALE, with_vt)
    once = dict(pipeline_mode=pl.Buffered(1))
    out_specs = [pl.BlockSpec((tm, TN), lambda i, j: (i, jnp.minimum(j, lay.nb - 1))),
                 pl.BlockSpec((tm, TN), lambda i, j: (i, 0), **once),
                 pl.BlockSpec((tm, TN), lambda i, j: (i, 0), **once),
                 pl.BlockSpec((tm, TN), lambda i, j: (i, 0), **once),
                 pl.BlockSpec((tm, TN), lambda i, j: (i, jnp.clip(j - lay.u, 0, nu - 1)))]
    out_shape = [jax.ShapeDtypeStruct((m, lay.nb * TN), BF16),
                 jax.ShapeDtypeStruct((m, TN), F32),
                 jax.ShapeDtypeStruct((m, TN), F32),
                 jax.ShapeDtypeStruct((m, TN), F32),
                 jax.ShapeDtypeStruct((m, nu * TN), F32)]
    if with_vt:
        for o in (1, 2):
            out_specs[o] = pl.BlockSpec((tm, N_KV_HEADS, HEAD_DIM), lambda i, j: (i, 0, 0), **once)
            out_shape[o] = jax.ShapeDtypeStruct((m, N_KV_HEADS, HEAD_DIM), F32)
        out_specs.append(pl.BlockSpec((1, TN, tm), lambda i, j: (i // tiles_per_group, 0, i % tiles_per_group),
                                      **once))
        out_shape.append(jax.ShapeDtypeStruct((grp, TN, tiles_per_group * tm), BF16))
        out_specs.append(pl.BlockSpec((IDX_DIM, tm), lambda i, j: (0, i), **once))
        out_shape.append(jax.ShapeDtypeStruct((IDX_DIM, m), F32))
    return pl.pallas_call(
        kern,
        grid=(m // tm, lay.end),
        in_specs=[pl.BlockSpec((tm, d), lambda i, j: (i, 0)),
                  pl.BlockSpec((1, d), lambda i, j: (0, 0)),
                  pl.BlockSpec((1, r, d), lambda i, j: (i // tiles_per_group, 0, 0)),
                  pl.BlockSpec((1, r, d), lambda i, j: (i // tiles_per_group, 0, 0)),
                  pl.BlockSpec((d, TN), lambda i, j: (0, j)),
                  pl.BlockSpec((tm, HEAD_DIM), lambda i, j: (i % tiles_per_pos, 0)),
                  pl.BlockSpec((tm, HEAD_DIM), lambda i, j: (i % tiles_per_pos, 0))],
        out_specs=out_specs,
        out_shape=out_shape,
        scratch_shapes=[pltpu.VMEM((tm, d), BF16)],
        compiler_params=_cparams(("arbitrary", "arbitrary")),
    )(x, g, shift, scale, w_p, cos, sin)


def _bisect(count_ge, lo, hi, clo, kf):
    def cond(st):
        return (st[4] > 0.0) & (st[5] < MAX_BISECT)

    def halve(lo, hi, clo, done):
        mid = 0.5 * lo + 0.5 * hi
        mid = jnp.where(mid <= lo, hi, mid)
        c = count_ge(mid)
        ge = c >= kf
        stuck = (mid >= hi) & jnp.logical_not(ge)
        lo2 = jnp.where(ge, mid, lo)
        hi2 = jnp.where(ge, hi, mid)
        clo2 = jnp.where(ge, c, clo)
        fin = (clo2 <= kf) | (lo2 >= hi2) | stuck
        return lo2, hi2, clo2, jnp.maximum(done, jnp.where(fin, 1.0, 0.0))

    def body(st):
        lo, hi, clo, done = halve(*halve(*st[:4]))
        return lo, hi, clo, done, jnp.sum(1.0 - done), st[5] + 1

    done0 = jnp.where((clo <= kf) | (lo >= hi), 1.0, 0.0)
    st = (lo, hi, clo, done0, jnp.sum(1.0 - done0), jnp.int32(0))
    out = lax.while_loop(cond, body, st)
    return out[0], out[2]


def _attn_kernel(topk, q_ref, qi_ref, sga_ref, k_ref, ki_ref, vt_ref, wt_ref, o_ref,
                 score_ref, bias_ref, cut_ref, m_ref, l_ref, alpha_ref, acc_ref, lg_ref, p_ref):
    i = pl.program_id(1)
    nck = i + 1
    n_heads = q_ref.shape[1] // HEAD_DIM
    group = n_heads // N_KV_HEADS
    nt = (((1,), (1,)), ((), ()))
    q_pos = i * TQ + lax.broadcasted_iota(jnp.int32, (1, TQ), 1)
    fold = lambda a: a.reshape(CK // SUBLANES, SUBLANES, TQ)

    def score_chunk(c, carry):
        mn, mx = carry
        off = pl.multiple_of(c * CK, CK)
        kc = ki_ref[pl.ds(off, CK), :]
        acc = jnp.zeros((CK, TQ), F32)
        for h in range(IDX_HEADS):
            s = lax.dot_general(kc, qi_ref[:, h * IDX_DIM:(h + 1) * IDX_DIM], nt,
                                preferred_element_type=F32)
            acc = acc + wt_ref[h:h + 1, :] * jnp.maximum(s, 0.0)
        k_pos = off + lax.broadcasted_iota(jnp.int32, (CK, 1), 0)
        causal = k_pos <= q_pos
        score_ref[pl.ds(off, CK), :] = jnp.where(causal, acc, -jnp.inf)
        mn = jnp.minimum(mn, fold(jnp.where(causal, acc, jnp.inf)).min(axis=0))
        mx = jnp.maximum(mx, fold(jnp.where(causal, acc, -jnp.inf)).max(axis=0))
        return mn, mx

    mn, mx = lax.fori_loop(0, nck, score_chunk,
                           (jnp.full((SUBLANES, TQ), jnp.inf, F32),
                            jnp.full((SUBLANES, TQ), -jnp.inf, F32)))
    lo = mn.min(axis=0, keepdims=True)
    hi = mx.max(axis=0, keepdims=True)

    ones_rows = jnp.ones((SUBLANES, CK), BF16)

    def count_ge(thr):
        def body(c, cnt):
            s = score_ref[pl.ds(pl.multiple_of(c * CK, CK), CK), :]
            return cnt + fold(jnp.where(s >= thr, 1.0, 0.0)).sum(axis=0)
        cnt = lax.fori_loop(0, nck, body, jnp.zeros((SUBLANES, TQ), F32))
        return cnt.sum(axis=0, keepdims=True)

    n_valid = (q_pos + 1).astype(F32)
    kf = float(topk)
    thr, n_ge = _bisect(count_ge, lo, hi, n_valid, kf)

    s_len = score_ref.shape[0]
    has_ties = jnp.sum(jnp.where(n_ge > kf, 1.0, 0.0)) > 0.0
    cut_ref[...] = jnp.full(cut_ref.shape, float(s_len), F32)

    @pl.when(has_ties)
    def _():
        def count(pred):
            def body(c, cnt):
                off = pl.multiple_of(c * CK, CK)
                k_pos = (off + lax.broadcasted_iota(jnp.int32, (CK, 1), 0)).astype(F32)
                hit = pred(score_ref[pl.ds(off, CK), :], k_pos)
                return cnt + fold(jnp.where(hit, 1.0, 0.0)).sum(axis=0)
            cnt = lax.fori_loop(0, nck, body, jnp.zeros((SUBLANES, TQ), F32))
            return cnt.sum(axis=0, keepdims=True)

        need = kf - count(lambda s, kp: s > thr)

        def step(_, st):
            lo_i, hi_i = st
            mid = jnp.floor(0.5 * (lo_i + hi_i))
            ge = count(lambda s, kp: (s == thr) & (kp <= mid)) >= need
            return jnp.where(ge, lo_i, mid), jnp.where(ge, mid, hi_i)

        st0 = (jnp.full((1, TQ), -1.0, F32), jnp.full((1, TQ), s_len - 1.0, F32))
        cut = lax.fori_loop(0, (s_len - 1).bit_length() + 1, step, st0)[1]
        cut_ref[...] = jnp.broadcast_to(cut, cut_ref.shape)

    m_ref[...] = jnp.full(m_ref.shape, NEG, F32)
    l_ref[...] = jnp.zeros(l_ref.shape, F32)
    acc_ref[...] = jnp.zeros(acc_ref.shape, F32)

    def att_chunk(c, carry):
        off = pl.multiple_of(c * CK, CK)
        sc = score_ref[pl.ds(off, CK), :]

        @pl.when(jnp.logical_not(has_ties))
        def _():
            bias_ref[...] = jnp.where(sc >= thr, 0.0, NEG)

        @pl.when(has_ties)
        def _():
            k_pos = (off + lax.broadcasted_iota(jnp.int32, (CK, 1), 0)).astype(F32)
            keep = (sc > thr) | ((sc == thr) & (k_pos <= cut_ref[0:1, :]))
            bias_ref[...] = jnp.where(keep, 0.0, NEG)

        kc = k_ref[pl.ds(off, CK), :]
        vtc = vt_ref[0, :, pl.ds(off, CK)]

        def qk(h):
            g = h // group
            return lax.dot_general(kc[:, g * HEAD_DIM:(g + 1) * HEAD_DIM],
                                   q_ref[:, h * HEAD_DIM:(h + 1) * HEAD_DIM], nt,
                                   preferred_element_type=F32)

        for h in range(n_heads):
            lg = qk(h) + bias_ref[...]
            m_old = m_ref[h:h + 1, :]
            m_new = jnp.maximum(m_old, lg.max(axis=0, keepdims=True))
            lg_ref[h] = lg
            alpha_ref[h:h + 1, :] = jnp.exp2(m_old - m_new)
            m_ref[h:h + 1, :] = m_new
        for h in range(n_heads):
            p_ref[h] = jnp.exp2(lg_ref[h] - m_ref[h:h + 1, :]).astype(BF16)
        for h in range(n_heads):
            g = h // group
            pv = jnp.dot(vtc[g * HEAD_DIM:(g + 1) * HEAD_DIM, :], p_ref[h], preferred_element_type=F32)
            acc_ref[h] = alpha_ref[h:h + 1, :] * acc_ref[h] + pv
            psum = jnp.dot(ones_rows, p_ref[h], preferred_element_type=F32)[0:1, :]
            l_ref[h:h + 1, :] = alpha_ref[h:h + 1, :] * l_ref[h:h + 1, :] + psum
        return carry

    lax.fori_loop(0, nck, att_chunk, 0)

    for h in range(n_heads):
        o = (acc_ref[h] / l_ref[h:h + 1, :]).T
        sl = slice(h * HEAD_DIM, (h + 1) * HEAD_DIM)
        o_ref[:, sl] = (o * sga_ref[:, sl].astype(F32)).astype(BF16)


def _prompt_attention(zb, vt, wt, lay, b, s, d, topk):
    nq = s // TQ
    n_heads = d // HEAD_DIM
    kern = functools.partial(_attn_kernel, topk)
    row = lambda bb, i: bb * nq + i
    return pl.pallas_call(
        kern,
        grid=(b, nq),
        in_specs=[pl.BlockSpec((TQ, d), lambda bb, i: (row(bb, i), lay.q * TN // d)),
                  pl.BlockSpec((TQ, d), lambda bb, i: (row(bb, i), lay.qi * TN // d)),
                  pl.BlockSpec((TQ, d), lambda bb, i: (row(bb, i), lay.ga * TN // d)),
                  pl.BlockSpec((s, TN), lambda bb, i: (bb, lay.k)),
                  pl.BlockSpec((s, IDX_DIM), lambda bb, i: (bb, lay.misc * TN // IDX_DIM)),
                  pl.BlockSpec((1, TN, s), lambda bb, i: (bb, 0, 0)),
                  pl.BlockSpec((IDX_HEADS, TQ), lambda bb, i: (0, row(bb, i)))],
        out_specs=pl.BlockSpec((TQ, d), lambda bb, i: (row(bb, i), 0)),
        out_shape=jax.ShapeDtypeStruct((b * s, d), BF16),
        scratch_shapes=[pltpu.VMEM((s, TQ), F32),
                        pltpu.VMEM((CK, TQ), F32),
                        pltpu.VMEM((SUBLANES, TQ), F32),
                        pltpu.VMEM((n_heads, TQ), F32),
                        pltpu.VMEM((n_heads, TQ), F32),
                        pltpu.VMEM((n_heads, TQ), F32),
                        pltpu.VMEM((n_heads, HEAD_DIM, TQ), F32),
                        pltpu.VMEM((n_heads, CK, TQ), F32),
                        pltpu.VMEM((n_heads, CK, TQ), BF16)],
        compiler_params=_cparams(("arbitrary", "arbitrary")),
    )(zb, zb, zb, zb, zb, vt, wt)


def _pool_kernel(ts, u_ref, sgb_ref, wp_ref, sp_ref, o_ref, ext_ref):
    i = pl.program_id(1)
    halo = POOL_STATE + 1

    @pl.when(i == 0)
    def _():
        ext_ref[0:halo, :] = jnp.zeros((halo, ext_ref.shape[1]), F32)

    @pl.when(i > 0)
    def _():
        ext_ref[0:halo, :] = ext_ref[ts:ts + halo, :]

    ext_ref[halo:halo + ts, :] = u_ref[...]
    pos = i * ts + lax.broadcasted_iota(jnp.int32, (ts, 1), 0)
    gd = wp_ref.shape[1]
    for g, w in enumerate(POOL_WINDOWS):
        cols = slice(g * gd, (g + 1) * gd)
        acc = ext_ref[halo:halo + ts, cols]
        for jj in range(1, w):
            acc = acc + ext_ref[halo - jj:halo - jj + ts, cols]
        cnt = jnp.minimum(pos + 1, w).astype(F32)
        pooled = acc / cnt - u_ref[:, cols]
        mixed = jnp.dot(pooled.astype(BF16), wp_ref[g], preferred_element_type=F32) * sp_ref[:, cols]
        o_ref[:, cols] = (mixed * sgb_ref[:, cols].astype(F32)).astype(BF16)


def _pool_prompt(u, zb, wp, sp, lay, b, s, ts):
    m, pw = u.shape
    nt = s // ts
    kern = functools.partial(_pool_kernel, ts)
    return pl.pallas_call(
        kern,
        grid=(b, nt),
        in_specs=[pl.BlockSpec((ts, pw), lambda bb, i: (bb * nt + i, 0)),
                  pl.BlockSpec((ts, pw), lambda bb, i: (bb * nt + i, lay.gb * TN // pw)),
                  pl.BlockSpec(wp.shape, lambda bb, i: (0, 0, 0)),
                  pl.BlockSpec((1, pw), lambda bb, i: (0, 0))],
        out_specs=pl.BlockSpec((ts, pw), lambda bb, i: (bb * nt + i, 0)),
        out_shape=jax.ShapeDtypeStruct((m, pw), BF16),
        scratch_shapes=[pltpu.VMEM((POOL_STATE + 1 + ts, pw), F32)],
        compiler_params=_cparams(("arbitrary", "arbitrary")),
    )(u, zb, wp, sp)


def _pool_step_kernel(cnts, ext_ref, sgb_ref, wp_ref, sp_ref, o_ref):
    gd = wp_ref.shape[1]
    rows = ext_ref.shape[0]
    for g, w in enumerate(POOL_WINDOWS):
        cols = slice(g * gd, (g + 1) * gd)
        acc = ext_ref[rows - 1, :, cols]
        for jj in range(1, w):
            acc = acc + ext_ref[rows - 1 - jj, :, cols]
        pooled = acc / cnts[g] - ext_ref[rows - 1, :, cols]
        mixed = jnp.dot(pooled.astype(BF16), wp_ref[g], preferred_element_type=F32) * sp_ref[:, cols]
        o_ref[:, cols] = (mixed * sgb_ref[:, cols].astype(F32)).astype(BF16)


def _pool_step(ext_t, sgb, wp, sp, past):
    rows, n, pw = ext_t.shape
    cnts = tuple(float(min(past + 1, w)) for w in POOL_WINDOWS)
    return pl.pallas_call(
        functools.partial(_pool_step_kernel, cnts),
        out_shape=jax.ShapeDtypeStruct((n, pw), BF16),
    )(ext_t, sgb, wp, sp)


def _merge1_kernel(ab_ref, pb_ref, wa_ref, wb_ref, sma_ref, smb_ref, o_ref):
    a = jnp.dot(ab_ref[...], wa_ref[...], preferred_element_type=F32)
    bb = jnp.dot(pb_ref[...], wb_ref[...], preferred_element_type=F32)
    o_ref[...] = (sma_ref[...].astype(F32) * a + smb_ref[...].astype(F32) * bb).astype(BF16)


def _merge1(ab, pb, wa, wb, zb, lay, tm):
    m, d = ab.shape
    pw = pb.shape[1]
    return pl.pallas_call(
        _merge1_kernel,
        grid=(m // tm, d // TN),
        in_specs=[pl.BlockSpec((tm, d), lambda i, j: (i, 0)),
                  pl.BlockSpec((tm, pw), lambda i, j: (i, 0)),
                  pl.BlockSpec((d, TN), lambda i, j: (0, j)),
                  pl.BlockSpec((pw, TN), lambda i, j: (0, j)),
                  pl.BlockSpec((tm, TN), lambda i, j: (i, lay.ma + j)),
                  pl.BlockSpec((tm, TN), lambda i, j: (i, lay.mb + j))],
        out_specs=pl.BlockSpec((tm, TN), lambda i, j: (i, j)),
        out_shape=jax.ShapeDtypeStruct((m, d), BF16),
        compiler_params=_cparams(("arbitrary", "arbitrary")),
    )(ab, pb, wa, wb, zb, zb)


def _merge2_kernel(final, mm_ref, x_ref, gate_ref, wo_ref, gf_ref, o_ref):
    y = x_ref[...] + gate_ref[0] * jnp.dot(mm_ref[...], wo_ref[...], preferred_element_type=F32)
    if final:
        y = y * lax.rsqrt(jnp.mean(y * y, axis=-1, keepdims=True) + EPS) * gf_ref[...]
    o_ref[...] = y


def _merge2(mm, x, gate, wo, gf, tm, tiles_per_group, final):
    m, d = x.shape
    r = gate.shape[1]
    return pl.pallas_call(
        functools.partial(_merge2_kernel, final),
        grid=(m // tm,),
        in_specs=[pl.BlockSpec((tm, d), lambda i: (i, 0)),
                  pl.BlockSpec((tm, d), lambda i: (i, 0)),
                  pl.BlockSpec((1, r, d), lambda i: (i // tiles_per_group, 0, 0)),
                  pl.BlockSpec((d, d), lambda i: (0, 0)),
                  pl.BlockSpec((1, d), lambda i: (0, 0))],
        out_specs=pl.BlockSpec((tm, d), lambda i: (i, 0)),
        out_shape=jax.ShapeDtypeStruct((m, d), F32),
        compiler_params=_cparams(("arbitrary",)),
    )(mm, x, gate, wo, gf)


def _page_specs(width, npg, npp):
    return [pl.BlockSpec((1, PAGE_SIZE, width),
                         lambda b, p, pt, r=r: (pt[b * npg + p * npp + r], 0, 0)) for r in range(npp)]


def _sidx_kernel(npp, pt_ref, *refs):
    kidx_refs = refs[:npp]
    qi_ref, w_ref, o_ref = refs[npp:]
    nt = (((1,), (1,)), ((), ()))
    qi, w = qi_ref[0], w_ref[0]
    rows = []
    for r in range(npp):
        s = lax.dot_general(qi, kidx_refs[r][0].astype(BF16), nt, preferred_element_type=F32)
        rows.append(jnp.sum(w * jnp.maximum(s, 0.0), axis=0, keepdims=True))
    o_ref[0] = jnp.concatenate(rows, axis=1)


def _sample_scores(page_table, cache_kidx, qi3, wcol, npp):
    n, npg = page_table.shape
    return pl.pallas_call(
        functools.partial(_sidx_kernel, npp),
        grid_spec=pltpu.PrefetchScalarGridSpec(
            num_scalar_prefetch=1,
            grid=(n, npg // npp),
            in_specs=_page_specs(IDX_DIM, npg, npp) + [
                pl.BlockSpec((1, IDX_HEADS, IDX_DIM), lambda b, p, pt: (b, 0, 0)),
                pl.BlockSpec((1, IDX_HEADS, 1), lambda b, p, pt: (b, 0, 0))],
            out_specs=pl.BlockSpec((1, 1, npp * PAGE_SIZE), lambda b, p, pt: (b, 0, p))),
        out_shape=jax.ShapeDtypeStruct((n, 1, npg * PAGE_SIZE), F32),
        compiler_params=_cparams(("arbitrary", "arbitrary")),
    )(page_table.reshape(-1), *([cache_kidx] * npp), qi3, wcol)


def _sthr_kernel(topk, sc_ref, qi_ref, kin_ref, w_ref, pos_ref, cnt_ref, sel_ref, thr_scr, new_scr):
    sc = sc_ref[...]
    kin = kin_ref[...].astype(BF16).astype(F32)
    s_new = jnp.sum(qi_ref[...].astype(F32) * kin, axis=2, keepdims=True)
    new = jnp.sum(w_ref[...] * jnp.maximum(s_new, 0.0), axis=1, keepdims=True)
    red = lambda f, a: f(f(a, axis=2, keepdims=True), axis=1, keepdims=True)
    lo = jnp.minimum(red(jnp.min, sc), new)
    hi = jnp.maximum(red(jnp.max, sc), new)

    def count_ge(thr):
        c = red(jnp.sum, jnp.where(sc >= thr, 1.0, 0.0))
        return c + jnp.where(new >= thr, 1.0, 0.0)

    n_all = jnp.full(new.shape, float(sc.shape[1] * sc.shape[2] + 1), F32)
    thr, _ = _bisect(count_ge, lo, hi, n_all, float(topk))
    thr_scr[...] = jnp.broadcast_to(thr, thr_scr.shape)
    new_scr[...] = jnp.broadcast_to(new, new_scr.shape)

    rows = sc.shape[1]
    ii = lambda shape, ax: lax.broadcasted_iota(jnp.int32, shape, ax)
    one = lambda m: jnp.where(m, 1.0, 0.0)
    tri_lane = one(ii((PAGE_SIZE, PAGE_SIZE), 0) <= ii((PAGE_SIZE, PAGE_SIZE), 1)).astype(BF16)
    lower = one(ii((rows, rows), 1) < ii((rows, rows), 0)).astype(BF16)
    upper = one(ii((rows, rows), 0) < ii((rows, rows), 1)).astype(BF16)
    ones8 = jnp.ones((SUBLANES, PAGE_SIZE), BF16)
    nt = (((1,), (1,)), ((), ()))
    total = lambda a: jnp.sum(jnp.sum(a, axis=1, keepdims=True), axis=0, keepdims=True)
    dotf = lambda a, b: jnp.dot(a.astype(BF16), b.astype(BF16), preferred_element_type=F32)
    kf = float(topk)
    j_col = ii((topk, 1), 0).astype(F32)
    r_row = ii((1, rows), 1).astype(F32)
    c_row = ii((1, PAGE_SIZE), 1).astype(F32)

    def compact(b, carry):
        s = sc_ref[b]
        t = thr_scr[b][:, 0:1]
        nw = new_scr[b][:, 0:1]
        gt, eq = s > t, s == t
        need = kf - total(one(gt)) - one(nw > t)
        e_lane = dotf(one(eq), tri_lane)
        e_rank = e_lane + dotf(lower, e_lane)[:, PAGE_SIZE - 1:PAGE_SIZE]
        x = one(gt | (eq & (e_rank <= need)))
        sel_new = (nw > t) | ((nw == t) & (total(one(eq)) < need))
        x_lane = dotf(x, tri_lane)
        n_rows = lax.dot_general(ones8, x.astype(BF16), nt, preferred_element_type=F32)
        n_row = n_rows[0:1, :]
        start_row = dotf(n_rows, upper)[0:1, :]
        hit = one((start_row <= j_col) & (j_col < start_row + n_row))
        lane_sum = lambda a: jnp.sum(a, axis=1, keepdims=True)
        page = lane_sum(hit * r_row)
        target = j_col - lane_sum(hit * start_row) + 1.0
        in_row = dotf(hit, x_lane * x)
        off = lane_sum(jnp.where(in_row == target, c_row, 0.0))
        pos_ref[b] = (page * PAGE_SIZE + off).astype(jnp.int32)
        cnt_ref[b] = jnp.broadcast_to(total(x), cnt_ref.shape[1:])
        sel_ref[b] = jnp.broadcast_to(one(sel_new), sel_ref.shape[1:])
        return carry

    lax.fori_loop(0, sc.shape[0], compact, 0)


def _sample_select(sc, qi3, kin, wcol, topk):
    n = sc.shape[0]
    return pl.pallas_call(
        functools.partial(_sthr_kernel, topk),
        out_shape=[jax.ShapeDtypeStruct((n, topk, 1), jnp.int32),
                   jax.ShapeDtypeStruct((n, 1, PAGE_SIZE), F32),
                   jax.ShapeDtypeStruct((n, 1, PAGE_SIZE), F32)],
        scratch_shapes=[pltpu.VMEM((n, 1, PAGE_SIZE), F32),
                        pltpu.VMEM((n, 1, PAGE_SIZE), F32)],
        compiler_params=pltpu.CompilerParams(vmem_limit_bytes=VMEM_LIMIT),
    )(sc, qi3, kin, wcol)


def _row_copy(cache_hbm, buf, sem, layer, page, off, j):
    return pltpu.make_async_copy(cache_hbm.at[layer, page, off], buf.at[j], sem)


def _sattn_kernel(layer, npg, pos_ref, pt_ref, ck_hbm, cv_hbm, cnt_ref, sel_ref, q_ref, kn_ref, vn_ref,
                  sga_ref, o_ref, kbuf, vbuf, ksem, vsem):
    b = pl.program_id(0)
    slot = b % 2
    topk = kbuf.shape[1]
    n_heads = q_ref.shape[1]
    group = n_heads // N_KV_HEADS
    nt = (((1,), (1,)), ((), ()))

    def fetch(bb, s):
        def issue(j, carry):
            pos = pos_ref[bb * topk + j]
            page = pt_ref[bb * npg + pos // PAGE_SIZE]
            off = pos % PAGE_SIZE
            _row_copy(ck_hbm, kbuf.at[s], ksem.at[s], layer, page, off, j).start()
            _row_copy(cv_hbm, vbuf.at[s], vsem.at[s], layer, page, off, j).start()
            return carry
        lax.fori_loop(0, topk, issue, 0)

    @pl.when(b == 0)
    def _():
        fetch(b, slot)

    @pl.when(b + 1 < pl.num_programs(0))
    def _():
        fetch(b + 1, 1 - slot)

    kbuf, vbuf = kbuf.at[slot], vbuf.at[slot]
    if topk % PAGE_SIZE == 0:
        for w in range(topk // PAGE_SIZE):
            rows = pl.ds(w * PAGE_SIZE, PAGE_SIZE)
            pltpu.make_async_copy(ck_hbm.at[layer, 0], kbuf.at[rows], ksem.at[slot]).wait()
            pltpu.make_async_copy(cv_hbm.at[layer, 0], vbuf.at[rows], vsem.at[slot]).wait()
    else:
        def wait(j, carry):
            _row_copy(ck_hbm, kbuf, ksem.at[slot], layer, 0, 0, j).wait()
            _row_copy(cv_hbm, vbuf, vsem.at[slot], layer, 0, 0, j).wait()
            return carry

        lax.fori_loop(0, topk, wait, 0)

    q = q_ref[0]
    head = lax.broadcasted_iota(jnp.int32, (n_heads, 1), 0)
    in_group = [(head >= g * group) & (head < (g + 1) * group) for g in range(N_KV_HEADS)]
    lg = jnp.zeros((n_heads, topk), F32)
    for g in range(N_KV_HEADS):
        lg_g = lax.dot_general(q, kbuf[:, g, :].astype(BF16), nt, preferred_element_type=F32)
        lg = lg + jnp.where(in_group[g], lg_g, 0.0)
    valid = lax.broadcasted_iota(jnp.int32, (1, topk), 1).astype(F32) < cnt_ref[0][:, 0:1]
    lg = jnp.where(valid, lg, NEG)
    kn = kn_ref[0].astype(BF16).astype(F32)
    vn = vn_ref[0].astype(BF16).astype(F32)
    lgn = jnp.sum(q.astype(F32) * kn, axis=1, keepdims=True)
    lgn = jnp.where(sel_ref[0][:, 0:1] > 0.0, lgn, NEG)
    m = jnp.maximum(lg.max(axis=1, keepdims=True), lgn)
    p = jnp.exp2(lg - m)
    pn = jnp.exp2(lgn - m)
    l = p.sum(axis=1, keepdims=True) + pn
    pb = p.astype(BF16)
    acc = pn.astype(BF16).astype(F32) * vn
    for g in range(N_KV_HEADS):
        pv = jnp.dot(pb, vbuf[:, g, :].astype(BF16), preferred_element_type=F32)
        acc = acc + jnp.where(in_group[g], pv, 0.0)
    o_ref[0] = ((acc / l) * sga_ref[0].astype(F32)).astype(BF16)


def _sample_attention(page_table, cache_k, cache_v, layer, pos, cnt, sel, q3, kn, vn, sga3):
    n, npg = page_table.shape
    n_heads = q3.shape[1]
    topk = pos.shape[1]
    per_b = lambda shape: pl.BlockSpec(shape, lambda b, ps, pt: (b, 0, 0))
    hbm = pl.BlockSpec(memory_space=pl.ANY)
    return pl.pallas_call(
        functools.partial(_sattn_kernel, layer, npg),
        grid_spec=pltpu.PrefetchScalarGridSpec(
            num_scalar_prefetch=2,
            grid=(n,),
            in_specs=[hbm, hbm,
                      per_b((1, 1, PAGE_SIZE)),
                      per_b((1, 1, PAGE_SIZE)),
                      per_b((1, n_heads, HEAD_DIM)),
                      per_b((1, n_heads, HEAD_DIM)),
                      per_b((1, n_heads, HEAD_DIM)),
                      per_b((1, n_heads, HEAD_DIM))],
            out_specs=per_b((1, n_heads, HEAD_DIM)),
            scratch_shapes=[pltpu.VMEM((2, topk, N_KV_HEADS, HEAD_DIM), cache_k.dtype),
                            pltpu.VMEM((2, topk, N_KV_HEADS, HEAD_DIM), cache_v.dtype),
                            pltpu.SemaphoreType.DMA((2,)),
                            pltpu.SemaphoreType.DMA((2,))]),
        out_shape=jax.ShapeDtypeStruct((n, n_heads, HEAD_DIM), BF16),
        compiler_params=_cparams(("arbitrary",)),
    )(pos.reshape(-1), page_table.reshape(-1), cache_k, cache_v, cnt, sel, q3, kn, vn, sga3)


def _rope_tables(pos):
    half = HEAD_DIM // 2
    inv = ROPE_THETA ** (-jnp.arange(half, dtype=F32) / half)
    ang = pos.astype(F32)[:, None] * inv[None, :]
    cos, sin = jnp.cos(ang), jnp.sin(ang)
    return jnp.concatenate([cos, cos], axis=1), jnp.concatenate([-sin, sin], axis=1)


def _tile_starts(d, lay):
    att, kv, idx, pool = d, N_KV_HEADS * HEAD_DIM, IDX_HEADS * IDX_DIM, d // 2
    sizes = (att, kv, kv, idx, IDX_DIM, IDX_HEADS, att, pool, pool, d, d)
    offs = [0]
    for n in sizes:
        offs.append(offs[-1] + n)
    q, k, v, qi, ki, wi, ga, u, gb, ma, mb = offs[:-1]
    starts = [0] * lay.end
    for first, tiles, col in ((lay.q, lay.qi - lay.q, q), (lay.qi, lay.ga - lay.qi, qi),
                              (lay.ga, lay.ma - lay.ga, ga), (lay.ma, lay.mb - lay.ma, ma),
                              (lay.mb, lay.gb - lay.mb, mb), (lay.gb, lay.k - lay.gb, gb),
                              (lay.k, 1, k), (lay.v, 1, v), (lay.misc, 1, ki), (lay.u, lay.end - lay.u, u)):
        for t in range(tiles):
            starts[first + t] = col + t * TN
    assert wi == ki + IDX_DIM and all(s + TN <= offs[-1] and s % SUBLANES == 0 for s in starts)
    return starts


def _pack_kernel(misc_tile, starts_ref, w_hbm, o_ref, buf, sem):
    t = pl.program_id(0)
    slot = t % 2

    def copy(tt, s):
        first = pl.multiple_of(starts_ref[tt], SUBLANES)
        return pltpu.make_async_copy(w_hbm.at[pl.ds(first, TN)], buf.at[s], sem.at[s])

    @pl.when(t == 0)
    def _():
        copy(t, slot).start()

    @pl.when(t + 1 < pl.num_programs(0))
    def _():
        copy(t + 1, 1 - slot).start()

    copy(t, slot).wait()
    x = buf[slot]
    row = lax.broadcasted_iota(jnp.int32, (TN, 1), 0)
    x = jnp.where((t == misc_tile) & (row >= IDX_DIM + IDX_HEADS), 0.0, x)
    o_ref[...] = x.T.astype(BF16)


def _pack_w_in(w_t, layer, in_width, d, lay):
    starts = jnp.asarray([layer * in_width + s for s in _tile_starts(d, lay)], jnp.int32)
    return pl.pallas_call(
        functools.partial(_pack_kernel, lay.misc),
        grid_spec=pltpu.PrefetchScalarGridSpec(
            num_scalar_prefetch=1,
            grid=(lay.end,),
            in_specs=[pl.BlockSpec(memory_space=pl.ANY)],
            out_specs=pl.BlockSpec((d, TN), lambda t, st: (0, t)),
            scratch_shapes=[pltpu.VMEM((2, TN, d), w_t.dtype),
                            pltpu.SemaphoreType.DMA((2,))]),
        out_shape=jax.ShapeDtypeStruct((d, lay.end * TN), BF16),
        compiler_params=_cparams(("arbitrary",)),
    )(starts, w_t)


def kernel(x_prompt, x_sample, c_prompt, c_sample, cache_k, cache_v, cache_kidx, state_pool, page_table,
           g_norm, w_ada, b_ada, w_in, w_pool, s_pool, w_a_proj, w_b_proj, w_out, g_final):
    bp, sp, d = x_prompt.shape
    ns, ds, _ = x_sample.shape
    depth = g_norm.shape[0]
    npg = page_table.shape[1]
    past = npg * PAGE_SIZE
    assert ds == 1 and d % (N_KV_HEADS * HEAD_DIM) == 0 and sp % TQ == 0 and sp >= POOL_STATE
    n_heads = d // HEAD_DIM
    kvw = N_KV_HEADS * HEAD_DIM
    lay = _Layout(d)
    pw = lay.pool
    idx_scale = (IDX_HEADS * IDX_DIM) ** -0.5
    topk_p = min(TOPK_MAX, sp // 4)
    topk_s = min(TOPK_MAX, (past + ds) // 4)
    tm = 1024
    ts = 512
    tm2 = 256

    cos_p, sin_p = _rope_tables(jnp.arange(sp, dtype=jnp.int32))
    cos_s, sin_s = _rope_tables(jnp.full((ns,), past, jnp.int32))
    n_c = bp + ns
    c_rows = -(-n_c // SUBLANES) * SUBLANES
    c_all = jnp.concatenate([c_prompt, c_sample, jnp.zeros((c_rows - n_c, d), F32)], axis=0)

    n_pool = cache_kidx.shape[1]
    kidx_pages = cache_kidx.reshape(depth * n_pool, PAGE_SIZE, IDX_DIM)
    npp = math.gcd(npg, PAGES_PER_STEP)
    assert npg % npp == 0

    in_width = w_in.shape[2]
    w_t = jnp.swapaxes(w_in, 1, 2).reshape(depth * in_width, d)

    xp = x_prompt.reshape(bp * sp, d)
    xs = x_sample.reshape(ns, d)
    outs = {n: [] for n in ("kp", "vp", "kip", "pp", "ks", "vs", "kis", "ps")}
    for l in range(depth):
        final = l == depth - 1
        w_p = _pack_w_in(w_t, l, in_width, d, lay)
        wa, wb, wo, wpl = (w_a_proj[l].astype(BF16), w_b_proj[l].astype(BF16), w_out[l].astype(BF16),
                           w_pool[l].astype(BF16))
        g = g_norm[l].reshape(1, d)
        spl = s_pool[l].reshape(1, pw)
        gf = g_final.reshape(1, d)

        mod = _ada(c_all, w_ada[l], b_ada[l])
        shift, scale, gate = mod[:, :d], mod[:, d:2 * d], mod[:, 2 * d:]
        pg = lambda a: a[:bp].reshape(bp, 1, d)
        sg = lambda a: a[bp:n_c].reshape(1, ns, d)

        zb, k_p, v_p, misc_p, u_p, vt, wt = _project(xp, g, pg(shift), pg(scale), w_p, cos_p, sin_p, lay, tm,
                                                     sp // tm, sp // tm, idx_scale, True)
        ab = _prompt_attention(zb, vt, wt, lay, bp, sp, d, topk_p)
        pb = _pool_prompt(u_p, zb, wpl, spl, lay, bp, sp, ts)
        mm = _merge1(ab, pb, wa, wb, zb, lay, tm)
        xp = _merge2(mm, xp, pg(gate), wo, gf, tm2, sp // tm2, final)
        outs["kp"].append(k_p.reshape(bp, sp, N_KV_HEADS, HEAD_DIM))
        outs["vp"].append(v_p.reshape(bp, sp, N_KV_HEADS, HEAD_DIM))
        outs["kip"].append(misc_p[:, :IDX_DIM].reshape(bp, sp, IDX_DIM))
        outs["pp"].append(u_p.reshape(bp, sp, pw)[:, sp - POOL_STATE:, :])

        zs, k_s, v_s, misc_s, u_s = _project(xs, g, sg(shift), sg(scale), w_p, cos_s, sin_s, lay, ns,
                                             1, 1, idx_scale, False)
        col = lambda t, n: zs[:, t * TN:t * TN + n]
        q3 = col(lay.q, d).reshape(ns, n_heads, HEAD_DIM)
        qi3 = col(lay.qi, IDX_HEADS * IDX_DIM).reshape(ns, IDX_HEADS, IDX_DIM)
        sga3 = col(lay.ga, d).reshape(ns, n_heads, HEAD_DIM)
        wcol = misc_s[:, IDX_DIM:IDX_DIM + IDX_HEADS].reshape(ns, IDX_HEADS, 1)
        kin = misc_s[:, :IDX_DIM].reshape(ns, 1, IDX_DIM)
        pt_l = page_table + l * n_pool
        sc = _sample_scores(pt_l, kidx_pages, qi3, wcol, npp)
        pos, cnt, sel = _sample_select(sc.reshape(ns, npg, PAGE_SIZE), qi3, kin, wcol, topk_s)
        rep = lambda a: jnp.repeat(a.reshape(ns, N_KV_HEADS, HEAD_DIM), n_heads // N_KV_HEADS, axis=1)
        ab_s = _sample_attention(page_table, cache_k, cache_v, l, pos, cnt, sel, q3,
                                 rep(k_s), rep(v_s), sga3).reshape(ns, d)
        ext = jnp.concatenate([state_pool[l], u_s[:, None, :]], axis=1)
        pb_s = _pool_step(ext.transpose(1, 0, 2), col(lay.gb, pw), wpl, spl, past)
        mm_s = _merge1(ab_s, pb_s, wa, wb, zs, lay, ns)
        xs = _merge2(mm_s, xs, sg(gate), wo, gf, ns, 1, final)
        outs["ks"].append(k_s.reshape(ns, ds, N_KV_HEADS, HEAD_DIM))
        outs["vs"].append(v_s.reshape(ns, ds, N_KV_HEADS, HEAD_DIM))
        outs["kis"].append(misc_s[:, :IDX_DIM].reshape(ns, ds, IDX_DIM))
        outs["ps"].append(ext[:, 1:, :])

    st = lambda n: outs[n][0][None] if depth == 1 else jnp.stack(outs[n])
    return (xp.reshape(bp, sp, d), xs.reshape(ns, ds, d),
            st("kp"), st("vp"), st("kip"), st("pp"), st("ks"), st("vs"), st("kis"), st("ps"))
```

```python
import functools
import math

import jax
import jax.numpy as jnp
from jax import lax
from jax.experimental import pallas as pl
from jax.experimental.pallas import tpu as pltpu

F32 = jnp.float32
BF16 = jnp.bfloat16

HEAD_DIM = 128
N_KV_HEADS = 4
IDX_HEADS = 16
IDX_DIM = 128
TOPK_MAX = 256
PAGE_SIZE = 128
POOL_WINDOWS = (2, 4, 8, 16)
POOL_STATE = max(POOL_WINDOWS) - 1
ROPE_THETA = 10000.0
EPS = 1e-6

LANES = 128
SUBLANES = 8
TN = 512
TQ = 256
CK = 256
NEG = -1e30
Q_SCALE = 1.4426950408889634 * HEAD_DIM ** -0.5
PAGES_PER_STEP = 32
MAX_BISECT = 512
VMEM_LIMIT = 56 * 1024 * 1024


def _cparams(sem):
    return pltpu.CompilerParams(dimension_semantics=sem, vmem_limit_bytes=VMEM_LIMIT)


def _sigmoid(z):
    return 0.5 * jnp.tanh(0.5 * z) + 0.5


def _ada_kernel(c_ref, w_ref, b_ref, o_ref):
    o_ref[...] = jnp.dot(c_ref[...].astype(BF16), w_ref[...].astype(BF16),
                         preferred_element_type=F32) + b_ref[...]


def _ada(c, w, b):
    r, d = c.shape
    n = w.shape[1]
    return pl.pallas_call(
        _ada_kernel,
        grid=(n // TN,),
        in_specs=[pl.BlockSpec((r, d), lambda j: (0, 0)),
                  pl.BlockSpec((d, TN), lambda j: (0, j)),
                  pl.BlockSpec((1, TN), lambda j: (0, j))],
        out_specs=pl.BlockSpec((r, TN), lambda j: (0, j)),
        out_shape=jax.ShapeDtypeStruct((r, n), F32),
        compiler_params=_cparams(("arbitrary",)),
    )(c, w, b.reshape(1, n))


class _Layout:
    def __init__(self, d):
        att, kv, idx, pool = d, N_KV_HEADS * HEAD_DIM, IDX_HEADS * IDX_DIM, d // 2
        t = lambda n: n // TN
        assert att % TN == 0 and kv == TN and idx % TN == 0 and pool % TN == 0
        self.q = 0
        self.qi = self.q + t(att)
        self.ga = self.qi + t(idx)
        self.ma = self.ga + t(att)
        self.mb = self.ma + t(d)
        self.gb = self.mb + t(d)
        self.k = self.gb + t(pool)
        self.v = self.k + 1
        self.misc = self.v + 1
        self.u = self.misc + 1
        self.end = self.u + t(pool)
        self.nb = self.u
        self.pool = pool


def _rope_tile(z, cos, sin):
    parts = []
    for s in range(z.shape[1] // HEAD_DIM):
        zs = z[:, s * HEAD_DIM:(s + 1) * HEAD_DIM]
        parts.append(zs * cos + pltpu.roll(zs, HEAD_DIM // 2, 1) * sin)
    return parts[0] if len(parts) == 1 else jnp.concatenate(parts, axis=1)


def _proj_kernel(lay, idx_scale, q_scale, with_vt, x_ref, g_ref, shift_ref, scale_ref, w_ref, cos_ref, sin_ref,
                 zb_ref, k_ref, v_ref, misc_ref, u_ref, *rest):
    vt_ref, wt_ref = rest[:2] if with_vt else (None, None)
    h_ref = rest[-1]
    j = pl.program_id(1)

    @pl.when(j == 0)
    def _():
        x = x_ref[...]
        r = lax.rsqrt(jnp.mean(x * x, axis=-1, keepdims=True) + EPS)
        h = (x * r * g_ref[...]) * (1.0 + scale_ref[0]) + shift_ref[0]
        h_ref[...] = h.astype(BF16)

    def tile():
        return jnp.dot(h_ref[...], w_ref[...], preferred_element_type=F32)

    def store_heads(ref, val):
        if len(ref.shape) == 2:
            ref[...] = val
        else:
            for g in range(N_KV_HEADS):
                ref[:, g, :] = val[:, g * HEAD_DIM:(g + 1) * HEAD_DIM]

    @pl.when(j < lay.qi)
    def _():
        zb_ref[...] = (_rope_tile(tile(), cos_ref[...], sin_ref[...]) * q_scale).astype(BF16)

    @pl.when((j >= lay.qi) & (j < lay.ga))
    def _():
        zb_ref[...] = _rope_tile(tile(), cos_ref[...], sin_ref[...]).astype(BF16)

    @pl.when(j == lay.k)
    def _():
        r = _rope_tile(tile(), cos_ref[...], sin_ref[...])
        zb_ref[...] = r.astype(BF16)
        store_heads(k_ref, r)

    @pl.when(((j >= lay.ga) & (j < lay.ma)) | ((j >= lay.gb) & (j < lay.k)))
    def _():
        z = tile()
        zb_ref[...] = (z * _sigmoid(z)).astype(BF16)

    @pl.when((j >= lay.ma) & (j < lay.gb))
    def _():
        zb_ref[...] = _sigmoid(tile()).astype(BF16)

    @pl.when(j == lay.v)
    def _():
        z = tile()
        zb_ref[...] = z.astype(BF16)
        store_heads(v_ref, z)
        if with_vt:
            vt_ref[0] = z.T.astype(BF16)

    @pl.when(j == lay.misc)
    def _():
        z = tile()
        ki = _rope_tile(z[:, :IDX_DIM], cos_ref[...], sin_ref[...])
        m = jnp.concatenate([ki, z[:, IDX_DIM:] * idx_scale], axis=1)
        zb_ref[...] = m.astype(BF16)
        misc_ref[...] = m
        if with_vt:
            wt_ref[...] = m[:, IDX_DIM:2 * IDX_DIM].T

    @pl.when(j >= lay.u)
    def _():
        u_ref[...] = tile()


def _project(x, g, shift, scale, w_p, cos, sin, lay, tm, tiles_per_group, tiles_per_pos, idx_scale, with_vt):
    m, d = x.shape
    grp, r, _ = shift.shape
    nu = lay.end - lay.u
    kern = functools.partial(_proj_kernel, lay, idx_scale, Q_SCALE, with_vt)
    once = dict(pipeline_mode=pl.Buffered(1))
    out_specs = [pl.BlockSpec((tm, TN), lambda i, j: (i, jnp.minimum(j, lay.nb - 1))),
                 pl.BlockSpec((tm, TN), lambda i, j: (i, 0), **once),
                 pl.BlockSpec((tm, TN), lambda i, j: (i, 0), **once),
                 pl.BlockSpec((tm, TN), lambda i, j: (i, 0), **once),
                 pl.BlockSpec((tm, TN), lambda i, j: (i, jnp.clip(j - lay.u, 0, nu - 1)))]
    out_shape = [jax.ShapeDtypeStruct((m, lay.nb * TN), BF16),
                 jax.ShapeDtypeStruct((m, TN), F32),
                 jax.ShapeDtypeStruct((m, TN), F32),
                 jax.ShapeDtypeStruct((m, TN), F32),
                 jax.ShapeDtypeStruct((m, nu * TN), F32)]
    if with_vt:
        for o in (1, 2):
            out_specs[o] = pl.BlockSpec((tm, N_KV_HEADS, HEAD_DIM), lambda i, j: (i, 0, 0), **once)
            out_shape[o] = jax.ShapeDtypeStruct((m, N_KV_HEADS, HEAD_DIM), F32)
        out_specs.append(pl.BlockSpec((1, TN, tm), lambda i, j: (i // tiles_per_group, 0, i % tiles_per_group),
                                      **once))
        out_shape.append(jax.ShapeDtypeStruct((grp, TN, tiles_per_group * tm), BF16))
        out_specs.append(pl.BlockSpec((IDX_DIM, tm), lambda i, j: (0, i), **once))
        out_shape.append(jax.ShapeDtypeStruct((IDX_DIM, m), F32))
    return pl.pallas_call(
        kern,
        grid=(m // tm, lay.end),
        in_specs=[pl.BlockSpec((tm, d), lambda i, j: (i, 0)),
                  pl.BlockSpec((1, d), lambda i, j: (0, 0)),
                  pl.BlockSpec((1, r, d), lambda i, j: (i // tiles_per_group, 0, 0)),
                  pl.BlockSpec((1, r, d), lambda i, j: (i // tiles_per_group, 0, 0)),
                  pl.BlockSpec((d, TN), lambda i, j: (0, j)),
                  pl.BlockSpec((tm, HEAD_DIM), lambda i, j: (i % tiles_per_pos, 0)),
                  pl.BlockSpec((tm, HEAD_DIM), lambda i, j: (i % tiles_per_pos, 0))],
        out_specs=out_specs,
        out_shape=out_shape,
        scratch_shapes=[pltpu.VMEM((tm, d), BF16)],
        compiler_params=_cparams(("arbitrary", "arbitrary")),
    )(x, g, shift, scale, w_p, cos, sin)


def _bisect(count_ge, lo, hi, clo, kf):
    def cond(st):
        return (st[4] > 0.0) & (st[5] < MAX_BISECT)

    def halve(lo, hi, clo, done):
        mid = 0.5 * lo + 0.5 * hi
        mid = jnp.where(mid <= lo, hi, mid)
        c = count_ge(mid)
        ge = c >= kf
        stuck = (mid >= hi) & jnp.logical_not(ge)
        lo2 = jnp.where(ge, mid, lo)
        hi2 = jnp.where(ge, hi, mid)
        clo2 = jnp.where(ge, c, clo)
        fin = (clo2 <= kf) | (lo2 >= hi2) | stuck
        return lo2, hi2, clo2, jnp.maximum(done, jnp.where(fin, 1.0, 0.0))

    def body(st):
        lo, hi, clo, done = halve(*halve(*st[:4]))
        return lo, hi, clo, done, jnp.sum(1.0 - done), st[5] + 1

    done0 = jnp.where((clo <= kf) | (lo >= hi), 1.0, 0.0)
    st = (lo, hi, clo, done0, jnp.sum(1.0 - done0), jnp.int32(0))
    out = lax.while_loop(cond, body, st)
    return out[0], out[2]


def _attn_kernel(topk, q_ref, qi_ref, sga_ref, k_ref, ki_ref, vt_ref, wt_ref, o_ref,
                 score_ref, bias_ref, cut_ref, m_ref, l_ref, alpha_ref, acc_ref, lg_ref):
    i = pl.program_id(1)
    nck = i + 1
    n_heads = q_ref.shape[1] // HEAD_DIM
    group = n_heads // N_KV_HEADS
    nt = (((1,), (1,)), ((), ()))
    q_pos = i * TQ + lax.broadcasted_iota(jnp.int32, (1, TQ), 1)
    fold = lambda a: a.reshape(CK // SUBLANES, SUBLANES, TQ)

    def score_chunk(c, carry):
        mn, mx = carry
        off = pl.multiple_of(c * CK, CK)
        kc = ki_ref[pl.ds(off, CK), :]
        acc = jnp.zeros((CK, TQ), F32)
        for h in range(IDX_HEADS):
            s = lax.dot_general(kc, qi_ref[:, h * IDX_DIM:(h + 1) * IDX_DIM], nt,
                                preferred_element_type=F32)
            acc = acc + wt_ref[h:h + 1, :] * jnp.maximum(s, 0.0)
        k_pos = off + lax.broadcasted_iota(jnp.int32, (CK, 1), 0)
        causal = k_pos <= q_pos
        score_ref[pl.ds(off, CK), :] = jnp.where(causal, acc, -jnp.inf)
        mn = jnp.minimum(mn, fold(jnp.where(causal, acc, jnp.inf)).min(axis=0))
        mx = jnp.maximum(mx, fold(jnp.where(causal, acc, -jnp.inf)).max(axis=0))
        return mn, mx

    mn, mx = lax.fori_loop(0, nck, score_chunk,
                           (jnp.full((SUBLANES, TQ), jnp.inf, F32),
                            jnp.full((SUBLANES, TQ), -jnp.inf, F32)))
    lo = mn.min(axis=0, keepdims=True)
    hi = mx.max(axis=0, keepdims=True)

    ones_rows = jnp.ones((SUBLANES, CK), BF16)

    def count_ge(thr):
        def body(c, cnt):
            s = score_ref[pl.ds(pl.multiple_of(c * CK, CK), CK), :]
            return cnt + fold(jnp.where(s >= thr, 1.0, 0.0)).sum(axis=0)
        cnt = lax.fori_loop(0, nck, body, jnp.zeros((SUBLANES, TQ), F32))
        return cnt.sum(axis=0, keepdims=True)

    n_valid = (q_pos + 1).astype(F32)
    kf = float(topk)
    thr, n_ge = _bisect(count_ge, lo, hi, n_valid, kf)

    s_len = score_ref.shape[0]
    has_ties = jnp.sum(jnp.where(n_ge > kf, 1.0, 0.0)) > 0.0
    cut_ref[...] = jnp.full(cut_ref.shape, float(s_len), F32)

    @pl.when(has_ties)
    def _():
        def count(pred):
            def body(c, cnt):
                off = pl.multiple_of(c * CK, CK)
                k_pos = (off + lax.broadcasted_iota(jnp.int32, (CK, 1), 0)).astype(F32)
                hit = pred(score_ref[pl.ds(off, CK), :], k_pos)
                return cnt + fold(jnp.where(hit, 1.0, 0.0)).sum(axis=0)
            cnt = lax.fori_loop(0, nck, body, jnp.zeros((SUBLANES, TQ), F32))
            return cnt.sum(axis=0, keepdims=True)

        need = kf - count(lambda s, kp: s > thr)

        def step(_, st):
            lo_i, hi_i = st
            mid = jnp.floor(0.5 * (lo_i + hi_i))
            ge = count(lambda s, kp: (s == thr) & (kp <= mid)) >= need
            return jnp.where(ge, lo_i, mid), jnp.where(ge, mid, hi_i)

        st0 = (jnp.full((1, TQ), -1.0, F32), jnp.full((1, TQ), s_len - 1.0, F32))
        cut = lax.fori_loop(0, (s_len - 1).bit_length() + 1, step, st0)[1]
        cut_ref[...] = jnp.broadcast_to(cut, cut_ref.shape)

    m_ref[...] = jnp.full(m_ref.shape, NEG, F32)
    l_ref[...] = jnp.zeros(l_ref.shape, F32)
    acc_ref[...] = jnp.zeros(acc_ref.shape, F32)

    def att_chunk(c, carry):
        off = pl.multiple_of(c * CK, CK)
        sc = score_ref[pl.ds(off, CK), :]

        @pl.when(jnp.logical_not(has_ties))
        def _():
            bias_ref[...] = jnp.where(sc >= thr, 0.0, NEG)

        @pl.when(has_ties)
        def _():
            k_pos = (off + lax.broadcasted_iota(jnp.int32, (CK, 1), 0)).astype(F32)
            keep = (sc > thr) | ((sc == thr) & (k_pos <= cut_ref[0:1, :]))
            bias_ref[...] = jnp.where(keep, 0.0, NEG)

        kc = k_ref[pl.ds(off, CK), :]
        vtc = vt_ref[0, :, pl.ds(off, CK)]

        def qk(h):
            g = h // group
            return lax.dot_general(kc[:, g * HEAD_DIM:(g + 1) * HEAD_DIM],
                                   q_ref[:, h * HEAD_DIM:(h + 1) * HEAD_DIM], nt,
                                   preferred_element_type=F32)

        for h in range(n_heads):
            lg = qk(h) + bias_ref[...]
            m_old = m_ref[h:h + 1, :]
            m_new = jnp.maximum(m_old, lg.max(axis=0, keepdims=True))
            lg_ref[h] = lg
            alpha_ref[h:h + 1, :] = jnp.exp2(m_old - m_new)
            m_ref[h:h + 1, :] = m_new
        for h in range(n_heads):
            g = h // group
            p = jnp.exp2(lg_ref[h] - m_ref[h:h + 1, :]).astype(BF16)
            pv = jnp.dot(vtc[g * HEAD_DIM:(g + 1) * HEAD_DIM, :], p, preferred_element_type=F32)
            acc_ref[h] = alpha_ref[h:h + 1, :] * acc_ref[h] + pv
            psum = jnp.dot(ones_rows, p, preferred_element_type=F32)[0:1, :]
            l_ref[h:h + 1, :] = alpha_ref[h:h + 1, :] * l_ref[h:h + 1, :] + psum
        return carry

    lax.fori_loop(0, nck, att_chunk, 0)

    for h in range(n_heads):
        o = (acc_ref[h] / l_ref[h:h + 1, :]).T
        sl = slice(h * HEAD_DIM, (h + 1) * HEAD_DIM)
        o_ref[:, sl] = (o * sga_ref[:, sl].astype(F32)).astype(BF16)


def _prompt_attention(zb, vt, wt, lay, b, s, d, topk):
    nq = s // TQ
    n_heads = d // HEAD_DIM
    kern = functools.partial(_attn_kernel, topk)
    row = lambda bb, i: bb * nq + i
    return pl.pallas_call(
        kern,
        grid=(b, nq),
        in_specs=[pl.BlockSpec((TQ, d), lambda bb, i: (row(bb, i), lay.q * TN // d)),
                  pl.BlockSpec((TQ, d), lambda bb, i: (row(bb, i), lay.qi * TN // d)),
                  pl.BlockSpec((TQ, d), lambda bb, i: (row(bb, i), lay.ga * TN // d)),
                  pl.BlockSpec((s, TN), lambda bb, i: (bb, lay.k)),
                  pl.BlockSpec((s, IDX_DIM), lambda bb, i: (bb, lay.misc * TN // IDX_DIM)),
                  pl.BlockSpec((1, TN, s), lambda bb, i: (bb, 0, 0)),
                  pl.BlockSpec((IDX_HEADS, TQ), lambda bb, i: (0, row(bb, i)))],
        out_specs=pl.BlockSpec((TQ, d), lambda bb, i: (row(bb, i), 0)),
        out_shape=jax.ShapeDtypeStruct((b * s, d), BF16),
        scratch_shapes=[pltpu.VMEM((s, TQ), F32),
                        pltpu.VMEM((CK, TQ), F32),
                        pltpu.VMEM((SUBLANES, TQ), F32),
                        pltpu.VMEM((n_heads, TQ), F32),
                        pltpu.VMEM((n_heads, TQ), F32),
                        pltpu.VMEM((n_heads, TQ), F32),
                        pltpu.VMEM((n_heads, HEAD_DIM, TQ), F32),
                        pltpu.VMEM((n_heads, CK, TQ), F32)],
        compiler_params=_cparams(("arbitrary", "arbitrary")),
    )(zb, zb, zb, zb, zb, vt, wt)


def _pool_kernel(ts, u_ref, sgb_ref, wp_ref, sp_ref, o_ref, ext_ref):
    i = pl.program_id(1)
    halo = POOL_STATE + 1

    @pl.when(i == 0)
    def _():
        ext_ref[0:halo, :] = jnp.zeros((halo, ext_ref.shape[1]), F32)

    @pl.when(i > 0)
    def _():
        ext_ref[0:halo, :] = ext_ref[ts:ts + halo, :]

    ext_ref[halo:halo + ts, :] = u_ref[...]
    pos = i * ts + lax.broadcasted_iota(jnp.int32, (ts, 1), 0)
    gd = wp_ref.shape[1]
    for g, w in enumerate(POOL_WINDOWS):
        cols = slice(g * gd, (g + 1) * gd)
        acc = ext_ref[halo:halo + ts, cols]
        for jj in range(1, w):
            acc = acc + ext_ref[halo - jj:halo - jj + ts, cols]
        cnt = jnp.minimum(pos + 1, w).astype(F32)
        pooled = acc / cnt - u_ref[:, cols]
        mixed = jnp.dot(pooled.astype(BF16), wp_ref[g], preferred_element_type=F32) * sp_ref[:, cols]
        o_ref[:, cols] = (mixed * sgb_ref[:, cols].astype(F32)).astype(BF16)


def _pool_prompt(u, zb, wp, sp, lay, b, s, ts):
    m, pw = u.shape
    nt = s // ts
    kern = functools.partial(_pool_kernel, ts)
    return pl.pallas_call(
        kern,
        grid=(b, nt),
        in_specs=[pl.BlockSpec((ts, pw), lambda bb, i: (bb * nt + i, 0)),
                  pl.BlockSpec((ts, pw), lambda bb, i: (bb * nt + i, lay.gb * TN // pw)),
                  pl.BlockSpec(wp.shape, lambda bb, i: (0, 0, 0)),
                  pl.BlockSpec((1, pw), lambda bb, i: (0, 0))],
        out_specs=pl.BlockSpec((ts, pw), lambda bb, i: (bb * nt + i, 0)),
        out_shape=jax.ShapeDtypeStruct((m, pw), BF16),
        scratch_shapes=[pltpu.VMEM((POOL_STATE + 1 + ts, pw), F32)],
        compiler_params=_cparams(("arbitrary", "arbitrary")),
    )(u, zb, wp, sp)


def _pool_step_kernel(cnts, ext_ref, sgb_ref, wp_ref, sp_ref, o_ref):
    gd = wp_ref.shape[1]
    rows = ext_ref.shape[0]
    for g, w in enumerate(POOL_WINDOWS):
        cols = slice(g * gd, (g + 1) * gd)
        acc = ext_ref[rows - 1, :, cols]
        for jj in range(1, w):
            acc = acc + ext_ref[rows - 1 - jj, :, cols]
        pooled = acc / cnts[g] - ext_ref[rows - 1, :, cols]
        mixed = jnp.dot(pooled.astype(BF16), wp_ref[g], preferred_element_type=F32) * sp_ref[:, cols]
        o_ref[:, cols] = (mixed * sgb_ref[:, cols].astype(F32)).astype(BF16)


def _pool_step(ext_t, sgb, wp, sp, past):
    rows, n, pw = ext_t.shape
    cnts = tuple(float(min(past + 1, w)) for w in POOL_WINDOWS)
    return pl.pallas_call(
        functools.partial(_pool_step_kernel, cnts),
        out_shape=jax.ShapeDtypeStruct((n, pw), BF16),
    )(ext_t, sgb, wp, sp)


def _merge1_kernel(ab_ref, pb_ref, wa_ref, wb_ref, sma_ref, smb_ref, o_ref):
    a = jnp.dot(ab_ref[...], wa_ref[...], preferred_element_type=F32)
    bb = jnp.dot(pb_ref[...], wb_ref[...], preferred_element_type=F32)
    o_ref[...] = (sma_ref[...].astype(F32) * a + smb_ref[...].astype(F32) * bb).astype(BF16)


def _merge1(ab, pb, wa, wb, zb, lay, tm):
    m, d = ab.shape
    pw = pb.shape[1]
    return pl.pallas_call(
        _merge1_kernel,
        grid=(m // tm, d // TN),
        in_specs=[pl.BlockSpec((tm, d), lambda i, j: (i, 0)),
                  pl.BlockSpec((tm, pw), lambda i, j: (i, 0)),
                  pl.BlockSpec((d, TN), lambda i, j: (0, j)),
                  pl.BlockSpec((pw, TN), lambda i, j: (0, j)),
                  pl.BlockSpec((tm, TN), lambda i, j: (i, lay.ma + j)),
                  pl.BlockSpec((tm, TN), lambda i, j: (i, lay.mb + j))],
        out_specs=pl.BlockSpec((tm, TN), lambda i, j: (i, j)),
        out_shape=jax.ShapeDtypeStruct((m, d), BF16),
        compiler_params=_cparams(("arbitrary", "arbitrary")),
    )(ab, pb, wa, wb, zb, zb)


def _merge2_kernel(final, mm_ref, x_ref, gate_ref, wo_ref, gf_ref, o_ref):
    y = x_ref[...] + gate_ref[0] * jnp.dot(mm_ref[...], wo_ref[...], preferred_element_type=F32)
    if final:
        y = y * lax.rsqrt(jnp.mean(y * y, axis=-1, keepdims=True) + EPS) * gf_ref[...]
    o_ref[...] = y


def _merge2(mm, x, gate, wo, gf, tm, tiles_per_group, final):
    m, d = x.shape
    r = gate.shape[1]
    return pl.pallas_call(
        functools.partial(_merge2_kernel, final),
        grid=(m // tm,),
        in_specs=[pl.BlockSpec((tm, d), lambda i: (i, 0)),
                  pl.BlockSpec((tm, d), lambda i: (i, 0)),
                  pl.BlockSpec((1, r, d), lambda i: (i // tiles_per_group, 0, 0)),
                  pl.BlockSpec((d, d), lambda i: (0, 0)),
                  pl.BlockSpec((1, d), lambda i: (0, 0))],
        out_specs=pl.BlockSpec((tm, d), lambda i: (i, 0)),
        out_shape=jax.ShapeDtypeStruct((m, d), F32),
        compiler_params=_cparams(("arbitrary",)),
    )(mm, x, gate, wo, gf)


def _page_specs(width, npg, npp):
    return [pl.BlockSpec((1, PAGE_SIZE, width),
                         lambda b, p, pt, r=r: (pt[b * npg + p * npp + r], 0, 0)) for r in range(npp)]


def _sidx_kernel(npp, pt_ref, *refs):
    kidx_refs = refs[:npp]
    qi_ref, w_ref, o_ref = refs[npp:]
    nt = (((1,), (1,)), ((), ()))
    qi, w = qi_ref[0], w_ref[0]
    rows = []
    for r in range(npp):
        s = lax.dot_general(qi, kidx_refs[r][0].astype(BF16), nt, preferred_element_type=F32)
        rows.append(jnp.sum(w * jnp.maximum(s, 0.0), axis=0, keepdims=True))
    o_ref[0] = jnp.concatenate(rows, axis=1)


def _sample_scores(page_table, cache_kidx, qi3, wcol, npp):
    n, npg = page_table.shape
    return pl.pallas_call(
        functools.partial(_sidx_kernel, npp),
        grid_spec=pltpu.PrefetchScalarGridSpec(
            num_scalar_prefetch=1,
            grid=(n, npg // npp),
            in_specs=_page_specs(IDX_DIM, npg, npp) + [
                pl.BlockSpec((1, IDX_HEADS, IDX_DIM), lambda b, p, pt: (b, 0, 0)),
                pl.BlockSpec((1, IDX_HEADS, 1), lambda b, p, pt: (b, 0, 0))],
            out_specs=pl.BlockSpec((1, 1, npp * PAGE_SIZE), lambda b, p, pt: (b, 0, p))),
        out_shape=jax.ShapeDtypeStruct((n, 1, npg * PAGE_SIZE), F32),
        compiler_params=_cparams(("arbitrary", "arbitrary")),
    )(page_table.reshape(-1), *([cache_kidx] * npp), qi3, wcol)


def _sthr_kernel(topk, row_base, sc_ref, qi_ref, kin_ref, w_ref, pt_ref, pos_ref, cnt_ref, sel_ref,
                 thr_scr, new_scr):
    sc = sc_ref[...]
    kin = kin_ref[...].astype(BF16).astype(F32)
    s_new = jnp.sum(qi_ref[...].astype(F32) * kin, axis=2, keepdims=True)
    new = jnp.sum(w_ref[...] * jnp.maximum(s_new, 0.0), axis=1, keepdims=True)
    red = lambda f, a: f(f(a, axis=2, keepdims=True), axis=1, keepdims=True)
    lo = jnp.minimum(red(jnp.min, sc), new)
    hi = jnp.maximum(red(jnp.max, sc), new)

    def count_ge(thr):
        c = red(jnp.sum, jnp.where(sc >= thr, 1.0, 0.0))
        return c + jnp.where(new >= thr, 1.0, 0.0)

    n_all = jnp.full(new.shape, float(sc.shape[1] * sc.shape[2] + 1), F32)
    thr, _ = _bisect(count_ge, lo, hi, n_all, float(topk))
    thr_scr[...] = jnp.broadcast_to(thr, thr_scr.shape)
    new_scr[...] = jnp.broadcast_to(new, new_scr.shape)

    rows = sc.shape[1]
    ii = lambda shape, ax: lax.broadcasted_iota(jnp.int32, shape, ax)
    one = lambda m: jnp.where(m, 1.0, 0.0)
    tri_lane = one(ii((PAGE_SIZE, PAGE_SIZE), 0) <= ii((PAGE_SIZE, PAGE_SIZE), 1)).astype(BF16)
    lower = one(ii((rows, rows), 1) < ii((rows, rows), 0)).astype(BF16)
    upper = one(ii((rows, rows), 0) < ii((rows, rows), 1)).astype(BF16)
    ones8 = jnp.ones((SUBLANES, PAGE_SIZE), BF16)
    nt = (((1,), (1,)), ((), ()))
    total = lambda a: jnp.sum(jnp.sum(a, axis=1, keepdims=True), axis=0, keepdims=True)
    dotf = lambda a, b: jnp.dot(a.astype(BF16), b.astype(BF16), preferred_element_type=F32)
    kf = float(topk)
    j_col = ii((topk, 1), 0).astype(F32)
    c_row = ii((1, PAGE_SIZE), 1).astype(F32)

    def compact(b, carry):
        s = sc_ref[b]
        t = thr_scr[b][:, 0:1]
        nw = new_scr[b][:, 0:1]
        gt, eq = s > t, s == t
        need = kf - total(one(gt)) - one(nw > t)
        e_lane = dotf(one(eq), tri_lane)
        e_rank = e_lane + dotf(lower, e_lane)[:, PAGE_SIZE - 1:PAGE_SIZE]
        x = one(gt | (eq & (e_rank <= need)))
        sel_new = (nw > t) | ((nw == t) & (total(one(eq)) < need))
        x_lane = dotf(x, tri_lane)
        n_rows = lax.dot_general(ones8, x.astype(BF16), nt, preferred_element_type=F32)
        n_row = n_rows[0:1, :]
        start_row = dotf(n_rows, upper)[0:1, :]
        hit = one((start_row <= j_col) & (j_col < start_row + n_row))
        lane_sum = lambda a: jnp.sum(a, axis=1, keepdims=True)
        page = lane_sum(hit * pt_ref[b])
        target = j_col - lane_sum(hit * start_row) + 1.0
        in_row = dotf(hit, x_lane * x)
        off = lane_sum(jnp.where(in_row == target, c_row, 0.0))
        pos_ref[b] = (page * PAGE_SIZE + off + row_base).astype(jnp.int32)
        cnt_ref[b] = jnp.broadcast_to(total(x), cnt_ref.shape[1:])
        sel_ref[b] = jnp.broadcast_to(one(sel_new), sel_ref.shape[1:])
        return carry

    lax.fori_loop(0, sc.shape[0], compact, 0)


def _sample_select(sc, qi3, kin, wcol, pt_rows, topk, row_base):
    n = sc.shape[0]
    return pl.pallas_call(
        functools.partial(_sthr_kernel, topk, row_base),
        out_shape=[jax.ShapeDtypeStruct((n, topk, 1), jnp.int32),
                   jax.ShapeDtypeStruct((n, 1, PAGE_SIZE), F32),
                   jax.ShapeDtypeStruct((n, 1, PAGE_SIZE), F32)],
        scratch_shapes=[pltpu.VMEM((n, 1, PAGE_SIZE), F32),
                        pltpu.VMEM((n, 1, PAGE_SIZE), F32)],
        compiler_params=pltpu.CompilerParams(vmem_limit_bytes=VMEM_LIMIT),
    )(sc, qi3, kin, wcol, pt_rows)


def _row_copy(cache_rows, buf, sem, row, j):
    return pltpu.make_async_copy(cache_rows.at[row], buf.at[j], sem)


def _sattn_kernel(pos_ref, ck_hbm, cv_hbm, cnt_ref, sel_ref, q_ref, kn_ref, vn_ref,
                  sga_ref, o_ref, kbuf, vbuf, ksem, vsem):
    b = pl.program_id(0)
    slot = b % 2
    topk = kbuf.shape[1]
    n_heads = q_ref.shape[1]
    group = n_heads // N_KV_HEADS
    nt = (((1,), (1,)), ((), ()))

    def fetch(bb, s):
        def issue(j, carry):
            row = pos_ref[bb * topk + j]
            _row_copy(ck_hbm, kbuf.at[s], ksem.at[s], row, j).start()
            _row_copy(cv_hbm, vbuf.at[s], vsem.at[s], row, j).start()
            return carry
        lax.fori_loop(0, topk, issue, 0, unroll=8)

    @pl.when(b == 0)
    def _():
        fetch(b, slot)

    @pl.when(b + 1 < pl.num_programs(0))
    def _():
        fetch(b + 1, 1 - slot)

    kbuf, vbuf = kbuf.at[slot], vbuf.at[slot]
    if topk % PAGE_SIZE == 0:
        for w in range(topk // PAGE_SIZE):
            rows = pl.ds(w * PAGE_SIZE, PAGE_SIZE)
            pltpu.make_async_copy(ck_hbm.at[rows], kbuf.at[rows], ksem.at[slot]).wait()
            pltpu.make_async_copy(cv_hbm.at[rows], vbuf.at[rows], vsem.at[slot]).wait()
    else:
        def wait(j, carry):
            _row_copy(ck_hbm, kbuf, ksem.at[slot], 0, j).wait()
            _row_copy(cv_hbm, vbuf, vsem.at[slot], 0, j).wait()
            return carry

        lax.fori_loop(0, topk, wait, 0)

    q = q_ref[0]
    head = lax.broadcasted_iota(jnp.int32, (n_heads, 1), 0)
    in_group = [(head >= g * group) & (head < (g + 1) * group) for g in range(N_KV_HEADS)]
    lg = jnp.zeros((n_heads, topk), F32)
    for g in range(N_KV_HEADS):
        lg_g = lax.dot_general(q, kbuf[:, g, :].astype(BF16), nt, preferred_element_type=F32)
        lg = lg + jnp.where(in_group[g], lg_g, 0.0)
    valid = lax.broadcasted_iota(jnp.int32, (1, topk), 1).astype(F32) < cnt_ref[0][:, 0:1]
    lg = jnp.where(valid, lg, NEG)
    kn = kn_ref[0].astype(BF16).astype(F32)
    vn = vn_ref[0].astype(BF16).astype(F32)
    lgn = jnp.sum(q.astype(F32) * kn, axis=1, keepdims=True)
    lgn = jnp.where(sel_ref[0][:, 0:1] > 0.0, lgn, NEG)
    m = jnp.maximum(lg.max(axis=1, keepdims=True), lgn)
    p = jnp.exp2(lg - m)
    pn = jnp.exp2(lgn - m)
    l = p.sum(axis=1, keepdims=True) + pn
    pb = p.astype(BF16)
    acc = pn.astype(BF16).astype(F32) * vn
    for g in range(N_KV_HEADS):
        pv = jnp.dot(pb, vbuf[:, g, :].astype(BF16), preferred_element_type=F32)
        acc = acc + jnp.where(in_group[g], pv, 0.0)
    o_ref[0] = ((acc / l) * sga_ref[0].astype(F32)).astype(BF16)


def _sample_attention(k_rows, v_rows, pos, cnt, sel, q3, kn, vn, sga3):
    n, n_heads = q3.shape[:2]
    topk = pos.shape[1]
    per_b = lambda shape: pl.BlockSpec(shape, lambda b, ps: (b, 0, 0))
    hbm = pl.BlockSpec(memory_space=pl.ANY)
    return pl.pallas_call(
        _sattn_kernel,
        grid_spec=pltpu.PrefetchScalarGridSpec(
            num_scalar_prefetch=1,
            grid=(n,),
            in_specs=[hbm, hbm,
                      per_b((1, 1, PAGE_SIZE)),
                      per_b((1, 1, PAGE_SIZE)),
                      per_b((1, n_heads, HEAD_DIM)),
                      per_b((1, n_heads, HEAD_DIM)),
                      per_b((1, n_heads, HEAD_DIM)),
                      per_b((1, n_heads, HEAD_DIM))],
            out_specs=per_b((1, n_heads, HEAD_DIM)),
            scratch_shapes=[pltpu.VMEM((2, topk, N_KV_HEADS, HEAD_DIM), k_rows.dtype),
                            pltpu.VMEM((2, topk, N_KV_HEADS, HEAD_DIM), v_rows.dtype),
                            pltpu.SemaphoreType.DMA((2,)),
                            pltpu.SemaphoreType.DMA((2,))]),
        out_shape=jax.ShapeDtypeStruct((n, n_heads, HEAD_DIM), BF16),
        compiler_params=_cparams(("arbitrary",)),
    )(pos.reshape(-1), k_rows, v_rows, cnt, sel, q3, kn, vn, sga3)


def _rope_tables(pos):
    half = HEAD_DIM // 2
    inv = ROPE_THETA ** (-jnp.arange(half, dtype=F32) / half)
    ang = pos.astype(F32)[:, None] * inv[None, :]
    cos, sin = jnp.cos(ang), jnp.sin(ang)
    return jnp.concatenate([cos, cos], axis=1), jnp.concatenate([-sin, sin], axis=1)


def _tile_starts(d, lay):
    att, kv, idx, pool = d, N_KV_HEADS * HEAD_DIM, IDX_HEADS * IDX_DIM, d // 2
    sizes = (att, kv, kv, idx, IDX_DIM, IDX_HEADS, att, pool, pool, d, d)
    offs = [0]
    for n in sizes:
        offs.append(offs[-1] + n)
    q, k, v, qi, ki, wi, ga, u, gb, ma, mb = offs[:-1]
    starts = [0] * lay.end
    for first, tiles, col in ((lay.q, lay.qi - lay.q, q), (lay.qi, lay.ga - lay.qi, qi),
                              (lay.ga, lay.ma - lay.ga, ga), (lay.ma, lay.mb - lay.ma, ma),
                              (lay.mb, lay.gb - lay.mb, mb), (lay.gb, lay.k - lay.gb, gb),
                              (lay.k, 1, k), (lay.v, 1, v), (lay.misc, 1, ki), (lay.u, lay.end - lay.u, u)):
        for t in range(tiles):
            starts[first + t] = col + t * TN
    assert wi == ki + IDX_DIM and all(s + TN <= offs[-1] and s % SUBLANES == 0 for s in starts)
    return starts


def _pack_kernel(misc_tile, starts_ref, w_hbm, o_ref, buf, sem):
    t = pl.program_id(0)
    slot = t % 2

    def copy(tt, s):
        first = pl.multiple_of(starts_ref[tt], SUBLANES)
        return pltpu.make_async_copy(w_hbm.at[pl.ds(first, TN)], buf.at[s], sem.at[s])

    @pl.when(t == 0)
    def _():
        copy(t, slot).start()

    @pl.when(t + 1 < pl.num_programs(0))
    def _():
        copy(t + 1, 1 - slot).start()

    copy(t, slot).wait()
    x = buf[slot]
    row = lax.broadcasted_iota(jnp.int32, (TN, 1), 0)
    x = jnp.where((t == misc_tile) & (row >= IDX_DIM + IDX_HEADS), 0.0, x)
    o_ref[...] = x.T.astype(BF16)


def _pack_w_in(w_t, layer, in_width, d, lay):
    starts = jnp.asarray([layer * in_width + s for s in _tile_starts(d, lay)], jnp.int32)
    return pl.pallas_call(
        functools.partial(_pack_kernel, lay.misc),
        grid_spec=pltpu.PrefetchScalarGridSpec(
            num_scalar_prefetch=1,
            grid=(lay.end,),
            in_specs=[pl.BlockSpec(memory_space=pl.ANY)],
            out_specs=pl.BlockSpec((d, TN), lambda t, st: (0, t)),
            scratch_shapes=[pltpu.VMEM((2, TN, d), w_t.dtype),
                            pltpu.SemaphoreType.DMA((2,))]),
        out_shape=jax.ShapeDtypeStruct((d, lay.end * TN), BF16),
        compiler_params=_cparams(("arbitrary",)),
    )(starts, w_t)


def kernel(x_prompt, x_sample, c_prompt, c_sample, cache_k, cache_v, cache_kidx, state_pool, page_table,
           g_norm, w_ada, b_ada, w_in, w_pool, s_pool, w_a_proj, w_b_proj, w_out, g_final):
    bp, sp, d = x_prompt.shape
    ns, ds, _ = x_sample.shape
    depth = g_norm.shape[0]
    npg = page_table.shape[1]
    past = npg * PAGE_SIZE
    assert ds == 1 and d % (N_KV_HEADS * HEAD_DIM) == 0 and sp % TQ == 0 and TQ == CK and sp >= POOL_STATE
    n_heads = d // HEAD_DIM
    kvw = N_KV_HEADS * HEAD_DIM
    lay = _Layout(d)
    pw = lay.pool
    idx_scale = (IDX_HEADS * IDX_DIM) ** -0.5
    topk_p = min(TOPK_MAX, sp // 4)
    topk_s = min(TOPK_MAX, (past + ds) // 4)
    tm = 1024
    ts = 512
    tm2 = 256

    cos_p, sin_p = _rope_tables(jnp.arange(sp, dtype=jnp.int32))
    cos_s, sin_s = _rope_tables(jnp.full((ns,), past, jnp.int32))
    n_c = bp + ns
    c_rows = -(-n_c // SUBLANES) * SUBLANES
    c_all = jnp.concatenate([c_prompt, c_sample, jnp.zeros((c_rows - n_c, d), F32)], axis=0)

    n_pool = cache_kidx.shape[1]
    kidx_pages = cache_kidx.reshape(depth * n_pool, PAGE_SIZE, IDX_DIM)
    npp = math.gcd(npg, PAGES_PER_STEP)
    k_rows = cache_k.reshape(depth * n_pool * PAGE_SIZE, N_KV_HEADS, HEAD_DIM)
    v_rows = cache_v.reshape(depth * n_pool * PAGE_SIZE, N_KV_HEADS, HEAD_DIM)
    pt_rows = page_table.astype(F32).reshape(ns, 1, npg)
    assert npg % npp == 0

    in_width = w_in.shape[2]
    w_t = jnp.swapaxes(w_in, 1, 2).reshape(depth * in_width, d)

    xp = x_prompt.reshape(bp * sp, d)
    xs = x_sample.reshape(ns, d)
    outs = {n: [] for n in ("kp", "vp", "kip", "pp", "ks", "vs", "kis", "ps")}
    for l in range(depth):
        final = l == depth - 1
        w_p = _pack_w_in(w_t, l, in_width, d, lay)
        wa, wb, wo, wpl = (w_a_proj[l].astype(BF16), w_b_proj[l].astype(BF16), w_out[l].astype(BF16),
                           w_pool[l].astype(BF16))
        g = g_norm[l].reshape(1, d)
        spl = s_pool[l].reshape(1, pw)
        gf = g_final.reshape(1, d)

        mod = _ada(c_all, w_ada[l], b_ada[l])
        shift, scale, gate = mod[:, :d], mod[:, d:2 * d], mod[:, 2 * d:]
        pg = lambda a: a[:bp].reshape(bp, 1, d)
        sg = lambda a: a[bp:n_c].reshape(1, ns, d)

        zb, k_p, v_p, misc_p, u_p, vt, wt = _project(xp, g, pg(shift), pg(scale), w_p, cos_p, sin_p, lay, tm,
                                                     sp // tm, sp // tm, idx_scale, True)
        ab = _prompt_attention(zb, vt, wt, lay, bp, sp, d, topk_p)
        pb = _pool_prompt(u_p, zb, wpl, spl, lay, bp, sp, ts)
        mm = _merge1(ab, pb, wa, wb, zb, lay, tm)
        xp = _merge2(mm, xp, pg(gate), wo, gf, tm2, sp // tm2, final)
        outs["kp"].append(k_p.reshape(bp, sp, N_KV_HEADS, HEAD_DIM))
        outs["vp"].append(v_p.reshape(bp, sp, N_KV_HEADS, HEAD_DIM))
        outs["kip"].append(misc_p[:, :IDX_DIM].reshape(bp, sp, IDX_DIM))
        outs["pp"].append(u_p.reshape(bp, sp, pw)[:, sp - POOL_STATE:, :])

        zs, k_s, v_s, misc_s, u_s = _project(xs, g, sg(shift), sg(scale), w_p, cos_s, sin_s, lay, ns,
                                             1, 1, idx_scale, False)
        col = lambda t, n: zs[:, t * TN:t * TN + n]
        q3 = col(lay.q, d).reshape(ns, n_heads, HEAD_DIM)
        qi3 = col(lay.qi, IDX_HEADS * IDX_DIM).reshape(ns, IDX_HEADS, IDX_DIM)
        sga3 = col(lay.ga, d).reshape(ns, n_heads, HEAD_DIM)
        wcol = misc_s[:, IDX_DIM:IDX_DIM + IDX_HEADS].reshape(ns, IDX_HEADS, 1)
        kin = misc_s[:, :IDX_DIM].reshape(ns, 1, IDX_DIM)
        pt_l = page_table + l * n_pool
        sc = _sample_scores(pt_l, kidx_pages, qi3, wcol, npp)
        pos, cnt, sel = _sample_select(sc.reshape(ns, npg, PAGE_SIZE), qi3, kin, wcol, pt_rows, topk_s,
                                       l * n_pool * PAGE_SIZE)
        rep = lambda a: jnp.repeat(a.reshape(ns, N_KV_HEADS, HEAD_DIM), n_heads // N_KV_HEADS, axis=1)
        ab_s = _sample_attention(k_rows, v_rows, pos, cnt, sel, q3, rep(k_s), rep(v_s), sga3).reshape(ns, d)
        ext = jnp.concatenate([state_pool[l], u_s[:, None, :]], axis=1)
        pb_s = _pool_step(ext.transpose(1, 0, 2), col(lay.gb, pw), wpl, spl, past)
        mm_s = _merge1(ab_s, pb_s, wa, wb, zs, lay, ns)
        xs = _merge2(mm_s, xs, sg(gate), wo, gf, ns, 1, final)
        outs["ks"].append(k_s.reshape(ns, ds, N_KV_HEADS, HEAD_DIM))
        outs["vs"].append(v_s.reshape(ns, ds, N_KV_HEADS, HEAD_DIM))
        outs["kis"].append(misc_s[:, :IDX_DIM].reshape(ns, ds, IDX_DIM))
        outs["ps"].append(ext[:, 1:, :])

    st = lambda n: outs[n][0][None] if depth == 1 else jnp.stack(outs[n])
    return (xp.reshape(bp, sp, d), xs.reshape(ns, ds, d),
            st("kp"), st("vp"), st("kip"), st("pp"), st("ks"), st("vs"), st("kis"), st("ps"))
```

```python
import functools
import math

import jax
import jax.numpy as jnp
from jax import lax
from jax.experimental import pallas as pl
from jax.experimental.pallas import tpu as pltpu

F32 = jnp.float32
BF16 = jnp.bfloat16

HEAD_DIM = 128
N_KV_HEADS = 4
IDX_HEADS = 16
IDX_DIM = 128
TOPK_MAX = 256
PAGE_SIZE = 128
POOL_WINDOWS = (2, 4, 8, 16)
POOL_STATE = max(POOL_WINDOWS) - 1
ROPE_THETA = 10000.0
EPS = 1e-6

V7X_VMEM_BYTES = 64 * 1024 * 1024
V7X_MXU_COLUMNS = 256
SUBLANES = 8
TN = 512
ROW_TILE = 1024
POOL_TILE = 512
OUT_TILE = 256
TQ = V7X_MXU_COLUMNS
CK = 256
NEG = -1e30
Q_SCALE = 1.4426950408889634 * HEAD_DIM ** -0.5
PAGES_PER_STEP = 32
MAX_BISECT = 512
VMEM_LIMIT = V7X_VMEM_BYTES * 7 // 8


def _cparams(sem):
    return pltpu.CompilerParams(dimension_semantics=sem, vmem_limit_bytes=VMEM_LIMIT)


def _sigmoid(z):
    return 0.5 * jnp.tanh(0.5 * z) + 0.5


def _ada_kernel(c_ref, w_ref, b_ref, o_ref):
    o_ref[...] = jnp.dot(c_ref[...].astype(BF16), w_ref[...].astype(BF16),
                         preferred_element_type=F32) + b_ref[...]


def _ada(c, w, b):
    r, d = c.shape
    n = w.shape[1]
    return pl.pallas_call(
        _ada_kernel,
        grid=(n // TN,),
        in_specs=[pl.BlockSpec((r, d), lambda j: (0, 0)),
                  pl.BlockSpec((d, TN), lambda j: (0, j)),
                  pl.BlockSpec((1, TN), lambda j: (0, j))],
        out_specs=pl.BlockSpec((r, TN), lambda j: (0, j)),
        out_shape=jax.ShapeDtypeStruct((r, n), F32),
        compiler_params=_cparams(("arbitrary",)),
    )(c, w, b.reshape(1, n))


class _Layout:
    def __init__(self, d):
        att, kv, idx, pool = d, N_KV_HEADS * HEAD_DIM, IDX_HEADS * IDX_DIM, d // 2
        t = lambda n: n // TN
        assert att % TN == 0 and kv == TN and idx % TN == 0 and pool % TN == 0
        self.q = 0
        self.qi = self.q + t(att)
        self.ga = self.qi + t(idx)
        self.ma = self.ga + t(att)
        self.mb = self.ma + t(d)
        self.gb = self.mb + t(d)
        self.k = self.gb + t(pool)
        self.v = self.k + 1
        self.misc = self.v + 1
        self.u = self.misc + 1
        self.end = self.u + t(pool)
        self.nb = self.u
        self.pool = pool


def _rope_tile(z, cos, sin):
    parts = []
    for s in range(z.shape[1] // HEAD_DIM):
        zs = z[:, s * HEAD_DIM:(s + 1) * HEAD_DIM]
        parts.append(zs * cos + pltpu.roll(zs, HEAD_DIM // 2, 1) * sin)
    return parts[0] if len(parts) == 1 else jnp.concatenate(parts, axis=1)


def _proj_kernel(lay, idx_scale, q_scale, with_vt, starts_ref, x_ref, g_ref, shift_ref, scale_ref, w_ref,
                 cos_ref, sin_ref, zb_ref, k_ref, v_ref, misc_ref, u_ref, *rest):
    vt_ref, wt_ref = rest[:2] if with_vt else (None, None)
    h_ref = rest[-1]
    j = pl.program_id(1)
    nt = (((1,), (1,)), ((), ()))

    @pl.when(j == 0)
    def _():
        x = x_ref[...]
        r = lax.rsqrt(jnp.mean(x * x, axis=-1, keepdims=True) + EPS)
        h = (x * r * g_ref[...]) * (1.0 + scale_ref[0]) + shift_ref[0]
        h_ref[...] = h.astype(BF16)

    def tile(keep_rows=None):
        w = w_ref[...]
        if keep_rows is not None:
            w = jnp.where(lax.broadcasted_iota(jnp.int32, (TN, 1), 0) < keep_rows, w, 0.0)
        return lax.dot_general(h_ref[...], w.astype(BF16), nt, preferred_element_type=F32)

    def store_heads(ref, val):
        if len(ref.shape) == 2:
            ref[...] = val
        else:
            for g in range(N_KV_HEADS):
                ref[:, g, :] = val[:, g * HEAD_DIM:(g + 1) * HEAD_DIM]

    @pl.when(j < lay.qi)
    def _():
        zb_ref[...] = (_rope_tile(tile(), cos_ref[...], sin_ref[...]) * q_scale).astype(BF16)

    @pl.when((j >= lay.qi) & (j < lay.ga))
    def _():
        zb_ref[...] = _rope_tile(tile(), cos_ref[...], sin_ref[...]).astype(BF16)

    @pl.when(j == lay.k)
    def _():
        r = _rope_tile(tile(), cos_ref[...], sin_ref[...])
        zb_ref[...] = r.astype(BF16)
        store_heads(k_ref, r)

    @pl.when(((j >= lay.ga) & (j < lay.ma)) | ((j >= lay.gb) & (j < lay.k)))
    def _():
        z = tile()
        zb_ref[...] = (z * _sigmoid(z)).astype(BF16)

    @pl.when((j >= lay.ma) & (j < lay.gb))
    def _():
        zb_ref[...] = _sigmoid(tile()).astype(BF16)

    @pl.when(j == lay.v)
    def _():
        z = tile()
        zb_ref[...] = z.astype(BF16)
        store_heads(v_ref, z)
        if with_vt:
            vt_ref[0] = z.T.astype(BF16)

    @pl.when(j == lay.misc)
    def _():
        z = tile(keep_rows=IDX_DIM + IDX_HEADS)
        ki = _rope_tile(z[:, :IDX_DIM], cos_ref[...], sin_ref[...])
        m = jnp.concatenate([ki, z[:, IDX_DIM:] * idx_scale], axis=1)
        zb_ref[...] = m.astype(BF16)
        misc_ref[...] = m
        if with_vt:
            wt_ref[...] = m[:, IDX_DIM:2 * IDX_DIM].T

    @pl.when(j >= lay.u)
    def _():
        u_ref[...] = tile()


def _project(x, g, shift, scale, w_t, starts, cos, sin, lay, tm, tiles_per_group, tiles_per_pos, idx_scale,
             with_vt):
    m, d = x.shape
    grp, r, _ = shift.shape
    nu = lay.end - lay.u
    kern = functools.partial(_proj_kernel, lay, idx_scale, Q_SCALE, with_vt)
    once = dict(pipeline_mode=pl.Buffered(1))
    out_specs = [pl.BlockSpec((tm, TN), lambda i, j, st: (i, jnp.minimum(j, lay.nb - 1))),
                 pl.BlockSpec((tm, TN), lambda i, j, st: (i, 0), **once),
                 pl.BlockSpec((tm, TN), lambda i, j, st: (i, 0), **once),
                 pl.BlockSpec((tm, TN), lambda i, j, st: (i, 0), **once),
                 pl.BlockSpec((tm, TN), lambda i, j, st: (i, jnp.clip(j - lay.u, 0, nu - 1)))]
    out_shape = [jax.ShapeDtypeStruct((m, lay.nb * TN), BF16),
                 jax.ShapeDtypeStruct((m, TN), F32),
                 jax.ShapeDtypeStruct((m, TN), F32),
                 jax.ShapeDtypeStruct((m, TN), F32),
                 jax.ShapeDtypeStruct((m, nu * TN), F32)]
    if with_vt:
        for o in (1, 2):
            out_specs[o] = pl.BlockSpec((tm, N_KV_HEADS, HEAD_DIM), lambda i, j, st: (i, 0, 0), **once)
            out_shape[o] = jax.ShapeDtypeStruct((m, N_KV_HEADS, HEAD_DIM), F32)
        out_specs.append(pl.BlockSpec((1, TN, tm),
                                      lambda i, j, st: (i // tiles_per_group, 0, i % tiles_per_group), **once))
        out_shape.append(jax.ShapeDtypeStruct((grp, TN, tiles_per_group * tm), BF16))
        out_specs.append(pl.BlockSpec((IDX_DIM, tm), lambda i, j, st: (0, i), **once))
        out_shape.append(jax.ShapeDtypeStruct((IDX_DIM, m), F32))
    return pl.pallas_call(
        kern,
        grid_spec=pltpu.PrefetchScalarGridSpec(
            num_scalar_prefetch=1,
            grid=(m // tm, lay.end),
            in_specs=[pl.BlockSpec((tm, d), lambda i, j, st: (i, 0)),
                      pl.BlockSpec((1, d), lambda i, j, st: (0, 0)),
                      pl.BlockSpec((1, r, d), lambda i, j, st: (i // tiles_per_group, 0, 0)),
                      pl.BlockSpec((1, r, d), lambda i, j, st: (i // tiles_per_group, 0, 0)),
                      pl.BlockSpec((pl.Element(TN), pl.Element(d)), lambda i, j, st: (pl.multiple_of(st[j], SUBLANES), 0)),
                      pl.BlockSpec((tm, HEAD_DIM), lambda i, j, st: (i % tiles_per_pos, 0)),
                      pl.BlockSpec((tm, HEAD_DIM), lambda i, j, st: (i % tiles_per_pos, 0))],
            out_specs=out_specs,
            scratch_shapes=[pltpu.VMEM((tm, d), BF16)]),
        out_shape=out_shape,
        compiler_params=_cparams(("arbitrary", "arbitrary")),
    )(starts, x, g, shift, scale, w_t, cos, sin)


def _bisect(count_ge, lo, hi, clo, kf):
    def cond(st):
        return (st[4] > 0.0) & (st[5] < MAX_BISECT)

    def halve(lo, hi, clo, done):
        mid = 0.5 * lo + 0.5 * hi
        mid = jnp.where(mid <= lo, hi, mid)
        c = count_ge(mid)
        ge = c >= kf
        stuck = (mid >= hi) & jnp.logical_not(ge)
        lo2 = jnp.where(ge, mid, lo)
        hi2 = jnp.where(ge, hi, mid)
        clo2 = jnp.where(ge, c, clo)
        fin = (clo2 <= kf) | (lo2 >= hi2) | stuck
        return lo2, hi2, clo2, jnp.maximum(done, jnp.where(fin, 1.0, 0.0))

    def body(st):
        lo, hi, clo, done = halve(*halve(*st[:4]))
        return lo, hi, clo, done, jnp.sum(1.0 - done), st[5] + 1

    done0 = jnp.where((clo <= kf) | (lo >= hi), 1.0, 0.0)
    st = (lo, hi, clo, done0, jnp.sum(1.0 - done0), jnp.int32(0))
    out = lax.while_loop(cond, body, st)
    return out[0], out[2]


def _attn_kernel(topk, q_ref, qi_ref, sga_ref, k_ref, ki_ref, vt_ref, wt_ref, o_ref,
                 score_ref, bias_ref, cut_ref, m_ref, l_ref, alpha_ref, acc_ref, lg_ref):
    i = pl.program_id(1)
    nck = i + 1
    n_heads = q_ref.shape[1] // HEAD_DIM
    group = n_heads // N_KV_HEADS
    nt = (((1,), (1,)), ((), ()))
    q_pos = i * TQ + lax.broadcasted_iota(jnp.int32, (1, TQ), 1)
    fold = lambda a: a.reshape(CK // SUBLANES, SUBLANES, TQ)

    def score_chunk(c, carry):
        mn, mx = carry
        off = pl.multiple_of(c * CK, CK)
        kc = ki_ref[pl.ds(off, CK), :]
        acc = jnp.zeros((CK, TQ), F32)
        for h in range(IDX_HEADS):
            s = lax.dot_general(kc, qi_ref[:, h * IDX_DIM:(h + 1) * IDX_DIM], nt,
                                preferred_element_type=F32)
            acc = acc + wt_ref[h:h + 1, :] * jnp.maximum(s, 0.0)
        k_pos = off + lax.broadcasted_iota(jnp.int32, (CK, 1), 0)
        causal = k_pos <= q_pos
        score_ref[pl.ds(off, CK), :] = jnp.where(causal, acc, -jnp.inf)
        mn = jnp.minimum(mn, fold(jnp.where(causal, acc, jnp.inf)).min(axis=0))
        mx = jnp.maximum(mx, fold(jnp.where(causal, acc, -jnp.inf)).max(axis=0))
        return mn, mx

    mn, mx = lax.fori_loop(0, nck, score_chunk,
                           (jnp.full((SUBLANES, TQ), jnp.inf, F32),
                            jnp.full((SUBLANES, TQ), -jnp.inf, F32)))
    lo = mn.min(axis=0, keepdims=True)
    hi = mx.max(axis=0, keepdims=True)

    ones_rows = jnp.ones((SUBLANES, CK), BF16)

    def count_ge(thr):
        def body(c, cnt):
            s = score_ref[pl.ds(pl.multiple_of(c * CK, CK), CK), :]
            return cnt + fold(jnp.where(s >= thr, 1.0, 0.0)).sum(axis=0)
        cnt = lax.fori_loop(0, nck, body, jnp.zeros((SUBLANES, TQ), F32))
        return cnt.sum(axis=0, keepdims=True)

    n_valid = (q_pos + 1).astype(F32)
    kf = float(topk)
    thr, n_ge = _bisect(count_ge, lo, hi, n_valid, kf)

    s_len = score_ref.shape[0]
    has_ties = jnp.sum(jnp.where(n_ge > kf, 1.0, 0.0)) > 0.0
    cut_ref[...] = jnp.full(cut_ref.shape, float(s_len), F32)

    @pl.when(has_ties)
    def _():
        def count(pred):
            def body(c, cnt):
                off = pl.multiple_of(c * CK, CK)
                k_pos = (off + lax.broadcasted_iota(jnp.int32, (CK, 1), 0)).astype(F32)
                hit = pred(score_ref[pl.ds(off, CK), :], k_pos)
                return cnt + fold(jnp.where(hit, 1.0, 0.0)).sum(axis=0)
            cnt = lax.fori_loop(0, nck, body, jnp.zeros((SUBLANES, TQ), F32))
            return cnt.sum(axis=0, keepdims=True)

        need = kf - count(lambda s, kp: s > thr)

        def step(_, st):
            lo_i, hi_i = st
            mid = jnp.floor(0.5 * (lo_i + hi_i))
            ge = count(lambda s, kp: (s == thr) & (kp <= mid)) >= need
            return jnp.where(ge, lo_i, mid), jnp.where(ge, mid, hi_i)

        st0 = (jnp.full((1, TQ), -1.0, F32), jnp.full((1, TQ), s_len - 1.0, F32))
        cut = lax.fori_loop(0, (s_len - 1).bit_length() + 1, step, st0)[1]
        cut_ref[...] = jnp.broadcast_to(cut, cut_ref.shape)

    m_ref[...] = jnp.full(m_ref.shape, NEG, F32)
    l_ref[...] = jnp.zeros(l_ref.shape, F32)
    acc_ref[...] = jnp.zeros(acc_ref.shape, F32)

    def att_chunk(c, carry):
        off = pl.multiple_of(c * CK, CK)
        sc = score_ref[pl.ds(off, CK), :]

        @pl.when(jnp.logical_not(has_ties))
        def _():
            bias_ref[...] = jnp.where(sc >= thr, 0.0, NEG)

        @pl.when(has_ties)
        def _():
            k_pos = (off + lax.broadcasted_iota(jnp.int32, (CK, 1), 0)).astype(F32)
            keep = (sc > thr) | ((sc == thr) & (k_pos <= cut_ref[0:1, :]))
            bias_ref[...] = jnp.where(keep, 0.0, NEG)

        kc = k_ref[pl.ds(off, CK), :]
        vtc = vt_ref[0, :, pl.ds(off, CK)]

        def qk(h):
            g = h // group
            return lax.dot_general(kc[:, g * HEAD_DIM:(g + 1) * HEAD_DIM],
                                   q_ref[:, h * HEAD_DIM:(h + 1) * HEAD_DIM], nt,
                                   preferred_element_type=F32)

        for h in range(n_heads):
            lg = qk(h) + bias_ref[...]
            m_old = m_ref[h:h + 1, :]
            m_new = jnp.maximum(m_old, lg.max(axis=0, keepdims=True))
            lg_ref[h] = lg
            alpha_ref[h:h + 1, :] = jnp.exp2(m_old - m_new)
            m_ref[h:h + 1, :] = m_new
        for h in range(n_heads):
            g = h // group
            p = jnp.exp2(lg_ref[h] - m_ref[h:h + 1, :]).astype(BF16)
            pv = jnp.dot(vtc[g * HEAD_DIM:(g + 1) * HEAD_DIM, :], p, preferred_element_type=F32)
            acc_ref[h] = alpha_ref[h:h + 1, :] * acc_ref[h] + pv
            psum = jnp.dot(ones_rows, p, preferred_element_type=F32)[0:1, :]
            l_ref[h:h + 1, :] = alpha_ref[h:h + 1, :] * l_ref[h:h + 1, :] + psum
        return carry

    lax.fori_loop(0, nck, att_chunk, 0)

    for h in range(n_heads):
        o = (acc_ref[h] / l_ref[h:h + 1, :]).T
        sl = slice(h * HEAD_DIM, (h + 1) * HEAD_DIM)
        o_ref[:, sl] = (o * sga_ref[:, sl].astype(F32)).astype(BF16)


def _prompt_attention(zb, vt, wt, lay, b, s, d, topk):
    nq = s // TQ
    n_heads = d // HEAD_DIM
    kern = functools.partial(_attn_kernel, topk)
    row = lambda bb, i: bb * nq + i
    return pl.pallas_call(
        kern,
        grid=(b, nq),
        in_specs=[pl.BlockSpec((TQ, d), lambda bb, i: (row(bb, i), lay.q * TN // d)),
                  pl.BlockSpec((TQ, d), lambda bb, i: (row(bb, i), lay.qi * TN // d)),
                  pl.BlockSpec((TQ, d), lambda bb, i: (row(bb, i), lay.ga * TN // d)),
                  pl.BlockSpec((s, TN), lambda bb, i: (bb, lay.k)),
                  pl.BlockSpec((s, IDX_DIM), lambda bb, i: (bb, lay.misc * TN // IDX_DIM)),
                  pl.BlockSpec((1, TN, s), lambda bb, i: (bb, 0, 0)),
                  pl.BlockSpec((IDX_HEADS, TQ), lambda bb, i: (0, row(bb, i)))],
        out_specs=pl.BlockSpec((TQ, d), lambda bb, i: (row(bb, i), 0)),
        out_shape=jax.ShapeDtypeStruct((b * s, d), BF16),
        scratch_shapes=[pltpu.VMEM((s, TQ), F32),
                        pltpu.VMEM((CK, TQ), F32),
                        pltpu.VMEM((SUBLANES, TQ), F32),
                        pltpu.VMEM((n_heads, TQ), F32),
                        pltpu.VMEM((n_heads, TQ), F32),
                        pltpu.VMEM((n_heads, TQ), F32),
                        pltpu.VMEM((n_heads, HEAD_DIM, TQ), F32),
                        pltpu.VMEM((n_heads, CK, TQ), F32)],
        compiler_params=_cparams(("arbitrary", "arbitrary")),
    )(zb, zb, zb, zb, zb, vt, wt)


def _pool_kernel(ts, u_ref, sgb_ref, wp_ref, sp_ref, o_ref, ext_ref):
    i = pl.program_id(1)
    halo = POOL_STATE + 1

    @pl.when(i == 0)
    def _():
        ext_ref[0:halo, :] = jnp.zeros((halo, ext_ref.shape[1]), F32)

    @pl.when(i > 0)
    def _():
        ext_ref[0:halo, :] = ext_ref[ts:ts + halo, :]

    ext_ref[halo:halo + ts, :] = u_ref[...]
    pos = i * ts + lax.broadcasted_iota(jnp.int32, (ts, 1), 0)
    gd = wp_ref.shape[1]
    for g, w in enumerate(POOL_WINDOWS):
        cols = slice(g * gd, (g + 1) * gd)
        acc = ext_ref[halo:halo + ts, cols]
        for jj in range(1, w):
            acc = acc + ext_ref[halo - jj:halo - jj + ts, cols]
        cnt = jnp.minimum(pos + 1, w).astype(F32)
        pooled = acc / cnt - u_ref[:, cols]
        mixed = jnp.dot(pooled.astype(BF16), wp_ref[g], preferred_element_type=F32) * sp_ref[:, cols]
        o_ref[:, cols] = (mixed * sgb_ref[:, cols].astype(F32)).astype(BF16)


def _pool_prompt(u, zb, wp, sp, lay, b, s, ts):
    m, pw = u.shape
    nt = s // ts
    kern = functools.partial(_pool_kernel, ts)
    return pl.pallas_call(
        kern,
        grid=(b, nt),
        in_specs=[pl.BlockSpec((ts, pw), lambda bb, i: (bb * nt + i, 0)),
                  pl.BlockSpec((ts, pw), lambda bb, i: (bb * nt + i, lay.gb * TN // pw)),
                  pl.BlockSpec(wp.shape, lambda bb, i: (0, 0, 0)),
                  pl.BlockSpec((1, pw), lambda bb, i: (0, 0))],
        out_specs=pl.BlockSpec((ts, pw), lambda bb, i: (bb * nt + i, 0)),
        out_shape=jax.ShapeDtypeStruct((m, pw), BF16),
        scratch_shapes=[pltpu.VMEM((POOL_STATE + 1 + ts, pw), F32)],
        compiler_params=_cparams(("arbitrary", "arbitrary")),
    )(u, zb, wp, sp)


def _pool_step_kernel(cnts, ext_ref, sgb_ref, wp_ref, sp_ref, o_ref):
    gd = wp_ref.shape[1]
    rows = ext_ref.shape[0]
    for g, w in enumerate(POOL_WINDOWS):
        cols = slice(g * gd, (g + 1) * gd)
        acc = ext_ref[rows - 1, :, cols]
        for jj in range(1, w):
            acc = acc + ext_ref[rows - 1 - jj, :, cols]
        pooled = acc / cnts[g] - ext_ref[rows - 1, :, cols]
        mixed = jnp.dot(pooled.astype(BF16), wp_ref[g], preferred_element_type=F32) * sp_ref[:, cols]
        o_ref[:, cols] = (mixed * sgb_ref[:, cols].astype(F32)).astype(BF16)


def _pool_step(ext_t, sgb, wp, sp, past):
    rows, n, pw = ext_t.shape
    cnts = tuple(float(min(past + 1, w)) for w in POOL_WINDOWS)
    return pl.pallas_call(
        functools.partial(_pool_step_kernel, cnts),
        out_shape=jax.ShapeDtypeStruct((n, pw), BF16),
    )(ext_t, sgb, wp, sp)


def _merge1_kernel(ab_ref, pb_ref, wa_ref, wb_ref, sma_ref, smb_ref, o_ref):
    a = jnp.dot(ab_ref[...], wa_ref[...], preferred_element_type=F32)
    bb = jnp.dot(pb_ref[...], wb_ref[...], preferred_element_type=F32)
    o_ref[...] = (sma_ref[...].astype(F32) * a + smb_ref[...].astype(F32) * bb).astype(BF16)


def _merge1(ab, pb, wa, wb, zb, lay, tm):
    m, d = ab.shape
    pw = pb.shape[1]
    return pl.pallas_call(
        _merge1_kernel,
        grid=(m // tm, d // TN),
        in_specs=[pl.BlockSpec((tm, d), lambda i, j: (i, 0)),
                  pl.BlockSpec((tm, pw), lambda i, j: (i, 0)),
                  pl.BlockSpec((d, TN), lambda i, j: (0, j)),
                  pl.BlockSpec((pw, TN), lambda i, j: (0, j)),
                  pl.BlockSpec((tm, TN), lambda i, j: (i, lay.ma + j)),
                  pl.BlockSpec((tm, TN), lambda i, j: (i, lay.mb + j))],
        out_specs=pl.BlockSpec((tm, TN), lambda i, j: (i, j)),
        out_shape=jax.ShapeDtypeStruct((m, d), BF16),
        compiler_params=_cparams(("arbitrary", "arbitrary")),
    )(ab, pb, wa, wb, zb, zb)


def _merge2_kernel(final, mm_ref, x_ref, gate_ref, wo_ref, gf_ref, o_ref):
    y = x_ref[...] + gate_ref[0] * jnp.dot(mm_ref[...], wo_ref[...], preferred_element_type=F32)
    if final:
        y = y * lax.rsqrt(jnp.mean(y * y, axis=-1, keepdims=True) + EPS) * gf_ref[...]
    o_ref[...] = y


def _merge2(mm, x, gate, wo, gf, tm, tiles_per_group, final):
    m, d = x.shape
    r = gate.shape[1]
    return pl.pallas_call(
        functools.partial(_merge2_kernel, final),
        grid=(m // tm,),
        in_specs=[pl.BlockSpec((tm, d), lambda i: (i, 0)),
                  pl.BlockSpec((tm, d), lambda i: (i, 0)),
                  pl.BlockSpec((1, r, d), lambda i: (i // tiles_per_group, 0, 0)),
                  pl.BlockSpec((d, d), lambda i: (0, 0)),
                  pl.BlockSpec((1, d), lambda i: (0, 0))],
        out_specs=pl.BlockSpec((tm, d), lambda i: (i, 0)),
        out_shape=jax.ShapeDtypeStruct((m, d), F32),
        compiler_params=_cparams(("arbitrary",)),
    )(mm, x, gate, wo, gf)


def _page_specs(width, npg, npp):
    return [pl.BlockSpec((1, PAGE_SIZE, width),
                         lambda b, p, pt, r=r: (pt[b * npg + p * npp + r], 0, 0)) for r in range(npp)]


def _sidx_kernel(npp, pt_ref, *refs):
    kidx_refs = refs[:npp]
    qi_ref, w_ref, o_ref = refs[npp:]
    nt = (((1,), (1,)), ((), ()))
    qi, w = qi_ref[0], w_ref[0]
    rows = []
    for r in range(npp):
        s = lax.dot_general(qi, kidx_refs[r][0].astype(BF16), nt, preferred_element_type=F32)
        rows.append(jnp.sum(w * jnp.maximum(s, 0.0), axis=0, keepdims=True))
    o_ref[0] = jnp.concatenate(rows, axis=1)


def _sample_scores(page_table, cache_kidx, qi3, wcol, npp):
    n, npg = page_table.shape
    return pl.pallas_call(
        functools.partial(_sidx_kernel, npp),
        grid_spec=pltpu.PrefetchScalarGridSpec(
            num_scalar_prefetch=1,
            grid=(n, npg // npp),
            in_specs=_page_specs(IDX_DIM, npg, npp) + [
                pl.BlockSpec((1, IDX_HEADS, IDX_DIM), lambda b, p, pt: (b, 0, 0)),
                pl.BlockSpec((1, IDX_HEADS, 1), lambda b, p, pt: (b, 0, 0))],
            out_specs=pl.BlockSpec((1, 1, npp * PAGE_SIZE), lambda b, p, pt: (b, 0, p))),
        out_shape=jax.ShapeDtypeStruct((n, 1, npg * PAGE_SIZE), F32),
        compiler_params=_cparams(("arbitrary", "arbitrary")),
    )(page_table.reshape(-1), *([cache_kidx] * npp), qi3, wcol)


def _sthr_kernel(topk, row_base, sc_ref, qi_ref, kin_ref, w_ref, pt_ref, pos_ref, cnt_ref, sel_ref,
                 thr_scr, new_scr):
    sc = sc_ref[...]
    kin = kin_ref[...].astype(BF16).astype(F32)
    s_new = jnp.sum(qi_ref[...].astype(F32) * kin, axis=2, keepdims=True)
    new = jnp.sum(w_ref[...] * jnp.maximum(s_new, 0.0), axis=1, keepdims=True)
    red = lambda f, a: f(f(a, axis=2, keepdims=True), axis=1, keepdims=True)
    lo = jnp.minimum(red(jnp.min, sc), new)
    hi = jnp.maximum(red(jnp.max, sc), new)

    def count_ge(thr):
        c = red(jnp.sum, jnp.where(sc >= thr, 1.0, 0.0))
        return c + jnp.where(new >= thr, 1.0, 0.0)

    n_all = jnp.full(new.shape, float(sc.shape[1] * sc.shape[2] + 1), F32)
    thr, _ = _bisect(count_ge, lo, hi, n_all, float(topk))
    thr_scr[...] = jnp.broadcast_to(thr, thr_scr.shape)
    new_scr[...] = jnp.broadcast_to(new, new_scr.shape)

    rows = sc.shape[1]
    ii = lambda shape, ax: lax.broadcasted_iota(jnp.int32, shape, ax)
    one = lambda m: jnp.where(m, 1.0, 0.0)
    tri_lane = one(ii((PAGE_SIZE, PAGE_SIZE), 0) <= ii((PAGE_SIZE, PAGE_SIZE), 1)).astype(BF16)
    lower = one(ii((rows, rows), 1) < ii((rows, rows), 0)).astype(BF16)
    upper = one(ii((rows, rows), 0) < ii((rows, rows), 1)).astype(BF16)
    ones8 = jnp.ones((SUBLANES, PAGE_SIZE), BF16)
    nt = (((1,), (1,)), ((), ()))
    total = lambda a: jnp.sum(jnp.sum(a, axis=1, keepdims=True), axis=0, keepdims=True)
    dotf = lambda a, b: jnp.dot(a.astype(BF16), b.astype(BF16), preferred_element_type=F32)
    kf = float(topk)
    j_col = ii((topk, 1), 0).astype(F32)
    c_row = ii((1, PAGE_SIZE), 1).astype(F32)

    def compact(b, carry):
        s = sc_ref[b]
        t = thr_scr[b][:, 0:1]
        nw = new_scr[b][:, 0:1]
        gt, eq = s > t, s == t
        need = kf - total(one(gt)) - one(nw > t)
        e_lane = dotf(one(eq), tri_lane)
        e_rank = e_lane + dotf(lower, e_lane)[:, PAGE_SIZE - 1:PAGE_SIZE]
        x = one(gt | (eq & (e_rank <= need)))
        sel_new = (nw > t) | ((nw == t) & (total(one(eq)) < need))
        x_lane = dotf(x, tri_lane)
        n_rows = lax.dot_general(ones8, x.astype(BF16), nt, preferred_element_type=F32)
        n_row = n_rows[0:1, :]
        start_row = dotf(n_rows, upper)[0:1, :]
        hit = one((start_row <= j_col) & (j_col < start_row + n_row))
        lane_sum = lambda a: jnp.sum(a, axis=1, keepdims=True)
        page = lane_sum(hit * pt_ref[b])
        target = j_col - lane_sum(hit * start_row) + 1.0
        in_row = dotf(hit, x_lane * x)
        off = lane_sum(jnp.where(in_row == target, c_row, 0.0))
        pos_ref[b] = (page * PAGE_SIZE + off + row_base).astype(jnp.int32)
        cnt_ref[b] = jnp.broadcast_to(total(x), cnt_ref.shape[1:])
        sel_ref[b] = jnp.broadcast_to(one(sel_new), sel_ref.shape[1:])
        return carry

    lax.fori_loop(0, sc.shape[0], compact, 0)


def _sample_select(sc, qi3, kin, wcol, pt_rows, topk, row_base):
    n = sc.shape[0]
    return pl.pallas_call(
        functools.partial(_sthr_kernel, topk, row_base),
        out_shape=[jax.ShapeDtypeStruct((n, topk, 1), jnp.int32),
                   jax.ShapeDtypeStruct((n, 1, PAGE_SIZE), F32),
                   jax.ShapeDtypeStruct((n, 1, PAGE_SIZE), F32)],
        scratch_shapes=[pltpu.VMEM((n, 1, PAGE_SIZE), F32),
                        pltpu.VMEM((n, 1, PAGE_SIZE), F32)],
        compiler_params=pltpu.CompilerParams(vmem_limit_bytes=VMEM_LIMIT),
    )(sc, qi3, kin, wcol, pt_rows)


def _row_copy(cache_rows, buf, sem, row, j):
    return pltpu.make_async_copy(cache_rows.at[row], buf.at[j], sem)


def _sattn_kernel(pos_ref, ck_hbm, cv_hbm, cnt_ref, sel_ref, q_ref, kn_ref, vn_ref,
                  sga_ref, o_ref, kbuf, vbuf, ksem, vsem):
    b = pl.program_id(0)
    slot = b % 2
    topk = kbuf.shape[1]
    n_heads = q_ref.shape[1]
    group = n_heads // N_KV_HEADS
    nt = (((1,), (1,)), ((), ()))

    def fetch(bb, s):
        def issue(j, carry):
            row = pos_ref[bb * topk + j]
            _row_copy(ck_hbm, kbuf.at[s], ksem.at[s], row, j).start()
            _row_copy(cv_hbm, vbuf.at[s], vsem.at[s], row, j).start()
            return carry
        lax.fori_loop(0, topk, issue, 0, unroll=8)

    @pl.when(b == 0)
    def _():
        fetch(b, slot)

    @pl.when(b + 1 < pl.num_programs(0))
    def _():
        fetch(b + 1, 1 - slot)

    kbuf, vbuf = kbuf.at[slot], vbuf.at[slot]
    if topk % PAGE_SIZE == 0:
        for w in range(topk // PAGE_SIZE):
            rows = pl.ds(w * PAGE_SIZE, PAGE_SIZE)
            pltpu.make_async_copy(ck_hbm.at[rows], kbuf.at[rows], ksem.at[slot]).wait()
            pltpu.make_async_copy(cv_hbm.at[rows], vbuf.at[rows], vsem.at[slot]).wait()
    else:
        def wait(j, carry):
            _row_copy(ck_hbm, kbuf, ksem.at[slot], 0, j).wait()
            _row_copy(cv_hbm, vbuf, vsem.at[slot], 0, j).wait()
            return carry

        lax.fori_loop(0, topk, wait, 0)

    q = q_ref[0]
    head = lax.broadcasted_iota(jnp.int32, (n_heads, 1), 0)
    in_group = [(head >= g * group) & (head < (g + 1) * group) for g in range(N_KV_HEADS)]
    lg = jnp.zeros((n_heads, topk), F32)
    for g in range(N_KV_HEADS):
        lg_g = lax.dot_general(q, kbuf[:, g, :].astype(BF16), nt, preferred_element_type=F32)
        lg = lg + jnp.where(in_group[g], lg_g, 0.0)
    valid = lax.broadcasted_iota(jnp.int32, (1, topk), 1).astype(F32) < cnt_ref[0][:, 0:1]
    lg = jnp.where(valid, lg, NEG)
    kn = kn_ref[0].astype(BF16).astype(F32)
    vn = vn_ref[0].astype(BF16).astype(F32)
    lgn = jnp.sum(q.astype(F32) * kn, axis=1, keepdims=True)
    lgn = jnp.where(sel_ref[0][:, 0:1] > 0.0, lgn, NEG)
    m = jnp.maximum(lg.max(axis=1, keepdims=True), lgn)
    p = jnp.exp2(lg - m)
    pn = jnp.exp2(lgn - m)
    l = p.sum(axis=1, keepdims=True) + pn
    pb = p.astype(BF16)
    acc = pn.astype(BF16).astype(F32) * vn
    for g in range(N_KV_HEADS):
        pv = jnp.dot(pb, vbuf[:, g, :].astype(BF16), preferred_element_type=F32)
        acc = acc + jnp.where(in_group[g], pv, 0.0)
    o_ref[0] = ((acc / l) * sga_ref[0].astype(F32)).astype(BF16)


def _sample_attention(k_rows, v_rows, pos, cnt, sel, q3, kn, vn, sga3):
    n, n_heads = q3.shape[:2]
    topk = pos.shape[1]
    per_b = lambda shape: pl.BlockSpec(shape, lambda b, ps: (b, 0, 0))
    hbm = pl.BlockSpec(memory_space=pl.ANY)
    return pl.pallas_call(
        _sattn_kernel,
        grid_spec=pltpu.PrefetchScalarGridSpec(
            num_scalar_prefetch=1,
            grid=(n,),
            in_specs=[hbm, hbm,
                      per_b((1, 1, PAGE_SIZE)),
                      per_b((1, 1, PAGE_SIZE)),
                      per_b((1, n_heads, HEAD_DIM)),
                      per_b((1, n_heads, HEAD_DIM)),
                      per_b((1, n_heads, HEAD_DIM)),
                      per_b((1, n_heads, HEAD_DIM))],
            out_specs=per_b((1, n_heads, HEAD_DIM)),
            scratch_shapes=[pltpu.VMEM((2, topk, N_KV_HEADS, HEAD_DIM), k_rows.dtype),
                            pltpu.VMEM((2, topk, N_KV_HEADS, HEAD_DIM), v_rows.dtype),
                            pltpu.SemaphoreType.DMA((2,)),
                            pltpu.SemaphoreType.DMA((2,))]),
        out_shape=jax.ShapeDtypeStruct((n, n_heads, HEAD_DIM), BF16),
        compiler_params=_cparams(("arbitrary",)),
    )(pos.reshape(-1), k_rows, v_rows, cnt, sel, q3, kn, vn, sga3)


def _rope_tables(pos):
    half = HEAD_DIM // 2
    inv = ROPE_THETA ** (-jnp.arange(half, dtype=F32) / half)
    ang = pos.astype(F32)[:, None] * inv[None, :]
    cos, sin = jnp.cos(ang), jnp.sin(ang)
    return jnp.concatenate([cos, cos], axis=1), jnp.concatenate([-sin, sin], axis=1)


def _tile_starts(d, lay):
    att, kv, idx, pool = d, N_KV_HEADS * HEAD_DIM, IDX_HEADS * IDX_DIM, d // 2
    sizes = (att, kv, kv, idx, IDX_DIM, IDX_HEADS, att, pool, pool, d, d)
    offs = [0]
    for n in sizes:
        offs.append(offs[-1] + n)
    q, k, v, qi, ki, wi, ga, u, gb, ma, mb = offs[:-1]
    starts = [0] * lay.end
    for first, tiles, col in ((lay.q, lay.qi - lay.q, q), (lay.qi, lay.ga - lay.qi, qi),
                              (lay.ga, lay.ma - lay.ga, ga), (lay.ma, lay.mb - lay.ma, ma),
                              (lay.mb, lay.gb - lay.mb, mb), (lay.gb, lay.k - lay.gb, gb),
                              (lay.k, 1, k), (lay.v, 1, v), (lay.misc, 1, ki), (lay.u, lay.end - lay.u, u)):
        for t in range(tiles):
            starts[first + t] = col + t * TN
    assert wi == ki + IDX_DIM and all(s + TN <= offs[-1] and s % SUBLANES == 0 for s in starts)
    return starts


def kernel(x_prompt, x_sample, c_prompt, c_sample, cache_k, cache_v, cache_kidx, state_pool, page_table,
           g_norm, w_ada, b_ada, w_in, w_pool, s_pool, w_a_proj, w_b_proj, w_out, g_final):
    bp, sp, d = x_prompt.shape
    ns, ds, _ = x_sample.shape
    depth = g_norm.shape[0]
    npg = page_table.shape[1]
    past = npg * PAGE_SIZE
    assert ds == 1 and d % (N_KV_HEADS * HEAD_DIM) == 0 and sp % TQ == 0 and TQ == CK and sp >= POOL_STATE
    n_heads = d // HEAD_DIM
    kvw = N_KV_HEADS * HEAD_DIM
    lay = _Layout(d)
    pw = lay.pool
    idx_scale = (IDX_HEADS * IDX_DIM) ** -0.5
    topk_p = min(TOPK_MAX, sp // 4)
    topk_s = min(TOPK_MAX, (past + ds) // 4)
    tm = math.gcd(sp, ROW_TILE)
    ts = math.gcd(sp, POOL_TILE)
    tm2 = math.gcd(sp, OUT_TILE)

    cos_p, sin_p = _rope_tables(jnp.arange(sp, dtype=jnp.int32))
    cos_s, sin_s = _rope_tables(jnp.full((ns,), past, jnp.int32))
    n_c = bp + ns
    c_rows = -(-n_c // SUBLANES) * SUBLANES
    c_all = jnp.concatenate([c_prompt, c_sample, jnp.zeros((c_rows - n_c, d), F32)], axis=0)

    n_pool = cache_kidx.shape[1]
    kidx_pages = cache_kidx.reshape(depth * n_pool, PAGE_SIZE, IDX_DIM)
    npp = math.gcd(npg, PAGES_PER_STEP)
    k_rows = cache_k.reshape(depth * n_pool * PAGE_SIZE, N_KV_HEADS, HEAD_DIM)
    v_rows = cache_v.reshape(depth * n_pool * PAGE_SIZE, N_KV_HEADS, HEAD_DIM)
    pt_rows = page_table.astype(F32).reshape(ns, 1, npg)
    assert npg % npp == 0

    in_width = w_in.shape[2]
    w_t = jnp.swapaxes(w_in, 1, 2).reshape(depth * in_width, d)

    xp = x_prompt.reshape(bp * sp, d)
    xs = x_sample.reshape(ns, d)
    outs = {n: [] for n in ("kp", "vp", "kip", "pp", "ks", "vs", "kis", "ps")}
    for l in range(depth):
        final = l == depth - 1
        starts = jnp.asarray([l * in_width + s for s in _tile_starts(d, lay)], jnp.int32)
        wa, wb, wo, wpl = (w_a_proj[l].astype(BF16), w_b_proj[l].astype(BF16), w_out[l].astype(BF16),
                           w_pool[l].astype(BF16))
        g = g_norm[l].reshape(1, d)
        spl = s_pool[l].reshape(1, pw)
        gf = g_final.reshape(1, d)

        mod = _ada(c_all, w_ada[l], b_ada[l])
        shift, scale, gate = mod[:, :d], mod[:, d:2 * d], mod[:, 2 * d:]
        pg = lambda a: a[:bp].reshape(bp, 1, d)
        sg = lambda a: a[bp:n_c].reshape(1, ns, d)

        zb, k_p, v_p, misc_p, u_p, vt, wt = _project(xp, g, pg(shift), pg(scale), w_t, starts, cos_p, sin_p, lay, tm,
                                                     sp // tm, sp // tm, idx_scale, True)
        ab = _prompt_attention(zb, vt, wt, lay, bp, sp, d, topk_p)
        pb = _pool_prompt(u_p, zb, wpl, spl, lay, bp, sp, ts)
        mm = _merge1(ab, pb, wa, wb, zb, lay, tm)
        xp = _merge2(mm, xp, pg(gate), wo, gf, tm2, sp // tm2, final)
        outs["kp"].append(k_p.reshape(bp, sp, N_KV_HEADS, HEAD_DIM))
        outs["vp"].append(v_p.reshape(bp, sp, N_KV_HEADS, HEAD_DIM))
        outs["kip"].append(misc_p[:, :IDX_DIM].reshape(bp, sp, IDX_DIM))
        outs["pp"].append(u_p.reshape(bp, sp, pw)[:, sp - POOL_STATE:, :])

        zs, k_s, v_s, misc_s, u_s = _project(xs, g, sg(shift), sg(scale), w_t, starts, cos_s, sin_s, lay, ns,
                                             1, 1, idx_scale, False)
        col = lambda t, n: zs[:, t * TN:t * TN + n]
        q3 = col(lay.q, d).reshape(ns, n_heads, HEAD_DIM)
        qi3 = col(lay.qi, IDX_HEADS * IDX_DIM).reshape(ns, IDX_HEADS, IDX_DIM)
        sga3 = col(lay.ga, d).reshape(ns, n_heads, HEAD_DIM)
        wcol = misc_s[:, IDX_DIM:IDX_DIM + IDX_HEADS].reshape(ns, IDX_HEADS, 1)
        kin = misc_s[:, :IDX_DIM].reshape(ns, 1, IDX_DIM)
        pt_l = page_table + l * n_pool
        sc = _sample_scores(pt_l, kidx_pages, qi3, wcol, npp)
        pos, cnt, sel = _sample_select(sc.reshape(ns, npg, PAGE_SIZE), qi3, kin, wcol, pt_rows, topk_s,
                                       l * n_pool * PAGE_SIZE)
        rep = lambda a: jnp.repeat(a.reshape(ns, N_KV_HEADS, HEAD_DIM), n_heads // N_KV_HEADS, axis=1)
        ab_s = _sample_attention(k_rows, v_rows, pos, cnt, sel, q3, rep(k_s), rep(v_s), sga3).reshape(ns, d)
        ext = jnp.concatenate([state_pool[l], u_s[:, None, :]], axis=1)
        pb_s = _pool_step(ext.transpose(1, 0, 2), col(lay.gb, pw), wpl, spl, past)
        mm_s = _merge1(ab_s, pb_s, wa, wb, zs, lay, ns)
        xs = _merge2(mm_s, xs, sg(gate), wo, gf, ns, 1, final)
        outs["ks"].append(k_s.reshape(ns, ds, N_KV_HEADS, HEAD_DIM))
        outs["vs"].append(v_s.reshape(ns, ds, N_KV_HEADS, HEAD_DIM))
        outs["kis"].append(misc_s[:, :IDX_DIM].reshape(ns, ds, IDX_DIM))
        outs["ps"].append(ext[:, 1:, :])

    st = lambda n: outs[n][0][None] if depth == 1 else jnp.stack(outs[n])
    return (xp.reshape(bp, sp, d), xs.reshape(ns, ds, d),
            st("kp"), st("vp"), st("kip"), st("pp"), st("ks"), st("vs"), st("kis"), st("ps"))
```

```python
import functools
import math

import jax
import jax.numpy as jnp
from jax import lax
from jax.experimental import pallas as pl
from jax.experimental.pallas import tpu as pltpu

F32 = jnp.float32
BF16 = jnp.bfloat16

HEAD_DIM = 128
N_KV_HEADS = 4
IDX_HEADS = 16
IDX_DIM = 128
TOPK_MAX = 256
PAGE_SIZE = 128
POOL_WINDOWS = (2, 4, 8, 16)
POOL_STATE = max(POOL_WINDOWS) - 1
ROPE_THETA = 10000.0
EPS = 1e-6

V7X_VMEM_BYTES = 64 * 1024 * 1024
V7X_MXU_COLUMNS = 256
SUBLANES = 8
TN = 512
ROW_TILE = 1024
POOL_TILE = 512
OUT_TILE = 256
TQ = V7X_MXU_COLUMNS
CK = 256
NEG = -1e30
Q_SCALE = 1.4426950408889634 * HEAD_DIM ** -0.5
PAGES_PER_STEP = 32
MAX_BISECT = 512
VMEM_LIMIT = V7X_VMEM_BYTES * 7 // 8


def _cparams(sem):
    return pltpu.CompilerParams(dimension_semantics=sem, vmem_limit_bytes=VMEM_LIMIT)


def _sigmoid(z):
    return 0.5 * jnp.tanh(0.5 * z) + 0.5


def _ada_kernel(c_ref, w_ref, b_ref, o_ref):
    o_ref[...] = jnp.dot(c_ref[...].astype(BF16), w_ref[...].astype(BF16),
                         preferred_element_type=F32) + b_ref[...]


def _ada(c, w, b):
    r, d = c.shape
    n = w.shape[1]
    return pl.pallas_call(
        _ada_kernel,
        grid=(n // TN,),
        in_specs=[pl.BlockSpec((r, d), lambda j: (0, 0)),
                  pl.BlockSpec((d, TN), lambda j: (0, j)),
                  pl.BlockSpec((1, TN), lambda j: (0, j))],
        out_specs=pl.BlockSpec((r, TN), lambda j: (0, j)),
        out_shape=jax.ShapeDtypeStruct((r, n), F32),
        compiler_params=_cparams(("arbitrary",)),
    )(c, w, b.reshape(1, n))


class _Layout:
    def __init__(self, d):
        att, kv, idx, pool = d, N_KV_HEADS * HEAD_DIM, IDX_HEADS * IDX_DIM, d // 2
        t = lambda n: n // TN
        assert att % TN == 0 and kv == TN and idx % TN == 0 and pool % TN == 0
        self.q = 0
        self.qi = self.q + t(att)
        self.ga = self.qi + t(idx)
        self.ma = self.ga + t(att)
        self.mb = self.ma + t(d)
        self.gb = self.mb + t(d)
        self.k = self.gb + t(pool)
        self.v = self.k + 1
        self.misc = self.v + 1
        self.u = self.misc + 1
        self.end = self.u + t(pool)
        self.nb = self.u
        self.pool = pool


def _rope_tile(z, cos, sin):
    parts = []
    for s in range(z.shape[1] // HEAD_DIM):
        zs = z[:, s * HEAD_DIM:(s + 1) * HEAD_DIM]
        parts.append(zs * cos + pltpu.roll(zs, HEAD_DIM // 2, 1) * sin)
    return parts[0] if len(parts) == 1 else jnp.concatenate(parts, axis=1)


def _modulated(x, g, shift, scale):
    r = lax.rsqrt(jnp.mean(x * x, axis=-1, keepdims=True) + EPS)
    return ((x * r * g) * (1.0 + scale) + shift).astype(BF16)


def _emit_tile(lay, idx_scale, q_scale, j, w_ref, h_ref, cos_ref, sin_ref, zb_ref, k_ref, v_ref, misc_ref, u_ref,
               vt_ref, wt_ref):
    nt = (((1,), (1,)), ((), ()))
    with_vt = vt_ref is not None

    def tile(keep_rows=None):
        w = w_ref[...]
        if keep_rows is not None:
            w = jnp.where(lax.broadcasted_iota(jnp.int32, (TN, 1), 0) < keep_rows, w, 0.0)
        return lax.dot_general(h_ref[...], w.astype(BF16), nt, preferred_element_type=F32)

    def store_heads(ref, val):
        if len(ref.shape) == 2:
            ref[...] = val
        else:
            for g in range(N_KV_HEADS):
                ref[:, g, :] = val[:, g * HEAD_DIM:(g + 1) * HEAD_DIM]

    @pl.when(j < lay.qi)
    def _():
        zb_ref[...] = (_rope_tile(tile(), cos_ref[...], sin_ref[...]) * q_scale).astype(BF16)

    @pl.when((j >= lay.qi) & (j < lay.ga))
    def _():
        zb_ref[...] = _rope_tile(tile(), cos_ref[...], sin_ref[...]).astype(BF16)

    @pl.when(j == lay.k)
    def _():
        r = _rope_tile(tile(), cos_ref[...], sin_ref[...])
        zb_ref[...] = r.astype(BF16)
        store_heads(k_ref, r)

    @pl.when(((j >= lay.ga) & (j < lay.ma)) | ((j >= lay.gb) & (j < lay.k)))
    def _():
        z = tile()
        zb_ref[...] = (z * _sigmoid(z)).astype(BF16)

    @pl.when((j >= lay.ma) & (j < lay.gb))
    def _():
        zb_ref[...] = _sigmoid(tile()).astype(BF16)

    @pl.when(j == lay.v)
    def _():
        z = tile()
        zb_ref[...] = z.astype(BF16)
        store_heads(v_ref, z)
        if with_vt:
            vt_ref[0] = z.T.astype(BF16)

    @pl.when(j == lay.misc)
    def _():
        z = tile(keep_rows=IDX_DIM + IDX_HEADS)
        ki = _rope_tile(z[:, :IDX_DIM], cos_ref[...], sin_ref[...])
        m = jnp.concatenate([ki, z[:, IDX_DIM:] * idx_scale], axis=1)
        zb_ref[...] = m.astype(BF16)
        misc_ref[...] = m
        if with_vt:
            wt_ref[...] = m[:, IDX_DIM:2 * IDX_DIM].T

    @pl.when(j >= lay.u)
    def _():
        u_ref[...] = tile()


def _proj_kernel(lay, idx_scale, q_scale, starts_ref, g_ref, w_ref,
                 x_ref, shift_ref, scale_ref, cos_ref, sin_ref,
                 xs_ref, shift_s_ref, scale_s_ref, cos_s_ref, sin_s_ref,
                 zb_ref, k_ref, v_ref, misc_ref, u_ref, vt_ref, wt_ref,
                 zs_ref, ks_ref, vs_ref, miscs_ref, us_ref, h_ref, hs_ref):
    i, j = pl.program_id(0), pl.program_id(1)

    @pl.when(j == 0)
    def _():
        h_ref[...] = _modulated(x_ref[...], g_ref[...], shift_ref[0], scale_ref[0])

    @pl.when((i == 0) & (j == 0))
    def _():
        hs_ref[...] = _modulated(xs_ref[...], g_ref[...], shift_s_ref[0], scale_s_ref[0])

    emit = functools.partial(_emit_tile, lay, idx_scale, q_scale, j, w_ref)
    emit(h_ref, cos_ref, sin_ref, zb_ref, k_ref, v_ref, misc_ref, u_ref, vt_ref, wt_ref)

    @pl.when(i == 0)
    def _():
        emit(hs_ref, cos_s_ref, sin_s_ref, zs_ref, ks_ref, vs_ref, miscs_ref, us_ref, None, None)


def _project(xp, xs, g, mod_p, mod_s, w_t, starts, rope_p, rope_s, lay, tm, tiles_per_group, idx_scale):
    m, d = xp.shape
    ns = xs.shape[0]
    grp = mod_p[0].shape[0]
    nu = lay.end - lay.u
    kern = functools.partial(_proj_kernel, lay, idx_scale, Q_SCALE)
    once = dict(pipeline_mode=pl.Buffered(1))
    zb_col = lambda j: jnp.minimum(j, lay.nb - 1)
    u_col = lambda j: jnp.clip(j - lay.u, 0, nu - 1)
    first = lambda i, col, last: jnp.where(i == 0, col, last)
    row_p = lambda shape: pl.BlockSpec(shape, lambda i, j, st: (i,) + (0,) * (len(shape) - 1), **once)
    fixed = lambda shape: pl.BlockSpec(shape, lambda i, j, st: (0,) * len(shape))
    grp_p = pl.BlockSpec((1, 1, d), lambda i, j, st: (i // tiles_per_group, 0, 0))
    pos_p = pl.BlockSpec((tm, HEAD_DIM), lambda i, j, st: (i % tiles_per_group, 0))
    out_specs = [pl.BlockSpec((tm, TN), lambda i, j, st: (i, zb_col(j))),
                 row_p((tm, N_KV_HEADS, HEAD_DIM)),
                 row_p((tm, N_KV_HEADS, HEAD_DIM)),
                 row_p((tm, TN)),
                 pl.BlockSpec((tm, TN), lambda i, j, st: (i, u_col(j))),
                 pl.BlockSpec((1, TN, tm), lambda i, j, st: (i // tiles_per_group, 0, i % tiles_per_group), **once),
                 pl.BlockSpec((IDX_DIM, tm), lambda i, j, st: (0, i), **once),
                 pl.BlockSpec((ns, TN), lambda i, j, st: (0, first(i, zb_col(j), lay.nb - 1))),
                 fixed((ns, TN)), fixed((ns, TN)), fixed((ns, TN)),
                 pl.BlockSpec((ns, TN), lambda i, j, st: (0, first(i, u_col(j), nu - 1)))]
    out_shape = [jax.ShapeDtypeStruct((m, lay.nb * TN), BF16),
                 jax.ShapeDtypeStruct((m, N_KV_HEADS, HEAD_DIM), F32),
                 jax.ShapeDtypeStruct((m, N_KV_HEADS, HEAD_DIM), F32),
                 jax.ShapeDtypeStruct((m, TN), F32),
                 jax.ShapeDtypeStruct((m, nu * TN), F32),
                 jax.ShapeDtypeStruct((grp, TN, tiles_per_group * tm), BF16),
                 jax.ShapeDtypeStruct((IDX_DIM, m), F32),
                 jax.ShapeDtypeStruct((ns, lay.nb * TN), BF16),
                 jax.ShapeDtypeStruct((ns, TN), F32),
                 jax.ShapeDtypeStruct((ns, TN), F32),
                 jax.ShapeDtypeStruct((ns, TN), F32),
                 jax.ShapeDtypeStruct((ns, nu * TN), F32)]
    outs = pl.pallas_call(
        kern,
        grid_spec=pltpu.PrefetchScalarGridSpec(
            num_scalar_prefetch=1,
            grid=(m // tm, lay.end),
            in_specs=[fixed((1, d)),
                      pl.BlockSpec((pl.Element(TN), pl.Element(d)),
                                   lambda i, j, st: (pl.multiple_of(st[j], SUBLANES), 0)),
                      pl.BlockSpec((tm, d), lambda i, j, st: (i, 0)), grp_p, grp_p, pos_p, pos_p,
                      fixed((ns, d)), fixed((1, ns, d)), fixed((1, ns, d)),
                      fixed((ns, HEAD_DIM)), fixed((ns, HEAD_DIM))],
            out_specs=out_specs,
            scratch_shapes=[pltpu.VMEM((tm, d), BF16), pltpu.VMEM((ns, d), BF16)]),
        out_shape=out_shape,
        compiler_params=_cparams(("arbitrary", "arbitrary")),
    )(starts, g, w_t, xp, *mod_p, *rope_p, xs, *mod_s, *rope_s)
    return outs[:7], outs[7:]


def _bisect(count_ge, lo, hi, clo, kf):
    def cond(st):
        return (st[4] > 0.0) & (st[5] < MAX_BISECT)

    def halve(lo, hi, clo, done):
        mid = 0.5 * lo + 0.5 * hi
        mid = jnp.where(mid <= lo, hi, mid)
        c = count_ge(mid)
        ge = c >= kf
        stuck = (mid >= hi) & jnp.logical_not(ge)
        lo2 = jnp.where(ge, mid, lo)
        hi2 = jnp.where(ge, hi, mid)
        clo2 = jnp.where(ge, c, clo)
        fin = (clo2 <= kf) | (lo2 >= hi2) | stuck
        return lo2, hi2, clo2, jnp.maximum(done, jnp.where(fin, 1.0, 0.0))

    def body(st):
        lo, hi, clo, done = halve(*halve(*st[:4]))
        return lo, hi, clo, done, jnp.sum(1.0 - done), st[5] + 1

    done0 = jnp.where((clo <= kf) | (lo >= hi), 1.0, 0.0)
    st = (lo, hi, clo, done0, jnp.sum(1.0 - done0), jnp.int32(0))
    out = lax.while_loop(cond, body, st)
    return out[0], out[2]


def _attn_kernel(topk, q_ref, qi_ref, sga_ref, k_ref, ki_ref, vt_ref, wt_ref, o_ref,
                 score_ref, bias_ref, cut_ref, m_ref, l_ref, alpha_ref, acc_ref, lg_ref):
    i = pl.program_id(1)
    nck = i + 1
    n_heads = q_ref.shape[1] // HEAD_DIM
    group = n_heads // N_KV_HEADS
    nt = (((1,), (1,)), ((), ()))
    q_pos = i * TQ + lax.broadcasted_iota(jnp.int32, (1, TQ), 1)
    fold = lambda a: a.reshape(CK // SUBLANES, SUBLANES, TQ)

    def score_chunk(c, carry):
        mn, mx = carry
        off = pl.multiple_of(c * CK, CK)
        kc = ki_ref[pl.ds(off, CK), :]
        acc = jnp.zeros((CK, TQ), F32)
        for h in range(IDX_HEADS):
            s = lax.dot_general(kc, qi_ref[:, h * IDX_DIM:(h + 1) * IDX_DIM], nt,
                                preferred_element_type=F32)
            acc = acc + wt_ref[h:h + 1, :] * jnp.maximum(s, 0.0)
        k_pos = off + lax.broadcasted_iota(jnp.int32, (CK, 1), 0)
        causal = k_pos <= q_pos
        score_ref[pl.ds(off, CK), :] = jnp.where(causal, acc, -jnp.inf)
        mn = jnp.minimum(mn, fold(jnp.where(causal, acc, jnp.inf)).min(axis=0))
        mx = jnp.maximum(mx, fold(jnp.where(causal, acc, -jnp.inf)).max(axis=0))
        return mn, mx

    mn, mx = lax.fori_loop(0, nck, score_chunk,
                           (jnp.full((SUBLANES, TQ), jnp.inf, F32),
                            jnp.full((SUBLANES, TQ), -jnp.inf, F32)))
    lo = mn.min(axis=0, keepdims=True)
    hi = mx.max(axis=0, keepdims=True)

    ones_rows = jnp.ones((2 * SUBLANES, CK), BF16)

    def count_ge(thr):
        def body(c, cnt):
            s = score_ref[pl.ds(pl.multiple_of(c * CK, CK), CK), :]
            return cnt + fold(jnp.where(s >= thr, 1.0, 0.0)).sum(axis=0)
        cnt = lax.fori_loop(0, nck, body, jnp.zeros((SUBLANES, TQ), F32))
        return cnt.sum(axis=0, keepdims=True)

    n_valid = (q_pos + 1).astype(F32)
    kf = float(topk)
    thr, n_ge = _bisect(count_ge, lo, hi, n_valid, kf)

    s_len = score_ref.shape[0]
    has_ties = jnp.sum(jnp.where(n_ge > kf, 1.0, 0.0)) > 0.0
    cut_ref[...] = jnp.full(cut_ref.shape, float(s_len), F32)

    @pl.when(has_ties)
    def _():
        def count(pred):
            def body(c, cnt):
                off = pl.multiple_of(c * CK, CK)
                k_pos = (off + lax.broadcasted_iota(jnp.int32, (CK, 1), 0)).astype(F32)
                hit = pred(score_ref[pl.ds(off, CK), :], k_pos)
                return cnt + fold(jnp.where(hit, 1.0, 0.0)).sum(axis=0)
            cnt = lax.fori_loop(0, nck, body, jnp.zeros((SUBLANES, TQ), F32))
            return cnt.sum(axis=0, keepdims=True)

        need = kf - count(lambda s, kp: s > thr)

        def step(_, st):
            lo_i, hi_i = st
            mid = jnp.floor(0.5 * (lo_i + hi_i))
            ge = count(lambda s, kp: (s == thr) & (kp <= mid)) >= need
            return jnp.where(ge, lo_i, mid), jnp.where(ge, mid, hi_i)

        st0 = (jnp.full((1, TQ), -1.0, F32), jnp.full((1, TQ), s_len - 1.0, F32))
        cut = lax.fori_loop(0, (s_len - 1).bit_length() + 1, step, st0)[1]
        cut_ref[...] = jnp.broadcast_to(cut, cut_ref.shape)

    m_ref[...] = jnp.full(m_ref.shape, NEG, F32)
    l_ref[...] = jnp.zeros(l_ref.shape, F32)
    acc_ref[...] = jnp.zeros(acc_ref.shape, F32)

    def att_chunk(c, carry):
        off = pl.multiple_of(c * CK, CK)
        sc = score_ref[pl.ds(off, CK), :]

        @pl.when(jnp.logical_not(has_ties))
        def _():
            bias_ref[...] = jnp.where(sc >= thr, 0.0, NEG)

        @pl.when(has_ties)
        def _():
            k_pos = (off + lax.broadcasted_iota(jnp.int32, (CK, 1), 0)).astype(F32)
            keep = (sc > thr) | ((sc == thr) & (k_pos <= cut_ref[0:1, :]))
            bias_ref[...] = jnp.where(keep, 0.0, NEG)

        kc = k_ref[pl.ds(off, CK), :]
        vtc = vt_ref[0, :, pl.ds(off, CK)]

        def qk(h):
            g = h // group
            return lax.dot_general(kc[:, g * HEAD_DIM:(g + 1) * HEAD_DIM],
                                   q_ref[:, h * HEAD_DIM:(h + 1) * HEAD_DIM], nt,
                                   preferred_element_type=F32)

        for h in range(n_heads):
            lg = qk(h) + bias_ref[...]
            m_old = m_ref[h:h + 1, :]
            m_new = jnp.maximum(m_old, lg.max(axis=0, keepdims=True))
            lg_ref[h] = lg
            alpha_ref[h:h + 1, :] = jnp.exp2(m_old - m_new)
            m_ref[h:h + 1, :] = m_new
        vt_ones = [jnp.concatenate([vtc[g * HEAD_DIM:(g + 1) * HEAD_DIM, :], ones_rows], axis=0)
                   for g in range(N_KV_HEADS)]
        for h in range(n_heads):
            p = jnp.exp2(lg_ref[h] - m_ref[h:h + 1, :]).astype(BF16)
            pv = jnp.dot(vt_ones[h // group], p, preferred_element_type=F32)
            acc_ref[h] = alpha_ref[h:h + 1, :] * acc_ref[h] + pv[:HEAD_DIM, :]
            l_ref[h:h + 1, :] = alpha_ref[h:h + 1, :] * l_ref[h:h + 1, :] + pv[HEAD_DIM:HEAD_DIM + 1, :]
        return carry

    lax.fori_loop(0, nck, att_chunk, 0)

    for h in range(n_heads):
        o = (acc_ref[h] / l_ref[h:h + 1, :]).T
        sl = slice(h * HEAD_DIM, (h + 1) * HEAD_DIM)
        o_ref[:, sl] = (o * sga_ref[:, sl].astype(F32)).astype(BF16)


def _prompt_attention(zb, vt, wt, lay, b, s, d, topk):
    nq = s // TQ
    n_heads = d // HEAD_DIM
    kern = functools.partial(_attn_kernel, topk)
    row = lambda bb, i: bb * nq + i
    return pl.pallas_call(
        kern,
        grid=(b, nq),
        in_specs=[pl.BlockSpec((TQ, d), lambda bb, i: (row(bb, i), lay.q * TN // d)),
                  pl.BlockSpec((TQ, d), lambda bb, i: (row(bb, i), lay.qi * TN // d)),
                  pl.BlockSpec((TQ, d), lambda bb, i: (row(bb, i), lay.ga * TN // d)),
                  pl.BlockSpec((s, TN), lambda bb, i: (bb, lay.k)),
                  pl.BlockSpec((s, IDX_DIM), lambda bb, i: (bb, lay.misc * TN // IDX_DIM)),
                  pl.BlockSpec((1, TN, s), lambda bb, i: (bb, 0, 0)),
                  pl.BlockSpec((IDX_HEADS, TQ), lambda bb, i: (0, row(bb, i)))],
        out_specs=pl.BlockSpec((TQ, d), lambda bb, i: (row(bb, i), 0)),
        out_shape=jax.ShapeDtypeStruct((b * s, d), BF16),
        scratch_shapes=[pltpu.VMEM((s, TQ), F32),
                        pltpu.VMEM((CK, TQ), F32),
                        pltpu.VMEM((SUBLANES, TQ), F32),
                        pltpu.VMEM((n_heads, TQ), F32),
                        pltpu.VMEM((n_heads, TQ), F32),
                        pltpu.VMEM((n_heads, TQ), F32),
                        pltpu.VMEM((n_heads, HEAD_DIM, TQ), F32),
                        pltpu.VMEM((n_heads, CK, TQ), F32)],
        compiler_params=_cparams(("arbitrary", "arbitrary")),
    )(zb, zb, zb, zb, zb, vt, wt)


def _pool_kernel(ts, u_ref, sgb_ref, wp_ref, sp_ref, o_ref, ext_ref):
    i = pl.program_id(1)
    halo = POOL_STATE + 1

    @pl.when(i == 0)
    def _():
        ext_ref[0:halo, :] = jnp.zeros((halo, ext_ref.shape[1]), F32)

    @pl.when(i > 0)
    def _():
        ext_ref[0:halo, :] = ext_ref[ts:ts + halo, :]

    ext_ref[halo:halo + ts, :] = u_ref[...]
    pos = i * ts + lax.broadcasted_iota(jnp.int32, (ts, 1), 0)
    gd = wp_ref.shape[1]
    for g, w in enumerate(POOL_WINDOWS):
        cols = slice(g * gd, (g + 1) * gd)
        acc = ext_ref[halo:halo + ts, cols]
        for jj in range(1, w):
            acc = acc + ext_ref[halo - jj:halo - jj + ts, cols]
        cnt = jnp.minimum(pos + 1, w).astype(F32)
        pooled = acc / cnt - u_ref[:, cols]
        mixed = jnp.dot(pooled.astype(BF16), wp_ref[g], preferred_element_type=F32) * sp_ref[:, cols]
        o_ref[:, cols] = (mixed * sgb_ref[:, cols].astype(F32)).astype(BF16)


def _pool_prompt(u, zb, wp, sp, lay, b, s, ts):
    m, pw = u.shape
    nt = s // ts
    kern = functools.partial(_pool_kernel, ts)
    return pl.pallas_call(
        kern,
        grid=(b, nt),
        in_specs=[pl.BlockSpec((ts, pw), lambda bb, i: (bb * nt + i, 0)),
                  pl.BlockSpec((ts, pw), lambda bb, i: (bb * nt + i, lay.gb * TN // pw)),
                  pl.BlockSpec(wp.shape, lambda bb, i: (0, 0, 0)),
                  pl.BlockSpec((1, pw), lambda bb, i: (0, 0))],
        out_specs=pl.BlockSpec((ts, pw), lambda bb, i: (bb * nt + i, 0)),
        out_shape=jax.ShapeDtypeStruct((m, pw), BF16),
        scratch_shapes=[pltpu.VMEM((POOL_STATE + 1 + ts, pw), F32)],
        compiler_params=_cparams(("arbitrary", "arbitrary")),
    )(u, zb, wp, sp)


def _pool_step_kernel(cnts, ext_ref, sgb_ref, wp_ref, sp_ref, o_ref):
    gd = wp_ref.shape[1]
    rows = ext_ref.shape[0]
    for g, w in enumerate(POOL_WINDOWS):
        cols = slice(g * gd, (g + 1) * gd)
        acc = ext_ref[rows - 1, :, cols]
        for jj in range(1, w):
            acc = acc + ext_ref[rows - 1 - jj, :, cols]
        pooled = acc / cnts[g] - ext_ref[rows - 1, :, cols]
        mixed = jnp.dot(pooled.astype(BF16), wp_ref[g], preferred_element_type=F32) * sp_ref[:, cols]
        o_ref[:, cols] = (mixed * sgb_ref[:, cols].astype(F32)).astype(BF16)


def _pool_step(ext_t, sgb, wp, sp, past):
    rows, n, pw = ext_t.shape
    cnts = tuple(float(min(past + 1, w)) for w in POOL_WINDOWS)
    return pl.pallas_call(
        functools.partial(_pool_step_kernel, cnts),
        out_shape=jax.ShapeDtypeStruct((n, pw), BF16),
    )(ext_t, sgb, wp, sp)


def _merge1_kernel(ab_ref, pb_ref, wa_ref, wb_ref, sma_ref, smb_ref, o_ref):
    a = jnp.dot(ab_ref[...], wa_ref[...], preferred_element_type=F32)
    bb = jnp.dot(pb_ref[...], wb_ref[...], preferred_element_type=F32)
    o_ref[...] = (sma_ref[...].astype(F32) * a + smb_ref[...].astype(F32) * bb).astype(BF16)


def _merge1(ab, pb, wa, wb, zb, lay, tm):
    m, d = ab.shape
    pw = pb.shape[1]
    return pl.pallas_call(
        _merge1_kernel,
        grid=(m // tm, d // TN),
        in_specs=[pl.BlockSpec((tm, d), lambda i, j: (i, 0)),
                  pl.BlockSpec((tm, pw), lambda i, j: (i, 0)),
                  pl.BlockSpec((d, TN), lambda i, j: (0, j)),
                  pl.BlockSpec((pw, TN), lambda i, j: (0, j)),
                  pl.BlockSpec((tm, TN), lambda i, j: (i, lay.ma + j)),
                  pl.BlockSpec((tm, TN), lambda i, j: (i, lay.mb + j))],
        out_specs=pl.BlockSpec((tm, TN), lambda i, j: (i, j)),
        out_shape=jax.ShapeDtypeStruct((m, d), BF16),
        compiler_params=_cparams(("arbitrary", "arbitrary")),
    )(ab, pb, wa, wb, zb, zb)


def _merge2_kernel(final, mm_ref, x_ref, gate_ref, wo_ref, gf_ref, o_ref):
    y = x_ref[...] + gate_ref[0] * jnp.dot(mm_ref[...], wo_ref[...], preferred_element_type=F32)
    if final:
        y = y * lax.rsqrt(jnp.mean(y * y, axis=-1, keepdims=True) + EPS) * gf_ref[...]
    o_ref[...] = y


def _merge2(mm, x, gate, wo, gf, tm, tiles_per_group, final):
    m, d = x.shape
    r = gate.shape[1]
    return pl.pallas_call(
        functools.partial(_merge2_kernel, final),
        grid=(m // tm,),
        in_specs=[pl.BlockSpec((tm, d), lambda i: (i, 0)),
                  pl.BlockSpec((tm, d), lambda i: (i, 0)),
                  pl.BlockSpec((1, r, d), lambda i: (i // tiles_per_group, 0, 0)),
                  pl.BlockSpec((d, d), lambda i: (0, 0)),
                  pl.BlockSpec((1, d), lambda i: (0, 0))],
        out_specs=pl.BlockSpec((tm, d), lambda i: (i, 0)),
        out_shape=jax.ShapeDtypeStruct((m, d), F32),
        compiler_params=_cparams(("arbitrary",)),
    )(mm, x, gate, wo, gf)


def _page_specs(width, npg, npp):
    return [pl.BlockSpec((1, PAGE_SIZE, width),
                         lambda b, p, pt, r=r: (pt[b * npg + p * npp + r], 0, 0)) for r in range(npp)]


def _sidx_kernel(npp, pt_ref, *refs):
    kidx_refs = refs[:npp]
    qi_ref, w_ref, o_ref = refs[npp:]
    nt = (((1,), (1,)), ((), ()))
    qi, w = qi_ref[0], w_ref[0]
    rows = []
    for r in range(npp):
        s = lax.dot_general(qi, kidx_refs[r][0].astype(BF16), nt, preferred_element_type=F32)
        rows.append(jnp.sum(w * jnp.maximum(s, 0.0), axis=0, keepdims=True))
    o_ref[0] = jnp.concatenate(rows, axis=1)


def _sample_scores(page_table, cache_kidx, qi3, wcol, npp):
    n, npg = page_table.shape
    return pl.pallas_call(
        functools.partial(_sidx_kernel, npp),
        grid_spec=pltpu.PrefetchScalarGridSpec(
            num_scalar_prefetch=1,
            grid=(n, npg // npp),
            in_specs=_page_specs(IDX_DIM, npg, npp) + [
                pl.BlockSpec((1, IDX_HEADS, IDX_DIM), lambda b, p, pt: (b, 0, 0)),
                pl.BlockSpec((1, IDX_HEADS, 1), lambda b, p, pt: (b, 0, 0))],
            out_specs=pl.BlockSpec((1, 1, npp * PAGE_SIZE), lambda b, p, pt: (b, 0, p))),
        out_shape=jax.ShapeDtypeStruct((n, 1, npg * PAGE_SIZE), F32),
        compiler_params=_cparams(("arbitrary", "arbitrary")),
    )(page_table.reshape(-1), *([cache_kidx] * npp), qi3, wcol)


def _sthr_kernel(topk, row_base, sc_ref, qi_ref, kin_ref, w_ref, pt_ref, pos_ref, cnt_ref, sel_ref,
                 thr_scr, new_scr):
    sc = sc_ref[...]
    kin = kin_ref[...].astype(BF16).astype(F32)
    s_new = jnp.sum(qi_ref[...].astype(F32) * kin, axis=2, keepdims=True)
    new = jnp.sum(w_ref[...] * jnp.maximum(s_new, 0.0), axis=1, keepdims=True)
    red = lambda f, a: f(f(a, axis=2, keepdims=True), axis=1, keepdims=True)
    lo = jnp.minimum(red(jnp.min, sc), new)
    hi = jnp.maximum(red(jnp.max, sc), new)

    def count_ge(thr):
        c = red(jnp.sum, jnp.where(sc >= thr, 1.0, 0.0))
        return c + jnp.where(new >= thr, 1.0, 0.0)

    n_all = jnp.full(new.shape, float(sc.shape[1] * sc.shape[2] + 1), F32)
    thr, _ = _bisect(count_ge, lo, hi, n_all, float(topk))
    thr_scr[...] = jnp.broadcast_to(thr, thr_scr.shape)
    new_scr[...] = jnp.broadcast_to(new, new_scr.shape)

    rows = sc.shape[1]
    ii = lambda shape, ax: lax.broadcasted_iota(jnp.int32, shape, ax)
    one = lambda m: jnp.where(m, 1.0, 0.0)
    tri_lane = one(ii((PAGE_SIZE, PAGE_SIZE), 0) <= ii((PAGE_SIZE, PAGE_SIZE), 1)).astype(BF16)
    lower = one(ii((rows, rows), 1) < ii((rows, rows), 0)).astype(BF16)
    upper = one(ii((rows, rows), 0) < ii((rows, rows), 1)).astype(BF16)
    ones8 = jnp.ones((SUBLANES, PAGE_SIZE), BF16)
    nt = (((1,), (1,)), ((), ()))
    total = lambda a: jnp.sum(jnp.sum(a, axis=1, keepdims=True), axis=0, keepdims=True)
    dotf = lambda a, b: jnp.dot(a.astype(BF16), b.astype(BF16), preferred_element_type=F32)
    kf = float(topk)
    j_col = ii((topk, 1), 0).astype(F32)
    c_row = ii((1, PAGE_SIZE), 1).astype(F32)

    def compact(b, carry):
        s = sc_ref[b]
        t = thr_scr[b][:, 0:1]
        nw = new_scr[b][:, 0:1]
        gt, eq = s > t, s == t
        need = kf - total(one(gt)) - one(nw > t)
        e_lane = dotf(one(eq), tri_lane)
        e_rank = e_lane + dotf(lower, e_lane)[:, PAGE_SIZE - 1:PAGE_SIZE]
        x = one(gt | (eq & (e_rank <= need)))
        sel_new = (nw > t) | ((nw == t) & (total(one(eq)) < need))
        x_lane = dotf(x, tri_lane)
        n_rows = lax.dot_general(ones8, x.astype(BF16), nt, preferred_element_type=F32)
        n_row = n_rows[0:1, :]
        start_row = dotf(n_rows, upper)[0:1, :]
        hit = one((start_row <= j_col) & (j_col < start_row + n_row))
        lane_sum = lambda a: jnp.sum(a, axis=1, keepdims=True)
        page = lane_sum(hit * pt_ref[b])
        target = j_col - lane_sum(hit * start_row) + 1.0
        in_row = dotf(hit, x_lane * x)
        off = lane_sum(jnp.where(in_row == target, c_row, 0.0))
        pos_ref[b] = (page * PAGE_SIZE + off + row_base).astype(jnp.int32)
        cnt_ref[b] = jnp.broadcast_to(total(x), cnt_ref.shape[1:])
        sel_ref[b] = jnp.broadcast_to(one(sel_new), sel_ref.shape[1:])
        return carry

    lax.fori_loop(0, sc.shape[0], compact, 0)


def _sample_select(sc, qi3, kin, wcol, pt_rows, topk, row_base):
    n = sc.shape[0]
    return pl.pallas_call(
        functools.partial(_sthr_kernel, topk, row_base),
        out_shape=[jax.ShapeDtypeStruct((n, topk, 1), jnp.int32),
                   jax.ShapeDtypeStruct((n, 1, PAGE_SIZE), F32),
                   jax.ShapeDtypeStruct((n, 1, PAGE_SIZE), F32)],
        scratch_shapes=[pltpu.VMEM((n, 1, PAGE_SIZE), F32),
                        pltpu.VMEM((n, 1, PAGE_SIZE), F32)],
        compiler_params=pltpu.CompilerParams(vmem_limit_bytes=VMEM_LIMIT),
    )(sc, qi3, kin, wcol, pt_rows)


def _row_copy(cache_rows, buf, sem, row, j):
    return pltpu.make_async_copy(cache_rows.at[row], buf.at[j], sem)


def _sattn_kernel(pos_ref, ck_hbm, cv_hbm, cnt_ref, sel_ref, q_ref, kn_ref, vn_ref,
                  sga_ref, o_ref, kbuf, vbuf, ksem, vsem):
    b = pl.program_id(0)
    slot = b % 2
    topk = kbuf.shape[1]
    n_heads = q_ref.shape[1]
    group = n_heads // N_KV_HEADS
    nt = (((1,), (1,)), ((), ()))

    def fetch(bb, s):
        def issue(j, carry):
            row = pos_ref[bb * topk + j]
            _row_copy(ck_hbm, kbuf.at[s], ksem.at[s], row, j).start()
            _row_copy(cv_hbm, vbuf.at[s], vsem.at[s], row, j).start()
            return carry
        lax.fori_loop(0, topk, issue, 0, unroll=8)

    @pl.when(b == 0)
    def _():
        fetch(b, slot)

    @pl.when(b + 1 < pl.num_programs(0))
    def _():
        fetch(b + 1, 1 - slot)

    kbuf, vbuf = kbuf.at[slot], vbuf.at[slot]
    if topk % PAGE_SIZE == 0:
        for w in range(topk // PAGE_SIZE):
            rows = pl.ds(w * PAGE_SIZE, PAGE_SIZE)
            pltpu.make_async_copy(ck_hbm.at[rows], kbuf.at[rows], ksem.at[slot]).wait()
            pltpu.make_async_copy(cv_hbm.at[rows], vbuf.at[rows], vsem.at[slot]).wait()
    else:
        def wait(j, carry):
            _row_copy(ck_hbm, kbuf, ksem.at[slot], 0, j).wait()
            _row_copy(cv_hbm, vbuf, vsem.at[slot], 0, j).wait()
            return carry

        lax.fori_loop(0, topk, wait, 0)

    q = q_ref[0]
    head = lax.broadcasted_iota(jnp.int32, (n_heads, 1), 0)
    in_group = [(head >= g * group) & (head < (g + 1) * group) for g in range(N_KV_HEADS)]
    lg = jnp.zeros((n_heads, topk), F32)
    for g in range(N_KV_HEADS):
        lg_g = lax.dot_general(q, kbuf[:, g, :].astype(BF16), nt, preferred_element_type=F32)
        lg = lg + jnp.where(in_group[g], lg_g, 0.0)
    valid = lax.broadcasted_iota(jnp.int32, (1, topk), 1).astype(F32) < cnt_ref[0][:, 0:1]
    lg = jnp.where(valid, lg, NEG)
    kn = kn_ref[0].astype(BF16).astype(F32)
    vn = vn_ref[0].astype(BF16).astype(F32)
    lgn = jnp.sum(q.astype(F32) * kn, axis=1, keepdims=True)
    lgn = jnp.where(sel_ref[0][:, 0:1] > 0.0, lgn, NEG)
    m = jnp.maximum(lg.max(axis=1, keepdims=True), lgn)
    p = jnp.exp2(lg - m)
    pn = jnp.exp2(lgn - m)
    l = p.sum(axis=1, keepdims=True) + pn
    pb = p.astype(BF16)
    acc = pn.astype(BF16).astype(F32) * vn
    for g in range(N_KV_HEADS):
        pv = jnp.dot(pb, vbuf[:, g, :].astype(BF16), preferred_element_type=F32)
        acc = acc + jnp.where(in_group[g], pv, 0.0)
    o_ref[0] = ((acc / l) * sga_ref[0].astype(F32)).astype(BF16)


def _sample_attention(k_rows, v_rows, pos, cnt, sel, q3, kn, vn, sga3):
    n, n_heads = q3.shape[:2]
    topk = pos.shape[1]
    per_b = lambda shape: pl.BlockSpec(shape, lambda b, ps: (b, 0, 0))
    hbm = pl.BlockSpec(memory_space=pl.ANY)
    return pl.pallas_call(
        _sattn_kernel,
        grid_spec=pltpu.PrefetchScalarGridSpec(
            num_scalar_prefetch=1,
            grid=(n,),
            in_specs=[hbm, hbm,
                      per_b((1, 1, PAGE_SIZE)),
                      per_b((1, 1, PAGE_SIZE)),
                      per_b((1, n_heads, HEAD_DIM)),
                      per_b((1, n_heads, HEAD_DIM)),
                      per_b((1, n_heads, HEAD_DIM)),
                      per_b((1, n_heads, HEAD_DIM))],
            out_specs=per_b((1, n_heads, HEAD_DIM)),
            scratch_shapes=[pltpu.VMEM((2, topk, N_KV_HEADS, HEAD_DIM), k_rows.dtype),
                            pltpu.VMEM((2, topk, N_KV_HEADS, HEAD_DIM), v_rows.dtype),
                            pltpu.SemaphoreType.DMA((2,)),
                            pltpu.SemaphoreType.DMA((2,))]),
        out_shape=jax.ShapeDtypeStruct((n, n_heads, HEAD_DIM), BF16),
        compiler_params=_cparams(("arbitrary",)),
    )(pos.reshape(-1), k_rows, v_rows, cnt, sel, q3, kn, vn, sga3)


def _rope_tables(pos):
    half = HEAD_DIM // 2
    inv = ROPE_THETA ** (-jnp.arange(half, dtype=F32) / half)
    ang = pos.astype(F32)[:, None] * inv[None, :]
    cos, sin = jnp.cos(ang), jnp.sin(ang)
    return jnp.concatenate([cos, cos], axis=1), jnp.concatenate([-sin, sin], axis=1)


def _tile_starts(d, lay):
    att, kv, idx, pool = d, N_KV_HEADS * HEAD_DIM, IDX_HEADS * IDX_DIM, d // 2
    sizes = (att, kv, kv, idx, IDX_DIM, IDX_HEADS, att, pool, pool, d, d)
    offs = [0]
    for n in sizes:
        offs.append(offs[-1] + n)
    q, k, v, qi, ki, wi, ga, u, gb, ma, mb = offs[:-1]
    starts = [0] * lay.end
    for first, tiles, col in ((lay.q, lay.qi - lay.q, q), (lay.qi, lay.ga - lay.qi, qi),
                              (lay.ga, lay.ma - lay.ga, ga), (lay.ma, lay.mb - lay.ma, ma),
                              (lay.mb, lay.gb - lay.mb, mb), (lay.gb, lay.k - lay.gb, gb),
                              (lay.k, 1, k), (lay.v, 1, v), (lay.misc, 1, ki), (lay.u, lay.end - lay.u, u)):
        for t in range(tiles):
            starts[first + t] = col + t * TN
    assert wi == ki + IDX_DIM and all(s + TN <= offs[-1] and s % SUBLANES == 0 for s in starts)
    return starts


def kernel(x_prompt, x_sample, c_prompt, c_sample, cache_k, cache_v, cache_kidx, state_pool, page_table,
           g_norm, w_ada, b_ada, w_in, w_pool, s_pool, w_a_proj, w_b_proj, w_out, g_final):
    bp, sp, d = x_prompt.shape
    ns, ds, _ = x_sample.shape
    depth = g_norm.shape[0]
    npg = page_table.shape[1]
    past = npg * PAGE_SIZE
    assert ds == 1 and d % (N_KV_HEADS * HEAD_DIM) == 0 and sp % TQ == 0 and TQ == CK and sp >= POOL_STATE
    n_heads = d // HEAD_DIM
    kvw = N_KV_HEADS * HEAD_DIM
    lay = _Layout(d)
    pw = lay.pool
    idx_scale = (IDX_HEADS * IDX_DIM) ** -0.5
    topk_p = min(TOPK_MAX, sp // 4)
    topk_s = min(TOPK_MAX, (past + ds) // 4)
    tm = math.gcd(sp, ROW_TILE)
    ts = math.gcd(sp, POOL_TILE)
    tm2 = math.gcd(sp, OUT_TILE)

    cos_p, sin_p = _rope_tables(jnp.arange(sp, dtype=jnp.int32))
    cos_s, sin_s = _rope_tables(jnp.full((ns,), past, jnp.int32))
    n_c = bp + ns
    c_rows = -(-n_c // SUBLANES) * SUBLANES
    c_all = jnp.concatenate([c_prompt, c_sample, jnp.zeros((c_rows - n_c, d), F32)], axis=0)

    n_pool = cache_kidx.shape[1]
    kidx_pages = cache_kidx.reshape(depth * n_pool, PAGE_SIZE, IDX_DIM)
    npp = math.gcd(npg, PAGES_PER_STEP)
    k_rows = cache_k.reshape(depth * n_pool * PAGE_SIZE, N_KV_HEADS, HEAD_DIM)
    v_rows = cache_v.reshape(depth * n_pool * PAGE_SIZE, N_KV_HEADS, HEAD_DIM)
    pt_rows = page_table.astype(F32).reshape(ns, 1, npg)
    assert npg % npp == 0

    in_width = w_in.shape[2]
    w_t = jnp.swapaxes(w_in, 1, 2).reshape(depth * in_width, d)

    xp = x_prompt.reshape(bp * sp, d)
    xs = x_sample.reshape(ns, d)
    outs = {n: [] for n in ("kp", "vp", "kip", "pp", "ks", "vs", "kis", "ps")}
    for l in range(depth):
        final = l == depth - 1
        starts = jnp.asarray([l * in_width + s for s in _tile_starts(d, lay)], jnp.int32)
        wa, wb, wo, wpl = (w_a_proj[l].astype(BF16), w_b_proj[l].astype(BF16), w_out[l].astype(BF16),
                           w_pool[l].astype(BF16))
        g = g_norm[l].reshape(1, d)
        spl = s_pool[l].reshape(1, pw)
        gf = g_final.reshape(1, d)

        mod = _ada(c_all, w_ada[l], b_ada[l])
        shift, scale, gate = mod[:, :d], mod[:, d:2 * d], mod[:, 2 * d:]
        pg = lambda a: a[:bp].reshape(bp, 1, d)
        sg = lambda a: a[bp:n_c].reshape(1, ns, d)

        (zb, k_p, v_p, misc_p, u_p, vt, wt), (zs, k_s, v_s, misc_s, u_s) = _project(
            xp, xs, g, (pg(shift), pg(scale)), (sg(shift), sg(scale)), w_t, starts,
            (cos_p, sin_p), (cos_s, sin_s), lay, tm, sp // tm, idx_scale)

        ab = _prompt_attention(zb, vt, wt, lay, bp, sp, d, topk_p)
        pb = _pool_prompt(u_p, zb, wpl, spl, lay, bp, sp, ts)
        mm = _merge1(ab, pb, wa, wb, zb, lay, tm)
        xp = _merge2(mm, xp, pg(gate), wo, gf, tm2, sp // tm2, final)
        outs["kp"].append(k_p.reshape(bp, sp, N_KV_HEADS, HEAD_DIM))
        outs["vp"].append(v_p.reshape(bp, sp, N_KV_HEADS, HEAD_DIM))
        outs["kip"].append(misc_p[:, :IDX_DIM].reshape(bp, sp, IDX_DIM))
        outs["pp"].append(u_p.reshape(bp, sp, pw)[:, sp - POOL_STATE:, :])

        col = lambda t, n: zs[:, t * TN:t * TN + n]
        q3 = col(lay.q, d).reshape(ns, n_heads, HEAD_DIM)
        qi3 = col(lay.qi, IDX_HEADS * IDX_DIM).reshape(ns, IDX_HEADS, IDX_DIM)
        sga3 = col(lay.ga, d).reshape(ns, n_heads, HEAD_DIM)
        wcol = misc_s[:, IDX_DIM:IDX_DIM + IDX_HEADS].reshape(ns, IDX_HEADS, 1)
        kin = misc_s[:, :IDX_DIM].reshape(ns, 1, IDX_DIM)
        pt_l = page_table + l * n_pool
        sc = _sample_scores(pt_l, kidx_pages, qi3, wcol, npp)
        pos, cnt, sel = _sample_select(sc.reshape(ns, npg, PAGE_SIZE), qi3, kin, wcol, pt_rows, topk_s,
                                       l * n_pool * PAGE_SIZE)
        rep = lambda a: jnp.repeat(a.reshape(ns, N_KV_HEADS, HEAD_DIM), n_heads // N_KV_HEADS, axis=1)
        ab_s = _sample_attention(k_rows, v_rows, pos, cnt, sel, q3, rep(k_s), rep(v_s), sga3).reshape(ns, d)
        ext = jnp.concatenate([state_pool[l], u_s[:, None, :]], axis=1)
        pb_s = _pool_step(ext.transpose(1, 0, 2), col(lay.gb, pw), wpl, spl, past)
        mm_s = _merge1(ab_s, pb_s, wa, wb, zs, lay, ns)
        xs = _merge2(mm_s, xs, sg(gate), wo, gf, ns, 1, final)
        outs["ks"].append(k_s.reshape(ns, ds, N_KV_HEADS, HEAD_DIM))
        outs["vs"].append(v_s.reshape(ns, ds, N_KV_HEADS, HEAD_DIM))
        outs["kis"].append(misc_s[:, :IDX_DIM].reshape(ns, ds, IDX_DIM))
        outs["ps"].append(ext[:, 1:, :])

    st = lambda n: outs[n][0][None] if depth == 1 else jnp.stack(outs[n])
    return (xp.reshape(bp, sp, d), xs.reshape(ns, ds, d),
            st("kp"), st("vp"), st("kip"), st("pp"), st("ks"), st("vs"), st("kis"), st("ps"))
```

```python
import functools
import math

import jax
import jax.numpy as jnp
from jax import lax
from jax.experimental import pallas as pl
from jax.experimental.pallas import tpu as pltpu

F32 = jnp.float32
BF16 = jnp.bfloat16

HEAD_DIM = 128
N_KV_HEADS = 4
IDX_HEADS = 16
IDX_DIM = 128
TOPK_MAX = 256
PAGE_SIZE = 128
POOL_WINDOWS = (2, 4, 8, 16)
POOL_STATE = max(POOL_WINDOWS) - 1
ROPE_THETA = 10000.0
EPS = 1e-6

V7X_VMEM_BYTES = 64 * 1024 * 1024
V7X_MXU_COLUMNS = 256
SUBLANES = 8
TN = 512
ROW_TILE = 1024
POOL_TILE = 512
OUT_TILE = 512
TQ = V7X_MXU_COLUMNS
CK = 256
NEG = -1e30
Q_SCALE = 1.4426950408889634 * HEAD_DIM ** -0.5
PAGES_PER_STEP = 32
MAX_BISECT = 512
VMEM_LIMIT = V7X_VMEM_BYTES * 7 // 8


def _cparams(sem):
    return pltpu.CompilerParams(dimension_semantics=sem, vmem_limit_bytes=VMEM_LIMIT)


def _sigmoid(z):
    return 0.5 * jnp.tanh(0.5 * z) + 0.5


def _ada_kernel(c_ref, w_ref, b_ref, o_ref):
    o_ref[...] = jnp.dot(c_ref[...].astype(BF16), w_ref[...].astype(BF16),
                         preferred_element_type=F32) + b_ref[...]


def _ada(c, w, b):
    r, d = c.shape
    n = w.shape[1]
    return pl.pallas_call(
        _ada_kernel,
        grid=(n // TN,),
        in_specs=[pl.BlockSpec((r, d), lambda j: (0, 0)),
                  pl.BlockSpec((d, TN), lambda j: (0, j)),
                  pl.BlockSpec((1, TN), lambda j: (0, j))],
        out_specs=pl.BlockSpec((r, TN), lambda j: (0, j)),
        out_shape=jax.ShapeDtypeStruct((r, n), F32),
        compiler_params=_cparams(("arbitrary",)),
    )(c, w, b.reshape(1, n))


class _Layout:
    def __init__(self, d):
        att, kv, idx, pool = d, N_KV_HEADS * HEAD_DIM, IDX_HEADS * IDX_DIM, d // 2
        t = lambda n: n // TN
        assert att % TN == 0 and kv == TN and idx % TN == 0 and pool % TN == 0
        self.q = 0
        self.qi = self.q + t(att)
        self.ga = self.qi + t(idx)
        self.ma = self.ga + t(att)
        self.mb = self.ma + t(d)
        self.gb = self.mb + t(d)
        self.k = self.gb + t(pool)
        self.v = self.k + 1
        self.misc = self.v + 1
        self.u = self.misc + 1
        self.end = self.u + t(pool)
        self.nb = self.u
        self.pool = pool


def _rope_tile(z, cos, sin):
    parts = []
    for s in range(z.shape[1] // HEAD_DIM):
        zs = z[:, s * HEAD_DIM:(s + 1) * HEAD_DIM]
        parts.append(zs * cos + pltpu.roll(zs, HEAD_DIM // 2, 1) * sin)
    return parts[0] if len(parts) == 1 else jnp.concatenate(parts, axis=1)


def _modulated(x, g, shift, scale):
    r = lax.rsqrt(jnp.mean(x * x, axis=-1, keepdims=True) + EPS)
    return ((x * r * g) * (1.0 + scale) + shift).astype(BF16)


def _emit_tile(lay, idx_scale, q_scale, j, w_ref, h_ref, cos_ref, sin_ref, zb_ref, k_ref, v_ref, misc_ref, u_ref,
               vt_ref, wt_ref):
    nt = (((1,), (1,)), ((), ()))
    with_vt = vt_ref is not None

    def tile(keep_rows=None):
        w = w_ref[...]
        if keep_rows is not None:
            w = jnp.where(lax.broadcasted_iota(jnp.int32, (TN, 1), 0) < keep_rows, w, 0.0)
        return lax.dot_general(h_ref[...], w.astype(BF16), nt, preferred_element_type=F32)

    def store_heads(ref, val):
        if len(ref.shape) == 2:
            ref[...] = val
        else:
            for g in range(N_KV_HEADS):
                ref[:, g, :] = val[:, g * HEAD_DIM:(g + 1) * HEAD_DIM]

    @pl.when(j < lay.qi)
    def _():
        zb_ref[...] = (_rope_tile(tile(), cos_ref[...], sin_ref[...]) * q_scale).astype(BF16)

    @pl.when((j >= lay.qi) & (j < lay.ga))
    def _():
        zb_ref[...] = _rope_tile(tile(), cos_ref[...], sin_ref[...]).astype(BF16)

    @pl.when(j == lay.k)
    def _():
        r = _rope_tile(tile(), cos_ref[...], sin_ref[...])
        zb_ref[...] = r.astype(BF16)
        store_heads(k_ref, r)

    @pl.when(((j >= lay.ga) & (j < lay.ma)) | ((j >= lay.gb) & (j < lay.k)))
    def _():
        z = tile()
        zb_ref[...] = (z * _sigmoid(z)).astype(BF16)

    @pl.when((j >= lay.ma) & (j < lay.gb))
    def _():
        zb_ref[...] = _sigmoid(tile()).astype(BF16)

    @pl.when(j == lay.v)
    def _():
        z = tile()
        zb_ref[...] = z.astype(BF16)
        store_heads(v_ref, z)
        if with_vt:
            vt_ref[0] = z.T.astype(BF16)

    @pl.when(j == lay.misc)
    def _():
        z = tile(keep_rows=IDX_DIM + IDX_HEADS)
        ki = _rope_tile(z[:, :IDX_DIM], cos_ref[...], sin_ref[...])
        m = jnp.concatenate([ki, z[:, IDX_DIM:] * idx_scale], axis=1)
        zb_ref[...] = m.astype(BF16)
        misc_ref[...] = m
        if with_vt:
            wt_ref[...] = m[:, IDX_DIM:2 * IDX_DIM].T

    @pl.when(j >= lay.u)
    def _():
        u_ref[...] = tile()


def _proj_kernel(lay, idx_scale, q_scale, starts_ref, g_ref, w_ref,
                 x_ref, shift_ref, scale_ref, cos_ref, sin_ref,
                 xs_ref, shift_s_ref, scale_s_ref, cos_s_ref, sin_s_ref,
                 zb_ref, k_ref, v_ref, misc_ref, u_ref, vt_ref, wt_ref,
                 zs_ref, ks_ref, vs_ref, miscs_ref, us_ref, h_ref, hs_ref):
    i, j = pl.program_id(0), pl.program_id(1)

    @pl.when(j == 0)
    def _():
        h_ref[...] = _modulated(x_ref[...], g_ref[...], shift_ref[0], scale_ref[0])

    @pl.when((i == 0) & (j == 0))
    def _():
        hs_ref[...] = _modulated(xs_ref[...], g_ref[...], shift_s_ref[0], scale_s_ref[0])

    emit = functools.partial(_emit_tile, lay, idx_scale, q_scale, j, w_ref)
    emit(h_ref, cos_ref, sin_ref, zb_ref, k_ref, v_ref, misc_ref, u_ref, vt_ref, wt_ref)

    @pl.when(i == 0)
    def _():
        emit(hs_ref, cos_s_ref, sin_s_ref, zs_ref, ks_ref, vs_ref, miscs_ref, us_ref, None, None)


def _project(xp, xs, g, mod_p, mod_s, w_t, starts, rope_p, rope_s, lay, tm, tiles_per_group, idx_scale):
    m, d = xp.shape
    ns = xs.shape[0]
    grp = mod_p[0].shape[0]
    nu = lay.end - lay.u
    kern = functools.partial(_proj_kernel, lay, idx_scale, Q_SCALE)
    once = dict(pipeline_mode=pl.Buffered(1))
    zb_col = lambda j: jnp.minimum(j, lay.nb - 1)
    u_col = lambda j: jnp.clip(j - lay.u, 0, nu - 1)
    first = lambda i, col, last: jnp.where(i == 0, col, last)
    row_p = lambda shape: pl.BlockSpec(shape, lambda i, j, st: (i,) + (0,) * (len(shape) - 1), **once)
    fixed = lambda shape: pl.BlockSpec(shape, lambda i, j, st: (0,) * len(shape))
    grp_p = pl.BlockSpec((1, 1, d), lambda i, j, st: (i // tiles_per_group, 0, 0))
    pos_p = pl.BlockSpec((tm, HEAD_DIM), lambda i, j, st: (i % tiles_per_group, 0))
    out_specs = [pl.BlockSpec((tm, TN), lambda i, j, st: (i, zb_col(j))),
                 row_p((tm, N_KV_HEADS, HEAD_DIM)),
                 row_p((tm, N_KV_HEADS, HEAD_DIM)),
                 row_p((tm, TN)),
                 pl.BlockSpec((tm, TN), lambda i, j, st: (i, u_col(j))),
                 pl.BlockSpec((1, TN, tm), lambda i, j, st: (i // tiles_per_group, 0, i % tiles_per_group), **once),
                 pl.BlockSpec((IDX_DIM, tm), lambda i, j, st: (0, i), **once),
                 pl.BlockSpec((ns, TN), lambda i, j, st: (0, first(i, zb_col(j), lay.nb - 1))),
                 fixed((ns, TN)), fixed((ns, TN)), fixed((ns, TN)),
                 pl.BlockSpec((ns, TN), lambda i, j, st: (0, first(i, u_col(j), nu - 1)))]
    out_shape = [jax.ShapeDtypeStruct((m, lay.nb * TN), BF16),
                 jax.ShapeDtypeStruct((m, N_KV_HEADS, HEAD_DIM), F32),
                 jax.ShapeDtypeStruct((m, N_KV_HEADS, HEAD_DIM), F32),
                 jax.ShapeDtypeStruct((m, TN), F32),
                 jax.ShapeDtypeStruct((m, nu * TN), F32),
                 jax.ShapeDtypeStruct((grp, TN, tiles_per_group * tm), BF16),
                 jax.ShapeDtypeStruct((IDX_DIM, m), F32),
                 jax.ShapeDtypeStruct((ns, lay.nb * TN), BF16),
                 jax.ShapeDtypeStruct((ns, TN), F32),
                 jax.ShapeDtypeStruct((ns, TN), F32),
                 jax.ShapeDtypeStruct((ns, TN), F32),
                 jax.ShapeDtypeStruct((ns, nu * TN), F32)]
    outs = pl.pallas_call(
        kern,
        grid_spec=pltpu.PrefetchScalarGridSpec(
            num_scalar_prefetch=1,
            grid=(m // tm, lay.end),
            in_specs=[fixed((1, d)),
                      pl.BlockSpec((pl.Element(TN), pl.Element(d)),
                                   lambda i, j, st: (pl.multiple_of(st[j], SUBLANES), 0)),
                      pl.BlockSpec((tm, d), lambda i, j, st: (i, 0)), grp_p, grp_p, pos_p, pos_p,
                      fixed((ns, d)), fixed((1, ns, d)), fixed((1, ns, d)),
                      fixed((ns, HEAD_DIM)), fixed((ns, HEAD_DIM))],
            out_specs=out_specs,
            scratch_shapes=[pltpu.VMEM((tm, d), BF16), pltpu.VMEM((ns, d), BF16)]),
        out_shape=out_shape,
        compiler_params=_cparams(("arbitrary", "arbitrary")),
    )(starts, g, w_t, xp, *mod_p, *rope_p, xs, *mod_s, *rope_s)
    return outs[:7], outs[7:]


def _bisect(count_ge, lo, hi, clo, kf):
    def cond(st):
        return (st[4] > 0.0) & (st[5] < MAX_BISECT)

    def halve(lo, hi, clo, done):
        mid = 0.5 * lo + 0.5 * hi
        mid = jnp.where(mid <= lo, hi, mid)
        c = count_ge(mid)
        ge = c >= kf
        stuck = (mid >= hi) & jnp.logical_not(ge)
        lo2 = jnp.where(ge, mid, lo)
        hi2 = jnp.where(ge, hi, mid)
        clo2 = jnp.where(ge, c, clo)
        fin = (clo2 <= kf) | (lo2 >= hi2) | stuck
        return lo2, hi2, clo2, jnp.maximum(done, jnp.where(fin, 1.0, 0.0))

    def body(st):
        lo, hi, clo, done = halve(*halve(*st[:4]))
        return lo, hi, clo, done, jnp.sum(1.0 - done), st[5] + 1

    done0 = jnp.where((clo <= kf) | (lo >= hi), 1.0, 0.0)
    st = (lo, hi, clo, done0, jnp.sum(1.0 - done0), jnp.int32(0))
    out = lax.while_loop(cond, body, st)
    return out[0], out[2]


def _attn_kernel(topk, q_ref, qi_ref, sga_ref, k_ref, ki_ref, vt_ref, wt_ref, o_ref,
                 score_ref, bias_ref, cut_ref, m_ref, l_ref, alpha_ref, acc_ref, lg_ref):
    i = pl.program_id(1)
    nck = i + 1
    n_heads = q_ref.shape[1] // HEAD_DIM
    group = n_heads // N_KV_HEADS
    nt = (((1,), (1,)), ((), ()))
    q_pos = i * TQ + lax.broadcasted_iota(jnp.int32, (1, TQ), 1)
    fold = lambda a: a.reshape(CK // SUBLANES, SUBLANES, TQ)

    def score_chunk(c, carry):
        mn, mx = carry
        off = pl.multiple_of(c * CK, CK)
        kc = ki_ref[pl.ds(off, CK), :]
        acc = jnp.zeros((CK, TQ), F32)
        for h in range(IDX_HEADS):
            s = lax.dot_general(kc, qi_ref[:, h * IDX_DIM:(h + 1) * IDX_DIM], nt,
                                preferred_element_type=F32)
            acc = acc + wt_ref[h:h + 1, :] * jnp.maximum(s, 0.0)
        k_pos = off + lax.broadcasted_iota(jnp.int32, (CK, 1), 0)
        causal = k_pos <= q_pos
        score_ref[pl.ds(off, CK), :] = jnp.where(causal, acc, -jnp.inf)
        mn = jnp.minimum(mn, fold(jnp.where(causal, acc, jnp.inf)).min(axis=0))
        mx = jnp.maximum(mx, fold(jnp.where(causal, acc, -jnp.inf)).max(axis=0))
        return mn, mx

    mn, mx = lax.fori_loop(0, nck, score_chunk,
                           (jnp.full((SUBLANES, TQ), jnp.inf, F32),
                            jnp.full((SUBLANES, TQ), -jnp.inf, F32)))
    lo = mn.min(axis=0, keepdims=True)
    hi = mx.max(axis=0, keepdims=True)

    ones_rows = jnp.ones((SUBLANES, CK), BF16)

    def count_ge(thr):
        def body(c, cnt):
            s = score_ref[pl.ds(pl.multiple_of(c * CK, CK), CK), :]
            return cnt + fold(jnp.where(s >= thr, 1.0, 0.0)).sum(axis=0)
        cnt = lax.fori_loop(0, nck, body, jnp.zeros((SUBLANES, TQ), F32))
        return cnt.sum(axis=0, keepdims=True)

    n_valid = (q_pos + 1).astype(F32)
    kf = float(topk)
    thr, n_ge = _bisect(count_ge, lo, hi, n_valid, kf)

    s_len = score_ref.shape[0]
    has_ties = jnp.sum(jnp.where(n_ge > kf, 1.0, 0.0)) > 0.0
    cut_ref[...] = jnp.full(cut_ref.shape, float(s_len), F32)

    @pl.when(has_ties)
    def _():
        def count(pred):
            def body(c, cnt):
                off = pl.multiple_of(c * CK, CK)
                k_pos = (off + lax.broadcasted_iota(jnp.int32, (CK, 1), 0)).astype(F32)
                hit = pred(score_ref[pl.ds(off, CK), :], k_pos)
                return cnt + fold(jnp.where(hit, 1.0, 0.0)).sum(axis=0)
            cnt = lax.fori_loop(0, nck, body, jnp.zeros((SUBLANES, TQ), F32))
            return cnt.sum(axis=0, keepdims=True)

        need = kf - count(lambda s, kp: s > thr)

        def step(_, st):
            lo_i, hi_i = st
            mid = jnp.floor(0.5 * (lo_i + hi_i))
            ge = count(lambda s, kp: (s == thr) & (kp <= mid)) >= need
            return jnp.where(ge, lo_i, mid), jnp.where(ge, mid, hi_i)

        st0 = (jnp.full((1, TQ), -1.0, F32), jnp.full((1, TQ), s_len - 1.0, F32))
        cut = lax.fori_loop(0, (s_len - 1).bit_length() + 1, step, st0)[1]
        cut_ref[...] = jnp.broadcast_to(cut, cut_ref.shape)

    m_ref[...] = jnp.full(m_ref.shape, NEG, F32)
    l_ref[...] = jnp.zeros(l_ref.shape, F32)
    acc_ref[...] = jnp.zeros(acc_ref.shape, F32)

    def att_chunk(c, carry):
        off = pl.multiple_of(c * CK, CK)
        sc = score_ref[pl.ds(off, CK), :]

        @pl.when(jnp.logical_not(has_ties))
        def _():
            bias_ref[...] = jnp.where(sc >= thr, 0.0, NEG)

        @pl.when(has_ties)
        def _():
            k_pos = (off + lax.broadcasted_iota(jnp.int32, (CK, 1), 0)).astype(F32)
            keep = (sc > thr) | ((sc == thr) & (k_pos <= cut_ref[0:1, :]))
            bias_ref[...] = jnp.where(keep, 0.0, NEG)

        kc = k_ref[pl.ds(off, CK), :]
        vtc = vt_ref[0, :, pl.ds(off, CK)]

        def qk(h):
            g = h // group
            return lax.dot_general(kc[:, g * HEAD_DIM:(g + 1) * HEAD_DIM],
                                   q_ref[:, h * HEAD_DIM:(h + 1) * HEAD_DIM], nt,
                                   preferred_element_type=F32)

        for h in range(n_heads):
            lg = qk(h) + bias_ref[...]
            m_old = m_ref[h:h + 1, :]
            m_new = jnp.maximum(m_old, lg.max(axis=0, keepdims=True))
            lg_ref[h] = lg
            alpha_ref[h:h + 1, :] = jnp.exp2(m_old - m_new)
            m_ref[h:h + 1, :] = m_new
        for h in range(n_heads):
            g = h // group
            p = jnp.exp2(lg_ref[h] - m_ref[h:h + 1, :]).astype(BF16)
            pv = jnp.dot(vtc[g * HEAD_DIM:(g + 1) * HEAD_DIM, :], p, preferred_element_type=F32)
            acc_ref[h] = alpha_ref[h:h + 1, :] * acc_ref[h] + pv
            psum = jnp.dot(ones_rows, p, preferred_element_type=F32)[0:1, :]
            l_ref[h:h + 1, :] = alpha_ref[h:h + 1, :] * l_ref[h:h + 1, :] + psum
        return carry

    lax.fori_loop(0, nck, att_chunk, 0)

    for h in range(n_heads):
        o = (acc_ref[h] / l_ref[h:h + 1, :]).T
        sl = slice(h * HEAD_DIM, (h + 1) * HEAD_DIM)
        o_ref[:, sl] = (o * sga_ref[:, sl].astype(F32)).astype(BF16)


def _prompt_attention(zb, vt, wt, lay, b, s, d, topk):
    nq = s // TQ
    n_heads = d // HEAD_DIM
    kern = functools.partial(_attn_kernel, topk)
    row = lambda bb, i: bb * nq + i
    return pl.pallas_call(
        kern,
        grid=(b, nq),
        in_specs=[pl.BlockSpec((TQ, d), lambda bb, i: (row(bb, i), lay.q * TN // d)),
                  pl.BlockSpec((TQ, d), lambda bb, i: (row(bb, i), lay.qi * TN // d)),
                  pl.BlockSpec((TQ, d), lambda bb, i: (row(bb, i), lay.ga * TN // d)),
                  pl.BlockSpec((s, TN), lambda bb, i: (bb, lay.k)),
                  pl.BlockSpec((s, IDX_DIM), lambda bb, i: (bb, lay.misc * TN // IDX_DIM)),
                  pl.BlockSpec((1, TN, s), lambda bb, i: (bb, 0, 0)),
                  pl.BlockSpec((IDX_HEADS, TQ), lambda bb, i: (0, row(bb, i)))],
        out_specs=pl.BlockSpec((TQ, d), lambda bb, i: (row(bb, i), 0)),
        out_shape=jax.ShapeDtypeStruct((b * s, d), BF16),
        scratch_shapes=[pltpu.VMEM((s, TQ), F32),
                        pltpu.VMEM((CK, TQ), F32),
                        pltpu.VMEM((SUBLANES, TQ), F32),
                        pltpu.VMEM((n_heads, TQ), F32),
                        pltpu.VMEM((n_heads, TQ), F32),
                        pltpu.VMEM((n_heads, TQ), F32),
                        pltpu.VMEM((n_heads, HEAD_DIM, TQ), F32),
                        pltpu.VMEM((n_heads, CK, TQ), F32)],
        compiler_params=_cparams(("arbitrary", "arbitrary")),
    )(zb, zb, zb, zb, zb, vt, wt)


def _pool_kernel(ts, u_ref, sgb_ref, wp_ref, sp_ref, o_ref, ext_ref):
    i = pl.program_id(1)
    halo = POOL_STATE + 1

    @pl.when(i == 0)
    def _():
        ext_ref[0:halo, :] = jnp.zeros((halo, ext_ref.shape[1]), F32)

    @pl.when(i > 0)
    def _():
        ext_ref[0:halo, :] = ext_ref[ts:ts + halo, :]

    ext_ref[halo:halo + ts, :] = u_ref[...]
    pos = i * ts + lax.broadcasted_iota(jnp.int32, (ts, 1), 0)
    gd = wp_ref.shape[1]
    for g, w in enumerate(POOL_WINDOWS):
        cols = slice(g * gd, (g + 1) * gd)
        acc = ext_ref[halo:halo + ts, cols]
        for jj in range(1, w):
            acc = acc + ext_ref[halo - jj:halo - jj + ts, cols]
        cnt = jnp.minimum(pos + 1, w).astype(F32)
        pooled = acc / cnt - u_ref[:, cols]
        mixed = jnp.dot(pooled.astype(BF16), wp_ref[g], preferred_element_type=F32) * sp_ref[:, cols]
        o_ref[:, cols] = (mixed * sgb_ref[:, cols].astype(F32)).astype(BF16)


def _pool_prompt(u, zb, wp, sp, lay, b, s, ts):
    m, pw = u.shape
    nt = s // ts
    kern = functools.partial(_pool_kernel, ts)
    return pl.pallas_call(
        kern,
        grid=(b, nt),
        in_specs=[pl.BlockSpec((ts, pw), lambda bb, i: (bb * nt + i, 0)),
                  pl.BlockSpec((ts, pw), lambda bb, i: (bb * nt + i, lay.gb * TN // pw)),
                  pl.BlockSpec(wp.shape, lambda bb, i: (0, 0, 0)),
                  pl.BlockSpec((1, pw), lambda bb, i: (0, 0))],
        out_specs=pl.BlockSpec((ts, pw), lambda bb, i: (bb * nt + i, 0)),
        out_shape=jax.ShapeDtypeStruct((m, pw), BF16),
        scratch_shapes=[pltpu.VMEM((POOL_STATE + 1 + ts, pw), F32)],
        compiler_params=_cparams(("arbitrary", "arbitrary")),
    )(u, zb, wp, sp)


def _pool_step_kernel(cnts, ext_ref, sgb_ref, wp_ref, sp_ref, o_ref):
    gd = wp_ref.shape[1]
    rows = ext_ref.shape[0]
    for g, w in enumerate(POOL_WINDOWS):
        cols = slice(g * gd, (g + 1) * gd)
        acc = ext_ref[rows - 1, :, cols]
        for jj in range(1, w):
            acc = acc + ext_ref[rows - 1 - jj, :, cols]
        pooled = acc / cnts[g] - ext_ref[rows - 1, :, cols]
        mixed = jnp.dot(pooled.astype(BF16), wp_ref[g], preferred_element_type=F32) * sp_ref[:, cols]
        o_ref[:, cols] = (mixed * sgb_ref[:, cols].astype(F32)).astype(BF16)


def _pool_step(ext_t, sgb, wp, sp, past):
    rows, n, pw = ext_t.shape
    cnts = tuple(float(min(past + 1, w)) for w in POOL_WINDOWS)
    return pl.pallas_call(
        functools.partial(_pool_step_kernel, cnts),
        out_shape=jax.ShapeDtypeStruct((n, pw), BF16),
    )(ext_t, sgb, wp, sp)


def _merge1_kernel(ab_ref, pb_ref, wa_ref, wb_ref, sma_ref, smb_ref, o_ref):
    a = jnp.dot(ab_ref[...], wa_ref[...], preferred_element_type=F32)
    bb = jnp.dot(pb_ref[...], wb_ref[...], preferred_element_type=F32)
    o_ref[...] = (sma_ref[...].astype(F32) * a + smb_ref[...].astype(F32) * bb).astype(BF16)


def _merge1(ab, pb, wa, wb, zb, lay, tm):
    m, d = ab.shape
    pw = pb.shape[1]
    return pl.pallas_call(
        _merge1_kernel,
        grid=(m // tm, d // TN),
        in_specs=[pl.BlockSpec((tm, d), lambda i, j: (i, 0)),
                  pl.BlockSpec((tm, pw), lambda i, j: (i, 0)),
                  pl.BlockSpec((d, TN), lambda i, j: (0, j)),
                  pl.BlockSpec((pw, TN), lambda i, j: (0, j)),
                  pl.BlockSpec((tm, TN), lambda i, j: (i, lay.ma + j)),
                  pl.BlockSpec((tm, TN), lambda i, j: (i, lay.mb + j))],
        out_specs=pl.BlockSpec((tm, TN), lambda i, j: (i, j)),
        out_shape=jax.ShapeDtypeStruct((m, d), BF16),
        compiler_params=_cparams(("arbitrary", "arbitrary")),
    )(ab, pb, wa, wb, zb, zb)


def _merge2_kernel(final, mm_ref, x_ref, gate_ref, wo_ref, gf_ref, o_ref):
    y = x_ref[...] + gate_ref[0] * jnp.dot(mm_ref[...], wo_ref[...], preferred_element_type=F32)
    if final:
        y = y * lax.rsqrt(jnp.mean(y * y, axis=-1, keepdims=True) + EPS) * gf_ref[...]
    o_ref[...] = y


def _merge2(mm, x, gate, wo, gf, tm, tiles_per_group, final):
    m, d = x.shape
    r = gate.shape[1]
    return pl.pallas_call(
        functools.partial(_merge2_kernel, final),
        grid=(m // tm,),
        in_specs=[pl.BlockSpec((tm, d), lambda i: (i, 0)),
                  pl.BlockSpec((tm, d), lambda i: (i, 0)),
                  pl.BlockSpec((1, r, d), lambda i: (i // tiles_per_group, 0, 0)),
                  pl.BlockSpec((d, d), lambda i: (0, 0)),
                  pl.BlockSpec((1, d), lambda i: (0, 0))],
        out_specs=pl.BlockSpec((tm, d), lambda i: (i, 0)),
        out_shape=jax.ShapeDtypeStruct((m, d), F32),
        compiler_params=_cparams(("arbitrary",)),
    )(mm, x, gate, wo, gf)


def _page_specs(width, npg, npp):
    return [pl.BlockSpec((1, PAGE_SIZE, width),
                         lambda b, p, pt, r=r: (pt[b * npg + p * npp + r], 0, 0)) for r in range(npp)]


def _sidx_kernel(npp, pt_ref, *refs):
    kidx_refs = refs[:npp]
    qi_ref, w_ref, o_ref = refs[npp:]
    nt = (((1,), (1,)), ((), ()))
    qi, w = qi_ref[0], w_ref[0]
    rows = []
    for r in range(npp):
        s = lax.dot_general(qi, kidx_refs[r][0].astype(BF16), nt, preferred_element_type=F32)
        rows.append(jnp.sum(w * jnp.maximum(s, 0.0), axis=0, keepdims=True))
    o_ref[0] = jnp.concatenate(rows, axis=1)


def _sample_scores(page_table, cache_kidx, qi3, wcol, npp):
    n, npg = page_table.shape
    return pl.pallas_call(
        functools.partial(_sidx_kernel, npp),
        grid_spec=pltpu.PrefetchScalarGridSpec(
            num_scalar_prefetch=1,
            grid=(n, npg // npp),
            in_specs=_page_specs(IDX_DIM, npg, npp) + [
                pl.BlockSpec((1, IDX_HEADS, IDX_DIM), lambda b, p, pt: (b, 0, 0)),
                pl.BlockSpec((1, IDX_HEADS, 1), lambda b, p, pt: (b, 0, 0))],
            out_specs=pl.BlockSpec((1, 1, npp * PAGE_SIZE), lambda b, p, pt: (b, 0, p))),
        out_shape=jax.ShapeDtypeStruct((n, 1, npg * PAGE_SIZE), F32),
        compiler_params=_cparams(("arbitrary", "arbitrary")),
    )(page_table.reshape(-1), *([cache_kidx] * npp), qi3, wcol)


def _sthr_kernel(topk, row_base, sc_ref, qi_ref, kin_ref, w_ref, pt_ref, pos_ref, cnt_ref, sel_ref,
                 thr_scr, new_scr):
    sc = sc_ref[...]
    kin = kin_ref[...].astype(BF16).astype(F32)
    s_new = jnp.sum(qi_ref[...].astype(F32) * kin, axis=2, keepdims=True)
    new = jnp.sum(w_ref[...] * jnp.maximum(s_new, 0.0), axis=1, keepdims=True)
    red = lambda f, a: f(f(a, axis=2, keepdims=True), axis=1, keepdims=True)
    lo = jnp.minimum(red(jnp.min, sc), new)
    hi = jnp.maximum(red(jnp.max, sc), new)

    def count_ge(thr):
        c = red(jnp.sum, jnp.where(sc >= thr, 1.0, 0.0))
        return c + jnp.where(new >= thr, 1.0, 0.0)

    n_all = jnp.full(new.shape, float(sc.shape[1] * sc.shape[2] + 1), F32)
    thr, _ = _bisect(count_ge, lo, hi, n_all, float(topk))
    thr_scr[...] = jnp.broadcast_to(thr, thr_scr.shape)
    new_scr[...] = jnp.broadcast_to(new, new_scr.shape)

    rows = sc.shape[1]
    ii = lambda shape, ax: lax.broadcasted_iota(jnp.int32, shape, ax)
    one = lambda m: jnp.where(m, 1.0, 0.0)
    tri_lane = one(ii((PAGE_SIZE, PAGE_SIZE), 0) <= ii((PAGE_SIZE, PAGE_SIZE), 1)).astype(BF16)
    lower = one(ii((rows, rows), 1) < ii((rows, rows), 0)).astype(BF16)
    upper = one(ii((rows, rows), 0) < ii((rows, rows), 1)).astype(BF16)
    ones8 = jnp.ones((SUBLANES, PAGE_SIZE), BF16)
    nt = (((1,), (1,)), ((), ()))
    total = lambda a: jnp.sum(jnp.sum(a, axis=1, keepdims=True), axis=0, keepdims=True)
    dotf = lambda a, b: jnp.dot(a.astype(BF16), b.astype(BF16), preferred_element_type=F32)
    kf = float(topk)
    j_col = ii((topk, 1), 0).astype(F32)
    c_row = ii((1, PAGE_SIZE), 1).astype(F32)

    def compact(b, carry):
        s = sc_ref[b]
        t = thr_scr[b][:, 0:1]
        nw = new_scr[b][:, 0:1]
        gt, eq = s > t, s == t
        need = kf - total(one(gt)) - one(nw > t)
        e_lane = dotf(one(eq), tri_lane)
        e_rank = e_lane + dotf(lower, e_lane)[:, PAGE_SIZE - 1:PAGE_SIZE]
        x = one(gt | (eq & (e_rank <= need)))
        sel_new = (nw > t) | ((nw == t) & (total(one(eq)) < need))
        x_lane = dotf(x, tri_lane)
        n_rows = lax.dot_general(ones8, x.astype(BF16), nt, preferred_element_type=F32)
        n_row = n_rows[0:1, :]
        start_row = dotf(n_rows, upper)[0:1, :]
        hit = one((start_row <= j_col) & (j_col < start_row + n_row))
        lane_sum = lambda a: jnp.sum(a, axis=1, keepdims=True)
        page = lane_sum(hit * pt_ref[b])
        target = j_col - lane_sum(hit * start_row) + 1.0
        in_row = dotf(hit, x_lane * x)
        off = lane_sum(jnp.where(in_row == target, c_row, 0.0))
        pos_ref[b] = (page * PAGE_SIZE + off + row_base).astype(jnp.int32)
        cnt_ref[b] = jnp.broadcast_to(total(x), cnt_ref.shape[1:])
        sel_ref[b] = jnp.broadcast_to(one(sel_new), sel_ref.shape[1:])
        return carry

    lax.fori_loop(0, sc.shape[0], compact, 0)


def _sample_select(sc, qi3, kin, wcol, pt_rows, topk, row_base):
    n = sc.shape[0]
    return pl.pallas_call(
        functools.partial(_sthr_kernel, topk, row_base),
        out_shape=[jax.ShapeDtypeStruct((n, topk, 1), jnp.int32),
                   jax.ShapeDtypeStruct((n, 1, PAGE_SIZE), F32),
                   jax.ShapeDtypeStruct((n, 1, PAGE_SIZE), F32)],
        scratch_shapes=[pltpu.VMEM((n, 1, PAGE_SIZE), F32),
                        pltpu.VMEM((n, 1, PAGE_SIZE), F32)],
        compiler_params=pltpu.CompilerParams(vmem_limit_bytes=VMEM_LIMIT),
    )(sc, qi3, kin, wcol, pt_rows)


def _row_copy(cache_rows, buf, sem, row, j):
    return pltpu.make_async_copy(cache_rows.at[row], buf.at[j], sem)


def _sattn_kernel(pos_ref, ck_hbm, cv_hbm, cnt_ref, sel_ref, q_ref, kn_ref, vn_ref,
                  sga_ref, o_ref, kbuf, vbuf, ksem, vsem):
    b = pl.program_id(0)
    slot = b % 2
    topk = kbuf.shape[1]
    n_heads = q_ref.shape[1]
    group = n_heads // N_KV_HEADS
    nt = (((1,), (1,)), ((), ()))

    def fetch(bb, s):
        def issue(j, carry):
            row = pos_ref[bb * topk + j]
            _row_copy(ck_hbm, kbuf.at[s], ksem.at[s], row, j).start()
            _row_copy(cv_hbm, vbuf.at[s], vsem.at[s], row, j).start()
            return carry
        lax.fori_loop(0, topk, issue, 0, unroll=8)

    @pl.when(b == 0)
    def _():
        fetch(b, slot)

    @pl.when(b + 1 < pl.num_programs(0))
    def _():
        fetch(b + 1, 1 - slot)

    kbuf, vbuf = kbuf.at[slot], vbuf.at[slot]
    if topk % PAGE_SIZE == 0:
        for w in range(topk // PAGE_SIZE):
            rows = pl.ds(w * PAGE_SIZE, PAGE_SIZE)
            pltpu.make_async_copy(ck_hbm.at[rows], kbuf.at[rows], ksem.at[slot]).wait()
            pltpu.make_async_copy(cv_hbm.at[rows], vbuf.at[rows], vsem.at[slot]).wait()
    else:
        def wait(j, carry):
            _row_copy(ck_hbm, kbuf, ksem.at[slot], 0, j).wait()
            _row_copy(cv_hbm, vbuf, vsem.at[slot], 0, j).wait()
            return carry

        lax.fori_loop(0, topk, wait, 0)

    q = q_ref[0]
    head = lax.broadcasted_iota(jnp.int32, (n_heads, 1), 0)
    in_group = [(head >= g * group) & (head < (g + 1) * group) for g in range(N_KV_HEADS)]
    lg = jnp.zeros((n_heads, topk), F32)
    for g in range(N_KV_HEADS):
        lg_g = lax.dot_general(q, kbuf[:, g, :].astype(BF16), nt, preferred_element_type=F32)
        lg = lg + jnp.where(in_group[g], lg_g, 0.0)
    valid = lax.broadcasted_iota(jnp.int32, (1, topk), 1).astype(F32) < cnt_ref[0][:, 0:1]
    lg = jnp.where(valid, lg, NEG)
    kn = kn_ref[0].astype(BF16).astype(F32)
    vn = vn_ref[0].astype(BF16).astype(F32)
    lgn = jnp.sum(q.astype(F32) * kn, axis=1, keepdims=True)
    lgn = jnp.where(sel_ref[0][:, 0:1] > 0.0, lgn, NEG)
    m = jnp.maximum(lg.max(axis=1, keepdims=True), lgn)
    p = jnp.exp2(lg - m)
    pn = jnp.exp2(lgn - m)
    l = p.sum(axis=1, keepdims=True) + pn
    pb = p.astype(BF16)
    acc = pn.astype(BF16).astype(F32) * vn
    for g in range(N_KV_HEADS):
        pv = jnp.dot(pb, vbuf[:, g, :].astype(BF16), preferred_element_type=F32)
        acc = acc + jnp.where(in_group[g], pv, 0.0)
    o_ref[0] = ((acc / l) * sga_ref[0].astype(F32)).astype(BF16)


def _sample_attention(k_rows, v_rows, pos, cnt, sel, q3, kn, vn, sga3):
    n, n_heads = q3.shape[:2]
    topk = pos.shape[1]
    per_b = lambda shape: pl.BlockSpec(shape, lambda b, ps: (b, 0, 0))
    hbm = pl.BlockSpec(memory_space=pl.ANY)
    return pl.pallas_call(
        _sattn_kernel,
        grid_spec=pltpu.PrefetchScalarGridSpec(
            num_scalar_prefetch=1,
            grid=(n,),
            in_specs=[hbm, hbm,
                      per_b((1, 1, PAGE_SIZE)),
                      per_b((1, 1, PAGE_SIZE)),
                      per_b((1, n_heads, HEAD_DIM)),
                      per_b((1, n_heads, HEAD_DIM)),
                      per_b((1, n_heads, HEAD_DIM)),
                      per_b((1, n_heads, HEAD_DIM))],
            out_specs=per_b((1, n_heads, HEAD_DIM)),
            scratch_shapes=[pltpu.VMEM((2, topk, N_KV_HEADS, HEAD_DIM), k_rows.dtype),
                            pltpu.VMEM((2, topk, N_KV_HEADS, HEAD_DIM), v_rows.dtype),
                            pltpu.SemaphoreType.DMA((2,)),
                            pltpu.SemaphoreType.DMA((2,))]),
        out_shape=jax.ShapeDtypeStruct((n, n_heads, HEAD_DIM), BF16),
        compiler_params=_cparams(("arbitrary",)),
    )(pos.reshape(-1), k_rows, v_rows, cnt, sel, q3, kn, vn, sga3)


def _rope_tables(pos):
    half = HEAD_DIM // 2
    inv = ROPE_THETA ** (-jnp.arange(half, dtype=F32) / half)
    ang = pos.astype(F32)[:, None] * inv[None, :]
    cos, sin = jnp.cos(ang), jnp.sin(ang)
    return jnp.concatenate([cos, cos], axis=1), jnp.concatenate([-sin, sin], axis=1)


def _tile_starts(d, lay):
    att, kv, idx, pool = d, N_KV_HEADS * HEAD_DIM, IDX_HEADS * IDX_DIM, d // 2
    sizes = (att, kv, kv, idx, IDX_DIM, IDX_HEADS, att, pool, pool, d, d)
    offs = [0]
    for n in sizes:
        offs.append(offs[-1] + n)
    q, k, v, qi, ki, wi, ga, u, gb, ma, mb = offs[:-1]
    starts = [0] * lay.end
    for first, tiles, col in ((lay.q, lay.qi - lay.q, q), (lay.qi, lay.ga - lay.qi, qi),
                              (lay.ga, lay.ma - lay.ga, ga), (lay.ma, lay.mb - lay.ma, ma),
                              (lay.mb, lay.gb - lay.mb, mb), (lay.gb, lay.k - lay.gb, gb),
                              (lay.k, 1, k), (lay.v, 1, v), (lay.misc, 1, ki), (lay.u, lay.end - lay.u, u)):
        for t in range(tiles):
            starts[first + t] = col + t * TN
    assert wi == ki + IDX_DIM and all(s + TN <= offs[-1] and s % SUBLANES == 0 for s in starts)
    return starts


def kernel(x_prompt, x_sample, c_prompt, c_sample, cache_k, cache_v, cache_kidx, state_pool, page_table,
           g_norm, w_ada, b_ada, w_in, w_pool, s_pool, w_a_proj, w_b_proj, w_out, g_final):
    bp, sp, d = x_prompt.shape
    ns, ds, _ = x_sample.shape
    depth = g_norm.shape[0]
    npg = page_table.shape[1]
    past = npg * PAGE_SIZE
    assert ds == 1 and d % (N_KV_HEADS * HEAD_DIM) == 0 and sp % TQ == 0 and TQ == CK and sp >= POOL_STATE
    n_heads = d // HEAD_DIM
    kvw = N_KV_HEADS * HEAD_DIM
    lay = _Layout(d)
    pw = lay.pool
    idx_scale = (IDX_HEADS * IDX_DIM) ** -0.5
    topk_p = min(TOPK_MAX, sp // 4)
    topk_s = min(TOPK_MAX, (past + ds) // 4)
    tm = math.gcd(sp, ROW_TILE)
    ts = math.gcd(sp, POOL_TILE)
    tm2 = math.gcd(sp, OUT_TILE)

    cos_p, sin_p = _rope_tables(jnp.arange(sp, dtype=jnp.int32))
    cos_s, sin_s = _rope_tables(jnp.full((ns,), past, jnp.int32))
    n_c = bp + ns
    c_rows = -(-n_c // SUBLANES) * SUBLANES
    c_all = jnp.concatenate([c_prompt, c_sample, jnp.zeros((c_rows - n_c, d), F32)], axis=0)

    n_pool = cache_kidx.shape[1]
    kidx_pages = cache_kidx.reshape(depth * n_pool, PAGE_SIZE, IDX_DIM)
    npp = math.gcd(npg, PAGES_PER_STEP)
    k_rows = cache_k.reshape(depth * n_pool * PAGE_SIZE, N_KV_HEADS, HEAD_DIM)
    v_rows = cache_v.reshape(depth * n_pool * PAGE_SIZE, N_KV_HEADS, HEAD_DIM)
    pt_rows = page_table.astype(F32).reshape(ns, 1, npg)
    assert npg % npp == 0

    in_width = w_in.shape[2]
    w_t = jnp.swapaxes(w_in, 1, 2).reshape(depth * in_width, d)

    xp = x_prompt.reshape(bp * sp, d)
    xs = x_sample.reshape(ns, d)
    outs = {n: [] for n in ("kp", "vp", "kip", "pp", "ks", "vs", "kis", "ps")}
    for l in range(depth):
        final = l == depth - 1
        starts = jnp.asarray([l * in_width + s for s in _tile_starts(d, lay)], jnp.int32)
        wa, wb, wo, wpl = (w_a_proj[l].astype(BF16), w_b_proj[l].astype(BF16), w_out[l].astype(BF16),
                           w_pool[l].astype(BF16))
        g = g_norm[l].reshape(1, d)
        spl = s_pool[l].reshape(1, pw)
        gf = g_final.reshape(1, d)

        mod = _ada(c_all, w_ada[l], b_ada[l])
        shift, scale, gate = mod[:, :d], mod[:, d:2 * d], mod[:, 2 * d:]
        pg = lambda a: a[:bp].reshape(bp, 1, d)
        sg = lambda a: a[bp:n_c].reshape(1, ns, d)

        (zb, k_p, v_p, misc_p, u_p, vt, wt), (zs, k_s, v_s, misc_s, u_s) = _project(
            xp, xs, g, (pg(shift), pg(scale)), (sg(shift), sg(scale)), w_t, starts,
            (cos_p, sin_p), (cos_s, sin_s), lay, tm, sp // tm, idx_scale)

        ab = _prompt_attention(zb, vt, wt, lay, bp, sp, d, topk_p)
        pb = _pool_prompt(u_p, zb, wpl, spl, lay, bp, sp, ts)
        mm = _merge1(ab, pb, wa, wb, zb, lay, tm)
        xp = _merge2(mm, xp, pg(gate), wo, gf, tm2, sp // tm2, final)
        outs["kp"].append(k_p.reshape(bp, sp, N_KV_HEADS, HEAD_DIM))
        outs["vp"].append(v_p.reshape(bp, sp, N_KV_HEADS, HEAD_DIM))
        outs["kip"].append(misc_p[:, :IDX_DIM].reshape(bp, sp, IDX_DIM))
        outs["pp"].append(u_p.reshape(bp, sp, pw)[:, sp - POOL_STATE:, :])

        col = lambda t, n: zs[:, t * TN:t * TN + n]
        q3 = col(lay.q, d).reshape(ns, n_heads, HEAD_DIM)
        qi3 = col(lay.qi, IDX_HEADS * IDX_DIM).reshape(ns, IDX_HEADS, IDX_DIM)
        sga3 = col(lay.ga, d).reshape(ns, n_heads, HEAD_DIM)
        wcol = misc_s[:, IDX_DIM:IDX_DIM + IDX_HEADS].reshape(ns, IDX_HEADS, 1)
        kin = misc_s[:, :IDX_DIM].reshape(ns, 1, IDX_DIM)
        pt_l = page_table + l * n_pool
        sc = _sample_scores(pt_l, kidx_pages, qi3, wcol, npp)
        pos, cnt, sel = _sample_select(sc.reshape(ns, npg, PAGE_SIZE), qi3, kin, wcol, pt_rows, topk_s,
                                       l * n_pool * PAGE_SIZE)
        rep = lambda a: jnp.repeat(a.reshape(ns, N_KV_HEADS, HEAD_DIM), n_heads // N_KV_HEADS, axis=1)
        ab_s = _sample_attention(k_rows, v_rows, pos, cnt, sel, q3, rep(k_s), rep(v_s), sga3).reshape(ns, d)
        ext = jnp.concatenate([state_pool[l], u_s[:, None, :]], axis=1)
        pb_s = _pool_step(ext.transpose(1, 0, 2), col(lay.gb, pw), wpl, spl, past)
        mm_s = _merge1(ab_s, pb_s, wa, wb, zs, lay, ns)
        xs = _merge2(mm_s, xs, sg(gate), wo, gf, ns, 1, final)
        outs["ks"].append(k_s.reshape(ns, ds, N_KV_HEADS, HEAD_DIM))
        outs["vs"].append(v_s.reshape(ns, ds, N_KV_HEADS, HEAD_DIM))
        outs["kis"].append(misc_s[:, :IDX_DIM].reshape(ns, ds, IDX_DIM))
        outs["ps"].append(ext[:, 1:, :])

    st = lambda n: outs[n][0][None] if depth == 1 else jnp.stack(outs[n])
    return (xp.reshape(bp, sp, d), xs.reshape(ns, ds, d),
            st("kp"), st("vp"), st("kip"), st("pp"), st("ks"), st("vs"), st("kis"), st("ps"))
```

```python
import functools
import math

import jax
import jax.numpy as jnp
from jax import lax
from jax.experimental import pallas as pl
from jax.experimental.pallas import tpu as pltpu

F32 = jnp.float32
BF16 = jnp.bfloat16

HEAD_DIM = 128
N_KV_HEADS = 4
IDX_HEADS = 16
IDX_DIM = 128
TOPK_MAX = 256
PAGE_SIZE = 128
POOL_WINDOWS = (2, 4, 8, 16)
POOL_STATE = max(POOL_WINDOWS) - 1
ROPE_THETA = 10000.0
EPS = 1e-6

V7X_VMEM_BYTES = 64 * 1024 * 1024
V7X_MXU_COLUMNS = 256
SUBLANES = 8
TN = 512
ROW_TILE = 1024
POOL_TILE = 512
assert all(w == 2 << g for g, w in enumerate(POOL_WINDOWS)) and max(POOL_WINDOWS) // 2 <= SUBLANES
POOL_HALO = SUBLANES * len(POOL_WINDOWS)
OUT_TILE = 512
TQ = V7X_MXU_COLUMNS
CK = 256
NEG = -1e30
Q_SCALE = 1.4426950408889634 * HEAD_DIM ** -0.5
PAGES_PER_STEP = 64
MAX_BISECT = 512
VMEM_LIMIT = V7X_VMEM_BYTES * 7 // 8


def _cparams(sem):
    return pltpu.CompilerParams(dimension_semantics=sem, vmem_limit_bytes=VMEM_LIMIT)


def _sigmoid(z):
    return 0.5 * jnp.tanh(0.5 * z) + 0.5


def _ada_kernel(c_ref, w_ref, b_ref, o_ref):
    o_ref[...] = jnp.dot(c_ref[...].astype(BF16), w_ref[...].astype(BF16),
                         preferred_element_type=F32) + b_ref[...]


def _ada(c, w, b):
    r, d = c.shape
    n = w.shape[1]
    return pl.pallas_call(
        _ada_kernel,
        grid=(n // TN,),
        in_specs=[pl.BlockSpec((r, d), lambda j: (0, 0)),
                  pl.BlockSpec((d, TN), lambda j: (0, j)),
                  pl.BlockSpec((1, TN), lambda j: (0, j))],
        out_specs=pl.BlockSpec((r, TN), lambda j: (0, j)),
        out_shape=jax.ShapeDtypeStruct((r, n), F32),
        compiler_params=_cparams(("arbitrary",)),
    )(c, w, b.reshape(1, n))


class _Layout:
    def __init__(self, d):
        att, kv, idx, pool = d, N_KV_HEADS * HEAD_DIM, IDX_HEADS * IDX_DIM, d // 2
        t = lambda n: n // TN
        assert att % TN == 0 and kv == TN and idx % TN == 0 and pool % TN == 0
        self.q = 0
        self.qi = self.q + t(att)
        self.ga = self.qi + t(idx)
        self.ma = self.ga + t(att)
        self.mb = self.ma + t(d)
        self.gb = self.mb + t(d)
        self.k = self.gb + t(pool)
        self.v = self.k + 1
        self.misc = self.v + 1
        self.u = self.misc + 1
        self.end = self.u + t(pool)
        self.nb = self.u
        self.pool = pool


def _rope_tile(z, cos, sin):
    parts = []
    for s in range(z.shape[1] // HEAD_DIM):
        zs = z[:, s * HEAD_DIM:(s + 1) * HEAD_DIM]
        parts.append(zs * cos + pltpu.roll(zs, HEAD_DIM // 2, 1) * sin)
    return parts[0] if len(parts) == 1 else jnp.concatenate(parts, axis=1)


def _modulated(x, g, shift, scale):
    r = lax.rsqrt(jnp.mean(x * x, axis=-1, keepdims=True) + EPS)
    return ((x * r * g) * (1.0 + scale) + shift).astype(BF16)


def _emit_tile(lay, idx_scale, q_scale, j, w_ref, h_ref, cos_ref, sin_ref, zb_ref, k_ref, v_ref, misc_ref, u_ref,
               vt_ref, wt_ref):
    nt = (((1,), (1,)), ((), ()))
    with_vt = vt_ref is not None

    def tile(keep_rows=None):
        w = w_ref[...]
        if keep_rows is not None:
            w = jnp.where(lax.broadcasted_iota(jnp.int32, (TN, 1), 0) < keep_rows, w, 0.0)
        return lax.dot_general(h_ref[...], w.astype(BF16), nt, preferred_element_type=F32)

    def store_heads(ref, val):
        if len(ref.shape) == 2:
            ref[...] = val
        else:
            for g in range(N_KV_HEADS):
                ref[:, g, :] = val[:, g * HEAD_DIM:(g + 1) * HEAD_DIM]

    @pl.when(j < lay.qi)
    def _():
        zb_ref[...] = (_rope_tile(tile(), cos_ref[...], sin_ref[...]) * q_scale).astype(BF16)

    @pl.when((j >= lay.qi) & (j < lay.ga))
    def _():
        zb_ref[...] = _rope_tile(tile(), cos_ref[...], sin_ref[...]).astype(BF16)

    @pl.when(j == lay.k)
    def _():
        r = _rope_tile(tile(), cos_ref[...], sin_ref[...])
        zb_ref[...] = r.astype(BF16)
        store_heads(k_ref, r)

    @pl.when(((j >= lay.ga) & (j < lay.ma)) | ((j >= lay.gb) & (j < lay.k)))
    def _():
        z = tile()
        zb_ref[...] = (z * _sigmoid(z)).astype(BF16)

    @pl.when((j >= lay.ma) & (j < lay.gb))
    def _():
        zb_ref[...] = _sigmoid(tile()).astype(BF16)

    @pl.when(j == lay.v)
    def _():
        z = tile()
        zb_ref[...] = z.astype(BF16)
        store_heads(v_ref, z)
        if with_vt:
            vt_ref[0] = z.T.astype(BF16)

    @pl.when(j == lay.misc)
    def _():
        z = tile(keep_rows=IDX_DIM + IDX_HEADS)
        ki = _rope_tile(z[:, :IDX_DIM], cos_ref[...], sin_ref[...])
        m = jnp.concatenate([ki, z[:, IDX_DIM:] * idx_scale], axis=1)
        zb_ref[...] = m.astype(BF16)
        misc_ref[...] = m
        if with_vt:
            wt_ref[...] = m[:, IDX_DIM:2 * IDX_DIM].T

    @pl.when(j >= lay.u)
    def _():
        u_ref[...] = tile()


def _proj_kernel(lay, idx_scale, q_scale, starts_ref, g_ref, w_ref,
                 x_ref, shift_ref, scale_ref, cos_ref, sin_ref,
                 xs_ref, shift_s_ref, scale_s_ref, cos_s_ref, sin_s_ref,
                 zb_ref, k_ref, v_ref, misc_ref, u_ref, vt_ref, wt_ref,
                 zs_ref, ks_ref, vs_ref, miscs_ref, us_ref, h_ref, hs_ref):
    i, j = pl.program_id(0), pl.program_id(1)

    @pl.when(j == 0)
    def _():
        h_ref[...] = _modulated(x_ref[...], g_ref[...], shift_ref[0], scale_ref[0])

    @pl.when((i == 0) & (j == 0))
    def _():
        hs_ref[...] = _modulated(xs_ref[...], g_ref[...], shift_s_ref[0], scale_s_ref[0])

    emit = functools.partial(_emit_tile, lay, idx_scale, q_scale, j, w_ref)
    emit(h_ref, cos_ref, sin_ref, zb_ref, k_ref, v_ref, misc_ref, u_ref, vt_ref, wt_ref)

    @pl.when(i == 0)
    def _():
        emit(hs_ref, cos_s_ref, sin_s_ref, zs_ref, ks_ref, vs_ref, miscs_ref, us_ref, None, None)


def _project(xp, xs, g, mod_p, mod_s, w_t, starts, rope_p, rope_s, lay, tm, tiles_per_group, idx_scale):
    m, d = xp.shape
    ns = xs.shape[0]
    grp = mod_p[0].shape[0]
    nu = lay.end - lay.u
    kern = functools.partial(_proj_kernel, lay, idx_scale, Q_SCALE)
    once = dict(pipeline_mode=pl.Buffered(1))
    zb_col = lambda j: jnp.minimum(j, lay.nb - 1)
    u_col = lambda j: jnp.clip(j - lay.u, 0, nu - 1)
    first = lambda i, col, last: jnp.where(i == 0, col, last)
    row_p = lambda shape: pl.BlockSpec(shape, lambda i, j, st: (i,) + (0,) * (len(shape) - 1), **once)
    fixed = lambda shape: pl.BlockSpec(shape, lambda i, j, st: (0,) * len(shape))
    grp_p = pl.BlockSpec((1, 1, d), lambda i, j, st: (i // tiles_per_group, 0, 0))
    pos_p = pl.BlockSpec((tm, HEAD_DIM), lambda i, j, st: (i % tiles_per_group, 0))
    out_specs = [pl.BlockSpec((tm, TN), lambda i, j, st: (i, zb_col(j))),
                 row_p((tm, N_KV_HEADS, HEAD_DIM)),
                 row_p((tm, N_KV_HEADS, HEAD_DIM)),
                 row_p((tm, TN)),
                 pl.BlockSpec((tm, TN), lambda i, j, st: (i, u_col(j))),
                 pl.BlockSpec((1, TN, tm), lambda i, j, st: (i // tiles_per_group, 0, i % tiles_per_group), **once),
                 pl.BlockSpec((IDX_DIM, tm), lambda i, j, st: (0, i), **once),
                 pl.BlockSpec((ns, TN), lambda i, j, st: (0, first(i, zb_col(j), lay.nb - 1))),
                 fixed((ns, TN)), fixed((ns, TN)), fixed((ns, TN)),
                 pl.BlockSpec((ns, TN), lambda i, j, st: (0, first(i, u_col(j), nu - 1)))]
    out_shape = [jax.ShapeDtypeStruct((m, lay.nb * TN), BF16),
                 jax.ShapeDtypeStruct((m, N_KV_HEADS, HEAD_DIM), F32),
                 jax.ShapeDtypeStruct((m, N_KV_HEADS, HEAD_DIM), F32),
                 jax.ShapeDtypeStruct((m, TN), F32),
                 jax.ShapeDtypeStruct((m, nu * TN), F32),
                 jax.ShapeDtypeStruct((grp, TN, tiles_per_group * tm), BF16),
                 jax.ShapeDtypeStruct((IDX_DIM, m), F32),
                 jax.ShapeDtypeStruct((ns, lay.nb * TN), BF16),
                 jax.ShapeDtypeStruct((ns, TN), F32),
                 jax.ShapeDtypeStruct((ns, TN), F32),
                 jax.ShapeDtypeStruct((ns, TN), F32),
                 jax.ShapeDtypeStruct((ns, nu * TN), F32)]
    outs = pl.pallas_call(
        kern,
        grid_spec=pltpu.PrefetchScalarGridSpec(
            num_scalar_prefetch=1,
            grid=(m // tm, lay.end),
            in_specs=[fixed((1, d)),
                      pl.BlockSpec((pl.Element(TN), pl.Element(d)),
                                   lambda i, j, st: (pl.multiple_of(st[j], SUBLANES), 0)),
                      pl.BlockSpec((tm, d), lambda i, j, st: (i, 0)), grp_p, grp_p, pos_p, pos_p,
                      fixed((ns, d)), fixed((1, ns, d)), fixed((1, ns, d)),
                      fixed((ns, HEAD_DIM)), fixed((ns, HEAD_DIM))],
            out_specs=out_specs,
            scratch_shapes=[pltpu.VMEM((tm, d), BF16), pltpu.VMEM((ns, d), BF16)]),
        out_shape=out_shape,
        compiler_params=_cparams(("arbitrary", "arbitrary")),
    )(starts, g, w_t, xp, *mod_p, *rope_p, xs, *mod_s, *rope_s)
    return outs[:7], outs[7:]


def _bisect(count_ge, lo, hi, clo, kf):
    def cond(st):
        return (st[4] > 0.0) & (st[5] < MAX_BISECT)

    def halve(lo, hi, clo, done):
        mid = 0.5 * lo + 0.5 * hi
        mid = jnp.where(mid <= lo, hi, mid)
        c = count_ge(mid)
        ge = c >= kf
        stuck = (mid >= hi) & jnp.logical_not(ge)
        lo2 = jnp.where(ge, mid, lo)
        hi2 = jnp.where(ge, hi, mid)
        clo2 = jnp.where(ge, c, clo)
        fin = (clo2 <= kf) | (lo2 >= hi2) | stuck
        return lo2, hi2, clo2, jnp.maximum(done, jnp.where(fin, 1.0, 0.0))

    def body(st):
        lo, hi, clo, done = halve(*halve(*st[:4]))
        return lo, hi, clo, done, jnp.sum(1.0 - done), st[5] + 1

    done0 = jnp.where((clo <= kf) | (lo >= hi), 1.0, 0.0)
    st = (lo, hi, clo, done0, jnp.sum(1.0 - done0), jnp.int32(0))
    out = lax.while_loop(cond, body, st)
    return out[0], out[2]


def _attn_kernel(topk, q_ref, qi_ref, sga_ref, k_ref, ki_ref, vt_ref, wt_ref, o_ref,
                 score_ref, bias_ref, cut_ref, m_ref, l_ref, alpha_ref, acc_ref, lg_ref):
    i = pl.program_id(1)
    nck = i + 1
    n_heads = q_ref.shape[1] // HEAD_DIM
    group = n_heads // N_KV_HEADS
    nt = (((1,), (1,)), ((), ()))
    q_pos = i * TQ + lax.broadcasted_iota(jnp.int32, (1, TQ), 1)
    fold = lambda a: a.reshape(CK // SUBLANES, SUBLANES, TQ)

    def score_chunk(c, carry):
        mn, mx = carry
        off = pl.multiple_of(c * CK, CK)
        kc = ki_ref[pl.ds(off, CK), :]
        acc = jnp.zeros((CK, TQ), F32)
        for h in range(IDX_HEADS):
            s = lax.dot_general(kc, qi_ref[:, h * IDX_DIM:(h + 1) * IDX_DIM], nt,
                                preferred_element_type=F32)
            acc = acc + wt_ref[h:h + 1, :] * jnp.maximum(s, 0.0)
        k_pos = off + lax.broadcasted_iota(jnp.int32, (CK, 1), 0)
        causal = k_pos <= q_pos
        score_ref[pl.ds(off, CK), :] = jnp.where(causal, acc, -jnp.inf)
        mn = jnp.minimum(mn, fold(jnp.where(causal, acc, jnp.inf)).min(axis=0))
        mx = jnp.maximum(mx, fold(jnp.where(causal, acc, -jnp.inf)).max(axis=0))
        return mn, mx

    mn, mx = lax.fori_loop(0, nck, score_chunk,
                           (jnp.full((SUBLANES, TQ), jnp.inf, F32),
                            jnp.full((SUBLANES, TQ), -jnp.inf, F32)))
    lo = mn.min(axis=0, keepdims=True)
    hi = mx.max(axis=0, keepdims=True)

    ones_rows = jnp.ones((SUBLANES, CK), BF16)

    def count_ge(thr):
        def body(c, cnt):
            s = score_ref[pl.ds(pl.multiple_of(c * CK, CK), CK), :]
            return cnt + fold(jnp.where(s >= thr, 1.0, 0.0)).sum(axis=0)
        cnt = lax.fori_loop(0, nck, body, jnp.zeros((SUBLANES, TQ), F32))
        return cnt.sum(axis=0, keepdims=True)

    n_valid = (q_pos + 1).astype(F32)
    kf = float(topk)
    thr, n_ge = _bisect(count_ge, lo, hi, n_valid, kf)

    s_len = score_ref.shape[0]
    has_ties = jnp.sum(jnp.where(n_ge > kf, 1.0, 0.0)) > 0.0
    cut_ref[...] = jnp.full(cut_ref.shape, float(s_len), F32)

    @pl.when(has_ties)
    def _():
        def count(pred):
            def body(c, cnt):
                off = pl.multiple_of(c * CK, CK)
                k_pos = (off + lax.broadcasted_iota(jnp.int32, (CK, 1), 0)).astype(F32)
                hit = pred(score_ref[pl.ds(off, CK), :], k_pos)
                return cnt + fold(jnp.where(hit, 1.0, 0.0)).sum(axis=0)
            cnt = lax.fori_loop(0, nck, body, jnp.zeros((SUBLANES, TQ), F32))
            return cnt.sum(axis=0, keepdims=True)

        need = kf - count(lambda s, kp: s > thr)

        def step(_, st):
            lo_i, hi_i = st
            mid = jnp.floor(0.5 * (lo_i + hi_i))
            ge = count(lambda s, kp: (s == thr) & (kp <= mid)) >= need
            return jnp.where(ge, lo_i, mid), jnp.where(ge, mid, hi_i)

        st0 = (jnp.full((1, TQ), -1.0, F32), jnp.full((1, TQ), s_len - 1.0, F32))
        cut = lax.fori_loop(0, (s_len - 1).bit_length() + 1, step, st0)[1]
        cut_ref[...] = jnp.broadcast_to(cut, cut_ref.shape)

    m_ref[...] = jnp.full(m_ref.shape, NEG, F32)
    l_ref[...] = jnp.zeros(l_ref.shape, F32)
    acc_ref[...] = jnp.zeros(acc_ref.shape, F32)

    def att_chunk(c, carry):
        off = pl.multiple_of(c * CK, CK)
        sc = score_ref[pl.ds(off, CK), :]

        @pl.when(jnp.logical_not(has_ties))
        def _():
            bias_ref[...] = jnp.where(sc >= thr, 0.0, NEG)

        @pl.when(has_ties)
        def _():
            k_pos = (off + lax.broadcasted_iota(jnp.int32, (CK, 1), 0)).astype(F32)
            keep = (sc > thr) | ((sc == thr) & (k_pos <= cut_ref[0:1, :]))
            bias_ref[...] = jnp.where(keep, 0.0, NEG)

        kc = k_ref[pl.ds(off, CK), :]
        vtc = vt_ref[0, :, pl.ds(off, CK)]

        def qk(h):
            g = h // group
            return lax.dot_general(kc[:, g * HEAD_DIM:(g + 1) * HEAD_DIM],
                                   q_ref[:, h * HEAD_DIM:(h + 1) * HEAD_DIM], nt,
                                   preferred_element_type=F32)

        for h in range(n_heads):
            lg = qk(h) + bias_ref[...]
            m_old = m_ref[h:h + 1, :]
            m_new = jnp.maximum(m_old, lg.max(axis=0, keepdims=True))
            lg_ref[h] = lg
            alpha_ref[h:h + 1, :] = jnp.exp2(m_old - m_new)
            m_ref[h:h + 1, :] = m_new
        for h in range(n_heads):
            g = h // group
            p = jnp.exp2(lg_ref[h] - m_ref[h:h + 1, :]).astype(BF16)
            pv = jnp.dot(vtc[g * HEAD_DIM:(g + 1) * HEAD_DIM, :], p, preferred_element_type=F32)
            acc_ref[h] = alpha_ref[h:h + 1, :] * acc_ref[h] + pv
            psum = jnp.dot(ones_rows, p, preferred_element_type=F32)[0:1, :]
            l_ref[h:h + 1, :] = alpha_ref[h:h + 1, :] * l_ref[h:h + 1, :] + psum
        return carry

    lax.fori_loop(0, nck, att_chunk, 0)

    for h in range(n_heads):
        o = (acc_ref[h] / l_ref[h:h + 1, :]).T
        sl = slice(h * HEAD_DIM, (h + 1) * HEAD_DIM)
        o_ref[:, sl] = (o * sga_ref[:, sl].astype(F32)).astype(BF16)


def _prompt_attention(zb, vt, wt, lay, b, s, d, topk):
    nq = s // TQ
    n_heads = d // HEAD_DIM
    kern = functools.partial(_attn_kernel, topk)
    row = lambda bb, i: bb * nq + i
    return pl.pallas_call(
        kern,
        grid=(b, nq),
        in_specs=[pl.BlockSpec((TQ, d), lambda bb, i: (row(bb, i), lay.q * TN // d)),
                  pl.BlockSpec((TQ, d), lambda bb, i: (row(bb, i), lay.qi * TN // d)),
                  pl.BlockSpec((TQ, d), lambda bb, i: (row(bb, i), lay.ga * TN // d)),
                  pl.BlockSpec((s, TN), lambda bb, i: (bb, lay.k)),
                  pl.BlockSpec((s, IDX_DIM), lambda bb, i: (bb, lay.misc * TN // IDX_DIM)),
                  pl.BlockSpec((1, TN, s), lambda bb, i: (bb, 0, 0)),
                  pl.BlockSpec((IDX_HEADS, TQ), lambda bb, i: (0, row(bb, i)))],
        out_specs=pl.BlockSpec((TQ, d), lambda bb, i: (row(bb, i), 0)),
        out_shape=jax.ShapeDtypeStruct((b * s, d), BF16),
        scratch_shapes=[pltpu.VMEM((s, TQ), F32),
                        pltpu.VMEM((CK, TQ), F32),
                        pltpu.VMEM((SUBLANES, TQ), F32),
                        pltpu.VMEM((n_heads, TQ), F32),
                        pltpu.VMEM((n_heads, TQ), F32),
                        pltpu.VMEM((n_heads, TQ), F32),
                        pltpu.VMEM((n_heads, HEAD_DIM, TQ), F32),
                        pltpu.VMEM((n_heads, CK, TQ), F32)],
        compiler_params=_cparams(("arbitrary", "arbitrary")),
    )(zb, zb, zb, zb, zb, vt, wt)


def _pool_kernel(ts, u_ref, sgb_ref, wp_ref, sp_ref, o_ref, *level_refs):
    i = pl.program_id(1)
    ext_ref = level_refs[0]
    halo = POOL_HALO
    n = halo + ts

    @pl.when(i == 0)
    def _():
        ext_ref[0:halo, :] = jnp.zeros((halo, ext_ref.shape[1]), F32)

    @pl.when(i > 0)
    def _():
        ext_ref[0:halo, :] = ext_ref[ts:ts + halo, :]

    ext_ref[halo:n, :] = u_ref[...]
    pos = i * ts + lax.broadcasted_iota(jnp.int32, (ts, 1), 0)
    gd = wp_ref.shape[1]
    top = len(POOL_WINDOWS) - 1
    for k in range(top):
        lo, sh, c0 = SUBLANES * (k + 1), POOL_WINDOWS[k] // 2, k * gd
        level_refs[k + 1][lo:n, c0:] = level_refs[k][lo:n, c0:] + level_refs[k][lo - sh:n - sh, c0:]
    sh = POOL_WINDOWS[top] // 2
    wins = [level_refs[g + 1][halo:n, g * gd:(g + 1) * gd] for g in range(top)]
    wins.append(level_refs[top][halo:n, top * gd:] + level_refs[top][halo - sh:n - sh, top * gd:])
    for g, w in enumerate(POOL_WINDOWS):
        cols = slice(g * gd, (g + 1) * gd)
        acc = wins[g]
        cnt = jnp.minimum(pos + 1, w).astype(F32)
        pooled = acc / cnt - u_ref[:, cols]
        mixed = jnp.dot(pooled.astype(BF16), wp_ref[g], preferred_element_type=F32) * sp_ref[:, cols]
        o_ref[:, cols] = (mixed * sgb_ref[:, cols].astype(F32)).astype(BF16)


def _pool_prompt(u, zb, wp, sp, lay, b, s, ts):
    m, pw = u.shape
    nt = s // ts
    kern = functools.partial(_pool_kernel, ts)
    return pl.pallas_call(
        kern,
        grid=(b, nt),
        in_specs=[pl.BlockSpec((ts, pw), lambda bb, i: (bb * nt + i, 0)),
                  pl.BlockSpec((ts, pw), lambda bb, i: (bb * nt + i, lay.gb * TN // pw)),
                  pl.BlockSpec(wp.shape, lambda bb, i: (0, 0, 0)),
                  pl.BlockSpec((1, pw), lambda bb, i: (0, 0))],
        out_specs=pl.BlockSpec((ts, pw), lambda bb, i: (bb * nt + i, 0)),
        out_shape=jax.ShapeDtypeStruct((m, pw), BF16),
        scratch_shapes=[pltpu.VMEM((POOL_HALO + ts, pw), F32) for _ in POOL_WINDOWS],
        compiler_params=_cparams(("arbitrary", "arbitrary")),
    )(u, zb, wp, sp)


def _pool_step_kernel(cnts, ext_ref, sgb_ref, wp_ref, sp_ref, o_ref):
    gd = wp_ref.shape[1]
    rows = ext_ref.shape[0]
    for g, w in enumerate(POOL_WINDOWS):
        cols = slice(g * gd, (g + 1) * gd)
        acc = ext_ref[rows - 1, :, cols]
        for jj in range(1, w):
            acc = acc + ext_ref[rows - 1 - jj, :, cols]
        pooled = acc / cnts[g] - ext_ref[rows - 1, :, cols]
        mixed = jnp.dot(pooled.astype(BF16), wp_ref[g], preferred_element_type=F32) * sp_ref[:, cols]
        o_ref[:, cols] = (mixed * sgb_ref[:, cols].astype(F32)).astype(BF16)


def _pool_step(ext_t, sgb, wp, sp, past):
    rows, n, pw = ext_t.shape
    cnts = tuple(float(min(past + 1, w)) for w in POOL_WINDOWS)
    return pl.pallas_call(
        functools.partial(_pool_step_kernel, cnts),
        out_shape=jax.ShapeDtypeStruct((n, pw), BF16),
    )(ext_t, sgb, wp, sp)


def _merge1_kernel(ab_ref, pb_ref, wa_ref, wb_ref, sma_ref, smb_ref, o_ref):
    a = jnp.dot(ab_ref[...], wa_ref[...], preferred_element_type=F32)
    bb = jnp.dot(pb_ref[...], wb_ref[...], preferred_element_type=F32)
    o_ref[...] = (sma_ref[...].astype(F32) * a + smb_ref[...].astype(F32) * bb).astype(BF16)


def _merge1(ab, pb, wa, wb, zb, lay, tm):
    m, d = ab.shape
    pw = pb.shape[1]
    return pl.pallas_call(
        _merge1_kernel,
        grid=(m // tm, d // TN),
        in_specs=[pl.BlockSpec((tm, d), lambda i, j: (i, 0)),
                  pl.BlockSpec((tm, pw), lambda i, j: (i, 0)),
                  pl.BlockSpec((d, TN), lambda i, j: (0, j)),
                  pl.BlockSpec((pw, TN), lambda i, j: (0, j)),
                  pl.BlockSpec((tm, TN), lambda i, j: (i, lay.ma + j)),
                  pl.BlockSpec((tm, TN), lambda i, j: (i, lay.mb + j))],
        out_specs=pl.BlockSpec((tm, TN), lambda i, j: (i, j)),
        out_shape=jax.ShapeDtypeStruct((m, d), BF16),
        compiler_params=_cparams(("arbitrary", "arbitrary")),
    )(ab, pb, wa, wb, zb, zb)


def _merge2_kernel(final, mm_ref, x_ref, gate_ref, wo_ref, gf_ref, o_ref):
    y = x_ref[...] + gate_ref[0] * jnp.dot(mm_ref[...], wo_ref[...], preferred_element_type=F32)
    if final:
        y = y * lax.rsqrt(jnp.mean(y * y, axis=-1, keepdims=True) + EPS) * gf_ref[...]
    o_ref[...] = y


def _merge2(mm, x, gate, wo, gf, tm, tiles_per_group, final):
    m, d = x.shape
    r = gate.shape[1]
    return pl.pallas_call(
        functools.partial(_merge2_kernel, final),
        grid=(m // tm,),
        in_specs=[pl.BlockSpec((tm, d), lambda i: (i, 0)),
                  pl.BlockSpec((tm, d), lambda i: (i, 0)),
                  pl.BlockSpec((1, r, d), lambda i: (i // tiles_per_group, 0, 0)),
                  pl.BlockSpec((d, d), lambda i: (0, 0)),
                  pl.BlockSpec((1, d), lambda i: (0, 0))],
        out_specs=pl.BlockSpec((tm, d), lambda i: (i, 0)),
        out_shape=jax.ShapeDtypeStruct((m, d), F32),
        compiler_params=_cparams(("arbitrary",)),
    )(mm, x, gate, wo, gf)


def _page_specs(width, npg, npp):
    return [pl.BlockSpec((1, PAGE_SIZE, width),
                         lambda b, p, pt, r=r: (pt[b * npg + p * npp + r], 0, 0)) for r in range(npp)]


def _sidx_kernel(npp, pt_ref, *refs):
    kidx_refs = refs[:npp]
    qi_ref, w_ref, o_ref = refs[npp:]
    nt = (((1,), (1,)), ((), ()))
    qi, w = qi_ref[0], w_ref[0]
    rows = []
    for r in range(npp):
        s = lax.dot_general(qi, kidx_refs[r][0].astype(BF16), nt, preferred_element_type=F32)
        rows.append(jnp.sum(w * jnp.maximum(s, 0.0), axis=0, keepdims=True))
    o_ref[0] = jnp.concatenate(rows, axis=1)


def _sample_scores(page_table, cache_kidx, qi3, wcol, npp):
    n, npg = page_table.shape
    return pl.pallas_call(
        functools.partial(_sidx_kernel, npp),
        grid_spec=pltpu.PrefetchScalarGridSpec(
            num_scalar_prefetch=1,
            grid=(n, npg // npp),
            in_specs=_page_specs(IDX_DIM, npg, npp) + [
                pl.BlockSpec((1, IDX_HEADS, IDX_DIM), lambda b, p, pt: (b, 0, 0)),
                pl.BlockSpec((1, IDX_HEADS, 1), lambda b, p, pt: (b, 0, 0))],
            out_specs=pl.BlockSpec((1, 1, npp * PAGE_SIZE), lambda b, p, pt: (b, 0, p))),
        out_shape=jax.ShapeDtypeStruct((n, 1, npg * PAGE_SIZE), F32),
        compiler_params=_cparams(("arbitrary", "arbitrary")),
    )(page_table.reshape(-1), *([cache_kidx] * npp), qi3, wcol)


def _sthr_kernel(topk, row_base, sc_ref, qi_ref, kin_ref, w_ref, pt_ref, pos_ref, cnt_ref, sel_ref,
                 thr_scr, new_scr):
    sc = sc_ref[...]
    kin = kin_ref[...].astype(BF16).astype(F32)
    s_new = jnp.sum(qi_ref[...].astype(F32) * kin, axis=2, keepdims=True)
    new = jnp.sum(w_ref[...] * jnp.maximum(s_new, 0.0), axis=1, keepdims=True)
    red = lambda f, a: f(f(a, axis=2, keepdims=True), axis=1, keepdims=True)
    lo = jnp.minimum(red(jnp.min, sc), new)
    hi = jnp.maximum(red(jnp.max, sc), new)

    def count_ge(thr):
        c = red(jnp.sum, jnp.where(sc >= thr, 1.0, 0.0))
        return c + jnp.where(new >= thr, 1.0, 0.0)

    n_all = jnp.full(new.shape, float(sc.shape[1] * sc.shape[2] + 1), F32)
    thr, _ = _bisect(count_ge, lo, hi, n_all, float(topk))
    thr_scr[...] = jnp.broadcast_to(thr, thr_scr.shape)
    new_scr[...] = jnp.broadcast_to(new, new_scr.shape)

    rows = sc.shape[1]
    ii = lambda shape, ax: lax.broadcasted_iota(jnp.int32, shape, ax)
    one = lambda m: jnp.where(m, 1.0, 0.0)
    tri_lane = one(ii((PAGE_SIZE, PAGE_SIZE), 0) <= ii((PAGE_SIZE, PAGE_SIZE), 1)).astype(BF16)
    lower = one(ii((rows, rows), 1) < ii((rows, rows), 0)).astype(BF16)
    upper = one(ii((rows, rows), 0) < ii((rows, rows), 1)).astype(BF16)
    ones8 = jnp.ones((SUBLANES, PAGE_SIZE), BF16)
    nt = (((1,), (1,)), ((), ()))
    total = lambda a: jnp.sum(jnp.sum(a, axis=1, keepdims=True), axis=0, keepdims=True)
    dotf = lambda a, b: jnp.dot(a.astype(BF16), b.astype(BF16), preferred_element_type=F32)
    kf = float(topk)
    j_col = ii((topk, 1), 0).astype(F32)
    c_row = ii((1, PAGE_SIZE), 1).astype(F32)

    def compact(b, carry):
        s = sc_ref[b]
        t = thr_scr[b][:, 0:1]
        nw = new_scr[b][:, 0:1]
        gt, eq = s > t, s == t
        need = kf - total(one(gt)) - one(nw > t)
        e_lane = dotf(one(eq), tri_lane)
        e_rank = e_lane + dotf(lower, e_lane)[:, PAGE_SIZE - 1:PAGE_SIZE]
        x = one(gt | (eq & (e_rank <= need)))
        sel_new = (nw > t) | ((nw == t) & (total(one(eq)) < need))
        x_lane = dotf(x, tri_lane)
        n_rows = lax.dot_general(ones8, x.astype(BF16), nt, preferred_element_type=F32)
        n_row = n_rows[0:1, :]
        start_row = dotf(n_rows, upper)[0:1, :]
        hit = one((start_row <= j_col) & (j_col < start_row + n_row))
        lane_sum = lambda a: jnp.sum(a, axis=1, keepdims=True)
        page = lane_sum(hit * pt_ref[b])
        target = j_col - lane_sum(hit * start_row) + 1.0
        in_row = dotf(hit, x_lane * x)
        off = lane_sum(jnp.where(in_row == target, c_row, 0.0))
        pos_ref[b] = (page * PAGE_SIZE + off + row_base).astype(jnp.int32)
        cnt_ref[b] = jnp.broadcast_to(total(x), cnt_ref.shape[1:])
        sel_ref[b] = jnp.broadcast_to(one(sel_new), sel_ref.shape[1:])
        return carry

    lax.fori_loop(0, sc.shape[0], compact, 0)


def _sample_select(sc, qi3, kin, wcol, pt_rows, topk, row_base):
    n = sc.shape[0]
    return pl.pallas_call(
        functools.partial(_sthr_kernel, topk, row_base),
        out_shape=[jax.ShapeDtypeStruct((n, topk, 1), jnp.int32),
                   jax.ShapeDtypeStruct((n, 1, PAGE_SIZE), F32),
                   jax.ShapeDtypeStruct((n, 1, PAGE_SIZE), F32)],
        scratch_shapes=[pltpu.VMEM((n, 1, PAGE_SIZE), F32),
                        pltpu.VMEM((n, 1, PAGE_SIZE), F32)],
        compiler_params=pltpu.CompilerParams(vmem_limit_bytes=VMEM_LIMIT),
    )(sc, qi3, kin, wcol, pt_rows)


def _row_copy(cache_rows, buf, sem, row, j):
    return pltpu.make_async_copy(cache_rows.at[row], buf.at[j], sem)


def _sattn_kernel(pos_ref, ck_hbm, cv_hbm, cnt_ref, sel_ref, q_ref, kn_ref, vn_ref,
                  sga_ref, o_ref, kbuf, vbuf, ksem, vsem):
    b = pl.program_id(0)
    slot = b % 2
    topk = kbuf.shape[1]
    n_heads = q_ref.shape[1]
    group = n_heads // N_KV_HEADS
    nt = (((1,), (1,)), ((), ()))

    def fetch(bb, s):
        def issue(j, carry):
            row = pos_ref[bb * topk + j]
            _row_copy(ck_hbm, kbuf.at[s], ksem.at[s], row, j).start()
            _row_copy(cv_hbm, vbuf.at[s], vsem.at[s], row, j).start()
            return carry
        lax.fori_loop(0, topk, issue, 0, unroll=8)

    @pl.when(b == 0)
    def _():
        fetch(b, slot)

    @pl.when(b + 1 < pl.num_programs(0))
    def _():
        fetch(b + 1, 1 - slot)

    kbuf, vbuf = kbuf.at[slot], vbuf.at[slot]
    if topk % PAGE_SIZE == 0:
        for w in range(topk // PAGE_SIZE):
            rows = pl.ds(w * PAGE_SIZE, PAGE_SIZE)
            pltpu.make_async_copy(ck_hbm.at[rows], kbuf.at[rows], ksem.at[slot]).wait()
            pltpu.make_async_copy(cv_hbm.at[rows], vbuf.at[rows], vsem.at[slot]).wait()
    else:
        def wait(j, carry):
            _row_copy(ck_hbm, kbuf, ksem.at[slot], 0, j).wait()
            _row_copy(cv_hbm, vbuf, vsem.at[slot], 0, j).wait()
            return carry

        lax.fori_loop(0, topk, wait, 0)

    q = q_ref[0]
    head = lax.broadcasted_iota(jnp.int32, (n_heads, 1), 0)
    in_group = [(head >= g * group) & (head < (g + 1) * group) for g in range(N_KV_HEADS)]
    lg = jnp.zeros((n_heads, topk), F32)
    for g in range(N_KV_HEADS):
        lg_g = lax.dot_general(q, kbuf[:, g, :].astype(BF16), nt, preferred_element_type=F32)
        lg = lg + jnp.where(in_group[g], lg_g, 0.0)
    valid = lax.broadcasted_iota(jnp.int32, (1, topk), 1).astype(F32) < cnt_ref[0][:, 0:1]
    lg = jnp.where(valid, lg, NEG)
    kn = kn_ref[0].astype(BF16).astype(F32)
    vn = vn_ref[0].astype(BF16).astype(F32)
    lgn = jnp.sum(q.astype(F32) * kn, axis=1, keepdims=True)
    lgn = jnp.where(sel_ref[0][:, 0:1] > 0.0, lgn, NEG)
    m = jnp.maximum(lg.max(axis=1, keepdims=True), lgn)
    p = jnp.exp2(lg - m)
    pn = jnp.exp2(lgn - m)
    l = p.sum(axis=1, keepdims=True) + pn
    pb = p.astype(BF16)
    acc = pn.astype(BF16).astype(F32) * vn
    for g in range(N_KV_HEADS):
        pv = jnp.dot(pb, vbuf[:, g, :].astype(BF16), preferred_element_type=F32)
        acc = acc + jnp.where(in_group[g], pv, 0.0)
    o_ref[0] = ((acc / l) * sga_ref[0].astype(F32)).astype(BF16)


def _sample_attention(k_rows, v_rows, pos, cnt, sel, q3, kn, vn, sga3):
    n, n_heads = q3.shape[:2]
    topk = pos.shape[1]
    per_b = lambda shape: pl.BlockSpec(shape, lambda b, ps: (b, 0, 0))
    hbm = pl.BlockSpec(memory_space=pl.ANY)
    return pl.pallas_call(
        _sattn_kernel,
        grid_spec=pltpu.PrefetchScalarGridSpec(
            num_scalar_prefetch=1,
            grid=(n,),
            in_specs=[hbm, hbm,
                      per_b((1, 1, PAGE_SIZE)),
                      per_b((1, 1, PAGE_SIZE)),
                      per_b((1, n_heads, HEAD_DIM)),
                      per_b((1, n_heads, HEAD_DIM)),
                      per_b((1, n_heads, HEAD_DIM)),
                      per_b((1, n_heads, HEAD_DIM))],
            out_specs=per_b((1, n_heads, HEAD_DIM)),
            scratch_shapes=[pltpu.VMEM((2, topk, N_KV_HEADS, HEAD_DIM), k_rows.dtype),
                            pltpu.VMEM((2, topk, N_KV_HEADS, HEAD_DIM), v_rows.dtype),
                            pltpu.SemaphoreType.DMA((2,)),
                            pltpu.SemaphoreType.DMA((2,))]),
        out_shape=jax.ShapeDtypeStruct((n, n_heads, HEAD_DIM), BF16),
        compiler_params=_cparams(("arbitrary",)),
    )(pos.reshape(-1), k_rows, v_rows, cnt, sel, q3, kn, vn, sga3)


def _rope_tables(pos):
    half = HEAD_DIM // 2
    inv = ROPE_THETA ** (-jnp.arange(half, dtype=F32) / half)
    ang = pos.astype(F32)[:, None] * inv[None, :]
    cos, sin = jnp.cos(ang), jnp.sin(ang)
    return jnp.concatenate([cos, cos], axis=1), jnp.concatenate([-sin, sin], axis=1)


def _tile_starts(d, lay):
    att, kv, idx, pool = d, N_KV_HEADS * HEAD_DIM, IDX_HEADS * IDX_DIM, d // 2
    sizes = (att, kv, kv, idx, IDX_DIM, IDX_HEADS, att, pool, pool, d, d)
    offs = [0]
    for n in sizes:
        offs.append(offs[-1] + n)
    q, k, v, qi, ki, wi, ga, u, gb, ma, mb = offs[:-1]
    starts = [0] * lay.end
    for first, tiles, col in ((lay.q, lay.qi - lay.q, q), (lay.qi, lay.ga - lay.qi, qi),
                              (lay.ga, lay.ma - lay.ga, ga), (lay.ma, lay.mb - lay.ma, ma),
                              (lay.mb, lay.gb - lay.mb, mb), (lay.gb, lay.k - lay.gb, gb),
                              (lay.k, 1, k), (lay.v, 1, v), (lay.misc, 1, ki), (lay.u, lay.end - lay.u, u)):
        for t in range(tiles):
            starts[first + t] = col + t * TN
    assert wi == ki + IDX_DIM and all(s + TN <= offs[-1] and s % SUBLANES == 0 for s in starts)
    return starts


def kernel(x_prompt, x_sample, c_prompt, c_sample, cache_k, cache_v, cache_kidx, state_pool, page_table,
           g_norm, w_ada, b_ada, w_in, w_pool, s_pool, w_a_proj, w_b_proj, w_out, g_final):
    bp, sp, d = x_prompt.shape
    ns, ds, _ = x_sample.shape
    depth = g_norm.shape[0]
    npg = page_table.shape[1]
    past = npg * PAGE_SIZE
    assert ds == 1 and d % (N_KV_HEADS * HEAD_DIM) == 0 and sp % TQ == 0 and TQ == CK and sp >= POOL_STATE
    n_heads = d // HEAD_DIM
    kvw = N_KV_HEADS * HEAD_DIM
    lay = _Layout(d)
    pw = lay.pool
    idx_scale = (IDX_HEADS * IDX_DIM) ** -0.5
    topk_p = min(TOPK_MAX, sp // 4)
    topk_s = min(TOPK_MAX, (past + ds) // 4)
    tm = math.gcd(sp, ROW_TILE)
    ts = math.gcd(sp, POOL_TILE)
    tm2 = math.gcd(sp, OUT_TILE)

    cos_p, sin_p = _rope_tables(jnp.arange(sp, dtype=jnp.int32))
    cos_s, sin_s = _rope_tables(jnp.full((ns,), past, jnp.int32))
    n_c = bp + ns
    c_rows = -(-n_c // SUBLANES) * SUBLANES
    c_all = jnp.concatenate([c_prompt, c_sample, jnp.zeros((c_rows - n_c, d), F32)], axis=0)

    n_pool = cache_kidx.shape[1]
    kidx_pages = cache_kidx.reshape(depth * n_pool, PAGE_SIZE, IDX_DIM)
    npp = math.gcd(npg, PAGES_PER_STEP)
    k_rows = cache_k.reshape(depth * n_pool * PAGE_SIZE, N_KV_HEADS, HEAD_DIM)
    v_rows = cache_v.reshape(depth * n_pool * PAGE_SIZE, N_KV_HEADS, HEAD_DIM)
    pt_rows = page_table.astype(F32).reshape(ns, 1, npg)
    assert npg % npp == 0

    in_width = w_in.shape[2]
    w_t = jnp.swapaxes(w_in, 1, 2).reshape(depth * in_width, d)

    xp = x_prompt.reshape(bp * sp, d)
    xs = x_sample.reshape(ns, d)
    outs = {n: [] for n in ("kp", "vp", "kip", "pp", "ks", "vs", "kis", "ps")}
    for l in range(depth):
        final = l == depth - 1
        starts = jnp.asarray([l * in_width + s for s in _tile_starts(d, lay)], jnp.int32)
        wa, wb, wo, wpl = (w_a_proj[l].astype(BF16), w_b_proj[l].astype(BF16), w_out[l].astype(BF16),
                           w_pool[l].astype(BF16))
        g = g_norm[l].reshape(1, d)
        spl = s_pool[l].reshape(1, pw)
        gf = g_final.reshape(1, d)

        mod = _ada(c_all, w_ada[l], b_ada[l])
        shift, scale, gate = mod[:, :d], mod[:, d:2 * d], mod[:, 2 * d:]
        pg = lambda a: a[:bp].reshape(bp, 1, d)
        sg = lambda a: a[bp:n_c].reshape(1, ns, d)

        (zb, k_p, v_p, misc_p, u_p, vt, wt), (zs, k_s, v_s, misc_s, u_s) = _project(
            xp, xs, g, (pg(shift), pg(scale)), (sg(shift), sg(scale)), w_t, starts,
            (cos_p, sin_p), (cos_s, sin_s), lay, tm, sp // tm, idx_scale)

        ab = _prompt_attention(zb, vt, wt, lay, bp, sp, d, topk_p)
        pb = _pool_prompt(u_p, zb, wpl, spl, lay, bp, sp, ts)
        mm = _merge1(ab, pb, wa, wb, zb, lay, tm)
        xp = _merge2(mm, xp, pg(gate), wo, gf, tm2, sp // tm2, final)
        outs["kp"].append(k_p.reshape(bp, sp, N_KV_HEADS, HEAD_DIM))
        outs["vp"].append(v_p.reshape(bp, sp, N_KV_HEADS, HEAD_DIM))
        outs["kip"].append(misc_p[:, :IDX_DIM].reshape(bp, sp, IDX_DIM))
        outs["pp"].append(u_p.reshape(bp, sp, pw)[:, sp - POOL_STATE:, :])

        col = lambda t, n: zs[:, t * TN:t * TN + n]
        q3 = col(lay.q, d).reshape(ns, n_heads, HEAD_DIM)
        qi3 = col(lay.qi, IDX_HEADS * IDX_DIM).reshape(ns, IDX_HEADS, IDX_DIM)
        sga3 = col(lay.ga, d).reshape(ns, n_heads, HEAD_DIM)
        wcol = misc_s[:, IDX_DIM:IDX_DIM + IDX_HEADS].reshape(ns, IDX_HEADS, 1)
        kin = misc_s[:, :IDX_DIM].reshape(ns, 1, IDX_DIM)
        pt_l = page_table + l * n_pool
        sc = _sample_scores(pt_l, kidx_pages, qi3, wcol, npp)
        pos, cnt, sel = _sample_select(sc.reshape(ns, npg, PAGE_SIZE), qi3, kin, wcol, pt_rows, topk_s,
                                       l * n_pool * PAGE_SIZE)
        rep = lambda a: jnp.repeat(a.reshape(ns, N_KV_HEADS, HEAD_DIM), n_heads // N_KV_HEADS, axis=1)
        ab_s = _sample_attention(k_rows, v_rows, pos, cnt, sel, q3, rep(k_s), rep(v_s), sga3).reshape(ns, d)
        ext = jnp.concatenate([state_pool[l], u_s[:, None, :]], axis=1)
        pb_s = _pool_step(ext.transpose(1, 0, 2), col(lay.gb, pw), wpl, spl, past)
        mm_s = _merge1(ab_s, pb_s, wa, wb, zs, lay, ns)
        xs = _merge2(mm_s, xs, sg(gate), wo, gf, ns, 1, final)
        outs["ks"].append(k_s.reshape(ns, ds, N_KV_HEADS, HEAD_DIM))
        outs["vs"].append(v_s.reshape(ns, ds, N_KV_HEADS, HEAD_DIM))
        outs["kis"].append(misc_s[:, :IDX_DIM].reshape(ns, ds, IDX_DIM))
        outs["ps"].append(ext[:, 1:, :])

    st = lambda n: outs[n][0][None] if depth == 1 else jnp.stack(outs[n])
    return (xp.reshape(bp, sp, d), xs.reshape(ns, ds, d),
            st("kp"), st("vp"), st("kip"), st("pp"), st("ks"), st("vs"), st("kis"), st("ps"))
```

```python
import functools
import math

import jax
import jax.numpy as jnp
from jax import lax
from jax.experimental import pallas as pl
from jax.experimental.pallas import tpu as pltpu

F32 = jnp.float32
BF16 = jnp.bfloat16

HEAD_DIM = 128
N_KV_HEADS = 4
IDX_HEADS = 16
IDX_DIM = 128
TOPK_MAX = 256
PAGE_SIZE = 128
POOL_WINDOWS = (2, 4, 8, 16)
POOL_STATE = max(POOL_WINDOWS) - 1
ROPE_THETA = 10000.0
EPS = 1e-6

V7X_VMEM_BYTES = 64 * 1024 * 1024
V7X_MXU_COLUMNS = 256
SUBLANES = 8
TN = 512
ROW_TILE = 1024
POOL_TILE = 512
assert all(w == 2 << g for g, w in enumerate(POOL_WINDOWS)) and max(POOL_WINDOWS) // 2 <= SUBLANES
POOL_HALO = SUBLANES * len(POOL_WINDOWS)
OUT_TILE = 512
TQ = V7X_MXU_COLUMNS
CK = 256
NEG = -1e30
Q_SCALE = 1.4426950408889634 * HEAD_DIM ** -0.5
PAGES_PER_STEP = 64
MAX_BISECT = 512
VMEM_LIMIT = V7X_VMEM_BYTES * 7 // 8


def _cparams(sem):
    return pltpu.CompilerParams(dimension_semantics=sem, vmem_limit_bytes=VMEM_LIMIT)


def _sigmoid(z):
    return 0.5 * jnp.tanh(0.5 * z) + 0.5


def _ada_kernel(c_ref, w_ref, b_ref, o_ref):
    o_ref[...] = jnp.dot(c_ref[...].astype(BF16), w_ref[...].astype(BF16),
                         preferred_element_type=F32) + b_ref[...]


def _ada(c, w, b):
    r, d = c.shape
    n = w.shape[1]
    return pl.pallas_call(
        _ada_kernel,
        grid=(n // TN,),
        in_specs=[pl.BlockSpec((r, d), lambda j: (0, 0)),
                  pl.BlockSpec((d, TN), lambda j: (0, j)),
                  pl.BlockSpec((1, TN), lambda j: (0, j))],
        out_specs=pl.BlockSpec((r, TN), lambda j: (0, j)),
        out_shape=jax.ShapeDtypeStruct((r, n), F32),
        compiler_params=_cparams(("arbitrary",)),
    )(c, w, b.reshape(1, n))


class _Layout:
    def __init__(self, d):
        att, kv, idx, pool = d, N_KV_HEADS * HEAD_DIM, IDX_HEADS * IDX_DIM, d // 2
        t = lambda n: n // TN
        assert att % TN == 0 and kv == TN and idx % TN == 0 and pool % TN == 0
        self.q = 0
        self.qi = self.q + t(att)
        self.ga = self.qi + t(idx)
        self.ma = self.ga + t(att)
        self.mb = self.ma + t(d)
        self.gb = self.mb + t(d)
        self.k = self.gb + t(pool)
        self.v = self.k + 1
        self.misc = self.v + 1
        self.u = self.misc + 1
        self.end = self.u + t(pool)
        self.nb = self.u
        self.pool = pool


def _rope_tile(z, cos, sin):
    parts = []
    for s in range(z.shape[1] // HEAD_DIM):
        zs = z[:, s * HEAD_DIM:(s + 1) * HEAD_DIM]
        parts.append(zs * cos + pltpu.roll(zs, HEAD_DIM // 2, 1) * sin)
    return parts[0] if len(parts) == 1 else jnp.concatenate(parts, axis=1)


def _modulated(x, g, shift, scale):
    r = lax.rsqrt(jnp.mean(x * x, axis=-1, keepdims=True) + EPS)
    return ((x * r * g) * (1.0 + scale) + shift).astype(BF16)


def _emit_tile(lay, idx_scale, q_scale, j, w_ref, h_ref, cos_ref, sin_ref, zb_ref, k_ref, v_ref, misc_ref, u_ref,
               vt_ref, wt_ref):
    nt = (((1,), (1,)), ((), ()))
    with_vt = vt_ref is not None

    def tile(keep_rows=None):
        w = w_ref[...]
        if keep_rows is not None:
            w = jnp.where(lax.broadcasted_iota(jnp.int32, (TN, 1), 0) < keep_rows, w, 0.0)
        return lax.dot_general(h_ref[...], w.astype(BF16), nt, preferred_element_type=F32)

    def store_heads(ref, val):
        if len(ref.shape) == 2:
            ref[...] = val
        else:
            for g in range(N_KV_HEADS):
                ref[:, g, :] = val[:, g * HEAD_DIM:(g + 1) * HEAD_DIM]

    @pl.when(j < lay.qi)
    def _():
        zb_ref[...] = (_rope_tile(tile(), cos_ref[...], sin_ref[...]) * q_scale).astype(BF16)

    @pl.when((j >= lay.qi) & (j < lay.ga))
    def _():
        zb_ref[...] = _rope_tile(tile(), cos_ref[...], sin_ref[...]).astype(BF16)

    @pl.when(j == lay.k)
    def _():
        r = _rope_tile(tile(), cos_ref[...], sin_ref[...])
        zb_ref[...] = r.astype(BF16)
        store_heads(k_ref, r)

    @pl.when(((j >= lay.ga) & (j < lay.ma)) | ((j >= lay.gb) & (j < lay.k)))
    def _():
        z = tile()
        zb_ref[...] = (z * _sigmoid(z)).astype(BF16)

    @pl.when((j >= lay.ma) & (j < lay.gb))
    def _():
        zb_ref[...] = _sigmoid(tile()).astype(BF16)

    @pl.when(j == lay.v)
    def _():
        z = tile()
        zb_ref[...] = z.astype(BF16)
        store_heads(v_ref, z)
        if with_vt:
            vt_ref[0] = z.T.astype(BF16)

    @pl.when(j == lay.misc)
    def _():
        z = tile(keep_rows=IDX_DIM + IDX_HEADS)
        ki = _rope_tile(z[:, :IDX_DIM], cos_ref[...], sin_ref[...])
        m = jnp.concatenate([ki, z[:, IDX_DIM:] * idx_scale], axis=1)
        zb_ref[...] = m.astype(BF16)
        misc_ref[...] = m
        if with_vt:
            wt_ref[...] = m[:, IDX_DIM:2 * IDX_DIM].T

    @pl.when(j >= lay.u)
    def _():
        u_ref[...] = tile()


def _proj_kernel(lay, idx_scale, q_scale, starts_ref, g_ref, w_ref,
                 x_ref, shift_ref, scale_ref, cos_ref, sin_ref,
                 xs_ref, shift_s_ref, scale_s_ref, cos_s_ref, sin_s_ref,
                 zb_ref, k_ref, v_ref, misc_ref, u_ref, vt_ref, wt_ref,
                 zs_ref, ks_ref, vs_ref, miscs_ref, us_ref, h_ref, hs_ref):
    i, j = pl.program_id(0), pl.program_id(1)

    @pl.when(j == 0)
    def _():
        h_ref[...] = _modulated(x_ref[...], g_ref[...], shift_ref[0], scale_ref[0])

    @pl.when((i == 0) & (j == 0))
    def _():
        hs_ref[...] = _modulated(xs_ref[...], g_ref[...], shift_s_ref[0], scale_s_ref[0])

    emit = functools.partial(_emit_tile, lay, idx_scale, q_scale, j, w_ref)
    emit(h_ref, cos_ref, sin_ref, zb_ref, k_ref, v_ref, misc_ref, u_ref, vt_ref, wt_ref)

    @pl.when(i == 0)
    def _():
        emit(hs_ref, cos_s_ref, sin_s_ref, zs_ref, ks_ref, vs_ref, miscs_ref, us_ref, None, None)


def _project(xp, xs, g, mod_p, mod_s, w_t, starts, rope_p, rope_s, lay, tm, tiles_per_group, idx_scale):
    m, d = xp.shape
    ns = xs.shape[0]
    grp = mod_p[0].shape[0]
    nu = lay.end - lay.u
    kern = functools.partial(_proj_kernel, lay, idx_scale, Q_SCALE)
    once = dict(pipeline_mode=pl.Buffered(1))
    zb_col = lambda j: jnp.minimum(j, lay.nb - 1)
    u_col = lambda j: jnp.clip(j - lay.u, 0, nu - 1)
    first = lambda i, col, last: jnp.where(i == 0, col, last)
    row_p = lambda shape: pl.BlockSpec(shape, lambda i, j, st: (i,) + (0,) * (len(shape) - 1), **once)
    fixed = lambda shape: pl.BlockSpec(shape, lambda i, j, st: (0,) * len(shape))
    grp_p = pl.BlockSpec((1, 1, d), lambda i, j, st: (i // tiles_per_group, 0, 0))
    pos_p = pl.BlockSpec((tm, HEAD_DIM), lambda i, j, st: (i % tiles_per_group, 0))
    out_specs = [pl.BlockSpec((tm, TN), lambda i, j, st: (i, zb_col(j))),
                 row_p((tm, N_KV_HEADS, HEAD_DIM)),
                 row_p((tm, N_KV_HEADS, HEAD_DIM)),
                 row_p((tm, TN)),
                 pl.BlockSpec((tm, TN), lambda i, j, st: (i, u_col(j))),
                 pl.BlockSpec((1, TN, tm), lambda i, j, st: (i // tiles_per_group, 0, i % tiles_per_group), **once),
                 pl.BlockSpec((IDX_DIM, tm), lambda i, j, st: (0, i), **once),
                 pl.BlockSpec((ns, TN), lambda i, j, st: (0, first(i, zb_col(j), lay.nb - 1))),
                 fixed((ns, TN)), fixed((ns, TN)), fixed((ns, TN)),
                 pl.BlockSpec((ns, TN), lambda i, j, st: (0, first(i, u_col(j), nu - 1)))]
    out_shape = [jax.ShapeDtypeStruct((m, lay.nb * TN), BF16),
                 jax.ShapeDtypeStruct((m, N_KV_HEADS, HEAD_DIM), F32),
                 jax.ShapeDtypeStruct((m, N_KV_HEADS, HEAD_DIM), F32),
                 jax.ShapeDtypeStruct((m, TN), F32),
                 jax.ShapeDtypeStruct((m, nu * TN), F32),
                 jax.ShapeDtypeStruct((grp, TN, tiles_per_group * tm), BF16),
                 jax.ShapeDtypeStruct((IDX_DIM, m), F32),
                 jax.ShapeDtypeStruct((ns, lay.nb * TN), BF16),
                 jax.ShapeDtypeStruct((ns, TN), F32),
                 jax.ShapeDtypeStruct((ns, TN), F32),
                 jax.ShapeDtypeStruct((ns, TN), F32),
                 jax.ShapeDtypeStruct((ns, nu * TN), F32)]
    outs = pl.pallas_call(
        kern,
        grid_spec=pltpu.PrefetchScalarGridSpec(
            num_scalar_prefetch=1,
            grid=(m // tm, lay.end),
            in_specs=[fixed((1, d)),
                      pl.BlockSpec((pl.Element(TN), pl.Element(d)),
                                   lambda i, j, st: (pl.multiple_of(st[j], SUBLANES), 0)),
                      pl.BlockSpec((tm, d), lambda i, j, st: (i, 0)), grp_p, grp_p, pos_p, pos_p,
                      fixed((ns, d)), fixed((1, ns, d)), fixed((1, ns, d)),
                      fixed((ns, HEAD_DIM)), fixed((ns, HEAD_DIM))],
            out_specs=out_specs,
            scratch_shapes=[pltpu.VMEM((tm, d), BF16), pltpu.VMEM((ns, d), BF16)]),
        out_shape=out_shape,
        compiler_params=_cparams(("arbitrary", "arbitrary")),
    )(starts, g, w_t, xp, *mod_p, *rope_p, xs, *mod_s, *rope_s)
    return outs[:7], outs[7:]


def _bisect(count_ge, lo, hi, clo, kf):
    def cond(st):
        return (st[4] > 0.0) & (st[5] < MAX_BISECT)

    def halve(lo, hi, clo, done):
        mid = 0.5 * lo + 0.5 * hi
        mid = jnp.where(mid <= lo, hi, mid)
        c = count_ge(mid)
        ge = c >= kf
        stuck = (mid >= hi) & jnp.logical_not(ge)
        lo2 = jnp.where(ge, mid, lo)
        hi2 = jnp.where(ge, hi, mid)
        clo2 = jnp.where(ge, c, clo)
        fin = (clo2 <= kf) | (lo2 >= hi2) | stuck
        return lo2, hi2, clo2, jnp.maximum(done, jnp.where(fin, 1.0, 0.0))

    def body(st):
        lo, hi, clo, done = halve(*halve(*st[:4]))
        return lo, hi, clo, done, jnp.sum(1.0 - done), st[5] + 1

    done0 = jnp.where((clo <= kf) | (lo >= hi), 1.0, 0.0)
    st = (lo, hi, clo, done0, jnp.sum(1.0 - done0), jnp.int32(0))
    out = lax.while_loop(cond, body, st)
    return out[0], out[2]


def _attn_kernel(topk, q_ref, qi_ref, sga_ref, k_ref, ki_ref, vt_ref, wt_ref, o_ref,
                 score_ref, bias_ref, cut_ref, m_ref, l_ref, alpha_ref, acc_ref, lg_ref):
    i = pl.program_id(1)
    nck = i + 1
    n_heads = q_ref.shape[1] // HEAD_DIM
    group = n_heads // N_KV_HEADS
    nt = (((1,), (1,)), ((), ()))
    q_pos = i * TQ + lax.broadcasted_iota(jnp.int32, (1, TQ), 1)
    fold = lambda a: a.reshape(CK // SUBLANES, SUBLANES, TQ)

    def score_chunk(c, carry):
        mn, mx = carry
        off = pl.multiple_of(c * CK, CK)
        kc = ki_ref[pl.ds(off, CK), :]
        acc = jnp.zeros((CK, TQ), F32)
        for h in range(IDX_HEADS):
            s = lax.dot_general(kc, qi_ref[:, h * IDX_DIM:(h + 1) * IDX_DIM], nt,
                                preferred_element_type=F32)
            acc = acc + wt_ref[h:h + 1, :] * jnp.maximum(s, 0.0)
        k_pos = off + lax.broadcasted_iota(jnp.int32, (CK, 1), 0)
        causal = k_pos <= q_pos
        score_ref[pl.ds(off, CK), :] = jnp.where(causal, acc, -jnp.inf)
        mn = jnp.minimum(mn, fold(jnp.where(causal, acc, jnp.inf)).min(axis=0))
        mx = jnp.maximum(mx, fold(jnp.where(causal, acc, -jnp.inf)).max(axis=0))
        return mn, mx

    mn, mx = lax.fori_loop(0, nck, score_chunk,
                           (jnp.full((SUBLANES, TQ), jnp.inf, F32),
                            jnp.full((SUBLANES, TQ), -jnp.inf, F32)))
    lo = mn.min(axis=0, keepdims=True)
    hi = mx.max(axis=0, keepdims=True)

    ones_rows = jnp.ones((SUBLANES, CK), BF16)

    def count_ge(thr):
        def body(c, cnt):
            s = score_ref[pl.ds(pl.multiple_of(c * CK, CK), CK), :]
            return cnt + fold(jnp.where(s >= thr, 1.0, 0.0)).sum(axis=0)
        cnt = lax.fori_loop(0, nck, body, jnp.zeros((SUBLANES, TQ), F32))
        return cnt.sum(axis=0, keepdims=True)

    n_valid = (q_pos + 1).astype(F32)
    kf = float(topk)
    thr, n_ge = _bisect(count_ge, lo, hi, n_valid, kf)

    s_len = score_ref.shape[0]
    has_ties = jnp.sum(jnp.where(n_ge > kf, 1.0, 0.0)) > 0.0
    cut_ref[...] = jnp.full(cut_ref.shape, float(s_len), F32)

    @pl.when(has_ties)
    def _():
        def count(pred):
            def body(c, cnt):
                off = pl.multiple_of(c * CK, CK)
                k_pos = (off + lax.broadcasted_iota(jnp.int32, (CK, 1), 0)).astype(F32)
                hit = pred(score_ref[pl.ds(off, CK), :], k_pos)
                return cnt + fold(jnp.where(hit, 1.0, 0.0)).sum(axis=0)
            cnt = lax.fori_loop(0, nck, body, jnp.zeros((SUBLANES, TQ), F32))
            return cnt.sum(axis=0, keepdims=True)

        need = kf - count(lambda s, kp: s > thr)

        def step(_, st):
            lo_i, hi_i = st
            mid = jnp.floor(0.5 * (lo_i + hi_i))
            ge = count(lambda s, kp: (s == thr) & (kp <= mid)) >= need
            return jnp.where(ge, lo_i, mid), jnp.where(ge, mid, hi_i)

        st0 = (jnp.full((1, TQ), -1.0, F32), jnp.full((1, TQ), s_len - 1.0, F32))
        cut = lax.fori_loop(0, (s_len - 1).bit_length() + 1, step, st0)[1]
        cut_ref[...] = jnp.broadcast_to(cut, cut_ref.shape)

    m_ref[...] = jnp.full(m_ref.shape, NEG, F32)
    l_ref[...] = jnp.zeros(l_ref.shape, F32)
    acc_ref[...] = jnp.zeros(acc_ref.shape, F32)

    def att_chunk(c, carry):
        off = pl.multiple_of(c * CK, CK)
        sc = score_ref[pl.ds(off, CK), :]

        @pl.when(jnp.logical_not(has_ties))
        def _():
            bias_ref[...] = jnp.where(sc >= thr, 0.0, NEG)

        @pl.when(has_ties)
        def _():
            k_pos = (off + lax.broadcasted_iota(jnp.int32, (CK, 1), 0)).astype(F32)
            keep = (sc > thr) | ((sc == thr) & (k_pos <= cut_ref[0:1, :]))
            bias_ref[...] = jnp.where(keep, 0.0, NEG)

        kc = k_ref[pl.ds(off, CK), :]
        vtc = vt_ref[0, :, pl.ds(off, CK)]

        def qk(h):
            g = h // group
            return lax.dot_general(kc[:, g * HEAD_DIM:(g + 1) * HEAD_DIM],
                                   q_ref[:, h * HEAD_DIM:(h + 1) * HEAD_DIM], nt,
                                   preferred_element_type=F32)

        for h in range(n_heads):
            lg = qk(h) + bias_ref[...]
            m_old = m_ref[h:h + 1, :]
            m_new = jnp.maximum(m_old, lg.max(axis=0, keepdims=True))
            lg_ref[h] = lg
            alpha_ref[h:h + 1, :] = jnp.exp2(m_old - m_new)
            m_ref[h:h + 1, :] = m_new
        for h in range(n_heads):
            g = h // group
            p = jnp.exp2(lg_ref[h] - m_ref[h:h + 1, :]).astype(BF16)
            pv = jnp.dot(vtc[g * HEAD_DIM:(g + 1) * HEAD_DIM, :], p, preferred_element_type=F32)
            acc_ref[h] = alpha_ref[h:h + 1, :] * acc_ref[h] + pv
            psum = jnp.dot(ones_rows, p, preferred_element_type=F32)[0:1, :]
            l_ref[h:h + 1, :] = alpha_ref[h:h + 1, :] * l_ref[h:h + 1, :] + psum
        return carry

    lax.fori_loop(0, nck, att_chunk, 0)

    for h in range(n_heads):
        o = (acc_ref[h] / l_ref[h:h + 1, :]).T
        sl = slice(h * HEAD_DIM, (h + 1) * HEAD_DIM)
        o_ref[:, sl] = (o * sga_ref[:, sl].astype(F32)).astype(BF16)


def _prompt_attention(zb, vt, wt, lay, b, s, d, topk):
    nq = s // TQ
    n_heads = d // HEAD_DIM
    kern = functools.partial(_attn_kernel, topk)
    row = lambda bb, i: bb * nq + i
    return pl.pallas_call(
        kern,
        grid=(b, nq),
        in_specs=[pl.BlockSpec((TQ, d), lambda bb, i: (row(bb, i), lay.q * TN // d)),
                  pl.BlockSpec((TQ, d), lambda bb, i: (row(bb, i), lay.qi * TN // d)),
                  pl.BlockSpec((TQ, d), lambda bb, i: (row(bb, i), lay.ga * TN // d)),
                  pl.BlockSpec((s, TN), lambda bb, i: (bb, lay.k)),
                  pl.BlockSpec((s, IDX_DIM), lambda bb, i: (bb, lay.misc * TN // IDX_DIM)),
                  pl.BlockSpec((1, TN, s), lambda bb, i: (bb, 0, 0)),
                  pl.BlockSpec((IDX_HEADS, TQ), lambda bb, i: (0, row(bb, i)))],
        out_specs=pl.BlockSpec((TQ, d), lambda bb, i: (row(bb, i), 0)),
        out_shape=jax.ShapeDtypeStruct((b * s, d), BF16),
        scratch_shapes=[pltpu.VMEM((s, TQ), F32),
                        pltpu.VMEM((CK, TQ), F32),
                        pltpu.VMEM((2 * SUBLANES, TQ), F32),
                        pltpu.VMEM((n_heads, TQ), F32),
                        pltpu.VMEM((n_heads, TQ), F32),
                        pltpu.VMEM((n_heads, TQ), F32),
                        pltpu.VMEM((n_heads, HEAD_DIM, TQ), F32),
                        pltpu.VMEM((n_heads, CK, TQ), F32)],
        compiler_params=_cparams(("arbitrary", "arbitrary")),
    )(zb, zb, zb, zb, zb, vt, wt)


def _pool_kernel(ts, u_ref, sgb_ref, wp_ref, sp_ref, o_ref, *level_refs):
    i = pl.program_id(1)
    ext_ref = level_refs[0]
    halo = POOL_HALO
    n = halo + ts

    @pl.when(i == 0)
    def _():
        ext_ref[0:halo, :] = jnp.zeros((halo, ext_ref.shape[1]), F32)

    @pl.when(i > 0)
    def _():
        ext_ref[0:halo, :] = ext_ref[ts:ts + halo, :]

    ext_ref[halo:n, :] = u_ref[...]
    pos = i * ts + lax.broadcasted_iota(jnp.int32, (ts, 1), 0)
    gd = wp_ref.shape[1]
    top = len(POOL_WINDOWS) - 1
    for k in range(top):
        lo, sh, c0 = SUBLANES * (k + 1), POOL_WINDOWS[k] // 2, k * gd
        level_refs[k + 1][lo:n, c0:] = level_refs[k][lo:n, c0:] + level_refs[k][lo - sh:n - sh, c0:]
    sh = POOL_WINDOWS[top] // 2
    wins = [level_refs[g + 1][halo:n, g * gd:(g + 1) * gd] for g in range(top)]
    wins.append(level_refs[top][halo:n, top * gd:] + level_refs[top][halo - sh:n - sh, top * gd:])
    for g, w in enumerate(POOL_WINDOWS):
        cols = slice(g * gd, (g + 1) * gd)
        acc = wins[g]
        cnt = jnp.minimum(pos + 1, w).astype(F32)
        pooled = acc / cnt - u_ref[:, cols]
        mixed = jnp.dot(pooled.astype(BF16), wp_ref[g], preferred_element_type=F32) * sp_ref[:, cols]
        o_ref[:, cols] = (mixed * sgb_ref[:, cols].astype(F32)).astype(BF16)


def _pool_prompt(u, zb, wp, sp, lay, b, s, ts):
    m, pw = u.shape
    nt = s // ts
    kern = functools.partial(_pool_kernel, ts)
    return pl.pallas_call(
        kern,
        grid=(b, nt),
        in_specs=[pl.BlockSpec((ts, pw), lambda bb, i: (bb * nt + i, 0)),
                  pl.BlockSpec((ts, pw), lambda bb, i: (bb * nt + i, lay.gb * TN // pw)),
                  pl.BlockSpec(wp.shape, lambda bb, i: (0, 0, 0)),
                  pl.BlockSpec((1, pw), lambda bb, i: (0, 0))],
        out_specs=pl.BlockSpec((ts, pw), lambda bb, i: (bb * nt + i, 0)),
        out_shape=jax.ShapeDtypeStruct((m, pw), BF16),
        scratch_shapes=[pltpu.VMEM((POOL_HALO + ts, pw), F32) for _ in POOL_WINDOWS],
        compiler_params=_cparams(("arbitrary", "arbitrary")),
    )(u, zb, wp, sp)


def _pool_step_kernel(cnts, ext_ref, sgb_ref, wp_ref, sp_ref, o_ref):
    gd = wp_ref.shape[1]
    rows = ext_ref.shape[0]
    for g, w in enumerate(POOL_WINDOWS):
        cols = slice(g * gd, (g + 1) * gd)
        acc = ext_ref[rows - 1, :, cols]
        for jj in range(1, w):
            acc = acc + ext_ref[rows - 1 - jj, :, cols]
        pooled = acc / cnts[g] - ext_ref[rows - 1, :, cols]
        mixed = jnp.dot(pooled.astype(BF16), wp_ref[g], preferred_element_type=F32) * sp_ref[:, cols]
        o_ref[:, cols] = (mixed * sgb_ref[:, cols].astype(F32)).astype(BF16)


def _pool_step(ext_t, sgb, wp, sp, past):
    rows, n, pw = ext_t.shape
    cnts = tuple(float(min(past + 1, w)) for w in POOL_WINDOWS)
    return pl.pallas_call(
        functools.partial(_pool_step_kernel, cnts),
        out_shape=jax.ShapeDtypeStruct((n, pw), BF16),
    )(ext_t, sgb, wp, sp)


def _merge1_kernel(ab_ref, pb_ref, wa_ref, wb_ref, sma_ref, smb_ref, o_ref):
    a = jnp.dot(ab_ref[...], wa_ref[...], preferred_element_type=F32)
    bb = jnp.dot(pb_ref[...], wb_ref[...], preferred_element_type=F32)
    o_ref[...] = (sma_ref[...].astype(F32) * a + smb_ref[...].astype(F32) * bb).astype(BF16)


def _merge1(ab, pb, wa, wb, zb, lay, tm):
    m, d = ab.shape
    pw = pb.shape[1]
    return pl.pallas_call(
        _merge1_kernel,
        grid=(m // tm, d // TN),
        in_specs=[pl.BlockSpec((tm, d), lambda i, j: (i, 0)),
                  pl.BlockSpec((tm, pw), lambda i, j: (i, 0)),
                  pl.BlockSpec((d, TN), lambda i, j: (0, j)),
                  pl.BlockSpec((pw, TN), lambda i, j: (0, j)),
                  pl.BlockSpec((tm, TN), lambda i, j: (i, lay.ma + j)),
                  pl.BlockSpec((tm, TN), lambda i, j: (i, lay.mb + j))],
        out_specs=pl.BlockSpec((tm, TN), lambda i, j: (i, j)),
        out_shape=jax.ShapeDtypeStruct((m, d), BF16),
        compiler_params=_cparams(("arbitrary", "arbitrary")),
    )(ab, pb, wa, wb, zb, zb)


def _merge2_kernel(final, mm_ref, x_ref, gate_ref, wo_ref, gf_ref, o_ref):
    y = x_ref[...] + gate_ref[0] * jnp.dot(mm_ref[...], wo_ref[...], preferred_element_type=F32)
    if final:
        y = y * lax.rsqrt(jnp.mean(y * y, axis=-1, keepdims=True) + EPS) * gf_ref[...]
    o_ref[...] = y


def _merge2(mm, x, gate, wo, gf, tm, tiles_per_group, final):
    m, d = x.shape
    r = gate.shape[1]
    return pl.pallas_call(
        functools.partial(_merge2_kernel, final),
        grid=(m // tm,),
        in_specs=[pl.BlockSpec((tm, d), lambda i: (i, 0)),
                  pl.BlockSpec((tm, d), lambda i: (i, 0)),
                  pl.BlockSpec((1, r, d), lambda i: (i // tiles_per_group, 0, 0)),
                  pl.BlockSpec((d, d), lambda i: (0, 0)),
                  pl.BlockSpec((1, d), lambda i: (0, 0))],
        out_specs=pl.BlockSpec((tm, d), lambda i: (i, 0)),
        out_shape=jax.ShapeDtypeStruct((m, d), F32),
        compiler_params=_cparams(("arbitrary",)),
    )(mm, x, gate, wo, gf)


def _page_specs(width, npg, npp):
    return [pl.BlockSpec((1, PAGE_SIZE, width),
                         lambda b, p, pt, r=r: (pt[b * npg + p * npp + r], 0, 0)) for r in range(npp)]


def _sidx_kernel(npp, pt_ref, *refs):
    kidx_refs = refs[:npp]
    qi_ref, w_ref, o_ref = refs[npp:]
    nt = (((1,), (1,)), ((), ()))
    qi, w = qi_ref[0], w_ref[0]
    rows = []
    for r in range(npp):
        s = lax.dot_general(qi, kidx_refs[r][0].astype(BF16), nt, preferred_element_type=F32)
        rows.append(jnp.sum(w * jnp.maximum(s, 0.0), axis=0, keepdims=True))
    o_ref[0] = jnp.concatenate(rows, axis=1)


def _sample_scores(page_table, cache_kidx, qi3, wcol, npp):
    n, npg = page_table.shape
    return pl.pallas_call(
        functools.partial(_sidx_kernel, npp),
        grid_spec=pltpu.PrefetchScalarGridSpec(
            num_scalar_prefetch=1,
            grid=(n, npg // npp),
            in_specs=_page_specs(IDX_DIM, npg, npp) + [
                pl.BlockSpec((1, IDX_HEADS, IDX_DIM), lambda b, p, pt: (b, 0, 0)),
                pl.BlockSpec((1, IDX_HEADS, 1), lambda b, p, pt: (b, 0, 0))],
            out_specs=pl.BlockSpec((1, 1, npp * PAGE_SIZE), lambda b, p, pt: (b, 0, p))),
        out_shape=jax.ShapeDtypeStruct((n, 1, npg * PAGE_SIZE), F32),
        compiler_params=_cparams(("arbitrary", "arbitrary")),
    )(page_table.reshape(-1), *([cache_kidx] * npp), qi3, wcol)


def _sthr_kernel(topk, row_base, sc_ref, qi_ref, kin_ref, w_ref, pt_ref, pos_ref, cnt_ref, sel_ref,
                 thr_scr, new_scr):
    sc = sc_ref[...]
    kin = kin_ref[...].astype(BF16).astype(F32)
    s_new = jnp.sum(qi_ref[...].astype(F32) * kin, axis=2, keepdims=True)
    new = jnp.sum(w_ref[...] * jnp.maximum(s_new, 0.0), axis=1, keepdims=True)
    red = lambda f, a: f(f(a, axis=2, keepdims=True), axis=1, keepdims=True)
    lo = jnp.minimum(red(jnp.min, sc), new)
    hi = jnp.maximum(red(jnp.max, sc), new)

    def count_ge(thr):
        c = red(jnp.sum, jnp.where(sc >= thr, 1.0, 0.0))
        return c + jnp.where(new >= thr, 1.0, 0.0)

    n_all = jnp.full(new.shape, float(sc.shape[1] * sc.shape[2] + 1), F32)
    thr, _ = _bisect(count_ge, lo, hi, n_all, float(topk))
    thr_scr[...] = jnp.broadcast_to(thr, thr_scr.shape)
    new_scr[...] = jnp.broadcast_to(new, new_scr.shape)

    rows = sc.shape[1]
    ii = lambda shape, ax: lax.broadcasted_iota(jnp.int32, shape, ax)
    one = lambda m: jnp.where(m, 1.0, 0.0)
    tri_lane = one(ii((PAGE_SIZE, PAGE_SIZE), 0) <= ii((PAGE_SIZE, PAGE_SIZE), 1)).astype(BF16)
    lower = one(ii((rows, rows), 1) < ii((rows, rows), 0)).astype(BF16)
    upper = one(ii((rows, rows), 0) < ii((rows, rows), 1)).astype(BF16)
    ones8 = jnp.ones((SUBLANES, PAGE_SIZE), BF16)
    nt = (((1,), (1,)), ((), ()))
    total = lambda a: jnp.sum(jnp.sum(a, axis=1, keepdims=True), axis=0, keepdims=True)
    dotf = lambda a, b: jnp.dot(a.astype(BF16), b.astype(BF16), preferred_element_type=F32)
    kf = float(topk)
    j_col = ii((topk, 1), 0).astype(F32)
    c_row = ii((1, PAGE_SIZE), 1).astype(F32)

    def compact(b, carry):
        s = sc_ref[b]
        t = thr_scr[b][:, 0:1]
        nw = new_scr[b][:, 0:1]
        gt, eq = s > t, s == t
        need = kf - total(one(gt)) - one(nw > t)
        e_lane = dotf(one(eq), tri_lane)
        e_rank = e_lane + dotf(lower, e_lane)[:, PAGE_SIZE - 1:PAGE_SIZE]
        x = one(gt | (eq & (e_rank <= need)))
        sel_new = (nw > t) | ((nw == t) & (total(one(eq)) < need))
        x_lane = dotf(x, tri_lane)
        n_rows = lax.dot_general(ones8, x.astype(BF16), nt, preferred_element_type=F32)
        n_row = n_rows[0:1, :]
        start_row = dotf(n_rows, upper)[0:1, :]
        hit = one((start_row <= j_col) & (j_col < start_row + n_row))
        lane_sum = lambda a: jnp.sum(a, axis=1, keepdims=True)
        page = lane_sum(hit * pt_ref[b])
        target = j_col - lane_sum(hit * start_row) + 1.0
        in_row = dotf(hit, x_lane * x)
        off = lane_sum(jnp.where(in_row == target, c_row, 0.0))
        pos_ref[b] = (page * PAGE_SIZE + off + row_base).astype(jnp.int32)
        cnt_ref[b] = jnp.broadcast_to(total(x), cnt_ref.shape[1:])
        sel_ref[b] = jnp.broadcast_to(one(sel_new), sel_ref.shape[1:])
        return carry

    lax.fori_loop(0, sc.shape[0], compact, 0)


def _sample_select(sc, qi3, kin, wcol, pt_rows, topk, row_base):
    n = sc.shape[0]
    return pl.pallas_call(
        functools.partial(_sthr_kernel, topk, row_base),
        out_shape=[jax.ShapeDtypeStruct((n, topk, 1), jnp.int32),
                   jax.ShapeDtypeStruct((n, 1, PAGE_SIZE), F32),
                   jax.ShapeDtypeStruct((n, 1, PAGE_SIZE), F32)],
        scratch_shapes=[pltpu.VMEM((n, 1, PAGE_SIZE), F32),
                        pltpu.VMEM((n, 1, PAGE_SIZE), F32)],
        compiler_params=pltpu.CompilerParams(vmem_limit_bytes=VMEM_LIMIT),
    )(sc, qi3, kin, wcol, pt_rows)


def _row_copy(cache_rows, buf, sem, row, j):
    return pltpu.make_async_copy(cache_rows.at[row], buf.at[j], sem)


def _sattn_kernel(pos_ref, ck_hbm, cv_hbm, cnt_ref, sel_ref, q_ref, kn_ref, vn_ref,
                  sga_ref, o_ref, kbuf, vbuf, ksem, vsem):
    b = pl.program_id(0)
    slot = b % 2
    topk = kbuf.shape[1]
    n_heads = q_ref.shape[1]
    group = n_heads // N_KV_HEADS
    nt = (((1,), (1,)), ((), ()))

    def fetch(bb, s):
        def issue(j, carry):
            row = pos_ref[bb * topk + j]
            _row_copy(ck_hbm, kbuf.at[s], ksem.at[s], row, j).start(priority=0)
            _row_copy(cv_hbm, vbuf.at[s], vsem.at[s], row, j).start(priority=1)
            return carry
        lax.fori_loop(0, topk, issue, 0, unroll=8)

    @pl.when(b == 0)
    def _():
        fetch(b, slot)

    @pl.when(b + 1 < pl.num_programs(0))
    def _():
        fetch(b + 1, 1 - slot)

    kbuf, vbuf = kbuf.at[slot], vbuf.at[slot]
    if topk % PAGE_SIZE == 0:
        for w in range(topk // PAGE_SIZE):
            rows = pl.ds(w * PAGE_SIZE, PAGE_SIZE)
            pltpu.make_async_copy(ck_hbm.at[rows], kbuf.at[rows], ksem.at[slot]).wait()
            pltpu.make_async_copy(cv_hbm.at[rows], vbuf.at[rows], vsem.at[slot]).wait()
    else:
        def wait(j, carry):
            _row_copy(ck_hbm, kbuf, ksem.at[slot], 0, j).wait()
            _row_copy(cv_hbm, vbuf, vsem.at[slot], 0, j).wait()
            return carry

        lax.fori_loop(0, topk, wait, 0)

    q = q_ref[0]
    head = lax.broadcasted_iota(jnp.int32, (n_heads, 1), 0)
    in_group = [(head >= g * group) & (head < (g + 1) * group) for g in range(N_KV_HEADS)]
    lg = jnp.zeros((n_heads, topk), F32)
    for g in range(N_KV_HEADS):
        lg_g = lax.dot_general(q, kbuf[:, g, :].astype(BF16), nt, preferred_element_type=F32)
        lg = lg + jnp.where(in_group[g], lg_g, 0.0)
    valid = lax.broadcasted_iota(jnp.int32, (1, topk), 1).astype(F32) < cnt_ref[0][:, 0:1]
    lg = jnp.where(valid, lg, NEG)
    kn = kn_ref[0].astype(BF16).astype(F32)
    vn = vn_ref[0].astype(BF16).astype(F32)
    lgn = jnp.sum(q.astype(F32) * kn, axis=1, keepdims=True)
    lgn = jnp.where(sel_ref[0][:, 0:1] > 0.0, lgn, NEG)
    m = jnp.maximum(lg.max(axis=1, keepdims=True), lgn)
    p = jnp.exp2(lg - m)
    pn = jnp.exp2(lgn - m)
    l = p.sum(axis=1, keepdims=True) + pn
    pb = p.astype(BF16)
    acc = pn.astype(BF16).astype(F32) * vn
    for g in range(N_KV_HEADS):
        pv = jnp.dot(pb, vbuf[:, g, :].astype(BF16), preferred_element_type=F32)
        acc = acc + jnp.where(in_group[g], pv, 0.0)
    o_ref[0] = ((acc / l) * sga_ref[0].astype(F32)).astype(BF16)


def _sample_attention(k_rows, v_rows, pos, cnt, sel, q3, kn, vn, sga3):
    n, n_heads = q3.shape[:2]
    topk = pos.shape[1]
    per_b = lambda shape: pl.BlockSpec(shape, lambda b, ps: (b, 0, 0))
    hbm = pl.BlockSpec(memory_space=pl.ANY)
    return pl.pallas_call(
        _sattn_kernel,
        grid_spec=pltpu.PrefetchScalarGridSpec(
            num_scalar_prefetch=1,
            grid=(n,),
            in_specs=[hbm, hbm,
                      per_b((1, 1, PAGE_SIZE)),
                      per_b((1, 1, PAGE_SIZE)),
                      per_b((1, n_heads, HEAD_DIM)),
                      per_b((1, n_heads, HEAD_DIM)),
                      per_b((1, n_heads, HEAD_DIM)),
                      per_b((1, n_heads, HEAD_DIM))],
            out_specs=per_b((1, n_heads, HEAD_DIM)),
            scratch_shapes=[pltpu.VMEM((2, topk, N_KV_HEADS, HEAD_DIM), k_rows.dtype),
                            pltpu.VMEM((2, topk, N_KV_HEADS, HEAD_DIM), v_rows.dtype),
                            pltpu.SemaphoreType.DMA((2,)),
                            pltpu.SemaphoreType.DMA((2,))]),
        out_shape=jax.ShapeDtypeStruct((n, n_heads, HEAD_DIM), BF16),
        compiler_params=_cparams(("arbitrary",)),
    )(pos.reshape(-1), k_rows, v_rows, cnt, sel, q3, kn, vn, sga3)


def _rope_tables(pos):
    half = HEAD_DIM // 2
    inv = ROPE_THETA ** (-jnp.arange(half, dtype=F32) / half)
    ang = pos.astype(F32)[:, None] * inv[None, :]
    cos, sin = jnp.cos(ang), jnp.sin(ang)
    return jnp.concatenate([cos, cos], axis=1), jnp.concatenate([-sin, sin], axis=1)


def _tile_starts(d, lay):
    att, kv, idx, pool = d, N_KV_HEADS * HEAD_DIM, IDX_HEADS * IDX_DIM, d // 2
    sizes = (att, kv, kv, idx, IDX_DIM, IDX_HEADS, att, pool, pool, d, d)
    offs = [0]
    for n in sizes:
        offs.append(offs[-1] + n)
    q, k, v, qi, ki, wi, ga, u, gb, ma, mb = offs[:-1]
    starts = [0] * lay.end
    for first, tiles, col in ((lay.q, lay.qi - lay.q, q), (lay.qi, lay.ga - lay.qi, qi),
                              (lay.ga, lay.ma - lay.ga, ga), (lay.ma, lay.mb - lay.ma, ma),
                              (lay.mb, lay.gb - lay.mb, mb), (lay.gb, lay.k - lay.gb, gb),
                              (lay.k, 1, k), (lay.v, 1, v), (lay.misc, 1, ki), (lay.u, lay.end - lay.u, u)):
        for t in range(tiles):
            starts[first + t] = col + t * TN
    assert wi == ki + IDX_DIM and all(s + TN <= offs[-1] and s % SUBLANES == 0 for s in starts)
    return starts


def kernel(x_prompt, x_sample, c_prompt, c_sample, cache_k, cache_v, cache_kidx, state_pool, page_table,
           g_norm, w_ada, b_ada, w_in, w_pool, s_pool, w_a_proj, w_b_proj, w_out, g_final):
    bp, sp, d = x_prompt.shape
    ns, ds, _ = x_sample.shape
    depth = g_norm.shape[0]
    npg = page_table.shape[1]
    past = npg * PAGE_SIZE
    assert ds == 1 and d % (N_KV_HEADS * HEAD_DIM) == 0 and sp % TQ == 0 and TQ == CK and sp >= POOL_STATE
    n_heads = d // HEAD_DIM
    kvw = N_KV_HEADS * HEAD_DIM
    lay = _Layout(d)
    pw = lay.pool
    idx_scale = (IDX_HEADS * IDX_DIM) ** -0.5
    topk_p = min(TOPK_MAX, sp // 4)
    topk_s = min(TOPK_MAX, (past + ds) // 4)
    tm = math.gcd(sp, ROW_TILE)
    ts = math.gcd(sp, POOL_TILE)
    tm2 = math.gcd(sp, OUT_TILE)

    cos_p, sin_p = _rope_tables(jnp.arange(sp, dtype=jnp.int32))
    cos_s, sin_s = _rope_tables(jnp.full((ns,), past, jnp.int32))
    n_c = bp + ns
    c_rows = -(-n_c // SUBLANES) * SUBLANES
    c_all = jnp.concatenate([c_prompt, c_sample, jnp.zeros((c_rows - n_c, d), F32)], axis=0)

    n_pool = cache_kidx.shape[1]
    kidx_pages = cache_kidx.reshape(depth * n_pool, PAGE_SIZE, IDX_DIM)
    npp = math.gcd(npg, PAGES_PER_STEP)
    k_rows = cache_k.reshape(depth * n_pool * PAGE_SIZE, N_KV_HEADS, HEAD_DIM)
    v_rows = cache_v.reshape(depth * n_pool * PAGE_SIZE, N_KV_HEADS, HEAD_DIM)
    pt_rows = page_table.astype(F32).reshape(ns, 1, npg)
    assert npg % npp == 0

    in_width = w_in.shape[2]
    w_t = jnp.swapaxes(w_in, 1, 2).reshape(depth * in_width, d)

    xp = x_prompt.reshape(bp * sp, d)
    xs = x_sample.reshape(ns, d)
    outs = {n: [] for n in ("kp", "vp", "kip", "pp", "ks", "vs", "kis", "ps")}
    for l in range(depth):
        final = l == depth - 1
        starts = jnp.asarray([l * in_width + s for s in _tile_starts(d, lay)], jnp.int32)
        wa, wb, wo, wpl = (w_a_proj[l].astype(BF16), w_b_proj[l].astype(BF16), w_out[l].astype(BF16),
                           w_pool[l].astype(BF16))
        g = g_norm[l].reshape(1, d)
        spl = s_pool[l].reshape(1, pw)
        gf = g_final.reshape(1, d)

        mod = _ada(c_all, w_ada[l], b_ada[l])
        shift, scale, gate = mod[:, :d], mod[:, d:2 * d], mod[:, 2 * d:]
        pg = lambda a: a[:bp].reshape(bp, 1, d)
        sg = lambda a: a[bp:n_c].reshape(1, ns, d)

        (zb, k_p, v_p, misc_p, u_p, vt, wt), (zs, k_s, v_s, misc_s, u_s) = _project(
            xp, xs, g, (pg(shift), pg(scale)), (sg(shift), sg(scale)), w_t, starts,
            (cos_p, sin_p), (cos_s, sin_s), lay, tm, sp // tm, idx_scale)

        ab = _prompt_attention(zb, vt, wt, lay, bp, sp, d, topk_p)
        pb = _pool_prompt(u_p, zb, wpl, spl, lay, bp, sp, ts)
        mm = _merge1(ab, pb, wa, wb, zb, lay, tm)
        xp = _merge2(mm, xp, pg(gate), wo, gf, tm2, sp // tm2, final)
        outs["kp"].append(k_p.reshape(bp, sp, N_KV_HEADS, HEAD_DIM))
        outs["vp"].append(v_p.reshape(bp, sp, N_KV_HEADS, HEAD_DIM))
        outs["kip"].append(misc_p[:, :IDX_DIM].reshape(bp, sp, IDX_DIM))
        outs["pp"].append(u_p.reshape(bp, sp, pw)[:, sp - POOL_STATE:, :])

        col = lambda t, n: zs[:, t * TN:t * TN + n]
        q3 = col(lay.q, d).reshape(ns, n_heads, HEAD_DIM)
        qi3 = col(lay.qi, IDX_HEADS * IDX_DIM).reshape(ns, IDX_HEADS, IDX_DIM)
        sga3 = col(lay.ga, d).reshape(ns, n_heads, HEAD_DIM)
        wcol = misc_s[:, IDX_DIM:IDX_DIM + IDX_HEADS].reshape(ns, IDX_HEADS, 1)
        kin = misc_s[:, :IDX_DIM].reshape(ns, 1, IDX_DIM)
        pt_l = page_table + l * n_pool
        sc = _sample_scores(pt_l, kidx_pages, qi3, wcol, npp)
        pos, cnt, sel = _sample_select(sc.reshape(ns, npg, PAGE_SIZE), qi3, kin, wcol, pt_rows, topk_s,
                                       l * n_pool * PAGE_SIZE)
        rep = lambda a: jnp.repeat(a.reshape(ns, N_KV_HEADS, HEAD_DIM), n_heads // N_KV_HEADS, axis=1)
        ab_s = _sample_attention(k_rows, v_rows, pos, cnt, sel, q3, rep(k_s), rep(v_s), sga3).reshape(ns, d)
        ext = jnp.concatenate([state_pool[l], u_s[:, None, :]], axis=1)
        pb_s = _pool_step(ext.transpose(1, 0, 2), col(lay.gb, pw), wpl, spl, past)
        mm_s = _merge1(ab_s, pb_s, wa, wb, zs, lay, ns)
        xs = _merge2(mm_s, xs, sg(gate), wo, gf, ns, 1, final)
        outs["ks"].append(k_s.reshape(ns, ds, N_KV_HEADS, HEAD_DIM))
        outs["vs"].append(v_s.reshape(ns, ds, N_KV_HEADS, HEAD_DIM))
        outs["kis"].append(misc_s[:, :IDX_DIM].reshape(ns, ds, IDX_DIM))
        outs["ps"].append(ext[:, 1:, :])

    st = lambda n: outs[n][0][None] if depth == 1 else jnp.stack(outs[n])
    return (xp.reshape(bp, sp, d), xs.reshape(ns, ds, d),
            st("kp"), st("vp"), st("kip"), st("pp"), st("ks"), st("vs"), st("kis"), st("ps"))
```

```python
import functools
import math

import jax
import jax.numpy as jnp
from jax import lax
from jax.experimental import pallas as pl
from jax.experimental.pallas import tpu as pltpu

F32 = jnp.float32
BF16 = jnp.bfloat16

HEAD_DIM = 128
N_KV_HEADS = 4
IDX_HEADS = 16
IDX_DIM = 128
TOPK_MAX = 256
PAGE_SIZE = 128
POOL_WINDOWS = (2, 4, 8, 16)
POOL_STATE = max(POOL_WINDOWS) - 1
ROPE_THETA = 10000.0
EPS = 1e-6

V7X_VMEM_BYTES = 64 * 1024 * 1024
V7X_MXU_COLUMNS = 256
SUBLANES = 8
TN = 512
ROW_TILE = 1024
POOL_TILE = 512
assert all(w == 2 << g for g, w in enumerate(POOL_WINDOWS)) and max(POOL_WINDOWS) // 2 <= SUBLANES
POOL_HALO = SUBLANES * len(POOL_WINDOWS)
OUT_TILE = 512
TQ = V7X_MXU_COLUMNS
CK = 256
NEG = -1e30
Q_SCALE = 1.4426950408889634 * HEAD_DIM ** -0.5
PAGES_PER_STEP = 64
MAX_BISECT = 512
FIXED_TRIPS = 6
VMEM_LIMIT = V7X_VMEM_BYTES * 7 // 8


def _cparams(sem):
    return pltpu.CompilerParams(dimension_semantics=sem, vmem_limit_bytes=VMEM_LIMIT)


def _sigmoid(z):
    return 0.5 * jnp.tanh(0.5 * z) + 0.5


def _ada_kernel(c_ref, w_ref, b_ref, o_ref):
    o_ref[...] = jnp.dot(c_ref[...].astype(BF16), w_ref[...].astype(BF16),
                         preferred_element_type=F32) + b_ref[...]


def _ada(c, w, b):
    r, d = c.shape
    n = w.shape[1]
    return pl.pallas_call(
        _ada_kernel,
        grid=(n // TN,),
        in_specs=[pl.BlockSpec((r, d), lambda j: (0, 0)),
                  pl.BlockSpec((d, TN), lambda j: (0, j)),
                  pl.BlockSpec((1, TN), lambda j: (0, j))],
        out_specs=pl.BlockSpec((r, TN), lambda j: (0, j)),
        out_shape=jax.ShapeDtypeStruct((r, n), F32),
        compiler_params=_cparams(("arbitrary",)),
    )(c, w, b.reshape(1, n))


class _Layout:
    def __init__(self, d):
        att, kv, idx, pool = d, N_KV_HEADS * HEAD_DIM, IDX_HEADS * IDX_DIM, d // 2
        t = lambda n: n // TN
        assert att % TN == 0 and kv == TN and idx % TN == 0 and pool % TN == 0
        self.q = 0
        self.qi = self.q + t(att)
        self.ga = self.qi + t(idx)
        self.ma = self.ga + t(att)
        self.mb = self.ma + t(d)
        self.gb = self.mb + t(d)
        self.k = self.gb + t(pool)
        self.v = self.k + 1
        self.misc = self.v + 1
        self.u = self.misc + 1
        self.end = self.u + t(pool)
        self.nb = self.u
        self.pool = pool


def _rope_tile(z, cos, sin):
    parts = []
    for s in range(z.shape[1] // HEAD_DIM):
        zs = z[:, s * HEAD_DIM:(s + 1) * HEAD_DIM]
        parts.append(zs * cos + pltpu.roll(zs, HEAD_DIM // 2, 1) * sin)
    return parts[0] if len(parts) == 1 else jnp.concatenate(parts, axis=1)


def _modulated(x, g, shift, scale):
    r = lax.rsqrt(jnp.mean(x * x, axis=-1, keepdims=True) + EPS)
    return ((x * r * g) * (1.0 + scale) + shift).astype(BF16)


def _emit_tile(lay, idx_scale, q_scale, j, w_ref, h_ref, cos_ref, sin_ref, zb_ref, k_ref, v_ref, misc_ref, u_ref,
               vt_ref, wt_ref):
    nt = (((1,), (1,)), ((), ()))
    with_vt = vt_ref is not None

    def tile(keep_rows=None):
        w = w_ref[...]
        if keep_rows is not None:
            w = jnp.where(lax.broadcasted_iota(jnp.int32, (TN, 1), 0) < keep_rows, w, 0.0)
        return lax.dot_general(h_ref[...], w.astype(BF16), nt, preferred_element_type=F32)

    def store_heads(ref, val):
        if len(ref.shape) == 2:
            ref[...] = val
        else:
            for g in range(N_KV_HEADS):
                ref[:, g, :] = val[:, g * HEAD_DIM:(g + 1) * HEAD_DIM]

    @pl.when(j < lay.qi)
    def _():
        zb_ref[...] = (_rope_tile(tile(), cos_ref[...], sin_ref[...]) * q_scale).astype(BF16)

    @pl.when((j >= lay.qi) & (j < lay.ga))
    def _():
        zb_ref[...] = _rope_tile(tile(), cos_ref[...], sin_ref[...]).astype(BF16)

    @pl.when(j == lay.k)
    def _():
        r = _rope_tile(tile(), cos_ref[...], sin_ref[...])
        zb_ref[...] = r.astype(BF16)
        store_heads(k_ref, r)

    @pl.when(((j >= lay.ga) & (j < lay.ma)) | ((j >= lay.gb) & (j < lay.k)))
    def _():
        z = tile()
        zb_ref[...] = (z * _sigmoid(z)).astype(BF16)

    @pl.when((j >= lay.ma) & (j < lay.gb))
    def _():
        zb_ref[...] = _sigmoid(tile()).astype(BF16)

    @pl.when(j == lay.v)
    def _():
        z = tile()
        zb_ref[...] = z.astype(BF16)
        store_heads(v_ref, z)
        if with_vt:
            vt_ref[0] = z.T.astype(BF16)

    @pl.when(j == lay.misc)
    def _():
        z = tile(keep_rows=IDX_DIM + IDX_HEADS)
        ki = _rope_tile(z[:, :IDX_DIM], cos_ref[...], sin_ref[...])
        m = jnp.concatenate([ki, z[:, IDX_DIM:] * idx_scale], axis=1)
        zb_ref[...] = m.astype(BF16)
        misc_ref[...] = m
        if with_vt:
            wt_ref[...] = m[:, IDX_DIM:2 * IDX_DIM].T

    @pl.when(j >= lay.u)
    def _():
        u_ref[...] = tile()


def _proj_kernel(lay, idx_scale, q_scale, starts_ref, g_ref, w_ref,
                 x_ref, shift_ref, scale_ref, cos_ref, sin_ref,
                 xs_ref, shift_s_ref, scale_s_ref, cos_s_ref, sin_s_ref,
                 zb_ref, k_ref, v_ref, misc_ref, u_ref, vt_ref, wt_ref,
                 zs_ref, ks_ref, vs_ref, miscs_ref, us_ref, h_ref, hs_ref):
    i, j = pl.program_id(0), pl.program_id(1)

    @pl.when(j == 0)
    def _():
        h_ref[...] = _modulated(x_ref[...], g_ref[...], shift_ref[0], scale_ref[0])

    @pl.when((i == 0) & (j == 0))
    def _():
        hs_ref[...] = _modulated(xs_ref[...], g_ref[...], shift_s_ref[0], scale_s_ref[0])

    emit = functools.partial(_emit_tile, lay, idx_scale, q_scale, j, w_ref)
    emit(h_ref, cos_ref, sin_ref, zb_ref, k_ref, v_ref, misc_ref, u_ref, vt_ref, wt_ref)

    @pl.when(i == 0)
    def _():
        emit(hs_ref, cos_s_ref, sin_s_ref, zs_ref, ks_ref, vs_ref, miscs_ref, us_ref, None, None)


def _project(xp, xs, g, mod_p, mod_s, w_t, starts, rope_p, rope_s, lay, tm, tiles_per_group, idx_scale):
    m, d = xp.shape
    ns = xs.shape[0]
    grp = mod_p[0].shape[0]
    nu = lay.end - lay.u
    kern = functools.partial(_proj_kernel, lay, idx_scale, Q_SCALE)
    once = dict(pipeline_mode=pl.Buffered(1))
    zb_col = lambda j: jnp.minimum(j, lay.nb - 1)
    u_col = lambda j: jnp.clip(j - lay.u, 0, nu - 1)
    first = lambda i, col, last: jnp.where(i == 0, col, last)
    row_p = lambda shape: pl.BlockSpec(shape, lambda i, j, st: (i,) + (0,) * (len(shape) - 1), **once)
    fixed = lambda shape: pl.BlockSpec(shape, lambda i, j, st: (0,) * len(shape))
    grp_p = pl.BlockSpec((1, 1, d), lambda i, j, st: (i // tiles_per_group, 0, 0))
    pos_p = pl.BlockSpec((tm, HEAD_DIM), lambda i, j, st: (i % tiles_per_group, 0))
    out_specs = [pl.BlockSpec((tm, TN), lambda i, j, st: (i, zb_col(j))),
                 row_p((tm, N_KV_HEADS, HEAD_DIM)),
                 row_p((tm, N_KV_HEADS, HEAD_DIM)),
                 row_p((tm, TN)),
                 pl.BlockSpec((tm, TN), lambda i, j, st: (i, u_col(j))),
                 pl.BlockSpec((1, TN, tm), lambda i, j, st: (i // tiles_per_group, 0, i % tiles_per_group), **once),
                 pl.BlockSpec((IDX_DIM, tm), lambda i, j, st: (0, i), **once),
                 pl.BlockSpec((ns, TN), lambda i, j, st: (0, first(i, zb_col(j), lay.nb - 1))),
                 fixed((ns, TN)), fixed((ns, TN)), fixed((ns, TN)),
                 pl.BlockSpec((ns, TN), lambda i, j, st: (0, first(i, u_col(j), nu - 1)))]
    out_shape = [jax.ShapeDtypeStruct((m, lay.nb * TN), BF16),
                 jax.ShapeDtypeStruct((m, N_KV_HEADS, HEAD_DIM), F32),
                 jax.ShapeDtypeStruct((m, N_KV_HEADS, HEAD_DIM), F32),
                 jax.ShapeDtypeStruct((m, TN), F32),
                 jax.ShapeDtypeStruct((m, nu * TN), F32),
                 jax.ShapeDtypeStruct((grp, TN, tiles_per_group * tm), BF16),
                 jax.ShapeDtypeStruct((IDX_DIM, m), F32),
                 jax.ShapeDtypeStruct((ns, lay.nb * TN), BF16),
                 jax.ShapeDtypeStruct((ns, TN), F32),
                 jax.ShapeDtypeStruct((ns, TN), F32),
                 jax.ShapeDtypeStruct((ns, TN), F32),
                 jax.ShapeDtypeStruct((ns, nu * TN), F32)]
    outs = pl.pallas_call(
        kern,
        grid_spec=pltpu.PrefetchScalarGridSpec(
            num_scalar_prefetch=1,
            grid=(m // tm, lay.end),
            in_specs=[fixed((1, d)),
                      pl.BlockSpec((pl.Element(TN), pl.Element(d)),
                                   lambda i, j, st: (pl.multiple_of(st[j], SUBLANES), 0)),
                      pl.BlockSpec((tm, d), lambda i, j, st: (i, 0)), grp_p, grp_p, pos_p, pos_p,
                      fixed((ns, d)), fixed((1, ns, d)), fixed((1, ns, d)),
                      fixed((ns, HEAD_DIM)), fixed((ns, HEAD_DIM))],
            out_specs=out_specs,
            scratch_shapes=[pltpu.VMEM((tm, d), BF16), pltpu.VMEM((ns, d), BF16)]),
        out_shape=out_shape,
        compiler_params=_cparams(("arbitrary", "arbitrary")),
    )(starts, g, w_t, xp, *mod_p, *rope_p, xs, *mod_s, *rope_s)
    return outs[:7], outs[7:]


def _bisect(count_ge, lo, hi, clo, kf):
    def cond(st):
        return (st[4] > 0.0) & (st[5] < MAX_BISECT)

    def halve(lo, hi, clo, done):
        mid = 0.5 * lo + 0.5 * hi
        mid = jnp.where(mid <= lo, hi, mid)
        c = count_ge(mid)
        ge = c >= kf
        stuck = (mid >= hi) & jnp.logical_not(ge)
        lo2 = jnp.where(ge, mid, lo)
        hi2 = jnp.where(ge, hi, mid)
        clo2 = jnp.where(ge, c, clo)
        fin = (clo2 <= kf) | (lo2 >= hi2) | stuck
        return lo2, hi2, clo2, jnp.maximum(done, jnp.where(fin, 1.0, 0.0))

    def body(st):
        lo, hi, clo, done = halve(*halve(*st[:4]))
        return lo, hi, clo, done, jnp.sum(1.0 - done), st[5] + 1

    def blind_cond(st):
        return (st[4] > 0.0) & (st[5] < FIXED_TRIPS)

    def blind_body(st):
        lo, hi, clo, done = halve(*halve(*st[:4]))
        return lo, hi, clo, done, st[4], st[5] + 1

    done0 = jnp.where((clo <= kf) | (lo >= hi), 1.0, 0.0)
    st = (lo, hi, clo, done0, jnp.sum(1.0 - done0), jnp.int32(0))
    lo, hi, clo, done, _, trips = lax.while_loop(blind_cond, blind_body, st)
    out = lax.while_loop(cond, body, (lo, hi, clo, done, jnp.sum(1.0 - done), trips))
    return out[0], out[2]


def _attn_kernel(topk, q_ref, qi_ref, sga_ref, k_ref, ki_ref, vt_ref, wt_ref, o_ref,
                 score_ref, bias_ref, cut_ref, m_ref, l_ref, alpha_ref, acc_ref, lg_ref):
    i = pl.program_id(1)
    nck = i + 1
    n_heads = q_ref.shape[1] // HEAD_DIM
    group = n_heads // N_KV_HEADS
    nt = (((1,), (1,)), ((), ()))
    q_pos = i * TQ + lax.broadcasted_iota(jnp.int32, (1, TQ), 1)
    fold = lambda a: a.reshape(CK // SUBLANES, SUBLANES, TQ)

    def score_chunk(c, carry):
        mn, mx = carry
        off = pl.multiple_of(c * CK, CK)
        kc = ki_ref[pl.ds(off, CK), :]
        acc = jnp.zeros((CK, TQ), F32)
        for h in range(IDX_HEADS):
            s = lax.dot_general(kc, qi_ref[:, h * IDX_DIM:(h + 1) * IDX_DIM], nt,
                                preferred_element_type=F32)
            acc = acc + wt_ref[h:h + 1, :] * jnp.maximum(s, 0.0)
        k_pos = off + lax.broadcasted_iota(jnp.int32, (CK, 1), 0)
        causal = k_pos <= q_pos
        score_ref[pl.ds(off, CK), :] = jnp.where(causal, acc, -jnp.inf)
        mn = jnp.minimum(mn, fold(jnp.where(causal, acc, jnp.inf)).min(axis=0))
        mx = jnp.maximum(mx, fold(jnp.where(causal, acc, -jnp.inf)).max(axis=0))
        return mn, mx

    mn, mx = lax.fori_loop(0, nck, score_chunk,
                           (jnp.full((SUBLANES, TQ), jnp.inf, F32),
                            jnp.full((SUBLANES, TQ), -jnp.inf, F32)))
    lo = mn.min(axis=0, keepdims=True)
    hi = mx.max(axis=0, keepdims=True)

    ones_rows = jnp.ones((SUBLANES, CK), BF16)

    def count_ge(thr):
        def body(c, cnt):
            s = score_ref[pl.ds(pl.multiple_of(c * CK, CK), CK), :]
            return cnt + fold(jnp.where(s >= thr, 1.0, 0.0)).sum(axis=0)
        cnt = lax.fori_loop(0, nck, body, jnp.zeros((SUBLANES, TQ), F32))
        return cnt.sum(axis=0, keepdims=True)

    n_valid = (q_pos + 1).astype(F32)
    kf = float(topk)
    thr, n_ge = _bisect(count_ge, lo, hi, n_valid, kf)

    s_len = score_ref.shape[0]
    has_ties = jnp.sum(jnp.where(n_ge > kf, 1.0, 0.0)) > 0.0
    cut_ref[...] = jnp.full(cut_ref.shape, float(s_len), F32)

    @pl.when(has_ties)
    def _():
        def count(pred):
            def body(c, cnt):
                off = pl.multiple_of(c * CK, CK)
                k_pos = (off + lax.broadcasted_iota(jnp.int32, (CK, 1), 0)).astype(F32)
                hit = pred(score_ref[pl.ds(off, CK), :], k_pos)
                return cnt + fold(jnp.where(hit, 1.0, 0.0)).sum(axis=0)
            cnt = lax.fori_loop(0, nck, body, jnp.zeros((SUBLANES, TQ), F32))
            return cnt.sum(axis=0, keepdims=True)

        need = kf - count(lambda s, kp: s > thr)

        def step(_, st):
            lo_i, hi_i = st
            mid = jnp.floor(0.5 * (lo_i + hi_i))
            ge = count(lambda s, kp: (s == thr) & (kp <= mid)) >= need
            return jnp.where(ge, lo_i, mid), jnp.where(ge, mid, hi_i)

        st0 = (jnp.full((1, TQ), -1.0, F32), jnp.full((1, TQ), s_len - 1.0, F32))
        cut = lax.fori_loop(0, (s_len - 1).bit_length() + 1, step, st0)[1]
        cut_ref[...] = jnp.broadcast_to(cut, cut_ref.shape)

    m_ref[...] = jnp.full(m_ref.shape, NEG, F32)
    l_ref[...] = jnp.zeros(l_ref.shape, F32)
    acc_ref[...] = jnp.zeros(acc_ref.shape, F32)

    def att_chunk(c, carry):
        off = pl.multiple_of(c * CK, CK)
        sc = score_ref[pl.ds(off, CK), :]

        @pl.when(jnp.logical_not(has_ties))
        def _():
            bias_ref[...] = jnp.where(sc >= thr, 0.0, NEG)

        @pl.when(has_ties)
        def _():
            k_pos = (off + lax.broadcasted_iota(jnp.int32, (CK, 1), 0)).astype(F32)
            keep = (sc > thr) | ((sc == thr) & (k_pos <= cut_ref[0:1, :]))
            bias_ref[...] = jnp.where(keep, 0.0, NEG)

        kc = k_ref[pl.ds(off, CK), :]
        vtc = vt_ref[0, :, pl.ds(off, CK)]

        def qk(h):
            g = h // group
            return lax.dot_general(kc[:, g * HEAD_DIM:(g + 1) * HEAD_DIM],
                                   q_ref[:, h * HEAD_DIM:(h + 1) * HEAD_DIM], nt,
                                   preferred_element_type=F32)

        for h in range(n_heads):
            lg = qk(h) + bias_ref[...]
            m_old = m_ref[h:h + 1, :]
            m_new = jnp.maximum(m_old, lg.max(axis=0, keepdims=True))
            lg_ref[h] = lg
            alpha_ref[h:h + 1, :] = jnp.exp2(m_old - m_new)
            m_ref[h:h + 1, :] = m_new
        for h in range(n_heads):
            g = h // group
            p = jnp.exp2(lg_ref[h] - m_ref[h:h + 1, :]).astype(BF16)
            pv = jnp.dot(vtc[g * HEAD_DIM:(g + 1) * HEAD_DIM, :], p, preferred_element_type=F32)
            acc_ref[h] = alpha_ref[h:h + 1, :] * acc_ref[h] + pv
            psum = jnp.dot(ones_rows, p, preferred_element_type=F32)[0:1, :]
            l_ref[h:h + 1, :] = alpha_ref[h:h + 1, :] * l_ref[h:h + 1, :] + psum
        return carry

    lax.fori_loop(0, nck, att_chunk, 0)

    for h in range(n_heads):
        o = (acc_ref[h] / l_ref[h:h + 1, :]).T
        sl = slice(h * HEAD_DIM, (h + 1) * HEAD_DIM)
        o_ref[:, sl] = (o * sga_ref[:, sl].astype(F32)).astype(BF16)


def _prompt_attention(zb, vt, wt, lay, b, s, d, topk):
    nq = s // TQ
    n_heads = d // HEAD_DIM
    kern = functools.partial(_attn_kernel, topk)
    row = lambda bb, i: bb * nq + i
    return pl.pallas_call(
        kern,
        grid=(b, nq),
        in_specs=[pl.BlockSpec((TQ, d), lambda bb, i: (row(bb, i), lay.q * TN // d)),
                  pl.BlockSpec((TQ, d), lambda bb, i: (row(bb, i), lay.qi * TN // d)),
                  pl.BlockSpec((TQ, d), lambda bb, i: (row(bb, i), lay.ga * TN // d)),
                  pl.BlockSpec((s, TN), lambda bb, i: (bb, lay.k)),
                  pl.BlockSpec((s, IDX_DIM), lambda bb, i: (bb, lay.misc * TN // IDX_DIM)),
                  pl.BlockSpec((1, TN, s), lambda bb, i: (bb, 0, 0)),
                  pl.BlockSpec((IDX_HEADS, TQ), lambda bb, i: (0, row(bb, i)))],
        out_specs=pl.BlockSpec((TQ, d), lambda bb, i: (row(bb, i), 0)),
        out_shape=jax.ShapeDtypeStruct((b * s, d), BF16),
        scratch_shapes=[pltpu.VMEM((s, TQ), F32),
                        pltpu.VMEM((CK, TQ), F32),
                        pltpu.VMEM((SUBLANES, TQ), F32),
                        pltpu.VMEM((n_heads, TQ), F32),
                        pltpu.VMEM((n_heads, TQ), F32),
                        pltpu.VMEM((n_heads, TQ), F32),
                        pltpu.VMEM((n_heads, HEAD_DIM, TQ), F32),
                        pltpu.VMEM((n_heads, CK, TQ), F32)],
        compiler_params=_cparams(("arbitrary", "arbitrary")),
    )(zb, zb, zb, zb, zb, vt, wt)


def _pool_kernel(ts, u_ref, sgb_ref, wp_ref, sp_ref, o_ref, *level_refs):
    i = pl.program_id(1)
    ext_ref = level_refs[0]
    halo = POOL_HALO
    n = halo + ts

    @pl.when(i == 0)
    def _():
        ext_ref[0:halo, :] = jnp.zeros((halo, ext_ref.shape[1]), F32)

    @pl.when(i > 0)
    def _():
        ext_ref[0:halo, :] = ext_ref[ts:ts + halo, :]

    ext_ref[halo:n, :] = u_ref[...]
    pos = i * ts + lax.broadcasted_iota(jnp.int32, (ts, 1), 0)
    gd = wp_ref.shape[1]
    top = len(POOL_WINDOWS) - 1
    for k in range(top):
        lo, sh, c0 = SUBLANES * (k + 1), POOL_WINDOWS[k] // 2, k * gd
        level_refs[k + 1][lo:n, c0:] = level_refs[k][lo:n, c0:] + level_refs[k][lo - sh:n - sh, c0:]
    sh = POOL_WINDOWS[top] // 2
    wins = [level_refs[g + 1][halo:n, g * gd:(g + 1) * gd] for g in range(top)]
    wins.append(level_refs[top][halo:n, top * gd:] + level_refs[top][halo - sh:n - sh, top * gd:])
    for g, w in enumerate(POOL_WINDOWS):
        cols = slice(g * gd, (g + 1) * gd)
        acc = wins[g]
        cnt = jnp.minimum(pos + 1, w).astype(F32)
        pooled = acc / cnt - u_ref[:, cols]
        mixed = jnp.dot(pooled.astype(BF16), wp_ref[g], preferred_element_type=F32) * sp_ref[:, cols]
        o_ref[:, cols] = (mixed * sgb_ref[:, cols].astype(F32)).astype(BF16)


def _pool_prompt(u, zb, wp, sp, lay, b, s, ts):
    m, pw = u.shape
    nt = s // ts
    kern = functools.partial(_pool_kernel, ts)
    return pl.pallas_call(
        kern,
        grid=(b, nt),
        in_specs=[pl.BlockSpec((ts, pw), lambda bb, i: (bb * nt + i, 0)),
                  pl.BlockSpec((ts, pw), lambda bb, i: (bb * nt + i, lay.gb * TN // pw)),
                  pl.BlockSpec(wp.shape, lambda bb, i: (0, 0, 0)),
                  pl.BlockSpec((1, pw), lambda bb, i: (0, 0))],
        out_specs=pl.BlockSpec((ts, pw), lambda bb, i: (bb * nt + i, 0)),
        out_shape=jax.ShapeDtypeStruct((m, pw), BF16),
        scratch_shapes=[pltpu.VMEM((POOL_HALO + ts, pw), F32) for _ in POOL_WINDOWS],
        compiler_params=_cparams(("arbitrary", "arbitrary")),
    )(u, zb, wp, sp)


def _pool_step_kernel(cnts, ext_ref, sgb_ref, wp_ref, sp_ref, o_ref):
    gd = wp_ref.shape[1]
    rows = ext_ref.shape[0]
    for g, w in enumerate(POOL_WINDOWS):
        cols = slice(g * gd, (g + 1) * gd)
        acc = ext_ref[rows - 1, :, cols]
        for jj in range(1, w):
            acc = acc + ext_ref[rows - 1 - jj, :, cols]
        pooled = acc / cnts[g] - ext_ref[rows - 1, :, cols]
        mixed = jnp.dot(pooled.astype(BF16), wp_ref[g], preferred_element_type=F32) * sp_ref[:, cols]
        o_ref[:, cols] = (mixed * sgb_ref[:, cols].astype(F32)).astype(BF16)


def _pool_step(ext_t, sgb, wp, sp, past):
    rows, n, pw = ext_t.shape
    cnts = tuple(float(min(past + 1, w)) for w in POOL_WINDOWS)
    return pl.pallas_call(
        functools.partial(_pool_step_kernel, cnts),
        out_shape=jax.ShapeDtypeStruct((n, pw), BF16),
    )(ext_t, sgb, wp, sp)


def _merge1_kernel(ab_ref, pb_ref, wa_ref, wb_ref, sma_ref, smb_ref, o_ref):
    a = jnp.dot(ab_ref[...], wa_ref[...], preferred_element_type=F32)
    bb = jnp.dot(pb_ref[...], wb_ref[...], preferred_element_type=F32)
    o_ref[...] = (sma_ref[...].astype(F32) * a + smb_ref[...].astype(F32) * bb).astype(BF16)


def _merge1(ab, pb, wa, wb, zb, lay, tm):
    m, d = ab.shape
    pw = pb.shape[1]
    return pl.pallas_call(
        _merge1_kernel,
        grid=(m // tm, d // TN),
        in_specs=[pl.BlockSpec((tm, d), lambda i, j: (i, 0)),
                  pl.BlockSpec((tm, pw), lambda i, j: (i, 0)),
                  pl.BlockSpec((d, TN), lambda i, j: (0, j)),
                  pl.BlockSpec((pw, TN), lambda i, j: (0, j)),
                  pl.BlockSpec((tm, TN), lambda i, j: (i, lay.ma + j)),
                  pl.BlockSpec((tm, TN), lambda i, j: (i, lay.mb + j))],
        out_specs=pl.BlockSpec((tm, TN), lambda i, j: (i, j)),
        out_shape=jax.ShapeDtypeStruct((m, d), BF16),
        compiler_params=_cparams(("arbitrary", "arbitrary")),
    )(ab, pb, wa, wb, zb, zb)


def _merge2_kernel(final, mm_ref, x_ref, gate_ref, wo_ref, gf_ref, o_ref):
    y = x_ref[...] + gate_ref[0] * jnp.dot(mm_ref[...], wo_ref[...], preferred_element_type=F32)
    if final:
        y = y * lax.rsqrt(jnp.mean(y * y, axis=-1, keepdims=True) + EPS) * gf_ref[...]
    o_ref[...] = y


def _merge2(mm, x, gate, wo, gf, tm, tiles_per_group, final):
    m, d = x.shape
    r = gate.shape[1]
    return pl.pallas_call(
        functools.partial(_merge2_kernel, final),
        grid=(m // tm,),
        in_specs=[pl.BlockSpec((tm, d), lambda i: (i, 0)),
                  pl.BlockSpec((tm, d), lambda i: (i, 0)),
                  pl.BlockSpec((1, r, d), lambda i: (i // tiles_per_group, 0, 0)),
                  pl.BlockSpec((d, d), lambda i: (0, 0)),
                  pl.BlockSpec((1, d), lambda i: (0, 0))],
        out_specs=pl.BlockSpec((tm, d), lambda i: (i, 0)),
        out_shape=jax.ShapeDtypeStruct((m, d), F32),
        compiler_params=_cparams(("arbitrary",)),
    )(mm, x, gate, wo, gf)


def _page_specs(width, npg, npp):
    return [pl.BlockSpec((1, PAGE_SIZE, width),
                         lambda b, p, pt, r=r: (pt[b * npg + p * npp + r], 0, 0)) for r in range(npp)]


def _sidx_kernel(npp, pt_ref, *refs):
    kidx_refs = refs[:npp]
    qi_ref, w_ref, o_ref = refs[npp:]
    nt = (((1,), (1,)), ((), ()))
    qi, w = qi_ref[0], w_ref[0]
    rows = []
    for r in range(npp):
        s = lax.dot_general(qi, kidx_refs[r][0].astype(BF16), nt, preferred_element_type=F32)
        rows.append(jnp.sum(w * jnp.maximum(s, 0.0), axis=0, keepdims=True))
    o_ref[0] = jnp.concatenate(rows, axis=1)


def _sample_scores(page_table, cache_kidx, qi3, wcol, npp):
    n, npg = page_table.shape
    return pl.pallas_call(
        functools.partial(_sidx_kernel, npp),
        grid_spec=pltpu.PrefetchScalarGridSpec(
            num_scalar_prefetch=1,
            grid=(n, npg // npp),
            in_specs=_page_specs(IDX_DIM, npg, npp) + [
                pl.BlockSpec((1, IDX_HEADS, IDX_DIM), lambda b, p, pt: (b, 0, 0)),
                pl.BlockSpec((1, IDX_HEADS, 1), lambda b, p, pt: (b, 0, 0))],
            out_specs=pl.BlockSpec((1, 1, npp * PAGE_SIZE), lambda b, p, pt: (b, 0, p))),
        out_shape=jax.ShapeDtypeStruct((n, 1, npg * PAGE_SIZE), F32),
        compiler_params=_cparams(("arbitrary", "arbitrary")),
    )(page_table.reshape(-1), *([cache_kidx] * npp), qi3, wcol)


def _sthr_kernel(topk, row_base, sc_ref, qi_ref, kin_ref, w_ref, pt_ref, pos_ref, cnt_ref, sel_ref,
                 thr_scr, new_scr):
    sc = sc_ref[...]
    kin = kin_ref[...].astype(BF16).astype(F32)
    s_new = jnp.sum(qi_ref[...].astype(F32) * kin, axis=2, keepdims=True)
    new = jnp.sum(w_ref[...] * jnp.maximum(s_new, 0.0), axis=1, keepdims=True)
    red = lambda f, a: f(f(a, axis=2, keepdims=True), axis=1, keepdims=True)
    lo = jnp.minimum(red(jnp.min, sc), new)
    hi = jnp.maximum(red(jnp.max, sc), new)

    def count_ge(thr):
        c = red(jnp.sum, jnp.where(sc >= thr, 1.0, 0.0))
        return c + jnp.where(new >= thr, 1.0, 0.0)

    n_all = jnp.full(new.shape, float(sc.shape[1] * sc.shape[2] + 1), F32)
    thr, _ = _bisect(count_ge, lo, hi, n_all, float(topk))
    thr_scr[...] = jnp.broadcast_to(thr, thr_scr.shape)
    new_scr[...] = jnp.broadcast_to(new, new_scr.shape)

    rows = sc.shape[1]
    ii = lambda shape, ax: lax.broadcasted_iota(jnp.int32, shape, ax)
    one = lambda m: jnp.where(m, 1.0, 0.0)
    tri_lane = one(ii((PAGE_SIZE, PAGE_SIZE), 0) <= ii((PAGE_SIZE, PAGE_SIZE), 1)).astype(BF16)
    lower = one(ii((rows, rows), 1) < ii((rows, rows), 0)).astype(BF16)
    upper = one(ii((rows, rows), 0) < ii((rows, rows), 1)).astype(BF16)
    ones8 = jnp.ones((SUBLANES, PAGE_SIZE), BF16)
    nt = (((1,), (1,)), ((), ()))
    total = lambda a: jnp.sum(jnp.sum(a, axis=1, keepdims=True), axis=0, keepdims=True)
    dotf = lambda a, b: jnp.dot(a.astype(BF16), b.astype(BF16), preferred_element_type=F32)
    kf = float(topk)
    j_col = ii((topk, 1), 0).astype(F32)
    c_row = ii((1, PAGE_SIZE), 1).astype(F32)

    def compact(b, carry):
        s = sc_ref[b]
        t = thr_scr[b][:, 0:1]
        nw = new_scr[b][:, 0:1]
        gt, eq = s > t, s == t
        need = kf - total(one(gt)) - one(nw > t)
        e_lane = dotf(one(eq), tri_lane)
        e_rank = e_lane + dotf(lower, e_lane)[:, PAGE_SIZE - 1:PAGE_SIZE]
        x = one(gt | (eq & (e_rank <= need)))
        sel_new = (nw > t) | ((nw == t) & (total(one(eq)) < need))
        x_lane = dotf(x, tri_lane)
        n_rows = lax.dot_general(ones8, x.astype(BF16), nt, preferred_element_type=F32)
        n_row = n_rows[0:1, :]
        start_row = dotf(n_rows, upper)[0:1, :]
        hit = one((start_row <= j_col) & (j_col < start_row + n_row))
        lane_sum = lambda a: jnp.sum(a, axis=1, keepdims=True)
        page = lane_sum(hit * pt_ref[b])
        target = j_col - lane_sum(hit * start_row) + 1.0
        in_row = dotf(hit, x_lane * x)
        off = lane_sum(jnp.where(in_row == target, c_row, 0.0))
        pos_ref[b] = (page * PAGE_SIZE + off + row_base).astype(jnp.int32)
        cnt_ref[b] = jnp.broadcast_to(total(x), cnt_ref.shape[1:])
        sel_ref[b] = jnp.broadcast_to(one(sel_new), sel_ref.shape[1:])
        return carry

    lax.fori_loop(0, sc.shape[0], compact, 0)


def _sample_select(sc, qi3, kin, wcol, pt_rows, topk, row_base):
    n = sc.shape[0]
    return pl.pallas_call(
        functools.partial(_sthr_kernel, topk, row_base),
        out_shape=[jax.ShapeDtypeStruct((n, topk, 1), jnp.int32),
                   jax.ShapeDtypeStruct((n, 1, PAGE_SIZE), F32),
                   jax.ShapeDtypeStruct((n, 1, PAGE_SIZE), F32)],
        scratch_shapes=[pltpu.VMEM((n, 1, PAGE_SIZE), F32),
                        pltpu.VMEM((n, 1, PAGE_SIZE), F32)],
        compiler_params=pltpu.CompilerParams(vmem_limit_bytes=VMEM_LIMIT),
    )(sc, qi3, kin, wcol, pt_rows)


def _row_copy(cache_rows, buf, sem, row, j):
    return pltpu.make_async_copy(cache_rows.at[row], buf.at[j], sem)


def _sattn_kernel(pos_ref, ck_hbm, cv_hbm, cnt_ref, sel_ref, q_ref, kn_ref, vn_ref,
                  sga_ref, o_ref, kbuf, vbuf, ksem, vsem):
    b = pl.program_id(0)
    slot = b % 2
    topk = kbuf.shape[1]
    n_heads = q_ref.shape[1]
    group = n_heads // N_KV_HEADS
    nt = (((1,), (1,)), ((), ()))

    def fetch(bb, s):
        def issue(j, carry):
            row = pos_ref[bb * topk + j]
            _row_copy(ck_hbm, kbuf.at[s], ksem.at[s], row, j).start()
            _row_copy(cv_hbm, vbuf.at[s], vsem.at[s], row, j).start()
            return carry
        lax.fori_loop(0, topk, issue, 0, unroll=8)

    @pl.when(b == 0)
    def _():
        fetch(b, slot)

    @pl.when(b + 1 < pl.num_programs(0))
    def _():
        fetch(b + 1, 1 - slot)

    kbuf, vbuf = kbuf.at[slot], vbuf.at[slot]
    if topk % PAGE_SIZE == 0:
        for w in range(topk // PAGE_SIZE):
            rows = pl.ds(w * PAGE_SIZE, PAGE_SIZE)
            pltpu.make_async_copy(ck_hbm.at[rows], kbuf.at[rows], ksem.at[slot]).wait()
            pltpu.make_async_copy(cv_hbm.at[rows], vbuf.at[rows], vsem.at[slot]).wait()
    else:
        def wait(j, carry):
            _row_copy(ck_hbm, kbuf, ksem.at[slot], 0, j).wait()
            _row_copy(cv_hbm, vbuf, vsem.at[slot], 0, j).wait()
            return carry

        lax.fori_loop(0, topk, wait, 0)

    q = q_ref[0]
    head = lax.broadcasted_iota(jnp.int32, (n_heads, 1), 0)
    in_group = [(head >= g * group) & (head < (g + 1) * group) for g in range(N_KV_HEADS)]
    lg = jnp.zeros((n_heads, topk), F32)
    for g in range(N_KV_HEADS):
        lg_g = lax.dot_general(q, kbuf[:, g, :].astype(BF16), nt, preferred_element_type=F32)
        lg = lg + jnp.where(in_group[g], lg_g, 0.0)
    valid = lax.broadcasted_iota(jnp.int32, (1, topk), 1).astype(F32) < cnt_ref[0][:, 0:1]
    lg = jnp.where(valid, lg, NEG)
    kn = kn_ref[0].astype(BF16).astype(F32)
    vn = vn_ref[0].astype(BF16).astype(F32)
    lgn = jnp.sum(q.astype(F32) * kn, axis=1, keepdims=True)
    lgn = jnp.where(sel_ref[0][:, 0:1] > 0.0, lgn, NEG)
    m = jnp.maximum(lg.max(axis=1, keepdims=True), lgn)
    p = jnp.exp2(lg - m)
    pn = jnp.exp2(lgn - m)
    l = p.sum(axis=1, keepdims=True) + pn
    pb = p.astype(BF16)
    acc = pn.astype(BF16).astype(F32) * vn
    for g in range(N_KV_HEADS):
        pv = jnp.dot(pb, vbuf[:, g, :].astype(BF16), preferred_element_type=F32)
        acc = acc + jnp.where(in_group[g], pv, 0.0)
    o_ref[0] = ((acc / l) * sga_ref[0].astype(F32)).astype(BF16)


def _sample_attention(k_rows, v_rows, pos, cnt, sel, q3, kn, vn, sga3):
    n, n_heads = q3.shape[:2]
    topk = pos.shape[1]
    per_b = lambda shape: pl.BlockSpec(shape, lambda b, ps: (b, 0, 0))
    hbm = pl.BlockSpec(memory_space=pl.ANY)
    return pl.pallas_call(
        _sattn_kernel,
        grid_spec=pltpu.PrefetchScalarGridSpec(
            num_scalar_prefetch=1,
            grid=(n,),
            in_specs=[hbm, hbm,
                      per_b((1, 1, PAGE_SIZE)),
                      per_b((1, 1, PAGE_SIZE)),
                      per_b((1, n_heads, HEAD_DIM)),
                      per_b((1, n_heads, HEAD_DIM)),
                      per_b((1, n_heads, HEAD_DIM)),
                      per_b((1, n_heads, HEAD_DIM))],
            out_specs=per_b((1, n_heads, HEAD_DIM)),
            scratch_shapes=[pltpu.VMEM((2, topk, N_KV_HEADS, HEAD_DIM), k_rows.dtype),
                            pltpu.VMEM((2, topk, N_KV_HEADS, HEAD_DIM), v_rows.dtype),
                            pltpu.SemaphoreType.DMA((2,)),
                            pltpu.SemaphoreType.DMA((2,))]),
        out_shape=jax.ShapeDtypeStruct((n, n_heads, HEAD_DIM), BF16),
        compiler_params=_cparams(("arbitrary",)),
    )(pos.reshape(-1), k_rows, v_rows, cnt, sel, q3, kn, vn, sga3)


def _rope_tables(pos):
    half = HEAD_DIM // 2
    inv = ROPE_THETA ** (-jnp.arange(half, dtype=F32) / half)
    ang = pos.astype(F32)[:, None] * inv[None, :]
    cos, sin = jnp.cos(ang), jnp.sin(ang)
    return jnp.concatenate([cos, cos], axis=1), jnp.concatenate([-sin, sin], axis=1)


def _tile_starts(d, lay):
    att, kv, idx, pool = d, N_KV_HEADS * HEAD_DIM, IDX_HEADS * IDX_DIM, d // 2
    sizes = (att, kv, kv, idx, IDX_DIM, IDX_HEADS, att, pool, pool, d, d)
    offs = [0]
    for n in sizes:
        offs.append(offs[-1] + n)
    q, k, v, qi, ki, wi, ga, u, gb, ma, mb = offs[:-1]
    starts = [0] * lay.end
    for first, tiles, col in ((lay.q, lay.qi - lay.q, q), (lay.qi, lay.ga - lay.qi, qi),
                              (lay.ga, lay.ma - lay.ga, ga), (lay.ma, lay.mb - lay.ma, ma),
                              (lay.mb, lay.gb - lay.mb, mb), (lay.gb, lay.k - lay.gb, gb),
                              (lay.k, 1, k), (lay.v, 1, v), (lay.misc, 1, ki), (lay.u, lay.end - lay.u, u)):
        for t in range(tiles):
            starts[first + t] = col + t * TN
    assert wi == ki + IDX_DIM and all(s + TN <= offs[-1] and s % SUBLANES == 0 for s in starts)
    return starts


def kernel(x_prompt, x_sample, c_prompt, c_sample, cache_k, cache_v, cache_kidx, state_pool, page_table,
           g_norm, w_ada, b_ada, w_in, w_pool, s_pool, w_a_proj, w_b_proj, w_out, g_final):
    bp, sp, d = x_prompt.shape
    ns, ds, _ = x_sample.shape
    depth = g_norm.shape[0]
    npg = page_table.shape[1]
    past = npg * PAGE_SIZE
    assert ds == 1 and d % (N_KV_HEADS * HEAD_DIM) == 0 and sp % TQ == 0 and TQ == CK and sp >= POOL_STATE
    n_heads = d // HEAD_DIM
    kvw = N_KV_HEADS * HEAD_DIM
    lay = _Layout(d)
    pw = lay.pool
    idx_scale = (IDX_HEADS * IDX_DIM) ** -0.5
    topk_p = min(TOPK_MAX, sp // 4)
    topk_s = min(TOPK_MAX, (past + ds) // 4)
    tm = math.gcd(sp, ROW_TILE)
    ts = math.gcd(sp, POOL_TILE)
    tm2 = math.gcd(sp, OUT_TILE)

    cos_p, sin_p = _rope_tables(jnp.arange(sp, dtype=jnp.int32))
    cos_s, sin_s = _rope_tables(jnp.full((ns,), past, jnp.int32))
    n_c = bp + ns
    c_rows = -(-n_c // SUBLANES) * SUBLANES
    c_all = jnp.concatenate([c_prompt, c_sample, jnp.zeros((c_rows - n_c, d), F32)], axis=0)

    n_pool = cache_kidx.shape[1]
    kidx_pages = cache_kidx.reshape(depth * n_pool, PAGE_SIZE, IDX_DIM)
    npp = math.gcd(npg, PAGES_PER_STEP)
    k_rows = cache_k.reshape(depth * n_pool * PAGE_SIZE, N_KV_HEADS, HEAD_DIM)
    v_rows = cache_v.reshape(depth * n_pool * PAGE_SIZE, N_KV_HEADS, HEAD_DIM)
    pt_rows = page_table.astype(F32).reshape(ns, 1, npg)
    assert npg % npp == 0

    in_width = w_in.shape[2]
    w_t = jnp.swapaxes(w_in, 1, 2).reshape(depth * in_width, d)

    xp = x_prompt.reshape(bp * sp, d)
    xs = x_sample.reshape(ns, d)
    outs = {n: [] for n in ("kp", "vp", "kip", "pp", "ks", "vs", "kis", "ps")}
    for l in range(depth):
        final = l == depth - 1
        starts = jnp.asarray([l * in_width + s for s in _tile_starts(d, lay)], jnp.int32)
        wa, wb, wo, wpl = (w_a_proj[l].astype(BF16), w_b_proj[l].astype(BF16), w_out[l].astype(BF16),
                           w_pool[l].astype(BF16))
        g = g_norm[l].reshape(1, d)
        spl = s_pool[l].reshape(1, pw)
        gf = g_final.reshape(1, d)

        mod = _ada(c_all, w_ada[l], b_ada[l])
        shift, scale, gate = mod[:, :d], mod[:, d:2 * d], mod[:, 2 * d:]
        pg = lambda a: a[:bp].reshape(bp, 1, d)
        sg = lambda a: a[bp:n_c].reshape(1, ns, d)

        (zb, k_p, v_p, misc_p, u_p, vt, wt), (zs, k_s, v_s, misc_s, u_s) = _project(
            xp, xs, g, (pg(shift), pg(scale)), (sg(shift), sg(scale)), w_t, starts,
            (cos_p, sin_p), (cos_s, sin_s), lay, tm, sp // tm, idx_scale)

        ab = _prompt_attention(zb, vt, wt, lay, bp, sp, d, topk_p)
        pb = _pool_prompt(u_p, zb, wpl, spl, lay, bp, sp, ts)
        mm = _merge1(ab, pb, wa, wb, zb, lay, tm)
        xp = _merge2(mm, xp, pg(gate), wo, gf, tm2, sp // tm2, final)
        outs["kp"].append(k_p.reshape(bp, sp, N_KV_HEADS, HEAD_DIM))
        outs["vp"].append(v_p.reshape(bp, sp, N_KV_HEADS, HEAD_DIM))
        outs["kip"].append(misc_p[:, :IDX_DIM].reshape(bp, sp, IDX_DIM))
        outs["pp"].append(u_p.reshape(bp, sp, pw)[:, sp - POOL_STATE:, :])

        col = lambda t, n: zs[:, t * TN:t * TN + n]
        q3 = col(lay.q, d).reshape(ns, n_heads, HEAD_DIM)
        qi3 = col(lay.qi, IDX_HEADS * IDX_DIM).reshape(ns, IDX_HEADS, IDX_DIM)
        sga3 = col(lay.ga, d).reshape(ns, n_heads, HEAD_DIM)
        wcol = misc_s[:, IDX_DIM:IDX_DIM + IDX_HEADS].reshape(ns, IDX_HEADS, 1)
        kin = misc_s[:, :IDX_DIM].reshape(ns, 1, IDX_DIM)
        pt_l = page_table + l * n_pool
        sc = _sample_scores(pt_l, kidx_pages, qi3, wcol, npp)
        pos, cnt, sel = _sample_select(sc.reshape(ns, npg, PAGE_SIZE), qi3, kin, wcol, pt_rows, topk_s,
                                       l * n_pool * PAGE_SIZE)
        rep = lambda a: jnp.repeat(a.reshape(ns, N_KV_HEADS, HEAD_DIM), n_heads // N_KV_HEADS, axis=1)
        ab_s = _sample_attention(k_rows, v_rows, pos, cnt, sel, q3, rep(k_s), rep(v_s), sga3).reshape(ns, d)
        ext = jnp.concatenate([state_pool[l], u_s[:, None, :]], axis=1)
        pb_s = _pool_step(ext.transpose(1, 0, 2), col(lay.gb, pw), wpl, spl, past)
        mm_s = _merge1(ab_s, pb_s, wa, wb, zs, lay, ns)
        xs = _merge2(mm_s, xs, sg(gate), wo, gf, ns, 1, final)
        outs["ks"].append(k_s.reshape(ns, ds, N_KV_HEADS, HEAD_DIM))
        outs["vs"].append(v_s.reshape(ns, ds, N_KV_HEADS, HEAD_DIM))
        outs["kis"].append(misc_s[:, :IDX_DIM].reshape(ns, ds, IDX_DIM))
        outs["ps"].append(ext[:, 1:, :])

    st = lambda n: outs[n][0][None] if depth == 1 else jnp.stack(outs[n])
    return (xp.reshape(bp, sp, d), xs.reshape(ns, ds, d),
            st("kp"), st("vp"), st("kip"), st("pp"), st("ks"), st("vs"), st("kis"), st("ps"))
```
